```python
import math
import jax
import jax.numpy as jnp
from jax import lax
import numpy as np

D_MODEL = 1024
BATCH = 8
SEQ = 2048
DEPTH = 2

GRID_W = 64
CTX_LEN = 256
N_DIR = 2
EPS = 1e-6

ML_HEADS = 4
ML_HEAD_DIM = 128
ML_WIDTH = ML_HEADS * ML_HEAD_DIM
ML_CHUNK = 64

SSD_HEADS = 8
SSD_HEAD_DIM = 64
SSD_WIDTH = SSD_HEADS * SSD_HEAD_DIM
SSD_STATE = 64
SSD_GROUPS = 2
SSD_CONV = 3
SSD_CHUNK = 64
SSD_XBC = SSD_WIDTH + 2 * SSD_GROUPS * SSD_STATE

ATT_Q_HEADS = 8
ATT_KV_HEADS = 2
ATT_HEAD_DIM = 64
ATT_WIDTH = ATT_Q_HEADS * ATT_HEAD_DIM
ATT_KV_WIDTH = ATT_KV_HEADS * ATT_HEAD_DIM
WINDOW = 128
ATT_BLOCK = 128
ROPE_BASE = 10000.0

MIX_WIDTH = ML_WIDTH + SSD_WIDTH + ATT_WIDTH
D_FF = 4 * D_MODEL

IN_LAYOUT = (
    ('ml_q', ML_WIDTH), ('ml_k', ML_WIDTH), ('ml_v', ML_WIDTH), ('ml_o', ML_WIDTH),
    ('ml_i', N_DIR * ML_HEADS), ('ml_f', N_DIR * ML_HEADS),
    ('ssd_z', SSD_WIDTH), ('ssd_xbc', SSD_XBC), ('ssd_dt', N_DIR * SSD_HEADS),
    ('att_q', ATT_WIDTH), ('att_k', ATT_KV_WIDTH), ('att_v', ATT_KV_WIDTH),
)
IN_WIDTH = (4 * ML_WIDTH + 2 * N_DIR * ML_HEADS + SSD_WIDTH + SSD_XBC + N_DIR * SSD_HEADS
            + ATT_WIDTH + 2 * ATT_KV_WIDTH)

kernel_name = 'hybrid_mlstm_ssd_swa_dit_block'


def rms_norm(x, g):
    xf = x.astype(jnp.float32)
    y = xf * lax.rsqrt(jnp.mean(xf * xf, axis=-1, keepdims=True) + EPS)
    return (y * g.astype(jnp.float32)).astype(x.dtype)


def modulate(h, shift, scale):
    return h * (1 + scale) + shift


def split_cols(u):
    out, off = {}, 0
    for name, width in IN_LAYOUT:
        out[name] = u[..., off:off + width]
        off += width
    return out


def to_chunks(a, q):
    bsz, n = a.shape[:2]
    a = a.reshape((bsz, n // q, q) + a.shape[2:])
    return jnp.moveaxis(jnp.moveaxis(a, 1, 0), 3, 2)


def from_chunks(a):
    nc, bsz, h, q, d = a.shape
    return jnp.moveaxis(jnp.moveaxis(a, 0, 1), 2, 3).reshape(bsz, nc * q, h, d)


def axial_rope(x, rows, cols):
    half = x.shape[-1] // 2
    nfreq = half // 2
    inv = ROPE_BASE ** (-jnp.arange(nfreq, dtype=jnp.float32) / nfreq)

    def rot(xa, pos):
        ang = pos.astype(jnp.float32)[:, None] * inv
        cos, sin = jnp.cos(ang)[None, :, None, :], jnp.sin(ang)[None, :, None, :]
        x1, x2 = xa[..., :nfreq].astype(jnp.float32), xa[..., nfreq:].astype(jnp.float32)
        return jnp.concatenate([x1 * cos - x2 * sin, x2 * cos + x1 * sin], axis=-1)

    return jnp.concatenate([rot(x[..., :half], rows), rot(x[..., half:], cols)], axis=-1).astype(x.dtype)


def depthwise_conv(u, w, b):
    pad = (SSD_CONV - 1) // 2
    y = lax.conv_general_dilated(u, w[:, None, :].astype(u.dtype), window_strides=(1,),
                                 padding=((pad, pad),), dimension_numbers=('NWC', 'WIO', 'NWC'),
                                 feature_group_count=u.shape[-1])
    return y + b.astype(u.dtype)


def mlstm_chunk_scan(q, k, v, log_i, log_f, state, with_outputs):
    tri = jnp.tril(jnp.ones((ML_CHUNK, ML_CHUNK), dtype=bool))

    def step(carry, inp):
        cmat, nvec, m = carry
        qc, kc, vc, li, lf = inp
        b = jnp.cumsum(lf, axis=-1)
        b_end = b[..., -1]
        wlog = b_end[..., None] - b + li
        m_new = jnp.maximum(b_end + m, jnp.max(wlog, axis=-1))
        w = jnp.exp(wlog - m_new[..., None])
        decay = jnp.exp(b_end + m - m_new)
        c_new = decay[..., None, None] * cmat + jnp.einsum('bhs,bhsk,bhsv->bhkv', w, kc, vc)
        n_new = decay[..., None] * nvec + jnp.einsum('bhs,bhsk->bhk', w, kc)
        if not with_outputs:
            return (c_new, n_new, m_new), None
        dlog = jnp.where(tri, b[..., :, None] - b[..., None, :] + li[..., None, :], -jnp.inf)
        inter = b + m[..., None]
        m_t = jnp.maximum(inter, jnp.max(dlog, axis=-1))
        s = jnp.einsum('bhtd,bhsd->bhts', qc, kc) * jnp.exp(dlog - m_t[..., None])
        g = jnp.exp(inter - m_t)
        num = jnp.einsum('bhts,bhsv->bhtv', s, vc) + g[..., None] * jnp.einsum('bhtk,bhkv->bhtv', qc, cmat)
        den = jnp.sum(s, axis=-1) + g * jnp.einsum('bhtk,bhk->bht', qc, nvec)
        h = num / jnp.maximum(jnp.abs(den), jnp.exp(-m_t))[..., None]
        return (c_new, n_new, m_new), h

    inputs = tuple(to_chunks(t, ML_CHUNK) for t in (q, k, v, log_i, log_f))
    state, hs = lax.scan(step, state, inputs)
    return (from_chunks(hs) if with_outputs else None), state


def ssd_chunk_scan(x, bm, cm, dt, a_head, state, with_outputs):
    la = dt * a_head
    tri = jnp.tril(jnp.ones((SSD_CHUNK, SSD_CHUNK), dtype=bool))

    def step(h, inp):
        xc, bc, cc, dtc, lac = inp
        acum = jnp.cumsum(lac, axis=-1)
        a_end = acum[..., -1]
        w = jnp.exp(a_end[..., None] - acum) * dtc
        h_new = jnp.exp(a_end)[..., None, None] * h + jnp.einsum('bhs,bhsp,bhsn->bhpn', w, xc, bc)
        if not with_outputs:
            return h_new, None
        seg = jnp.exp(jnp.where(tri, acum[..., :, None] - acum[..., None, :], -jnp.inf))
        scores = jnp.einsum('bhtn,bhsn->bhts', cc, bc) * seg * dtc[..., None, :]
        y = (jnp.einsum('bhts,bhsp->bhtp', scores, xc)
             + jnp.exp(acum)[..., None] * jnp.einsum('bhtn,bhpn->bhtp', cc, h))
        return h_new, y

    inputs = tuple(to_chunks(t, SSD_CHUNK) for t in (x, bm, cm, dt, la))
    state, ys = lax.scan(step, state, inputs)
    return (from_chunks(ys) if with_outputs else None), state


def orient_fn(d):
    return (lambda a: jnp.flip(a, axis=1)) if d == 1 else (lambda a: a)


def mlstm_mixer(pl, pc, b_i, b_f, norm_g, need_ctx_out):
    f32 = jnp.float32

    def prep(p):
        bsz, n = p['ml_q'].shape[:2]
        shp, gshp = (bsz, n, ML_HEADS, ML_HEAD_DIM), (bsz, n, N_DIR, ML_HEADS)
        q = p['ml_q'].astype(f32).reshape(shp)
        k = p['ml_k'].astype(f32).reshape(shp) * (ML_HEAD_DIM ** -0.5)
        v = p['ml_v'].astype(f32).reshape(shp)
        log_i = p['ml_i'].astype(f32).reshape(gshp) + b_i.astype(f32)
        log_f = jax.nn.log_sigmoid(p['ml_f'].astype(f32).reshape(gshp) + b_f.astype(f32))
        return q, k, v, log_i, log_f

    def finish(h, p):
        bsz, n = h.shape[:2]
        h = rms_norm(h, norm_g.reshape(ML_HEADS, ML_HEAD_DIM)).reshape(bsz, n, ML_WIDTH)
        return (h * jax.nn.sigmoid(p['ml_o'].astype(f32))).astype(p['ml_o'].dtype)

    lat, con = prep(pl), prep(pc)
    bsz = lat[0].shape[0]
    h_lat, h_ctx = [], []
    for d in range(N_DIR):
        orient = orient_fn(d)
        state0 = (jnp.zeros((bsz, ML_HEADS, ML_HEAD_DIM, ML_HEAD_DIM), f32),
                  jnp.zeros((bsz, ML_HEADS, ML_HEAD_DIM), f32),
                  jnp.zeros((bsz, ML_HEADS), f32))
        args_c = [orient(a) for a in (con[0], con[1], con[2], con[3][:, :, d], con[4][:, :, d])]
        args_l = [orient(a) for a in (lat[0], lat[1], lat[2], lat[3][:, :, d], lat[4][:, :, d])]
        hc, st = mlstm_chunk_scan(*args_c, state0, need_ctx_out)
        hl, _ = mlstm_chunk_scan(*args_l, st, True)
        h_lat.append(orient(hl))
        if need_ctx_out:
            h_ctx.append(orient(hc))
    out_l = finish(h_lat[0] + h_lat[1], pl)
    out_c = finish(h_ctx[0] + h_ctx[1], pc) if need_ctx_out else None
    return out_l, out_c


def ssd_mixer(pl, pc, conv_w, conv_b, a_log, dt_bias, d_skip, norm_g, need_ctx_out):
    f32 = jnp.float32
    gn = SSD_GROUPS * SSD_STATE
    rep = SSD_HEADS // SSD_GROUPS

    def prep(p):
        bsz, n = p['ssd_xbc'].shape[:2]
        xbc = jax.nn.silu(depthwise_conv(p['ssd_xbc'], conv_w, conv_b)).astype(f32)
        xs = xbc[..., :SSD_WIDTH].reshape(bsz, n, SSD_HEADS, SSD_HEAD_DIM)
        bm = jnp.repeat(xbc[..., SSD_WIDTH:SSD_WIDTH + gn].reshape(bsz, n, SSD_GROUPS, SSD_STATE), rep, axis=2)
        cm = jnp.repeat(xbc[..., SSD_WIDTH + gn:].reshape(bsz, n, SSD_GROUPS, SSD_STATE), rep, axis=2)
        dt = jax.nn.softplus(p['ssd_dt'].astype(f32).reshape(bsz, n, N_DIR, SSD_HEADS) + dt_bias.astype(f32))
        return xs, bm, cm, dt

    def finish(y, xs, p):
        bsz, n = y.shape[:2]
        y = (y + d_skip.astype(f32)[:, None] * xs).reshape(bsz, n, SSD_WIDTH)
        return rms_norm(y * jax.nn.silu(p['ssd_z'].astype(f32)), norm_g).astype(p['ssd_z'].dtype)

    a = -jnp.exp(a_log.astype(f32))
    lat, con = prep(pl), prep(pc)
    bsz = lat[0].shape[0]
    y_lat, y_ctx = [], []
    for d in range(N_DIR):
        orient = orient_fn(d)
        state0 = jnp.zeros((bsz, SSD_HEADS, SSD_HEAD_DIM, SSD_STATE), f32)
        args_c = [orient(t) for t in (con[0], con[1], con[2], con[3][:, :, d])]
        args_l = [orient(t) for t in (lat[0], lat[1], lat[2], lat[3][:, :, d])]
        yc, st = ssd_chunk_scan(*args_c, a[d], state0, need_ctx_out)
        yl, _ = ssd_chunk_scan(*args_l, a[d], st, True)
        y_lat.append(orient(yl))
        if need_ctx_out:
            y_ctx.append(orient(yc))
    out_l = finish(y_lat[0] + y_lat[1], lat[0], pl)
    out_c = finish(y_ctx[0] + y_ctx[1], con[0], pc) if need_ctx_out else None
    return out_l, out_c


def window_attention(pl, pc, qn_g, kn_g, sink, rows, cols, need_ctx_out):
    f32 = jnp.float32
    bsz, n = pl['att_q'].shape[:2]
    lc = pc['att_k'].shape[1]
    rep = ATT_Q_HEADS // ATT_KV_HEADS
    dh = ATT_HEAD_DIM
    scale = dh ** -0.5
    ql = axial_rope(rms_norm(pl['att_q'].reshape(bsz, n, ATT_Q_HEADS, dh), qn_g), rows, cols)
    kl = axial_rope(rms_norm(pl['att_k'].reshape(bsz, n, ATT_KV_HEADS, dh), kn_g), rows, cols)
    vl = pl['att_v'].reshape(bsz, n, ATT_KV_HEADS, dh)
    kc = rms_norm(pc['att_k'].reshape(bsz, lc, ATT_KV_HEADS, dh), kn_g)
    vc = pc['att_v'].reshape(bsz, lc, ATT_KV_HEADS, dh)
    sink_g = sink.astype(f32).reshape(ATT_KV_HEADS, rep)

    nb = n // ATT_BLOCK
    qb = ql.reshape(bsz, nb, ATT_BLOCK, ATT_KV_HEADS, rep, dh)

    def band(t):
        tp = jnp.pad(t, ((0, 0), (ATT_BLOCK, ATT_BLOCK), (0, 0), (0, 0)))
        tp = tp.reshape(bsz, nb + 2, ATT_BLOCK, ATT_KV_HEADS, dh)
        return jnp.concatenate([tp[:, :-2], tp[:, 1:-1], tp[:, 2:]], axis=2)

    kb, vb = band(kl), band(vl)
    blk = jnp.arange(nb)[:, None, None]
    qpos = blk * ATT_BLOCK + jnp.arange(ATT_BLOCK)[None, :, None]
    kpos = (blk - 1) * ATT_BLOCK + jnp.arange(3 * ATT_BLOCK)[None, None, :]
    valid = (jnp.abs(kpos - qpos) <= WINDOW) & (kpos >= 0) & (kpos < n)
    s_band = jnp.einsum('bnqgrd,bnkgd->bngrqk', qb, kb).astype(f32) * scale
    s_band = jnp.where(valid[None, :, None, None], s_band, -jnp.inf)
    s_ctx = jnp.einsum('bnqgrd,bkgd->bngrqk', qb, kc).astype(f32) * scale
    s_sink = jnp.broadcast_to(sink_g[None, None, :, :, None, None], s_ctx.shape[:-1] + (1,))
    prob = jax.nn.softmax(jnp.concatenate([s_sink, s_ctx, s_band], axis=-1), axis=-1).astype(vl.dtype)
    o = (jnp.einsum('bngrqk,bkgd->bnqgrd', prob[..., 1:1 + lc], vc)
         + jnp.einsum('bngrqk,bnkgd->bnqgrd', prob[..., 1 + lc:], vb))
    out_l = o.reshape(bsz, n, ATT_WIDTH)

    out_c = None
    if need_ctx_out:
        qc = rms_norm(pc['att_q'].reshape(bsz, lc, ATT_Q_HEADS, dh), qn_g).reshape(bsz, lc, ATT_KV_HEADS, rep, dh)
        s_c = jnp.einsum('bqgrd,bkgd->bgrqk', qc, kc).astype(f32) * scale
        s_cs = jnp.broadcast_to(sink_g[None, :, :, None, None], s_c.shape[:-1] + (1,))
        prob_c = jax.nn.softmax(jnp.concatenate([s_cs, s_c], axis=-1), axis=-1).astype(vc.dtype)
        out_c = jnp.einsum('bgrqk,bkgd->bqgrd', prob_c[..., 1:], vc).reshape(bsz, lc, ATT_WIDTH)
    return out_l, out_c


def token_mixers(ul, uc, ml_b_i, ml_b_f, ml_norm_g, ssd_conv_w, ssd_conv_b, ssd_a_log, ssd_dt_bias,
                 ssd_d, ssd_norm_g, att_qn_g, att_kn_g, att_sink, rows, cols, need_ctx_out):
    pl, pc = split_cols(ul), split_cols(uc)
    m_l, m_c = mlstm_mixer(pl, pc, ml_b_i, ml_b_f, ml_norm_g, need_ctx_out)
    s_l, s_c = ssd_mixer(pl, pc, ssd_conv_w, ssd_conv_b, ssd_a_log, ssd_dt_bias, ssd_d, ssd_norm_g, need_ctx_out)
    a_l, a_c = window_attention(pl, pc, att_qn_g, att_kn_g, att_sink, rows, cols, need_ctx_out)
    y_l = jnp.concatenate([m_l, s_l, a_l], axis=-1)
    y_c = jnp.concatenate([m_c, s_c, a_c], axis=-1) if need_ctx_out else None
    return y_l, y_c


def sq_relu_mlp(h, w_up, w_down):
    return jnp.square(jax.nn.relu(h @ w_up)) @ w_down


def setup_inputs(seed: int = 0) -> dict:
    key = jax.random.key(seed)
    ks = jax.random.split(key, 24)
    f32 = jnp.float32

    def nrm(k, shape, scale):
        return jax.random.normal(k, shape, f32) * scale

    def gain(k, shape):
        return 1.0 + 0.02 * jax.random.normal(k, shape, f32)

    dt0 = jnp.exp(jax.random.uniform(ks[14], (DEPTH, N_DIR, SSD_HEADS), f32, math.log(1e-3), math.log(1e-1)))
    return {
        'x': nrm(ks[0], (BATCH, SEQ, D_MODEL), 1.0),
        'c': nrm(ks[1], (BATCH, D_MODEL), 1.0),
        'ctx': nrm(ks[2], (BATCH, CTX_LEN, D_MODEL), 1.0),
        'c_ctx': nrm(ks[3], (D_MODEL,), 1.0),
        'w_mod': nrm(ks[4], (DEPTH, D_MODEL, 6 * D_MODEL), D_MODEL ** -0.5),
        'b_mod': nrm(ks[5], (DEPTH, 6 * D_MODEL), 0.02),
        'norm1_g': gain(ks[6], (DEPTH, D_MODEL)),
        'w_in': nrm(ks[7], (DEPTH, D_MODEL, IN_WIDTH), D_MODEL ** -0.5),
        'ml_b_i': nrm(ks[8], (DEPTH, N_DIR, ML_HEADS), 0.1),
        'ml_b_f': jnp.linspace(3.0, 6.0, ML_HEADS, dtype=f32) + nrm(ks[9], (DEPTH, N_DIR, ML_HEADS), 0.1),
        'ml_norm_g': gain(ks[10], (DEPTH, ML_WIDTH)),
        'ssd_conv_w': nrm(ks[11], (DEPTH, SSD_CONV, SSD_XBC), SSD_CONV ** -0.5),
        'ssd_conv_b': nrm(ks[12], (DEPTH, SSD_XBC), 0.02),
        'ssd_a_log': jnp.log(jax.random.uniform(ks[13], (DEPTH, N_DIR, SSD_HEADS), f32, 1.0, 16.0)),
        'ssd_dt_bias': dt0 + jnp.log(-jnp.expm1(-dt0)),
        'ssd_d': gain(ks[15], (DEPTH, SSD_HEADS)),
        'ssd_norm_g': gain(ks[16], (DEPTH, SSD_WIDTH)),
        'att_qn_g': gain(ks[17], (DEPTH, ATT_HEAD_DIM)),
        'att_kn_g': gain(ks[18], (DEPTH, ATT_HEAD_DIM)),
        'att_sink': nrm(ks[19], (DEPTH, ATT_Q_HEADS), 0.5),
        'w_out': nrm(ks[20], (DEPTH, MIX_WIDTH, D_MODEL), MIX_WIDTH ** -0.5),
        'norm2_g': gain(ks[21], (DEPTH, D_MODEL)),
        'w_up': nrm(ks[22], (DEPTH, D_MODEL, D_FF), D_MODEL ** -0.5),
        'w_down': nrm(ks[23], (DEPTH, D_FF, D_MODEL), D_FF ** -0.5),
    }


def reference(x, c, ctx, c_ctx, w_mod, b_mod, norm1_g, w_in, ml_b_i, ml_b_f, ml_norm_g,
              ssd_conv_w, ssd_conv_b, ssd_a_log, ssd_dt_bias, ssd_d, ssd_norm_g,
              att_qn_g, att_kn_g, att_sink, w_out, norm2_g, w_up, w_down):
    n_lat = x.shape[1]
    n_rows = n_lat // GRID_W
    rows = jnp.repeat(jnp.arange(n_rows, dtype=jnp.int32), GRID_W)
    cols = jnp.tile(jnp.arange(GRID_W, dtype=jnp.int32), n_rows)
    s_lat = jax.nn.silu(c)
    s_ctx = jax.nn.silu(c_ctx)
    h_ctx = ctx
    for layer in range(DEPTH):
        need_ctx_out = layer < DEPTH - 1
        mod_l = jnp.split((s_lat @ w_mod[layer] + b_mod[layer])[:, None, :], 6, axis=-1)
        mod_c = jnp.split(s_ctx @ w_mod[layer] + b_mod[layer], 6, axis=-1)
        ul = modulate(rms_norm(x, norm1_g[layer]), mod_l[0], mod_l[1]) @ w_in[layer]
        uc = modulate(rms_norm(h_ctx, norm1_g[layer]), mod_c[0], mod_c[1]) @ w_in[layer]
        y_l, y_c = token_mixers(ul, uc, ml_b_i[layer], ml_b_f[layer], ml_norm_g[layer],
                                ssd_conv_w[layer], ssd_conv_b[layer], ssd_a_log[layer], ssd_dt_bias[layer],
                                ssd_d[layer], ssd_norm_g[layer], att_qn_g[layer], att_kn_g[layer],
                                att_sink[layer], rows, cols, need_ctx_out)
        x = x + mod_l[2] * (y_l @ w_out[layer])
        x = x + mod_l[5] * sq_relu_mlp(modulate(rms_norm(x, norm2_g[layer]), mod_l[3], mod_l[4]),
                                       w_up[layer], w_down[layer])
        if need_ctx_out:
            h_ctx = h_ctx + mod_c[2] * (y_c @ w_out[layer])
            h_ctx = h_ctx + mod_c[5] * sq_relu_mlp(modulate(rms_norm(h_ctx, norm2_g[layer]), mod_c[3], mod_c[4]),
                                                   w_up[layer], w_down[layer])
    return x
```

```python
import functools
import math

import numpy as np
import jax
import jax.numpy as jnp
from jax import lax
from jax.experimental import pallas as pl
from jax.experimental.pallas import tpu as pltpu

F32 = jnp.float32
BF16 = jnp.bfloat16

EPS = 1e-6
GRID_W = 64
N_DIR = 2
ML_HEADS = 4
ML_HEAD_DIM = 128
ML_WIDTH = ML_HEADS * ML_HEAD_DIM
SSD_HEADS = 8
SSD_HEAD_DIM = 64
SSD_WIDTH = SSD_HEADS * SSD_HEAD_DIM
SSD_STATE = 64
SSD_GROUPS = 2
SSD_XBC = SSD_WIDTH + 2 * SSD_GROUPS * SSD_STATE
ATT_Q_HEADS = 8
ATT_KV_HEADS = 2
ATT_HEAD_DIM = 64
ATT_WIDTH = ATT_Q_HEADS * ATT_HEAD_DIM
ATT_KV_WIDTH = ATT_KV_HEADS * ATT_HEAD_DIM
WINDOW = 128
ROPE_BASE = 10000.0
MIX_WIDTH = ML_WIDTH + SSD_WIDTH + ATT_WIDTH

LANES = 128
CHUNK = 128
VMEM_LIMIT = 56 * 1024 * 1024

U_ML = 4 * ML_WIDTH
U_SSD = SSD_WIDTH + SSD_XBC
U_ATT = ATT_WIDTH + 2 * ATT_KV_WIDTH
U_GATE = 2 * LANES
U_TOTAL = U_ML + U_SSD + U_ATT + U_GATE
DT_LANE0 = N_DIR * ML_HEADS

ML_SCALE = ML_HEAD_DIM ** -0.5
ATT_SCALE = ATT_HEAD_DIM ** -0.5

_NT = (((1,), (1,)), ((), ()))
_TN = (((0,), (0,)), ((), ()))


def _dot(a, b):
    return jnp.dot(a, b, preferred_element_type=F32)


def _dot_nt(a, b):
    return lax.dot_general(a, b, _NT, preferred_element_type=F32)


def _dot_tn(a, b):
    return lax.dot_general(a, b, _TN, preferred_element_type=F32)


def _split2(x):
    hi = x.astype(BF16)
    lo = (x - hi.astype(F32)).astype(BF16)
    return hi, lo


def _split3(x):
    hi = x.astype(BF16)
    r = x - hi.astype(F32)
    mid = r.astype(BF16)
    lo = (r - mid.astype(F32)).astype(BF16)
    return hi, mid, lo


def _tri_cumsum(tri, x):
    hi, mid, lo = _split3(x)
    return _dot(tri, hi) + _dot(tri, mid) + _dot(tri, lo)


def _expand(x, onehot):
    hi, lo = _split2(x)
    return _dot(hi, onehot) + _dot(lo, onehot)


def _sigmoid(x):
    return 1.0 / (1.0 + jnp.exp(-x))


def _log_sigmoid(x):
    return jnp.minimum(x, 0.0) - jnp.log1p(jnp.exp(-jnp.abs(x)))


def _softplus(x):
    return jnp.maximum(x, 0.0) + jnp.log1p(jnp.exp(-jnp.abs(x)))


def _iota(shape, dim):
    return lax.broadcasted_iota(jnp.int32, shape, dim)


def _chunk_rows(c):
    return pl.ds(pl.multiple_of(c * CHUNK, CHUNK), CHUNK)


def _mod_kernel(s_ref, w_ref, b_ref, o_ref):
    s = s_ref[...]
    s = s * _sigmoid(s)
    w = w_ref[...]
    s_hi, s_lo = _split2(s)
    w_hi, w_lo = _split2(w)
    o_ref[...] = _dot(s_hi, w_hi) + (_dot(s_hi, w_lo) + _dot(s_lo, w_hi)) + b_ref[...]


def _modulation(cc, w_mod, b_mod):
    depth, d, d6 = w_mod.shape
    r = cc.shape[0]
    bn = 1024
    return pl.pallas_call(
        _mod_kernel,
        grid=(depth, d6 // bn),
        in_specs=[
            pl.BlockSpec((r, d), lambda l, j: (0, 0)),
            pl.BlockSpec((None, d, bn), lambda l, j: (l, 0, j)),
            pl.BlockSpec((None, 1, bn), lambda l, j: (l, 0, j)),
        ],
        out_specs=pl.BlockSpec((None, r, bn), lambda l, j: (l, 0, j)),
        out_shape=jax.ShapeDtypeStruct((depth, r, d6), F32),
        compiler_params=pltpu.CompilerParams(
            dimension_semantics=("arbitrary", "arbitrary"), vmem_limit_bytes=VMEM_LIMIT),
        name="modulation",
    )(cc, w_mod, b_mod.reshape(depth, 1, d6))


def _inproj_kernel(x_ref, mod_ref, g_ref, w_ref, oml_ref, ossd_ref, oatt_ref, og_ref):
    x = x_ref[...]
    ms = jnp.mean(x * x, axis=-1, keepdims=True)
    xn = x * lax.rsqrt(ms + EPS)
    h = xn * (g_ref[...] * (1.0 + mod_ref[1:2, :])) + mod_ref[0:1, :]
    hb = h.astype(BF16)
    step = 512
    off = 0
    for o_ref, width in ((oml_ref, U_ML), (ossd_ref, U_SSD), (oatt_ref, U_ATT), (og_ref, U_GATE)):
        for j in range(0, width, step):
            wj = min(step, width - j)
            o_ref[:, j:j + wj] = _dot(hb, w_ref[:, off + j:off + j + wj]).astype(o_ref.dtype)
        off += width


def _in_proj(x, mods, mod_row, g1, w, tm):
    bsz, t, d = x.shape
    widths = (U_ML, U_SSD, U_ATT, U_GATE)
    dtypes = (BF16, BF16, BF16, F32)
    return pl.pallas_call(
        _inproj_kernel,
        grid=(bsz, t // tm),
        in_specs=[
            pl.BlockSpec((None, tm, d), lambda b, i: (b, i, 0)),
            pl.BlockSpec((None, 6, d), lambda b, i: (mod_row(b), 0, 0)),
            pl.BlockSpec((1, d), lambda b, i: (0, 0)),
            pl.BlockSpec((d, U_TOTAL), lambda b, i: (0, 0), pipeline_mode=pl.Buffered(1)),
        ],
        out_specs=[pl.BlockSpec((None, tm, wd), lambda b, i: (b, i, 0)) for wd in widths],
        out_shape=[jax.ShapeDtypeStruct((bsz, t, wd), dt) for wd, dt in zip(widths, dtypes)],
        compiler_params=pltpu.CompilerParams(
            dimension_semantics=("arbitrary", "arbitrary"), vmem_limit_bytes=VMEM_LIMIT),
        name="in_proj",
    )(x, mods, g1, w)


def _post_kernel(x_ref, yml_ref, yssd_ref, yatt_ref, mod_ref, g_ref, wo_ref, wup_ref, wdn_ref, o_ref):
    proj = (_dot(yml_ref[...], wo_ref[0:ML_WIDTH, :])
            + _dot(yssd_ref[...], wo_ref[ML_WIDTH:ML_WIDTH + SSD_WIDTH, :])
            + _dot(yatt_ref[...], wo_ref[ML_WIDTH + SSD_WIDTH:MIX_WIDTH, :]))
    x1 = x_ref[...] + mod_ref[2:3, :] * proj
    ms = jnp.mean(x1 * x1, axis=-1, keepdims=True)
    xn = x1 * lax.rsqrt(ms + EPS)
    hb = (xn * (g_ref[...] * (1.0 + mod_ref[4:5, :])) + mod_ref[3:4, :]).astype(BF16)
    d_ff = wup_ref.shape[1]
    step = 1024
    acc = None
    for j in range(0, d_ff, step):
        up = jnp.maximum(_dot(hb, wup_ref[:, j:j + step]), 0.0)
        part = _dot((up * up).astype(BF16), wdn_ref[j:j + step, :])
        acc = part if acc is None else acc + part
    o_ref[...] = x1 + mod_ref[5:6, :] * acc


def _post(x, yml, yssd, yatt, mods, mod_row, g2, wo, wup, wdn, tm):
    bsz, t, d = x.shape
    d_ff = wup.shape[1]
    const = dict(pipeline_mode=pl.Buffered(1))
    return pl.pallas_call(
        _post_kernel,
        grid=(bsz, t // tm),
        in_specs=[
            pl.BlockSpec((None, tm, d), lambda b, i: (b, i, 0)),
            pl.BlockSpec((None, tm, ML_WIDTH), lambda b, i: (b, i, 0)),
            pl.BlockSpec((None, tm, SSD_WIDTH), lambda b, i: (b, i, 0)),
            pl.BlockSpec((None, tm, ATT_WIDTH), lambda b, i: (b, i, 0)),
            pl.BlockSpec((None, 6, d), lambda b, i: (mod_row(b), 0, 0)),
            pl.BlockSpec((1, d), lambda b, i: (0, 0)),
            pl.BlockSpec((MIX_WIDTH, d), lambda b, i: (0, 0), **const),
            pl.BlockSpec((d, d_ff), lambda b, i: (0, 0), **const),
            pl.BlockSpec((d_ff, d), lambda b, i: (0, 0), **const),
        ],
        out_specs=pl.BlockSpec((None, tm, d), lambda b, i: (b, i, 0)),
        out_shape=jax.ShapeDtypeStruct((bsz, t, d), F32),
        compiler_params=pltpu.CompilerParams(
            dimension_semantics=("arbitrary", "arbitrary"), vmem_limit_bytes=VMEM_LIMIT),
        name="post",
    )(x, yml, yssd, yatt, mods, g2, wo, wup, wdn)


def _ml_update(k, v, a_col, b_end, cmat, nvec, m):
    m_end = jnp.maximum(jnp.max(a_col, axis=0, keepdims=True), m)
    w = jnp.exp(a_col - m_end) * ML_SCALE
    decay = jnp.exp(m - m_end)
    vw = (v.astype(F32) * w).astype(BF16)
    c_new = decay * cmat + _dot_tn(k, vw)
    n_new = decay * nvec + jnp.sum(k.astype(F32) * w, axis=0, keepdims=True)
    return c_new, n_new, b_end + m_end


def _mlstm_kernel(ul_ref, uc_ref, gl_ref, gc_ref, pv_ref, ng_ref, *rest, need_ctx_out, n_lat, n_ctx):
    if need_ctx_out:
        yl_ref, yc_ref, acol, bcol, arow, cbk, nmbk, crun, nmrun = rest
    else:
        yl_ref, acol, bcol, arow, cbk, nmbk, crun, nmrun = rest
        yc_ref = None
    ncl, ncc = n_lat // CHUNK, n_ctx // CHUNK
    nh = ML_HEADS
    ti = _iota((CHUNK, CHUNK), 0)
    si = _iota((CHUNK, CHUNK), 1)
    low = si <= ti
    upp = si >= ti
    tri_low = jnp.where(low, 1.0, 0.0).astype(BF16)
    tri_upp = jnp.where(upp, 1.0, 0.0).astype(BF16)
    fwd_lane = _iota((CHUNK, LANES), 1) < nh
    bias_f = pv_ref[0:1, :]
    bias_i = pv_ref[1:2, :]
    neg_inf = -jnp.inf

    def prep(g_ref, nch, goff):
        def body(c, carry):
            g = g_ref[_chunk_rows(c), :]
            lf = _log_sigmoid(g[:, 0:LANES] + bias_f)
            li = g[:, LANES:2 * LANES] + bias_i
            b = jnp.where(fwd_lane, _tri_cumsum(tri_low, lf), _tri_cumsum(tri_upp, lf))
            a = li - b
            acol[goff + c] = a
            bcol[goff + c] = b
            arow[goff + c] = a.T
            return carry
        lax.fori_loop(0, nch, body, 0)

    prep(gc_ref, ncc, 0)
    prep(gl_ref, ncl, ncc)

    def reset_state():
        crun[...] = jnp.zeros(crun.shape, F32)
        nmrun[...] = jnp.zeros(nmrun.shape, F32)

    def load_state(h):
        return crun[h], nmrun[h, 0:1, :], nmrun[h, 1:2, 0:1]

    def store_state(h, cmat, nvec, m):
        crun[h] = cmat
        nmrun[h, 0:1, :] = nvec
        nmrun[h, 1:2, :] = jnp.broadcast_to(m, (1, LANES))

    def bwd_pass(u_ref, nch, goff, keep):
        def body(i, carry):
            c = nch - 1 - i
            rows = _chunk_rows(c)
            a_t = acol[goff + c]
            b_t = bcol[goff + c]
            for h in range(nh):
                cmat, nvec, m = load_state(h)
                if keep:
                    cbk[goff + c, h] = cmat
                    nmbk[goff + c, h, 0:1, :] = nvec
                    nmbk[goff + c, h, 1:2, :] = jnp.broadcast_to(m, (1, LANES))
                k = u_ref[rows, ML_WIDTH + h * LANES:ML_WIDTH + (h + 1) * LANES]
                v = u_ref[rows, 2 * ML_WIDTH + h * LANES:2 * ML_WIDTH + (h + 1) * LANES]
                j = nh + h
                store_state(h, *_ml_update(k, v, a_t[:, j:j + 1], b_t[0:1, j:j + 1], cmat, nvec, m))
            return carry
        lax.fori_loop(0, nch, body, 0)

    def fwd_pass(u_ref, y_ref, nch, goff):
        def body(c, carry):
            rows = _chunk_rows(c)
            a_t = acol[goff + c]
            b_t = bcol[goff + c]
            a_r = arow[goff + c]
            for h in range(nh):
                k = u_ref[rows, ML_WIDTH + h * LANES:ML_WIDTH + (h + 1) * LANES]
                v = u_ref[rows, 2 * ML_WIDTH + h * LANES:2 * ML_WIDTH + (h + 1) * LANES]
                c_f, n_f, m_f = load_state(h)
                if y_ref is not None:
                    q = u_ref[rows, h * LANES:(h + 1) * LANES]
                    o = u_ref[rows, 3 * ML_WIDTH + h * LANES:3 * ML_WIDTH + (h + 1) * LANES]
                    c_b = cbk[goff + c, h]
                    n_b = nmbk[goff + c, h, 0:1, :]
                    m_b = nmbk[goff + c, h, 1:2, 0:1]
                    sqk = _dot_nt(q, k) * ML_SCALE
                    qf = q.astype(F32)
                    qc = _dot(q, jnp.concatenate([c_f, c_b], axis=1).astype(BF16))
                    s_dirs, m_dirs = [], []
                    for mask, jd, m_prev in ((low, h, m_f), (upp, nh + h, m_b)):
                        a_mat = jnp.where(mask, a_r[jd:jd + 1, :], neg_inf)
                        m_t = jnp.maximum(jnp.max(a_mat, axis=1, keepdims=True), m_prev)
                        s_dirs.append(sqk * jnp.exp(a_mat - m_t))
                        m_dirs.append(m_t)
                    sv = _dot(jnp.concatenate(s_dirs, axis=0).astype(BF16), v)
                    hsum = None
                    for d, (jd, m_prev, nvec) in enumerate(((h, m_f, n_f), (nh + h, m_b, n_b))):
                        g = jnp.exp(m_prev - m_dirs[d])
                        num = sv[d * CHUNK:(d + 1) * CHUNK, :] + g * qc[:, d * LANES:(d + 1) * LANES]
                        den = (jnp.sum(s_dirs[d], axis=1, keepdims=True)
                               + g * jnp.sum(qf * nvec, axis=1, keepdims=True))
                        floor = jnp.exp(-(b_t[:, jd:jd + 1] + m_dirs[d]))
                        hd = num / jnp.maximum(jnp.abs(den), floor)
                        hsum = hd if hsum is None else hsum + hd
                    ms = jnp.mean(hsum * hsum, axis=1, keepdims=True)
                    y = hsum * lax.rsqrt(ms + EPS) * ng_ref[:, h * LANES:(h + 1) * LANES]
                    y = y * _sigmoid(o.astype(F32))
                    y_ref[rows, h * LANES:(h + 1) * LANES] = y.astype(y_ref.dtype)
                store_state(h, *_ml_update(k, v, a_t[:, h:h + 1], b_t[CHUNK - 1:CHUNK, h:h + 1],
                                           c_f, n_f, m_f))
            return carry
        lax.fori_loop(0, nch, body, 0)

    reset_state()
    bwd_pass(uc_ref, ncc, 0, need_ctx_out)
    bwd_pass(ul_ref, ncl, ncc, True)
    reset_state()
    fwd_pass(uc_ref, yc_ref, ncc, 0)
    fwd_pass(ul_ref, yl_ref, ncl, ncc)


def _mlstm_mixer(uml_l, uml_c, g_l, g_c, pvec, norm_g, need_ctx_out):
    bsz, n_lat, _ = uml_l.shape
    n_ctx = uml_c.shape[1]
    nct = (n_lat + n_ctx) // CHUNK
    kern = functools.partial(_mlstm_kernel, need_ctx_out=need_ctx_out, n_lat=n_lat, n_ctx=n_ctx)
    out_specs = [pl.BlockSpec((None, n_lat, ML_WIDTH), lambda b: (b, 0, 0))]
    out_shape = [jax.ShapeDtypeStruct((bsz, n_lat, ML_WIDTH), BF16)]
    if need_ctx_out:
        out_specs.append(pl.BlockSpec((None, n_ctx, ML_WIDTH), lambda b: (b, 0, 0)))
        out_shape.append(jax.ShapeDtypeStruct((bsz, n_ctx, ML_WIDTH), BF16))
    outs = pl.pallas_call(
        kern,
        grid=(bsz,),
        in_specs=[
            pl.BlockSpec((None, n_lat, U_ML), lambda b: (b, 0, 0)),
            pl.BlockSpec((None, n_ctx, U_ML), lambda b: (b, 0, 0)),
            pl.BlockSpec((None, n_lat, U_GATE), lambda b: (b, 0, 0)),
            pl.BlockSpec((None, n_ctx, U_GATE), lambda b: (b, 0, 0)),
            pl.BlockSpec((8, LANES), lambda b: (0, 0)),
            pl.BlockSpec((1, ML_WIDTH), lambda b: (0, 0)),
        ],
        out_specs=out_specs,
        out_shape=out_shape,
        scratch_shapes=[
            pltpu.VMEM((nct, CHUNK, LANES), F32),
            pltpu.VMEM((nct, CHUNK, LANES), F32),
            pltpu.VMEM((nct, LANES, CHUNK), F32),
            pltpu.VMEM((nct, ML_HEADS, ML_HEAD_DIM, ML_HEAD_DIM), F32),
            pltpu.VMEM((nct, ML_HEADS, 8, LANES), F32),
            pltpu.VMEM((ML_HEADS, ML_HEAD_DIM, ML_HEAD_DIM), F32),
            pltpu.VMEM((ML_HEADS, 8, LANES), F32),
        ],
        compiler_params=pltpu.CompilerParams(
            dimension_semantics=("arbitrary",), vmem_limit_bytes=VMEM_LIMIT),
        name="mlstm",
    )(uml_l, uml_c, g_l, g_c, pvec, norm_g)
    return (outs[0], outs[1]) if need_ctx_out else (outs[0], None)


def _ssd_kernel(ul_ref, uc_ref, gl_ref, gc_ref, pv_ref, cw_ref, cb_ref, dsk_ref, ng_ref, ex_ref, *rest,
                need_ctx_out, n_lat, n_ctx):
    if need_ctx_out:
        yl_ref, yc_ref, xact, acum, wsc, act, dtt, dec, hbk, hrun = rest
    else:
        yl_ref, xact, acum, wsc, act, dtt, dec, hbk, hrun = rest
        yc_ref = None
    ncl, ncc = n_lat // CHUNK, n_ctx // CHUNK
    nh, hg = SSD_HEADS, SSD_HEADS // SSD_GROUPS
    gw = hg * SSD_HEAD_DIM
    gn = SSD_GROUPS * SSD_STATE
    ti = _iota((CHUNK, CHUNK), 0)
    si = _iota((CHUNK, CHUNK), 1)
    low = si <= ti
    upp = si >= ti
    tri_low = jnp.where(low, 1.0, 0.0).astype(BF16)
    tri_upp = jnp.where(upp, 1.0, 0.0).astype(BF16)
    lane = _iota((CHUNK, LANES), 1)
    fwd_lane = (lane >= DT_LANE0) & (lane < DT_LANE0 + nh)
    bias0 = pv_ref[0:1, :]
    a_row = -jnp.exp(pv_ref[2:3, :])
    row_id = _iota((CHUNK, SSD_XBC), 0)
    neg_inf = -jnp.inf
    block_mask = jnp.where((_iota((gn, SSD_WIDTH), 0) < SSD_STATE) == (_iota((gn, SSD_WIDTH), 1) < gw),
                           1.0, 0.0)
    lane_group0 = lane < SSD_STATE

    def prep(u_ref, g_ref, nch, goff, nrows):
        def body(c, carry):
            r = pl.multiple_of(c * CHUNK, CHUNK)
            cur = u_ref[pl.ds(r, CHUNK), SSD_WIDTH:U_SSD].astype(F32)
            rp = pl.multiple_of(jnp.maximum(r - 16, 0), 16)
            rn = pl.multiple_of(jnp.minimum(r + CHUNK, nrows - 16), 16)
            prev = u_ref[pl.ds(rp, 16), SSD_WIDTH:U_SSD].astype(F32)[15:16, :]
            nxt = u_ref[pl.ds(rn, 16), SSD_WIDTH:U_SSD].astype(F32)[0:1, :]
            prev = jnp.where(c > 0, prev, 0.0)
            nxt = jnp.where(c < nch - 1, nxt, 0.0)
            xm = jnp.where(row_id == 0, prev, pltpu.roll(cur, 1, 0))
            xp = jnp.where(row_id == CHUNK - 1, nxt, pltpu.roll(cur, CHUNK - 1, 0))
            conv = cw_ref[0:1, :] * xm + cw_ref[1:2, :] * cur + cw_ref[2:3, :] * xp + cb_ref[...]
            xact[goff + c] = (conv * _sigmoid(conv)).astype(BF16)

            dt = _softplus(g_ref[pl.ds(r, CHUNK), :] + bias0)
            la = dt * a_row
            ac = jnp.where(fwd_lane, _tri_cumsum(tri_low, la), _tri_cumsum(tri_upp, la))
            a_end = jnp.where(fwd_lane[0:1, :], ac[CHUNK - 1:CHUNK, :], ac[0:1, :])
            acum[goff + c] = ac
            wsc[goff + c] = jnp.exp(a_end - ac) * dt
            act[goff + c] = ac.T
            dtt[goff + c] = dt.T
            dec[goff + c] = jnp.broadcast_to(jnp.exp(a_end), (8, LANES))
            return carry
        lax.fori_loop(0, nch, body, 0)

    prep(uc_ref, gc_ref, ncc, 0, n_ctx)
    prep(ul_ref, gl_ref, ncl, ncc, n_lat)

    def update_state(d, cg):
        onehot = ex_ref[:, d * SSD_WIDTH:(d + 1) * SSD_WIDTH]
        xa = xact[cg]
        wx = (_expand(wsc[cg], onehot) * xa[:, 0:SSD_WIDTH].astype(F32)).astype(BF16)
        upd = _dot_tn(xa[:, SSD_WIDTH:SSD_WIDTH + gn], wx)
        decay = _expand(dec[cg], onehot)[0:1, :]
        hrun[d] = (decay * hrun[d] + upd) * block_mask

    def bwd_pass(nch, goff, keep):
        def body(i, carry):
            cg = goff + nch - 1 - i
            if keep:
                hbk[cg] = hrun[1]
            update_state(1, cg)
            return carry
        lax.fori_loop(0, nch, body, 0)

    def fwd_pass(u_ref, y_ref, nch, goff):
        def body(c, carry):
            cg = goff + c
            if y_ref is not None:
                rows = _chunk_rows(c)
                xa = xact[cg]
                x = xa[:, 0:SSD_WIDTH]
                b_all = xa[:, SSD_WIDTH:SSD_WIDTH + gn]
                c_all = xa[:, SSD_WIDTH + gn:SSD_XBC]
                ac = acum[cg]
                ac_t = act[cg]
                dt_t = dtt[cg]
                lane_w = _iota((CHUNK, gw), 1)
                y_groups = []
                for g in range(SSD_GROUPS):
                    c_g = jnp.where(lane_group0 == (g == 0), c_all, jnp.zeros_like(c_all))
                    cb = _dot_nt(c_g, b_all)
                    s_heads = []
                    for hh in range(hg):
                        jf = DT_LANE0 + g * hg + hh
                        jb = jf + nh
                        ef = jnp.exp(jnp.where(low, ac[:, jf:jf + 1] - ac_t[jf:jf + 1, :], neg_inf))
                        eb = jnp.exp(jnp.where(upp, ac[:, jb:jb + 1] - ac_t[jb:jb + 1, :], neg_inf))
                        s_heads.append((cb * (ef * dt_t[jf:jf + 1, :] + eb * dt_t[jb:jb + 1, :])).astype(BF16))
                    x_g = x[:, g * gw:(g + 1) * gw]
                    bd = jnp.concatenate(
                        [jnp.where((lane_w >= hh * SSD_HEAD_DIM) & (lane_w < (hh + 1) * SSD_HEAD_DIM),
                                   x_g, jnp.zeros_like(x_g)) for hh in range(hg)], axis=0)
                    y_groups.append(_dot(jnp.concatenate(s_heads, axis=1), bd))
                y = jnp.concatenate(y_groups, axis=1)
                e = jnp.exp(ac)
                y = y + _dot(c_all, hrun[0].astype(BF16)) * _expand(e, ex_ref[:, 0:SSD_WIDTH])
                y = y + _dot(c_all, hbk[cg].astype(BF16)) * _expand(e, ex_ref[:, SSD_WIDTH:2 * SSD_WIDTH])
                y = y + dsk_ref[...] * x.astype(F32)
                z = u_ref[rows, 0:SSD_WIDTH].astype(F32)
                yz = y * (z * _sigmoid(z))
                ms = jnp.mean(yz * yz, axis=1, keepdims=True)
                y_ref[rows, :] = (yz * lax.rsqrt(ms + EPS) * ng_ref[...]).astype(y_ref.dtype)
            update_state(0, cg)
            return carry
        lax.fori_loop(0, nch, body, 0)

    hrun[...] = jnp.zeros(hrun.shape, F32)
    bwd_pass(ncc, 0, need_ctx_out)
    bwd_pass(ncl, ncc, True)
    fwd_pass(uc_ref, yc_ref, ncc, 0)
    fwd_pass(ul_ref, yl_ref, ncl, ncc)


def _ssd_mixer(ussd_l, ussd_c, g_l, g_c, pvec, conv_w, conv_b, d_skip, norm_g, onehot, need_ctx_out):
    bsz, n_lat, _ = ussd_l.shape
    n_ctx = ussd_c.shape[1]
    nct = (n_lat + n_ctx) // CHUNK
    gn = SSD_GROUPS * SSD_STATE
    kern = functools.partial(_ssd_kernel, need_ctx_out=need_ctx_out, n_lat=n_lat, n_ctx=n_ctx)
    out_specs = [pl.BlockSpec((None, n_lat, SSD_WIDTH), lambda b: (b, 0, 0))]
    out_shape = [jax.ShapeDtypeStruct((bsz, n_lat, SSD_WIDTH), BF16)]
    if need_ctx_out:
        out_specs.append(pl.BlockSpec((None, n_ctx, SSD_WIDTH), lambda b: (b, 0, 0)))
        out_shape.append(jax.ShapeDtypeStruct((bsz, n_ctx, SSD_WIDTH), BF16))
    outs = pl.pallas_call(
        kern,
        grid=(bsz,),
        in_specs=[
            pl.BlockSpec((None, n_lat, U_SSD), lambda b: (b, 0, 0)),
            pl.BlockSpec((None, n_ctx, U_SSD), lambda b: (b, 0, 0)),
            pl.BlockSpec((None, n_lat, LANES), lambda b: (b, 0, 0)),
            pl.BlockSpec((None, n_ctx, LANES), lambda b: (b, 0, 0)),
            pl.BlockSpec((8, LANES), lambda b: (0, 0)),
            pl.BlockSpec((3, SSD_XBC), lambda b: (0, 0)),
            pl.BlockSpec((1, SSD_XBC), lambda b: (0, 0)),
            pl.BlockSpec((1, SSD_WIDTH), lambda b: (0, 0)),
            pl.BlockSpec((1, SSD_WIDTH), lambda b: (0, 0)),
            pl.BlockSpec((LANES, 2 * SSD_WIDTH), lambda b: (0, 0)),
        ],
        out_specs=out_specs,
        out_shape=out_shape,
        scratch_shapes=[
            pltpu.VMEM((nct, CHUNK, SSD_XBC), BF16),
            pltpu.VMEM((nct, CHUNK, LANES), F32),
            pltpu.VMEM((nct, CHUNK, LANES), F32),
            pltpu.VMEM((nct, LANES, CHUNK), F32),
            pltpu.VMEM((nct, LANES, CHUNK), F32),
            pltpu.VMEM((nct, 8, LANES), F32),
            pltpu.VMEM((nct, gn, SSD_WIDTH), F32),
            pltpu.VMEM((N_DIR, gn, SSD_WIDTH), F32),
        ],
        compiler_params=pltpu.CompilerParams(
            dimension_semantics=("arbitrary",), vmem_limit_bytes=VMEM_LIMIT),
        name="ssd",
    )(ussd_l, ussd_c, g_l, g_c, pvec, conv_w, conv_b, d_skip, norm_g, onehot)
    return (outs[0], outs[1]) if need_ctx_out else (outs[0], None)


def _head_norm(x, gain):
    lane_lo = _iota((x.shape[0], LANES), 1) < ATT_HEAD_DIM
    outs = []
    for j in range(x.shape[1] // LANES):
        blk = x[:, j * LANES:(j + 1) * LANES]
        sq = blk * blk
        s0 = jnp.sum(jnp.where(lane_lo, sq, 0.0), axis=1, keepdims=True)
        s1 = jnp.sum(jnp.where(lane_lo, 0.0, sq), axis=1, keepdims=True)
        ms = jnp.where(lane_lo, s0, s1) * (1.0 / ATT_HEAD_DIM)
        outs.append(blk * lax.rsqrt(ms + EPS) * gain)
    return outs[0] if len(outs) == 1 else jnp.concatenate(outs, axis=1)


def _rope(x, cos_t, sin_t):
    w = x.shape[1]
    reps = w // LANES
    nfreq = ATT_HEAD_DIM // 4
    partner = jnp.where((_iota(x.shape, 1) & (2 * nfreq - 1)) < nfreq,
                        pltpu.roll(x, w - nfreq, 1), pltpu.roll(x, nfreq, 1))
    if reps > 1:
        cos_t = jnp.concatenate([cos_t] * reps, axis=1)
        sin_t = jnp.concatenate([sin_t] * reps, axis=1)
    return x * cos_t + partner * sin_t


def _dup_heads(x):
    lane_lo = _iota(x.shape, 1) < ATT_HEAD_DIM
    sw = pltpu.roll(x, ATT_HEAD_DIM, 1)
    return jnp.concatenate([jnp.where(lane_lo, x, sw), jnp.where(lane_lo, sw, x)], axis=1)


def _attn_kernel(ul_ref, uc_ref, cos_ref, sin_ref, qg_ref, kg_ref, sink_ref, *rest,
                 need_ctx_out, n_lat, n_ctx):
    if need_ctx_out:
        yl_ref, yc_ref, qp, kb, vb, kc, vc, qcp = rest
    else:
        yl_ref, qp, kb, vb, kc, vc = rest
        yc_ref = qcp = None
    ncl, ncc = n_lat // CHUNK, n_ctx // CHUNK
    rep = ATT_Q_HEADS // ATT_KV_HEADS
    kv0 = ATT_WIDTH
    lane_lo = _iota((CHUNK, LANES), 1) < ATT_HEAD_DIM
    neg_inf = -jnp.inf
    qg = qg_ref[...]
    kg = kg_ref[...]

    zeros_blk = jnp.zeros((CHUNK, 2 * LANES), BF16)
    kb[0:CHUNK, :] = zeros_blk
    vb[0:CHUNK, :] = zeros_blk
    kb[CHUNK + n_lat:2 * CHUNK + n_lat, :] = zeros_blk
    vb[CHUNK + n_lat:2 * CHUNK + n_lat, :] = zeros_blk

    def prep_lat(c, carry):
        rows = _chunk_rows(c)
        cos_t = cos_ref[rows, :]
        sin_t = sin_ref[rows, :]
        q = _rope(_head_norm(ul_ref[rows, 0:ATT_WIDTH].astype(F32), qg), cos_t, sin_t)
        qp[rows, :] = (q * ATT_SCALE).astype(BF16)
        k = _rope(_head_norm(ul_ref[rows, kv0:kv0 + LANES].astype(F32), kg), cos_t, sin_t)
        v = ul_ref[rows, kv0 + LANES:kv0 + 2 * LANES].astype(F32)
        dst = pl.ds(pl.multiple_of((c + 1) * CHUNK, CHUNK), CHUNK)
        kb[dst, :] = _dup_heads(k).astype(BF16)
        vb[dst, :] = _dup_heads(v).astype(BF16)
        return carry
    lax.fori_loop(0, ncl, prep_lat, 0)

    def prep_ctx(c, carry):
        rows = _chunk_rows(c)
        k = _head_norm(uc_ref[rows, kv0:kv0 + LANES].astype(F32), kg)
        v = uc_ref[rows, kv0 + LANES:kv0 + 2 * LANES].astype(F32)
        kc[rows, :] = _dup_heads(k).astype(BF16)
        vc[rows, :] = _dup_heads(v).astype(BF16)
        if need_ctx_out:
            q = _head_norm(uc_ref[rows, 0:ATT_WIDTH].astype(F32), qg)
            qcp[rows, :] = (q * ATT_SCALE).astype(BF16)
        return carry
    lax.fori_loop(0, ncc, prep_ctx, 0)

    def attend(q_ref, y_ref, rows, band):
        halves = []
        for hq in range(ATT_Q_HEADS):
            g, jb, upper = hq // rep, hq // 2, hq % 2 == 1
            qb = q_ref[rows, jb * LANES:(jb + 1) * LANES]
            qm = jnp.where(lane_lo != upper, qb, jnp.zeros_like(qb))
            gl = slice(g * LANES, (g + 1) * LANES)
            sk = sink_ref[hq]
            s_c = _dot_nt(qm, kc[:, gl])
            m = jnp.maximum(jnp.max(s_c, axis=1, keepdims=True), sk)
            if band is not None:
                band_rows, valid = band
                s_b = jnp.where(valid, _dot_nt(qm, kb[band_rows, gl]), neg_inf)
                m = jnp.maximum(m, jnp.max(s_b, axis=1, keepdims=True))
            p_c = jnp.exp(s_c - m)
            den = jnp.sum(p_c, axis=1, keepdims=True) + jnp.exp(sk - m)
            acc = _dot(p_c.astype(BF16), vc[:, gl])
            if band is not None:
                p_b = jnp.exp(s_b - m)
                den = den + jnp.sum(p_b, axis=1, keepdims=True)
                acc = acc + _dot(p_b.astype(BF16), vb[band_rows, gl])
            halves.append(acc / den)
            if upper:
                y_ref[rows, jb * LANES:(jb + 1) * LANES] = jnp.where(
                    lane_lo, halves[-2], halves[-1]).astype(y_ref.dtype)

    qq = _iota((CHUNK, 3 * CHUNK), 0)
    kk = _iota((CHUNK, 3 * CHUNK), 1)
    in_window = (kk >= qq) & (kk <= qq + 2 * WINDOW)

    def lat_block(i, carry):
        kpos = kk + (i - 1) * CHUNK
        valid = in_window & (kpos >= 0) & (kpos < n_lat)
        band_rows = pl.ds(pl.multiple_of(i * CHUNK, CHUNK), 3 * CHUNK)
        attend(qp, yl_ref, _chunk_rows(i), (band_rows, valid))
        return carry
    lax.fori_loop(0, ncl, lat_block, 0)

    if need_ctx_out:
        def ctx_block(i, carry):
            attend(qcp, yc_ref, _chunk_rows(i), None)
            return carry
        lax.fori_loop(0, ncc, ctx_block, 0)


def _attn_mixer(uatt_l, uatt_c, cos_t, sin_t, qgain, kgain, sink, need_ctx_out):
    bsz, n_lat, _ = uatt_l.shape
    n_ctx = uatt_c.shape[1]
    kern = functools.partial(_attn_kernel, need_ctx_out=need_ctx_out, n_lat=n_lat, n_ctx=n_ctx)
    out_specs = [pl.BlockSpec((None, n_lat, ATT_WIDTH), lambda b: (b, 0, 0))]
    out_shape = [jax.ShapeDtypeStruct((bsz, n_lat, ATT_WIDTH), BF16)]
    scratch = [
        pltpu.VMEM((n_lat, ATT_WIDTH), BF16),
        pltpu.VMEM((n_lat + 2 * CHUNK, 2 * LANES), BF16),
        pltpu.VMEM((n_lat + 2 * CHUNK, 2 * LANES), BF16),
        pltpu.VMEM((n_ctx, 2 * LANES), BF16),
        pltpu.VMEM((n_ctx, 2 * LANES), BF16),
    ]
    if need_ctx_out:
        out_specs.append(pl.BlockSpec((None, n_ctx, ATT_WIDTH), lambda b: (b, 0, 0)))
        out_shape.append(jax.ShapeDtypeStruct((bsz, n_ctx, ATT_WIDTH), BF16))
        scratch.append(pltpu.VMEM((n_ctx, ATT_WIDTH), BF16))
    outs = pl.pallas_call(
        kern,
        grid=(bsz,),
        in_specs=[
            pl.BlockSpec((None, n_lat, U_ATT), lambda b: (b, 0, 0)),
            pl.BlockSpec((None, n_ctx, U_ATT), lambda b: (b, 0, 0)),
            pl.BlockSpec((n_lat, LANES), lambda b: (0, 0)),
            pl.BlockSpec((n_lat, LANES), lambda b: (0, 0)),
            pl.BlockSpec((1, LANES), lambda b: (0, 0)),
            pl.BlockSpec((1, LANES), lambda b: (0, 0)),
            pl.BlockSpec(memory_space=pltpu.SMEM),
        ],
        out_specs=out_specs,
        out_shape=out_shape,
        scratch_shapes=scratch,
        compiler_params=pltpu.CompilerParams(
            dimension_semantics=("arbitrary",), vmem_limit_bytes=VMEM_LIMIT),
        name="attention",
    )(uatt_l, uatt_c, cos_t, sin_t, qgain, kgain, sink)
    return (outs[0], outs[1]) if need_ctx_out else (outs[0], None)


def _in_layout():
    names = (('ml_q', ML_WIDTH), ('ml_k', ML_WIDTH), ('ml_v', ML_WIDTH), ('ml_o', ML_WIDTH),
             ('ml_i', N_DIR * ML_HEADS), ('ml_f', N_DIR * ML_HEADS),
             ('ssd_z', SSD_WIDTH), ('ssd_xbc', SSD_XBC), ('ssd_dt', N_DIR * SSD_HEADS),
             ('att_q', ATT_WIDTH), ('att_k', ATT_KV_WIDTH), ('att_v', ATT_KV_WIDTH))
    out, off = {}, 0
    for name, width in names:
        out[name] = (off, off + width)
        off += width
    return out


def _permute_cols(a):
    lay = _in_layout()
    take = lambda name: a[..., lay[name][0]:lay[name][1]]
    zeros = lambda n: jnp.zeros(a.shape[:-1] + (n,), a.dtype)
    n_f, n_dt, n_i = N_DIR * ML_HEADS, N_DIR * SSD_HEADS, N_DIR * ML_HEADS
    return jnp.concatenate(
        [take('ml_q'), take('ml_k'), take('ml_v'), take('ml_o'),
         take('ssd_z'), take('ssd_xbc'),
         take('att_q'), take('att_k'), take('att_v'),
         take('ml_f'), take('ssd_dt'), zeros(LANES - n_f - n_dt),
         take('ml_i'), zeros(LANES - n_i)], axis=-1)


def _gate_rows(ml_b_i, ml_b_f, ssd_dt_bias, ssd_a_log):
    n_f, n_dt = N_DIR * ML_HEADS, N_DIR * SSD_HEADS
    row0 = jnp.concatenate([ml_b_f.reshape(-1), ssd_dt_bias.reshape(-1), jnp.zeros((LANES - n_f - n_dt,), F32)])
    row1 = jnp.concatenate([ml_b_i.reshape(-1), jnp.zeros((LANES - n_f,), F32)])
    row2 = jnp.concatenate([jnp.zeros((n_f,), F32), ssd_a_log.reshape(-1), jnp.zeros((LANES - n_f - n_dt,), F32)])
    return jnp.concatenate([row0[None], row1[None], row2[None], jnp.zeros((5, LANES), F32)], axis=0)


def _ssd_onehot():
    m = np.zeros((LANES, N_DIR * SSD_WIDTH), np.float32)
    for d in range(N_DIR):
        for h in range(SSD_HEADS):
            c0 = d * SSD_WIDTH + h * SSD_HEAD_DIM
            m[DT_LANE0 + d * SSD_HEADS + h, c0:c0 + SSD_HEAD_DIM] = 1.0
    return jnp.asarray(m, BF16)


def _rope_tables(n_lat):
    nfreq = ATT_HEAD_DIM // 4
    pos = np.arange(n_lat)
    rows, cols = pos // GRID_W, pos % GRID_W
    inv = jnp.asarray(ROPE_BASE, F32) ** (-jnp.arange(nfreq, dtype=F32) / nfreq)
    ang_r = jnp.asarray(rows, F32)[:, None] * inv
    ang_c = jnp.asarray(cols, F32)[:, None] * inv
    cos_h = jnp.concatenate([jnp.cos(ang_r)] * 2 + [jnp.cos(ang_c)] * 2, axis=1)
    sin_h = jnp.concatenate([-jnp.sin(ang_r), jnp.sin(ang_r), -jnp.sin(ang_c), jnp.sin(ang_c)], axis=1)
    return jnp.concatenate([cos_h] * 2, axis=1), jnp.concatenate([sin_h] * 2, axis=1)


def _token_mixers(u_l, u_c, p, layer, tables, need_ctx_out):
    uml_l, ussd_l, uatt_l, g_l = u_l
    uml_c, ussd_c, uatt_c, g_c = u_c
    pvec = _gate_rows(p['ml_b_i'][layer], p['ml_b_f'][layer], p['ssd_dt_bias'][layer], p['ssd_a_log'][layer])
    m_l, m_c = _mlstm_mixer(uml_l, uml_c, g_l, g_c, pvec, p['ml_norm_g'][layer][None], need_ctx_out)
    s_l, s_c = _ssd_mixer(ussd_l, ussd_c, g_l, g_c, pvec, p['ssd_conv_w'][layer], p['ssd_conv_b'][layer][None],
                          jnp.repeat(p['ssd_d'][layer], SSD_HEAD_DIM)[None], p['ssd_norm_g'][layer][None],
                          tables['onehot'], need_ctx_out)
    a_l, a_c = _attn_mixer(uatt_l, uatt_c, tables['cos'], tables['sin'],
                           jnp.tile(p['att_qn_g'][layer], 2)[None], jnp.tile(p['att_kn_g'][layer], 2)[None],
                           p['att_sink'][layer], need_ctx_out)
    return (m_l, s_l, a_l), (m_c, s_c, a_c)


def kernel(x, c, ctx, c_ctx, w_mod, b_mod, norm1_g, w_in, ml_b_i, ml_b_f, ml_norm_g, ssd_conv_w, ssd_conv_b,
           ssd_a_log, ssd_dt_bias, ssd_d, ssd_norm_g, att_qn_g, att_kn_g, att_sink, w_out, norm2_g, w_up, w_down):
    p = dict(ml_b_i=ml_b_i, ml_b_f=ml_b_f, ml_norm_g=ml_norm_g, ssd_conv_w=ssd_conv_w, ssd_conv_b=ssd_conv_b,
             ssd_a_log=ssd_a_log, ssd_dt_bias=ssd_dt_bias, ssd_d=ssd_d, ssd_norm_g=ssd_norm_g,
             att_qn_g=att_qn_g, att_kn_g=att_kn_g, att_sink=att_sink)
    bsz, n_lat, d = x.shape
    n_ctx = ctx.shape[1]
    depth = w_mod.shape[0]
    mod_rows = 16
    cc = jnp.concatenate([c, c_ctx[None], jnp.zeros((mod_rows - bsz - 1, d), F32)], axis=0)
    mods = _modulation(cc, w_mod, b_mod).reshape(depth, mod_rows, 6, d)
    cos_t, sin_t = _rope_tables(n_lat)
    tables = dict(cos=cos_t, sin=sin_t, onehot=_ssd_onehot())
    lat_row = lambda b: b
    ctx_row = lambda b: bsz
    h_ctx = ctx
    for layer in range(depth):
        need_ctx_out = layer < depth - 1
        w = _permute_cols(w_in[layer]).astype(BF16)
        g1 = norm1_g[layer][None]
        u_l = _in_proj(x, mods[layer], lat_row, g1, w, tm=256)
        u_c = _in_proj(h_ctx, mods[layer], ctx_row, g1, w, tm=256)
        y_l, y_c = _token_mixers(u_l, u_c, p, layer, tables, need_ctx_out)
        wo, wup, wdn = w_out[layer].astype(BF16), w_up[layer].astype(BF16), w_down[layer].astype(BF16)
        g2 = norm2_g[layer][None]
        x = _post(x, *y_l, mods[layer], lat_row, g2, wo, wup, wdn, tm=512)
        if need_ctx_out:
            h_ctx = _post(h_ctx, *y_c, mods[layer], ctx_row, g2, wo, wup, wdn, tm=256)
    return x
```

```python
import functools
import math

import numpy as np
import jax
import jax.numpy as jnp
from jax import lax
from jax.experimental import pallas as pl
from jax.experimental.pallas import tpu as pltpu

F32 = jnp.float32
BF16 = jnp.bfloat16

EPS = 1e-6
GRID_W = 64
N_DIR = 2
ML_HEADS = 4
ML_HEAD_DIM = 128
ML_WIDTH = ML_HEADS * ML_HEAD_DIM
SSD_HEADS = 8
SSD_HEAD_DIM = 64
SSD_WIDTH = SSD_HEADS * SSD_HEAD_DIM
SSD_STATE = 64
SSD_GROUPS = 2
SSD_XBC = SSD_WIDTH + 2 * SSD_GROUPS * SSD_STATE
ATT_Q_HEADS = 8
ATT_KV_HEADS = 2
ATT_HEAD_DIM = 64
ATT_WIDTH = ATT_Q_HEADS * ATT_HEAD_DIM
ATT_KV_WIDTH = ATT_KV_HEADS * ATT_HEAD_DIM
WINDOW = 128
ROPE_BASE = 10000.0
MIX_WIDTH = ML_WIDTH + SSD_WIDTH + ATT_WIDTH

LANES = 128
CHUNK = 128
VMEM_LIMIT = 56 * 1024 * 1024

U_ML = 4 * ML_WIDTH
U_SSD = SSD_WIDTH + SSD_XBC
U_ATT = ATT_WIDTH + 2 * ATT_KV_WIDTH
U_GATE = 2 * LANES
U_TOTAL = U_ML + U_SSD + U_ATT + U_GATE
DT_LANE0 = N_DIR * ML_HEADS

ML_SCALE = ML_HEAD_DIM ** -0.5
ATT_SCALE = ATT_HEAD_DIM ** -0.5

_NT = (((1,), (1,)), ((), ()))
_TN = (((0,), (0,)), ((), ()))


def _dot(a, b):
    return jnp.dot(a, b, preferred_element_type=F32)


def _dot_nt(a, b):
    return lax.dot_general(a, b, _NT, preferred_element_type=F32)


def _dot_tn(a, b):
    return lax.dot_general(a, b, _TN, preferred_element_type=F32)


def _split2(x):
    hi = x.astype(BF16)
    lo = (x - hi.astype(F32)).astype(BF16)
    return hi, lo


def _split3(x):
    hi = x.astype(BF16)
    r = x - hi.astype(F32)
    mid = r.astype(BF16)
    lo = (r - mid.astype(F32)).astype(BF16)
    return hi, mid, lo


def _tri_cumsum(tri, x):
    hi, mid, lo = _split3(x)
    return _dot(tri, hi) + _dot(tri, mid) + _dot(tri, lo)


def _expand(x, onehot):
    hi, lo = _split2(x)
    return _dot(hi, onehot) + _dot(lo, onehot)


def _sigmoid(x):
    return 1.0 / (1.0 + jnp.exp(-x))


def _log_sigmoid(x):
    return jnp.minimum(x, 0.0) - jnp.log1p(jnp.exp(-jnp.abs(x)))


def _softplus(x):
    return jnp.maximum(x, 0.0) + jnp.log1p(jnp.exp(-jnp.abs(x)))


def _iota(shape, dim):
    return lax.broadcasted_iota(jnp.int32, shape, dim)


def _chunk_rows(c):
    return pl.ds(pl.multiple_of(c * CHUNK, CHUNK), CHUNK)


def _mod_kernel(s_ref, w_ref, b_ref, o_ref):
    s = s_ref[...]
    s = s * _sigmoid(s)
    w = w_ref[...]
    s_hi, s_lo = _split2(s)
    w_hi, w_lo = _split2(w)
    o_ref[...] = _dot(s_hi, w_hi) + (_dot(s_hi, w_lo) + _dot(s_lo, w_hi)) + b_ref[...]


def _modulation(cc, w_mod, b_mod):
    depth, d, d6 = w_mod.shape
    r = cc.shape[0]
    bn = 1024
    return pl.pallas_call(
        _mod_kernel,
        grid=(depth, d6 // bn),
        in_specs=[
            pl.BlockSpec((r, d), lambda l, j: (0, 0)),
            pl.BlockSpec((None, d, bn), lambda l, j: (l, 0, j)),
            pl.BlockSpec((None, 1, bn), lambda l, j: (l, 0, j)),
        ],
        out_specs=pl.BlockSpec((None, r, bn), lambda l, j: (l, 0, j)),
        out_shape=jax.ShapeDtypeStruct((depth, r, d6), F32),
        compiler_params=pltpu.CompilerParams(
            dimension_semantics=("arbitrary", "arbitrary"), vmem_limit_bytes=VMEM_LIMIT),
        name="modulation",
    )(cc, w_mod, b_mod.reshape(depth, 1, d6))


def _inproj_kernel(x_ref, mod_ref, g_ref, w_ref, oml_ref, ossd_ref, oatt_ref, og_ref):
    x = x_ref[...]
    ms = jnp.mean(x * x, axis=-1, keepdims=True)
    xn = x * lax.rsqrt(ms + EPS)
    h = xn * (g_ref[...] * (1.0 + mod_ref[1:2, :])) + mod_ref[0:1, :]
    hb = h.astype(BF16)
    step = 512
    off = 0
    for o_ref, width in ((oml_ref, U_ML), (ossd_ref, U_SSD), (oatt_ref, U_ATT), (og_ref, U_GATE)):
        for j in range(0, width, step):
            wj = min(step, width - j)
            o_ref[:, j:j + wj] = _dot(hb, w_ref[:, off + j:off + j + wj]).astype(o_ref.dtype)
        off += width


def _in_proj(x, mods, mod_row, g1, w, tm):
    bsz, t, d = x.shape
    widths = (U_ML, U_SSD, U_ATT, U_GATE)
    dtypes = (BF16, BF16, BF16, F32)
    return pl.pallas_call(
        _inproj_kernel,
        grid=(bsz, t // tm),
        in_specs=[
            pl.BlockSpec((None, tm, d), lambda b, i: (b, i, 0)),
            pl.BlockSpec((None, 6, d), lambda b, i: (mod_row(b), 0, 0)),
            pl.BlockSpec((1, d), lambda b, i: (0, 0)),
            pl.BlockSpec((d, U_TOTAL), lambda b, i: (0, 0), pipeline_mode=pl.Buffered(1)),
        ],
        out_specs=[pl.BlockSpec((None, tm, wd), lambda b, i: (b, i, 0)) for wd in widths],
        out_shape=[jax.ShapeDtypeStruct((bsz, t, wd), dt) for wd, dt in zip(widths, dtypes)],
        compiler_params=pltpu.CompilerParams(
            dimension_semantics=("arbitrary", "arbitrary"), vmem_limit_bytes=VMEM_LIMIT),
        name="in_proj",
    )(x, mods, g1, w)


def _post_kernel(x_ref, yml_ref, yssd_ref, yatt_ref, mod_ref, g_ref, wo_ref, wup_ref, wdn_ref, o_ref):
    proj = (_dot(yml_ref[...], wo_ref[0:ML_WIDTH, :])
            + _dot(yssd_ref[...], wo_ref[ML_WIDTH:ML_WIDTH + SSD_WIDTH, :])
            + _dot(yatt_ref[...], wo_ref[ML_WIDTH + SSD_WIDTH:MIX_WIDTH, :]))
    x1 = x_ref[...] + mod_ref[2:3, :] * proj
    ms = jnp.mean(x1 * x1, axis=-1, keepdims=True)
    xn = x1 * lax.rsqrt(ms + EPS)
    hb = (xn * (g_ref[...] * (1.0 + mod_ref[4:5, :])) + mod_ref[3:4, :]).astype(BF16)
    d_ff = wup_ref.shape[1]
    step = 1024
    acc = None
    for j in range(0, d_ff, step):
        up = jnp.maximum(_dot(hb, wup_ref[:, j:j + step]), 0.0)
        part = _dot((up * up).astype(BF16), wdn_ref[j:j + step, :])
        acc = part if acc is None else acc + part
    o_ref[...] = x1 + mod_ref[5:6, :] * acc


def _post(x, yml, yssd, yatt, mods, mod_row, g2, wo, wup, wdn, tm):
    bsz, t, d = x.shape
    d_ff = wup.shape[1]
    const = dict(pipeline_mode=pl.Buffered(1))
    return pl.pallas_call(
        _post_kernel,
        grid=(bsz, t // tm),
        in_specs=[
            pl.BlockSpec((None, tm, d), lambda b, i: (b, i, 0)),
            pl.BlockSpec((None, tm, ML_WIDTH), lambda b, i: (b, i, 0)),
            pl.BlockSpec((None, tm, SSD_WIDTH), lambda b, i: (b, i, 0)),
            pl.BlockSpec((None, tm, ATT_WIDTH), lambda b, i: (b, i, 0)),
            pl.BlockSpec((None, 6, d), lambda b, i: (mod_row(b), 0, 0)),
            pl.BlockSpec((1, d), lambda b, i: (0, 0)),
            pl.BlockSpec((MIX_WIDTH, d), lambda b, i: (0, 0), **const),
            pl.BlockSpec((d, d_ff), lambda b, i: (0, 0), **const),
            pl.BlockSpec((d_ff, d), lambda b, i: (0, 0), **const),
        ],
        out_specs=pl.BlockSpec((None, tm, d), lambda b, i: (b, i, 0)),
        out_shape=jax.ShapeDtypeStruct((bsz, t, d), F32),
        compiler_params=pltpu.CompilerParams(
            dimension_semantics=("arbitrary", "arbitrary"), vmem_limit_bytes=VMEM_LIMIT),
        name="post",
    )(x, yml, yssd, yatt, mods, g2, wo, wup, wdn)


ML_GATES = N_DIR * ML_HEADS


def _mlstm_kernel(ul_ref, uc_ref, gl_ref, gc_ref, pv_ref, ng_ref, sh_ref, ex_ref, *rest,
                  need_ctx_out, n_lat, n_ctx):
    if need_ctx_out:
        yl_ref, yc_ref, arow, packed, ends, mprev, cbk, crun = rest
    else:
        yl_ref, arow, packed, ends, mprev, cbk, crun = rest
        yc_ref = None
    ncl, ncc = n_lat // CHUNK, n_ctx // CHUNK
    nct = ncl + ncc
    nh = ML_HEADS
    ti = _iota((CHUNK, CHUNK), 0)
    si = _iota((CHUNK, CHUNK), 1)
    low = si <= ti
    upp = si >= ti
    tri_low = jnp.where(low, 1.0, 0.0).astype(BF16)
    tri_upp = jnp.where(upp, 1.0, 0.0).astype(BF16)
    fwd_lane = _iota((CHUNK, LANES), 1) < nh
    row = _iota((CHUNK, LANES), 0)
    bias_f = pv_ref[0:1, :]
    bias_i = pv_ref[1:2, :]
    neg_inf = -jnp.inf
    ones_blk = jnp.ones((CHUNK, LANES), BF16)

    def prep(g_ref, nch, goff):
        def body(c, carry):
            g = g_ref[_chunk_rows(c), :]
            lf = _log_sigmoid(g[:, 0:LANES] + bias_f)
            li = g[:, LANES:2 * LANES] + bias_i
            b = jnp.where(fwd_lane, _tri_cumsum(tri_low, lf), _tri_cumsum(tri_upp, lf))
            a = li - b
            cm = a
            k = 1
            while k < CHUNK:
                before = jnp.where(row >= k, pltpu.roll(cm, k, 0), neg_inf)
                after = jnp.where(row < CHUNK - k, pltpu.roll(cm, CHUNK - k, 0), neg_inf)
                cm = jnp.maximum(cm, jnp.where(fwd_lane, before, after))
                k *= 2
            arow[goff + c] = a.T
            pk = None
            for i, part in enumerate(_split3(cm) + _split3(b)):
                t = _dot(part, sh_ref[:, i * LANES:(i + 1) * LANES])
                pk = t if pk is None else pk + t
            packed[goff + c] = pk.astype(BF16)
            ends[goff + c, 0:1, :] = jnp.where(fwd_lane[0:1], cm[CHUNK - 1:CHUNK], cm[0:1])
            ends[goff + c, 1:2, :] = jnp.where(fwd_lane[0:1], b[CHUNK - 1:CHUNK], b[0:1])
            return carry
        lax.fori_loop(0, nch, body, 0)

    prep(gc_ref, ncc, 0)
    prep(gl_ref, ncl, ncc)

    def m_scan(lo, hi, reverse, m0):
        def body(i, m):
            cg = hi - 1 - i if reverse else lo + i
            mprev[cg, int(reverse):int(reverse) + 1, :] = m
            return ends[cg, 1:2, :] + jnp.maximum(m, ends[cg, 0:1, :])
        return lax.fori_loop(0, hi - lo, body, m0)

    m_zero = jnp.zeros((1, LANES), F32)
    m_scan(0, nct, False, m_zero)
    m_scan(ncc, nct, True, m_scan(0, ncc, True, m_zero))

    def load_kv(u_ref, rows, h):
        k = u_ref[rows, ML_WIDTH + h * LANES:ML_WIDTH + (h + 1) * LANES]
        v = u_ref[rows, 2 * ML_WIDTH + h * LANES:2 * ML_WIDTH + (h + 1) * LANES]
        return k, jnp.concatenate([v, ones_blk], axis=1)

    def update(d, h, cg, k, v1):
        j = d * nh + h
        m_prev = mprev[cg, d:d + 1, j:j + 1]
        m_end = jnp.maximum(m_prev, ends[cg, 0:1, j:j + 1])
        w_row = jnp.exp(arow[cg, j:j + 1, :] - m_end) * ML_SCALE
        ktw = (k.T.astype(F32) * w_row).astype(BF16)
        crun[d, h] = jnp.exp(m_prev - m_end) * crun[d, h] + _dot(ktw, v1)

    def bwd_pass(u_ref, nch, goff, keep):
        def body(i, carry):
            c = nch - 1 - i
            rows = _chunk_rows(c)
            for h in range(nh):
                if keep:
                    cbk[goff + c, h] = crun[1, h].astype(BF16)
                k, v1 = load_kv(u_ref, rows, h)
                update(1, h, goff + c, k, v1)
            return carry
        lax.fori_loop(0, nch, body, 0)

    def fwd_pass(u_ref, y_ref, nch, goff):
        def body(c, carry):
            rows = _chunk_rows(c)
            cg = goff + c
            for h in range(nh):
                k, v1 = load_kv(u_ref, rows, h)
                if y_ref is not None:
                    q = u_ref[rows, h * LANES:(h + 1) * LANES]
                    o = u_ref[rows, 3 * ML_WIDTH + h * LANES:3 * ML_WIDTH + (h + 1) * LANES]
                    sqk = _dot_nt(q, k) * ML_SCALE
                    qc = _dot(q, jnp.concatenate([crun[0, h].astype(BF16), cbk[cg, h]], axis=1))
                    s_dirs, stats = [], []
                    for d, mask in ((0, low), (1, upp)):
                        j = d * nh + h
                        rep = _dot(packed[cg], ex_ref[:, 2 * j * LANES:2 * (j + 1) * LANES])
                        m_prev = mprev[cg, d:d + 1, j:j + 1]
                        m_t = jnp.maximum(rep[:, 0:LANES], m_prev)
                        s_dirs.append(sqk * jnp.exp(jnp.where(mask, arow[cg, j:j + 1, :] - m_t, neg_inf)))
                        stats.append((m_prev, m_t, rep[:, LANES:2 * LANES]))
                    sv = _dot(jnp.concatenate(s_dirs, axis=0).astype(BF16), v1)
                    hsum = None
                    for d, (m_prev, m_t, b_t) in enumerate(stats):
                        g = jnp.exp(m_prev - m_t)
                        sv_d = sv[d * CHUNK:(d + 1) * CHUNK, :]
                        qc_d = qc[:, 2 * d * LANES:2 * (d + 1) * LANES]
                        num = sv_d[:, 0:LANES] + g * qc_d[:, 0:LANES]
                        den = sv_d[:, LANES:2 * LANES] + g * qc_d[:, LANES:2 * LANES]
                        hd = num / jnp.maximum(jnp.abs(den), jnp.exp(-(b_t + m_t)))
                        hsum = hd if hsum is None else hsum + hd
                    ms = jnp.mean(hsum * hsum, axis=1, keepdims=True)
                    y = hsum * lax.rsqrt(ms + EPS) * ng_ref[:, h * LANES:(h + 1) * LANES]
                    y = y * _sigmoid(o.astype(F32))
                    y_ref[rows, h * LANES:(h + 1) * LANES] = y.astype(y_ref.dtype)
                update(0, h, cg, k, v1)
            return carry
        lax.fori_loop(0, nch, body, 0)

    crun[...] = jnp.zeros(crun.shape, F32)
    bwd_pass(uc_ref, ncc, 0, need_ctx_out)
    bwd_pass(ul_ref, ncl, ncc, True)
    fwd_pass(uc_ref, yc_ref, ncc, 0)
    fwd_pass(ul_ref, yl_ref, ncl, ncc)


def _ml_lane_tables():
    shifts = np.zeros((LANES, 6 * LANES), np.float32)
    expand = np.zeros((LANES, ML_GATES * 2 * LANES), np.float32)
    for j in range(ML_GATES):
        for i in range(6):
            shifts[j, i * LANES + ML_GATES * i + j] = 1.0
            half = i // 3
            expand[ML_GATES * i + j, (2 * j + half) * LANES:(2 * j + half + 1) * LANES] = 1.0
    return jnp.asarray(shifts, BF16), jnp.asarray(expand, BF16)


def _mlstm_mixer(uml_l, uml_c, g_l, g_c, pvec, norm_g, need_ctx_out):
    shifts, expand = _ml_lane_tables()
    bsz, n_lat, _ = uml_l.shape
    n_ctx = uml_c.shape[1]
    nct = (n_lat + n_ctx) // CHUNK
    kern = functools.partial(_mlstm_kernel, need_ctx_out=need_ctx_out, n_lat=n_lat, n_ctx=n_ctx)
    out_specs = [pl.BlockSpec((None, n_lat, ML_WIDTH), lambda b: (b, 0, 0))]
    out_shape = [jax.ShapeDtypeStruct((bsz, n_lat, ML_WIDTH), BF16)]
    if need_ctx_out:
        out_specs.append(pl.BlockSpec((None, n_ctx, ML_WIDTH), lambda b: (b, 0, 0)))
        out_shape.append(jax.ShapeDtypeStruct((bsz, n_ctx, ML_WIDTH), BF16))
    outs = pl.pallas_call(
        kern,
        grid=(bsz,),
        in_specs=[
            pl.BlockSpec((None, n_lat, U_ML), lambda b: (b, 0, 0)),
            pl.BlockSpec((None, n_ctx, U_ML), lambda b: (b, 0, 0)),
            pl.BlockSpec((None, n_lat, U_GATE), lambda b: (b, 0, 0)),
            pl.BlockSpec((None, n_ctx, U_GATE), lambda b: (b, 0, 0)),
            pl.BlockSpec((8, LANES), lambda b: (0, 0)),
            pl.BlockSpec((1, ML_WIDTH), lambda b: (0, 0)),
            pl.BlockSpec(shifts.shape, lambda b: (0, 0)),
            pl.BlockSpec(expand.shape, lambda b: (0, 0)),
        ],
        out_specs=out_specs,
        out_shape=out_shape,
        scratch_shapes=[
            pltpu.VMEM((nct, LANES, CHUNK), F32),
            pltpu.VMEM((nct, CHUNK, LANES), BF16),
            pltpu.VMEM((nct, 8, LANES), F32),
            pltpu.VMEM((nct, 8, LANES), F32),
            pltpu.VMEM((nct, ML_HEADS, ML_HEAD_DIM, 2 * LANES), BF16),
            pltpu.VMEM((N_DIR, ML_HEADS, ML_HEAD_DIM, 2 * LANES), F32),
        ],
        compiler_params=pltpu.CompilerParams(
            dimension_semantics=("arbitrary",), vmem_limit_bytes=VMEM_LIMIT),
        name="mlstm",
    )(uml_l, uml_c, g_l, g_c, pvec, norm_g, shifts, expand)
    return (outs[0], outs[1]) if need_ctx_out else (outs[0], None)


def _ssd_kernel(ul_ref, uc_ref, gl_ref, gc_ref, pv_ref, cw_ref, cb_ref, dsk_ref, ng_ref, ex_ref, *rest,
                need_ctx_out, n_lat, n_ctx):
    if need_ctx_out:
        yl_ref, yc_ref, xact, acum, wsc, act, dtt, dec, hbk, hrun = rest
    else:
        yl_ref, xact, acum, wsc, act, dtt, dec, hbk, hrun = rest
        yc_ref = None
    ncl, ncc = n_lat // CHUNK, n_ctx // CHUNK
    nh, hg = SSD_HEADS, SSD_HEADS // SSD_GROUPS
    gw = hg * SSD_HEAD_DIM
    gn = SSD_GROUPS * SSD_STATE
    ti = _iota((CHUNK, CHUNK), 0)
    si = _iota((CHUNK, CHUNK), 1)
    low = si <= ti
    upp = si >= ti
    tri_low = jnp.where(low, 1.0, 0.0).astype(BF16)
    tri_upp = jnp.where(upp, 1.0, 0.0).astype(BF16)
    lane = _iota((CHUNK, LANES), 1)
    fwd_lane = (lane >= DT_LANE0) & (lane < DT_LANE0 + nh)
    bias0 = pv_ref[0:1, :]
    a_row = -jnp.exp(pv_ref[2:3, :])
    row_id = _iota((CHUNK, SSD_XBC), 0)
    neg_inf = -jnp.inf
    block_mask = jnp.where((_iota((gn, SSD_WIDTH), 0) < SSD_STATE) == (_iota((gn, SSD_WIDTH), 1) < gw),
                           1.0, 0.0)
    lane_group0 = lane < SSD_STATE

    def prep(u_ref, g_ref, nch, goff, nrows):
        def body(c, carry):
            r = pl.multiple_of(c * CHUNK, CHUNK)
            cur = u_ref[pl.ds(r, CHUNK), SSD_WIDTH:U_SSD].astype(F32)
            rp = pl.multiple_of(jnp.maximum(r - 16, 0), 16)
            rn = pl.multiple_of(jnp.minimum(r + CHUNK, nrows - 16), 16)
            prev = u_ref[pl.ds(rp, 16), SSD_WIDTH:U_SSD].astype(F32)[15:16, :]
            nxt = u_ref[pl.ds(rn, 16), SSD_WIDTH:U_SSD].astype(F32)[0:1, :]
            prev = jnp.where(c > 0, prev, 0.0)
            nxt = jnp.where(c < nch - 1, nxt, 0.0)
            xm = jnp.where(row_id == 0, prev, pltpu.roll(cur, 1, 0))
            xp = jnp.where(row_id == CHUNK - 1, nxt, pltpu.roll(cur, CHUNK - 1, 0))
            conv = cw_ref[0:1, :] * xm + cw_ref[1:2, :] * cur + cw_ref[2:3, :] * xp + cb_ref[...]
            xact[goff + c] = (conv * _sigmoid(conv)).astype(BF16)

            dt = _softplus(g_ref[pl.ds(r, CHUNK), :] + bias0)
            la = dt * a_row
            ac = jnp.where(fwd_lane, _tri_cumsum(tri_low, la), _tri_cumsum(tri_upp, la))
            a_end = jnp.where(fwd_lane[0:1, :], ac[CHUNK - 1:CHUNK, :], ac[0:1, :])
            acum[goff + c] = ac
            wsc[goff + c] = jnp.exp(a_end - ac) * dt
            act[goff + c] = ac.T
            dtt[goff + c] = dt.T
            dec[goff + c] = jnp.broadcast_to(jnp.exp(a_end), (8, LANES))
            return carry
        lax.fori_loop(0, nch, body, 0)

    prep(uc_ref, gc_ref, ncc, 0, n_ctx)
    prep(ul_ref, gl_ref, ncl, ncc, n_lat)

    def update_state(d, cg):
        onehot = ex_ref[:, d * SSD_WIDTH:(d + 1) * SSD_WIDTH]
        xa = xact[cg]
        wx = (_expand(wsc[cg], onehot) * xa[:, 0:SSD_WIDTH].astype(F32)).astype(BF16)
        upd = _dot_tn(xa[:, SSD_WIDTH:SSD_WIDTH + gn], wx)
        decay = _expand(dec[cg], onehot)[0:1, :]
        hrun[d] = (decay * hrun[d] + upd) * block_mask

    def bwd_pass(nch, goff, keep):
        def body(i, carry):
            cg = goff + nch - 1 - i
            if keep:
                hbk[cg] = hrun[1]
            update_state(1, cg)
            return carry
        lax.fori_loop(0, nch, body, 0)

    def fwd_pass(u_ref, y_ref, nch, goff):
        def body(c, carry):
            cg = goff + c
            if y_ref is not None:
                rows = _chunk_rows(c)
                xa = xact[cg]
                x = xa[:, 0:SSD_WIDTH]
                b_all = xa[:, SSD_WIDTH:SSD_WIDTH + gn]
                c_all = xa[:, SSD_WIDTH + gn:SSD_XBC]
                ac = acum[cg]
                ac_t = act[cg]
                dt_t = dtt[cg]
                lane_w = _iota((CHUNK, gw), 1)
                y_groups = []
                for g in range(SSD_GROUPS):
                    c_g = jnp.where(lane_group0 == (g == 0), c_all, jnp.zeros_like(c_all))
                    cb = _dot_nt(c_g, b_all)
                    s_heads = []
                    for hh in range(hg):
                        jf = DT_LANE0 + g * hg + hh
                        jb = jf + nh
                        ef = jnp.exp(jnp.where(low, ac[:, jf:jf + 1] - ac_t[jf:jf + 1, :], neg_inf))
                        eb = jnp.exp(jnp.where(upp, ac[:, jb:jb + 1] - ac_t[jb:jb + 1, :], neg_inf))
                        s_heads.append((cb * (ef * dt_t[jf:jf + 1, :] + eb * dt_t[jb:jb + 1, :])).astype(BF16))
                    x_g = x[:, g * gw:(g + 1) * gw]
                    bd = jnp.concatenate(
                        [jnp.where((lane_w >= hh * SSD_HEAD_DIM) & (lane_w < (hh + 1) * SSD_HEAD_DIM),
                                   x_g, jnp.zeros_like(x_g)) for hh in range(hg)], axis=0)
                    y_groups.append(_dot(jnp.concatenate(s_heads, axis=1), bd))
                y = jnp.concatenate(y_groups, axis=1)
                e = jnp.exp(ac)
                y = y + _dot(c_all, hrun[0].astype(BF16)) * _expand(e, ex_ref[:, 0:SSD_WIDTH])
                y = y + _dot(c_all, hbk[cg].astype(BF16)) * _expand(e, ex_ref[:, SSD_WIDTH:2 * SSD_WIDTH])
                y = y + dsk_ref[...] * x.astype(F32)
                z = u_ref[rows, 0:SSD_WIDTH].astype(F32)
                yz = y * (z * _sigmoid(z))
                ms = jnp.mean(yz * yz, axis=1, keepdims=True)
                y_ref[rows, :] = (yz * lax.rsqrt(ms + EPS) * ng_ref[...]).astype(y_ref.dtype)
            update_state(0, cg)
            return carry
        lax.fori_loop(0, nch, body, 0)

    hrun[...] = jnp.zeros(hrun.shape, F32)
    bwd_pass(ncc, 0, need_ctx_out)
    bwd_pass(ncl, ncc, True)
    fwd_pass(uc_ref, yc_ref, ncc, 0)
    fwd_pass(ul_ref, yl_ref, ncl, ncc)


def _ssd_mixer(ussd_l, ussd_c, g_l, g_c, pvec, conv_w, conv_b, d_skip, norm_g, onehot, need_ctx_out):
    bsz, n_lat, _ = ussd_l.shape
    n_ctx = ussd_c.shape[1]
    nct = (n_lat + n_ctx) // CHUNK
    gn = SSD_GROUPS * SSD_STATE
    kern = functools.partial(_ssd_kernel, need_ctx_out=need_ctx_out, n_lat=n_lat, n_ctx=n_ctx)
    out_specs = [pl.BlockSpec((None, n_lat, SSD_WIDTH), lambda b: (b, 0, 0))]
    out_shape = [jax.ShapeDtypeStruct((bsz, n_lat, SSD_WIDTH), BF16)]
    if need_ctx_out:
        out_specs.append(pl.BlockSpec((None, n_ctx, SSD_WIDTH), lambda b: (b, 0, 0)))
        out_shape.append(jax.ShapeDtypeStruct((bsz, n_ctx, SSD_WIDTH), BF16))
    outs = pl.pallas_call(
        kern,
        grid=(bsz,),
        in_specs=[
            pl.BlockSpec((None, n_lat, U_SSD), lambda b: (b, 0, 0)),
            pl.BlockSpec((None, n_ctx, U_SSD), lambda b: (b, 0, 0)),
            pl.BlockSpec((None, n_lat, LANES), lambda b: (b, 0, 0)),
            pl.BlockSpec((None, n_ctx, LANES), lambda b: (b, 0, 0)),
            pl.BlockSpec((8, LANES), lambda b: (0, 0)),
            pl.BlockSpec((3, SSD_XBC), lambda b: (0, 0)),
            pl.BlockSpec((1, SSD_XBC), lambda b: (0, 0)),
            pl.BlockSpec((1, SSD_WIDTH), lambda b: (0, 0)),
            pl.BlockSpec((1, SSD_WIDTH), lambda b: (0, 0)),
            pl.BlockSpec((LANES, 2 * SSD_WIDTH), lambda b: (0, 0)),
        ],
        out_specs=out_specs,
        out_shape=out_shape,
        scratch_shapes=[
            pltpu.VMEM((nct, CHUNK, SSD_XBC), BF16),
            pltpu.VMEM((nct, CHUNK, LANES), F32),
            pltpu.VMEM((nct, CHUNK, LANES), F32),
            pltpu.VMEM((nct, LANES, CHUNK), F32),
            pltpu.VMEM((nct, LANES, CHUNK), F32),
            pltpu.VMEM((nct, 8, LANES), F32),
            pltpu.VMEM((nct, gn, SSD_WIDTH), F32),
            pltpu.VMEM((N_DIR, gn, SSD_WIDTH), F32),
        ],
        compiler_params=pltpu.CompilerParams(
            dimension_semantics=("arbitrary",), vmem_limit_bytes=VMEM_LIMIT),
        name="ssd",
    )(ussd_l, ussd_c, g_l, g_c, pvec, conv_w, conv_b, d_skip, norm_g, onehot)
    return (outs[0], outs[1]) if need_ctx_out else (outs[0], None)


def _head_norm(x, gain):
    lane_lo = _iota((x.shape[0], LANES), 1) < ATT_HEAD_DIM
    outs = []
    for j in range(x.shape[1] // LANES):
        blk = x[:, j * LANES:(j + 1) * LANES]
        sq = blk * blk
        s0 = jnp.sum(jnp.where(lane_lo, sq, 0.0), axis=1, keepdims=True)
        s1 = jnp.sum(jnp.where(lane_lo, 0.0, sq), axis=1, keepdims=True)
        ms = jnp.where(lane_lo, s0, s1) * (1.0 / ATT_HEAD_DIM)
        outs.append(blk * lax.rsqrt(ms + EPS) * gain)
    return outs[0] if len(outs) == 1 else jnp.concatenate(outs, axis=1)


def _rope(x, cos_t, sin_t):
    w = x.shape[1]
    reps = w // LANES
    nfreq = ATT_HEAD_DIM // 4
    partner = jnp.where((_iota(x.shape, 1) & (2 * nfreq - 1)) < nfreq,
                        pltpu.roll(x, w - nfreq, 1), pltpu.roll(x, nfreq, 1))
    if reps > 1:
        cos_t = jnp.concatenate([cos_t] * reps, axis=1)
        sin_t = jnp.concatenate([sin_t] * reps, axis=1)
    return x * cos_t + partner * sin_t


def _dup_heads(x):
    lane_lo = _iota(x.shape, 1) < ATT_HEAD_DIM
    sw = pltpu.roll(x, ATT_HEAD_DIM, 1)
    return jnp.concatenate([jnp.where(lane_lo, x, sw), jnp.where(lane_lo, sw, x)], axis=1)


def _attn_kernel(ul_ref, uc_ref, cos_ref, sin_ref, qg_ref, kg_ref, sink_ref, *rest,
                 need_ctx_out, n_lat, n_ctx):
    if need_ctx_out:
        yl_ref, yc_ref, qp, kb, vb, kc, vc, qcp = rest
    else:
        yl_ref, qp, kb, vb, kc, vc = rest
        yc_ref = qcp = None
    ncl, ncc = n_lat // CHUNK, n_ctx // CHUNK
    rep = ATT_Q_HEADS // ATT_KV_HEADS
    kv0 = ATT_WIDTH
    lane_lo = _iota((CHUNK, LANES), 1) < ATT_HEAD_DIM
    neg_inf = -jnp.inf
    qg = qg_ref[...]
    kg = kg_ref[...]

    for ref in (kb, vb):
        zeros_blk = jnp.zeros((CHUNK, ref.shape[1]), BF16)
        ref[0:CHUNK, :] = zeros_blk
        ref[CHUNK + n_lat:2 * CHUNK + n_lat, :] = zeros_blk

    def values_with_ones(v):
        d = _dup_heads(v).astype(BF16)
        ones = jnp.ones((v.shape[0], LANES), BF16)
        return jnp.concatenate([d[:, 0:LANES], ones, d[:, LANES:2 * LANES], ones], axis=1)

    def prep_lat(c, carry):
        rows = _chunk_rows(c)
        cos_t = cos_ref[rows, :]
        sin_t = sin_ref[rows, :]
        q = _rope(_head_norm(ul_ref[rows, 0:ATT_WIDTH].astype(F32), qg), cos_t, sin_t)
        qp[rows, :] = (q * ATT_SCALE).astype(BF16)
        k = _rope(_head_norm(ul_ref[rows, kv0:kv0 + LANES].astype(F32), kg), cos_t, sin_t)
        v = ul_ref[rows, kv0 + LANES:kv0 + 2 * LANES].astype(F32)
        dst = pl.ds(pl.multiple_of((c + 1) * CHUNK, CHUNK), CHUNK)
        kb[dst, :] = _dup_heads(k).astype(BF16)
        vb[dst, :] = values_with_ones(v)
        return carry
    lax.fori_loop(0, ncl, prep_lat, 0)

    def prep_ctx(c, carry):
        rows = _chunk_rows(c)
        k = _head_norm(uc_ref[rows, kv0:kv0 + LANES].astype(F32), kg)
        v = uc_ref[rows, kv0 + LANES:kv0 + 2 * LANES].astype(F32)
        kc[rows, :] = _dup_heads(k).astype(BF16)
        vc[rows, :] = values_with_ones(v)
        if need_ctx_out:
            q = _head_norm(uc_ref[rows, 0:ATT_WIDTH].astype(F32), qg)
            qcp[rows, :] = (q * ATT_SCALE).astype(BF16)
        return carry
    lax.fori_loop(0, ncc, prep_ctx, 0)

    def attend(q_ref, y_ref, rows, band):
        for g in range(ATT_KV_HEADS):
            kl = slice(g * LANES, (g + 1) * LANES)
            vl = slice(2 * g * LANES, 2 * (g + 1) * LANES)
            qms = []
            for r in range(rep):
                hq = g * rep + r
                qb = q_ref[rows, (hq // 2) * LANES:(hq // 2 + 1) * LANES]
                qms.append(jnp.where(lane_lo != (hq % 2 == 1), qb, jnp.zeros_like(qb)))
            qs = jnp.concatenate(qms, axis=0)
            s_c = _dot_nt(qs, kc[:, kl])
            if band is not None:
                band_rows, valid = band
                s_b = _dot_nt(qs, kb[band_rows, kl])
            ps, sink_terms = [], []
            for r in range(rep):
                sl = slice(r * CHUNK, (r + 1) * CHUNK)
                sk = sink_ref[g * rep + r]
                s = s_c[sl]
                if band is not None:
                    s = jnp.concatenate([s, jnp.where(valid, s_b[sl], neg_inf)], axis=1)
                m = jnp.maximum(jnp.max(s, axis=1, keepdims=True), sk)
                ps.append(jnp.exp(s - m).astype(BF16))
                sink_terms.append(jnp.exp(sk - m))
            p = jnp.concatenate(ps, axis=0)
            acc = _dot(p[:, 0:n_ctx], vc[:, vl])
            if band is not None:
                acc = acc + _dot(p[:, n_ctx:], vb[band_rows, vl])
            for r in range(rep):
                sl = slice(r * CHUNK, (r + 1) * CHUNK)
                o = acc[sl, 0:LANES] / (acc[sl, LANES:2 * LANES] + sink_terms[r])
                if r % 2 == 1:
                    jb = (g * rep + r) // 2
                    y_ref[rows, jb * LANES:(jb + 1) * LANES] = jnp.where(lane_lo, o_even, o).astype(y_ref.dtype)
                o_even = o

    qq = _iota((CHUNK, 3 * CHUNK), 0)
    kk = _iota((CHUNK, 3 * CHUNK), 1)
    in_window = (kk >= qq) & (kk <= qq + 2 * WINDOW)

    def lat_block(i, carry):
        kpos = kk + (i - 1) * CHUNK
        valid = in_window & (kpos >= 0) & (kpos < n_lat)
        band_rows = pl.ds(pl.multiple_of(i * CHUNK, CHUNK), 3 * CHUNK)
        attend(qp, yl_ref, _chunk_rows(i), (band_rows, valid))
        return carry
    lax.fori_loop(0, ncl, lat_block, 0)

    if need_ctx_out:
        def ctx_block(i, carry):
            attend(qcp, yc_ref, _chunk_rows(i), None)
            return carry
        lax.fori_loop(0, ncc, ctx_block, 0)


def _attn_mixer(uatt_l, uatt_c, cos_t, sin_t, qgain, kgain, sink, need_ctx_out):
    bsz, n_lat, _ = uatt_l.shape
    n_ctx = uatt_c.shape[1]
    kern = functools.partial(_attn_kernel, need_ctx_out=need_ctx_out, n_lat=n_lat, n_ctx=n_ctx)
    out_specs = [pl.BlockSpec((None, n_lat, ATT_WIDTH), lambda b: (b, 0, 0))]
    out_shape = [jax.ShapeDtypeStruct((bsz, n_lat, ATT_WIDTH), BF16)]
    scratch = [
        pltpu.VMEM((n_lat, ATT_WIDTH), BF16),
        pltpu.VMEM((n_lat + 2 * CHUNK, 2 * LANES), BF16),
        pltpu.VMEM((n_lat + 2 * CHUNK, 4 * LANES), BF16),
        pltpu.VMEM((n_ctx, 2 * LANES), BF16),
        pltpu.VMEM((n_ctx, 4 * LANES), BF16),
    ]
    if need_ctx_out:
        out_specs.append(pl.BlockSpec((None, n_ctx, ATT_WIDTH), lambda b: (b, 0, 0)))
        out_shape.append(jax.ShapeDtypeStruct((bsz, n_ctx, ATT_WIDTH), BF16))
        scratch.append(pltpu.VMEM((n_ctx, ATT_WIDTH), BF16))
    outs = pl.pallas_call(
        kern,
        grid=(bsz,),
        in_specs=[
            pl.BlockSpec((None, n_lat, U_ATT), lambda b: (b, 0, 0)),
            pl.BlockSpec((None, n_ctx, U_ATT), lambda b: (b, 0, 0)),
            pl.BlockSpec((n_lat, LANES), lambda b: (0, 0)),
            pl.BlockSpec((n_lat, LANES), lambda b: (0, 0)),
            pl.BlockSpec((1, LANES), lambda b: (0, 0)),
            pl.BlockSpec((1, LANES), lambda b: (0, 0)),
            pl.BlockSpec(memory_space=pltpu.SMEM),
        ],
        out_specs=out_specs,
        out_shape=out_shape,
        scratch_shapes=scratch,
        compiler_params=pltpu.CompilerParams(
            dimension_semantics=("arbitrary",), vmem_limit_bytes=VMEM_LIMIT),
        name="attention",
    )(uatt_l, uatt_c, cos_t, sin_t, qgain, kgain, sink)
    return (outs[0], outs[1]) if need_ctx_out else (outs[0], None)


def _in_layout():
    names = (('ml_q', ML_WIDTH), ('ml_k', ML_WIDTH), ('ml_v', ML_WIDTH), ('ml_o', ML_WIDTH),
             ('ml_i', N_DIR * ML_HEADS), ('ml_f', N_DIR * ML_HEADS),
             ('ssd_z', SSD_WIDTH), ('ssd_xbc', SSD_XBC), ('ssd_dt', N_DIR * SSD_HEADS),
             ('att_q', ATT_WIDTH), ('att_k', ATT_KV_WIDTH), ('att_v', ATT_KV_WIDTH))
    out, off = {}, 0
    for name, width in names:
        out[name] = (off, off + width)
        off += width
    return out


def _permute_cols(a):
    lay = _in_layout()
    take = lambda name: a[..., lay[name][0]:lay[name][1]]
    zeros = lambda n: jnp.zeros(a.shape[:-1] + (n,), a.dtype)
    n_f, n_dt, n_i = N_DIR * ML_HEADS, N_DIR * SSD_HEADS, N_DIR * ML_HEADS
    return jnp.concatenate(
        [take('ml_q'), take('ml_k'), take('ml_v'), take('ml_o'),
         take('ssd_z'), take('ssd_xbc'),
         take('att_q'), take('att_k'), take('att_v'),
         take('ml_f'), take('ssd_dt'), zeros(LANES - n_f - n_dt),
         take('ml_i'), zeros(LANES - n_i)], axis=-1)


def _gate_rows(ml_b_i, ml_b_f, ssd_dt_bias, ssd_a_log):
    n_f, n_dt = N_DIR * ML_HEADS, N_DIR * SSD_HEADS
    row0 = jnp.concatenate([ml_b_f.reshape(-1), ssd_dt_bias.reshape(-1), jnp.zeros((LANES - n_f - n_dt,), F32)])
    row1 = jnp.concatenate([ml_b_i.reshape(-1), jnp.zeros((LANES - n_f,), F32)])
    row2 = jnp.concatenate([jnp.zeros((n_f,), F32), ssd_a_log.reshape(-1), jnp.zeros((LANES - n_f - n_dt,), F32)])
    return jnp.concatenate([row0[None], row1[None], row2[None], jnp.zeros((5, LANES), F32)], axis=0)


def _ssd_onehot():
    m = np.zeros((LANES, N_DIR * SSD_WIDTH), np.float32)
    for d in range(N_DIR):
        for h in range(SSD_HEADS):
            c0 = d * SSD_WIDTH + h * SSD_HEAD_DIM
            m[DT_LANE0 + d * SSD_HEADS + h, c0:c0 + SSD_HEAD_DIM] = 1.0
    return jnp.asarray(m, BF16)


def _rope_tables(n_lat):
    nfreq = ATT_HEAD_DIM // 4
    pos = np.arange(n_lat)
    rows, cols = pos // GRID_W, pos % GRID_W
    inv = jnp.asarray(ROPE_BASE, F32) ** (-jnp.arange(nfreq, dtype=F32) / nfreq)
    ang_r = jnp.asarray(rows, F32)[:, None] * inv
    ang_c = jnp.asarray(cols, F32)[:, None] * inv
    cos_h = jnp.concatenate([jnp.cos(ang_r)] * 2 + [jnp.cos(ang_c)] * 2, axis=1)
    sin_h = jnp.concatenate([-jnp.sin(ang_r), jnp.sin(ang_r), -jnp.sin(ang_c), jnp.sin(ang_c)], axis=1)
    return jnp.concatenate([cos_h] * 2, axis=1), jnp.concatenate([sin_h] * 2, axis=1)


def _token_mixers(u_l, u_c, p, layer, tables, need_ctx_out):
    uml_l, ussd_l, uatt_l, g_l = u_l
    uml_c, ussd_c, uatt_c, g_c = u_c
    pvec = _gate_rows(p['ml_b_i'][layer], p['ml_b_f'][layer], p['ssd_dt_bias'][layer], p['ssd_a_log'][layer])
    m_l, m_c = _mlstm_mixer(uml_l, uml_c, g_l, g_c, pvec, p['ml_norm_g'][layer][None], need_ctx_out)
    s_l, s_c = _ssd_mixer(ussd_l, ussd_c, g_l, g_c, pvec, p['ssd_conv_w'][layer], p['ssd_conv_b'][layer][None],
                          jnp.repeat(p['ssd_d'][layer], SSD_HEAD_DIM)[None], p['ssd_norm_g'][layer][None],
                          tables['onehot'], need_ctx_out)
    a_l, a_c = _attn_mixer(uatt_l, uatt_c, tables['cos'], tables['sin'],
                           jnp.tile(p['att_qn_g'][layer], 2)[None], jnp.tile(p['att_kn_g'][layer], 2)[None],
                           p['att_sink'][layer], need_ctx_out)
    return (m_l, s_l, a_l), (m_c, s_c, a_c)


def kernel(x, c, ctx, c_ctx, w_mod, b_mod, norm1_g, w_in, ml_b_i, ml_b_f, ml_norm_g, ssd_conv_w, ssd_conv_b,
           ssd_a_log, ssd_dt_bias, ssd_d, ssd_norm_g, att_qn_g, att_kn_g, att_sink, w_out, norm2_g, w_up, w_down):
    p = dict(ml_b_i=ml_b_i, ml_b_f=ml_b_f, ml_norm_g=ml_norm_g, ssd_conv_w=ssd_conv_w, ssd_conv_b=ssd_conv_b,
             ssd_a_log=ssd_a_log, ssd_dt_bias=ssd_dt_bias, ssd_d=ssd_d, ssd_norm_g=ssd_norm_g,
             att_qn_g=att_qn_g, att_kn_g=att_kn_g, att_sink=att_sink)
    bsz, n_lat, d = x.shape
    n_ctx = ctx.shape[1]
    depth = w_mod.shape[0]
    mod_rows = 16
    cc = jnp.concatenate([c, c_ctx[None], jnp.zeros((mod_rows - bsz - 1, d), F32)], axis=0)
    mods = _modulation(cc, w_mod, b_mod).reshape(depth, mod_rows, 6, d)
    cos_t, sin_t = _rope_tables(n_lat)
    tables = dict(cos=cos_t, sin=sin_t, onehot=_ssd_onehot())
    lat_row = lambda b: b
    ctx_row = lambda b: bsz
    h_ctx = ctx
    for layer in range(depth):
        need_ctx_out = layer < depth - 1
        w = _permute_cols(w_in[layer]).astype(BF16)
        g1 = norm1_g[layer][None]
        u_l = _in_proj(x, mods[layer], lat_row, g1, w, tm=256)
        u_c = _in_proj(h_ctx, mods[layer], ctx_row, g1, w, tm=256)
        y_l, y_c = _token_mixers(u_l, u_c, p, layer, tables, need_ctx_out)
        wo, wup, wdn = w_out[layer].astype(BF16), w_up[layer].astype(BF16), w_down[layer].astype(BF16)
        g2 = norm2_g[layer][None]
        x = _post(x, *y_l, mods[layer], lat_row, g2, wo, wup, wdn, tm=512)
        if need_ctx_out:
            h_ctx = _post(h_ctx, *y_c, mods[layer], ctx_row, g2, wo, wup, wdn, tm=256)
    return x
```

```python
import functools
import math

import numpy as np
import jax
import jax.numpy as jnp
from jax import lax
from jax.experimental import pallas as pl
from jax.experimental.pallas import tpu as pltpu

F32 = jnp.float32
BF16 = jnp.bfloat16

EPS = 1e-6
GRID_W = 64
N_DIR = 2
ML_HEADS = 4
ML_HEAD_DIM = 128
ML_WIDTH = ML_HEADS * ML_HEAD_DIM
SSD_HEADS = 8
SSD_HEAD_DIM = 64
SSD_WIDTH = SSD_HEADS * SSD_HEAD_DIM
SSD_STATE = 64
SSD_GROUPS = 2
SSD_XBC = SSD_WIDTH + 2 * SSD_GROUPS * SSD_STATE
ATT_Q_HEADS = 8
ATT_KV_HEADS = 2
ATT_HEAD_DIM = 64
ATT_WIDTH = ATT_Q_HEADS * ATT_HEAD_DIM
ATT_KV_WIDTH = ATT_KV_HEADS * ATT_HEAD_DIM
WINDOW = 128
ROPE_BASE = 10000.0
MIX_WIDTH = ML_WIDTH + SSD_WIDTH + ATT_WIDTH

LANES = 128
CHUNK = 128
VMEM_LIMIT = 56 * 1024 * 1024
UNROLL = 4

U_ML = 4 * ML_WIDTH
U_SSD = SSD_WIDTH + SSD_XBC
U_ATT = ATT_WIDTH + 2 * ATT_KV_WIDTH
U_GATE = 2 * LANES
U_TOTAL = U_ML + U_SSD + U_ATT + U_GATE
DT_LANE0 = N_DIR * ML_HEADS

ML_SCALE = ML_HEAD_DIM ** -0.5
ATT_SCALE = ATT_HEAD_DIM ** -0.5

_NT = (((1,), (1,)), ((), ()))
_TN = (((0,), (0,)), ((), ()))


def _dot(a, b):
    return jnp.dot(a, b, preferred_element_type=F32)


def _dot_nt(a, b):
    return lax.dot_general(a, b, _NT, preferred_element_type=F32)


def _dot_tn(a, b):
    return lax.dot_general(a, b, _TN, preferred_element_type=F32)


def _split2(x):
    hi = x.astype(BF16)
    lo = (x - hi.astype(F32)).astype(BF16)
    return hi, lo


def _split3(x):
    hi = x.astype(BF16)
    r = x - hi.astype(F32)
    mid = r.astype(BF16)
    lo = (r - mid.astype(F32)).astype(BF16)
    return hi, mid, lo


def _tri_cumsum(tri, x):
    hi, mid, lo = _split3(x)
    return _dot(tri, hi) + _dot(tri, mid) + _dot(tri, lo)


def _expand(x, onehot):
    hi, lo = _split2(x)
    return _dot(hi, onehot) + _dot(lo, onehot)


def _sigmoid(x):
    return 1.0 / (1.0 + jnp.exp(-x))


def _log_sigmoid(x):
    return jnp.minimum(x, 0.0) - jnp.log1p(jnp.exp(-jnp.abs(x)))


def _softplus(x):
    return jnp.maximum(x, 0.0) + jnp.log1p(jnp.exp(-jnp.abs(x)))


def _iota(shape, dim):
    return lax.broadcasted_iota(jnp.int32, shape, dim)


def _chunk_rows(c):
    return pl.ds(pl.multiple_of(c * CHUNK, CHUNK), CHUNK)


def _mod_kernel(s_ref, w_ref, b_ref, o_ref):
    s = s_ref[...]
    s = s * _sigmoid(s)
    w = w_ref[...]
    s_hi, s_lo = _split2(s)
    w_hi, w_lo = _split2(w)
    o_ref[...] = _dot(s_hi, w_hi) + (_dot(s_hi, w_lo) + _dot(s_lo, w_hi)) + b_ref[...]


def _modulation(cc, w_mod, b_mod):
    depth, d, d6 = w_mod.shape
    r = cc.shape[0]
    bn = 1024
    return pl.pallas_call(
        _mod_kernel,
        grid=(depth, d6 // bn),
        in_specs=[
            pl.BlockSpec((r, d), lambda l, j: (0, 0)),
            pl.BlockSpec((None, d, bn), lambda l, j: (l, 0, j)),
            pl.BlockSpec((None, 1, bn), lambda l, j: (l, 0, j)),
        ],
        out_specs=pl.BlockSpec((None, r, bn), lambda l, j: (l, 0, j)),
        out_shape=jax.ShapeDtypeStruct((depth, r, d6), F32),
        compiler_params=pltpu.CompilerParams(
            dimension_semantics=("arbitrary", "arbitrary"), vmem_limit_bytes=VMEM_LIMIT),
        name="modulation",
    )(cc, w_mod, b_mod.reshape(depth, 1, d6))


def _inproj_kernel(x_ref, mod_ref, g_ref, w_ref, oml_ref, ossd_ref, oatt_ref, og_ref):
    x = x_ref[...]
    ms = jnp.mean(x * x, axis=-1, keepdims=True)
    xn = x * lax.rsqrt(ms + EPS)
    h = xn * (g_ref[...] * (1.0 + mod_ref[1:2, :])) + mod_ref[0:1, :]
    hb = h.astype(BF16)
    step = 512
    off = 0
    for o_ref, width in ((oml_ref, U_ML), (ossd_ref, U_SSD), (oatt_ref, U_ATT), (og_ref, U_GATE)):
        for j in range(0, width, step):
            wj = min(step, width - j)
            o_ref[:, j:j + wj] = _dot(hb, w_ref[:, off + j:off + j + wj]).astype(o_ref.dtype)
        off += width


def _in_proj(x, mods, mod_row, g1, w, tm):
    bsz, t, d = x.shape
    widths = (U_ML, U_SSD, U_ATT, U_GATE)
    dtypes = (BF16, BF16, BF16, F32)
    return pl.pallas_call(
        _inproj_kernel,
        grid=(bsz, t // tm),
        in_specs=[
            pl.BlockSpec((None, tm, d), lambda b, i: (b, i, 0)),
            pl.BlockSpec((None, 6, d), lambda b, i: (mod_row(b), 0, 0)),
            pl.BlockSpec((1, d), lambda b, i: (0, 0)),
            pl.BlockSpec((d, U_TOTAL), lambda b, i: (0, 0), pipeline_mode=pl.Buffered(1)),
        ],
        out_specs=[pl.BlockSpec((None, tm, wd), lambda b, i: (b, i, 0)) for wd in widths],
        out_shape=[jax.ShapeDtypeStruct((bsz, t, wd), dt) for wd, dt in zip(widths, dtypes)],
        compiler_params=pltpu.CompilerParams(
            dimension_semantics=("arbitrary", "arbitrary"), vmem_limit_bytes=VMEM_LIMIT),
        name="in_proj",
    )(x, mods, g1, w)


def _post_kernel(x_ref, yml_ref, yssd_ref, yatt_ref, mod_ref, g_ref, wo_ref, wup_ref, wdn_ref, o_ref):
    proj = (_dot(yml_ref[...], wo_ref[0:ML_WIDTH, :])
            + _dot(yssd_ref[...], wo_ref[ML_WIDTH:ML_WIDTH + SSD_WIDTH, :])
            + _dot(yatt_ref[...], wo_ref[ML_WIDTH + SSD_WIDTH:MIX_WIDTH, :]))
    x1 = x_ref[...] + mod_ref[2:3, :] * proj
    ms = jnp.mean(x1 * x1, axis=-1, keepdims=True)
    xn = x1 * lax.rsqrt(ms + EPS)
    hb = (xn * (g_ref[...] * (1.0 + mod_ref[4:5, :])) + mod_ref[3:4, :]).astype(BF16)
    d_ff = wup_ref.shape[1]
    step = 1024
    acc = None
    for j in range(0, d_ff, step):
        up = jnp.maximum(_dot(hb, wup_ref[:, j:j + step]), 0.0)
        part = _dot((up * up).astype(BF16), wdn_ref[j:j + step, :])
        acc = part if acc is None else acc + part
    o_ref[...] = x1 + mod_ref[5:6, :] * acc


def _post(x, yml, yssd, yatt, mods, mod_row, g2, wo, wup, wdn, tm):
    bsz, t, d = x.shape
    d_ff = wup.shape[1]
    const = dict(pipeline_mode=pl.Buffered(1))
    return pl.pallas_call(
        _post_kernel,
        grid=(bsz, t // tm),
        in_specs=[
            pl.BlockSpec((None, tm, d), lambda b, i: (b, i, 0)),
            pl.BlockSpec((None, tm, ML_WIDTH), lambda b, i: (b, i, 0)),
            pl.BlockSpec((None, tm, SSD_WIDTH), lambda b, i: (b, i, 0)),
            pl.BlockSpec((None, tm, ATT_WIDTH), lambda b, i: (b, i, 0)),
            pl.BlockSpec((None, 6, d), lambda b, i: (mod_row(b), 0, 0)),
            pl.BlockSpec((1, d), lambda b, i: (0, 0)),
            pl.BlockSpec((MIX_WIDTH, d), lambda b, i: (0, 0), **const),
            pl.BlockSpec((d, d_ff), lambda b, i: (0, 0), **const),
            pl.BlockSpec((d_ff, d), lambda b, i: (0, 0), **const),
        ],
        out_specs=pl.BlockSpec((None, tm, d), lambda b, i: (b, i, 0)),
        out_shape=jax.ShapeDtypeStruct((bsz, t, d), F32),
        compiler_params=pltpu.CompilerParams(
            dimension_semantics=("arbitrary", "arbitrary"), vmem_limit_bytes=VMEM_LIMIT),
        name="post",
    )(x, yml, yssd, yatt, mods, g2, wo, wup, wdn)


ML_GATES = N_DIR * ML_HEADS


def _mlstm_kernel(ul_ref, uc_ref, gl_ref, gc_ref, pv_ref, ng_ref, sh_ref, ex_ref, *rest,
                  need_ctx_out, n_lat, n_ctx):
    if need_ctx_out:
        yl_ref, yc_ref, arow, packed, ends, mprev, cbk, crun = rest
    else:
        yl_ref, arow, packed, ends, mprev, cbk, crun = rest
        yc_ref = None
    ncl, ncc = n_lat // CHUNK, n_ctx // CHUNK
    nct = ncl + ncc
    nh = ML_HEADS
    ti = _iota((CHUNK, CHUNK), 0)
    si = _iota((CHUNK, CHUNK), 1)
    low = si <= ti
    upp = si >= ti
    tri_low = jnp.where(low, 1.0, 0.0).astype(BF16)
    tri_upp = jnp.where(upp, 1.0, 0.0).astype(BF16)
    fwd_lane = _iota((CHUNK, LANES), 1) < nh
    row = _iota((CHUNK, LANES), 0)
    bias_f = pv_ref[0:1, :]
    bias_i = pv_ref[1:2, :]
    neg_inf = -jnp.inf
    ones_blk = jnp.ones((CHUNK, LANES), BF16)

    def prep(g_ref, nch, goff):
        def body(c, carry):
            g = g_ref[_chunk_rows(c), :]
            lf = _log_sigmoid(g[:, 0:LANES] + bias_f)
            li = g[:, LANES:2 * LANES] + bias_i
            b = jnp.where(fwd_lane, _tri_cumsum(tri_low, lf), _tri_cumsum(tri_upp, lf))
            a = li - b
            cm = a
            k = 1
            while k < CHUNK:
                before = jnp.where(row >= k, pltpu.roll(cm, k, 0), neg_inf)
                after = jnp.where(row < CHUNK - k, pltpu.roll(cm, CHUNK - k, 0), neg_inf)
                cm = jnp.maximum(cm, jnp.where(fwd_lane, before, after))
                k *= 2
            arow[goff + c] = a.T
            pk = None
            for i, part in enumerate(_split3(cm) + _split3(b)):
                t = _dot(part, sh_ref[:, i * LANES:(i + 1) * LANES])
                pk = t if pk is None else pk + t
            packed[goff + c] = pk.astype(BF16)
            ends[goff + c, 0:1, :] = jnp.where(fwd_lane[0:1], cm[CHUNK - 1:CHUNK], cm[0:1])
            ends[goff + c, 1:2, :] = jnp.where(fwd_lane[0:1], b[CHUNK - 1:CHUNK], b[0:1])
            return carry
        lax.fori_loop(0, nch, body, 0, unroll=UNROLL)

    prep(gc_ref, ncc, 0)
    prep(gl_ref, ncl, ncc)

    def m_scan(lo, hi, reverse, m0):
        def body(i, m):
            cg = hi - 1 - i if reverse else lo + i
            mprev[cg, int(reverse):int(reverse) + 1, :] = m
            return ends[cg, 1:2, :] + jnp.maximum(m, ends[cg, 0:1, :])
        return lax.fori_loop(0, hi - lo, body, m0)

    m_zero = jnp.zeros((1, LANES), F32)
    m_scan(0, nct, False, m_zero)
    m_scan(ncc, nct, True, m_scan(0, ncc, True, m_zero))

    def load_kv(u_ref, rows, h):
        k = u_ref[rows, ML_WIDTH + h * LANES:ML_WIDTH + (h + 1) * LANES]
        v = u_ref[rows, 2 * ML_WIDTH + h * LANES:2 * ML_WIDTH + (h + 1) * LANES]
        return k, jnp.concatenate([v, ones_blk], axis=1)

    def update(d, h, cg, k, v1):
        j = d * nh + h
        m_prev = mprev[cg, d:d + 1, j:j + 1]
        m_end = jnp.maximum(m_prev, ends[cg, 0:1, j:j + 1])
        w_row = jnp.exp(arow[cg, j:j + 1, :] - m_end) * ML_SCALE
        ktw = (k.T.astype(F32) * w_row).astype(BF16)
        crun[d, h] = jnp.exp(m_prev - m_end) * crun[d, h] + _dot(ktw, v1)

    def bwd_pass(u_ref, nch, goff, keep):
        def body(i, carry):
            c = nch - 1 - i
            rows = _chunk_rows(c)
            for h in range(nh):
                if keep:
                    cbk[goff + c, h] = crun[1, h].astype(BF16)
                k, v1 = load_kv(u_ref, rows, h)
                update(1, h, goff + c, k, v1)
            return carry
        lax.fori_loop(0, nch, body, 0, unroll=UNROLL)

    def fwd_pass(u_ref, y_ref, nch, goff):
        def body(c, carry):
            rows = _chunk_rows(c)
            cg = goff + c
            for h in range(nh):
                k, v1 = load_kv(u_ref, rows, h)
                if y_ref is not None:
                    q = u_ref[rows, h * LANES:(h + 1) * LANES]
                    o = u_ref[rows, 3 * ML_WIDTH + h * LANES:3 * ML_WIDTH + (h + 1) * LANES]
                    sqk = _dot_nt(q, k) * ML_SCALE
                    qc = _dot(q, jnp.concatenate([crun[0, h].astype(BF16), cbk[cg, h]], axis=1))
                    s_dirs, stats = [], []
                    for d, mask in ((0, low), (1, upp)):
                        j = d * nh + h
                        rep = _dot(packed[cg], ex_ref[:, 2 * j * LANES:2 * (j + 1) * LANES])
                        m_prev = mprev[cg, d:d + 1, j:j + 1]
                        m_t = jnp.maximum(rep[:, 0:LANES], m_prev)
                        s_dirs.append(sqk * jnp.exp(jnp.where(mask, arow[cg, j:j + 1, :] - m_t, neg_inf)))
                        stats.append((m_prev, m_t, rep[:, LANES:2 * LANES]))
                    sv = _dot(jnp.concatenate(s_dirs, axis=0).astype(BF16), v1)
                    hsum = None
                    for d, (m_prev, m_t, b_t) in enumerate(stats):
                        g = jnp.exp(m_prev - m_t)
                        sv_d = sv[d * CHUNK:(d + 1) * CHUNK, :]
                        qc_d = qc[:, 2 * d * LANES:2 * (d + 1) * LANES]
                        num = sv_d[:, 0:LANES] + g * qc_d[:, 0:LANES]
                        den = sv_d[:, LANES:2 * LANES] + g * qc_d[:, LANES:2 * LANES]
                        hd = num / jnp.maximum(jnp.abs(den), jnp.exp(-(b_t + m_t)))
                        hsum = hd if hsum is None else hsum + hd
                    ms = jnp.mean(hsum * hsum, axis=1, keepdims=True)
                    y = hsum * lax.rsqrt(ms + EPS) * ng_ref[:, h * LANES:(h + 1) * LANES]
                    y = y * _sigmoid(o.astype(F32))
                    y_ref[rows, h * LANES:(h + 1) * LANES] = y.astype(y_ref.dtype)
                update(0, h, cg, k, v1)
            return carry
        lax.fori_loop(0, nch, body, 0, unroll=UNROLL)

    crun[...] = jnp.zeros(crun.shape, F32)
    bwd_pass(uc_ref, ncc, 0, need_ctx_out)
    bwd_pass(ul_ref, ncl, ncc, True)
    fwd_pass(uc_ref, yc_ref, ncc, 0)
    fwd_pass(ul_ref, yl_ref, ncl, ncc)


def _ml_lane_tables():
    shifts = np.zeros((LANES, 6 * LANES), np.float32)
    expand = np.zeros((LANES, ML_GATES * 2 * LANES), np.float32)
    for j in range(ML_GATES):
        for i in range(6):
            shifts[j, i * LANES + ML_GATES * i + j] = 1.0
            half = i // 3
            expand[ML_GATES * i + j, (2 * j + half) * LANES:(2 * j + half + 1) * LANES] = 1.0
    return jnp.asarray(shifts, BF16), jnp.asarray(expand, BF16)


def _mlstm_mixer(uml_l, uml_c, g_l, g_c, pvec, norm_g, need_ctx_out):
    shifts, expand = _ml_lane_tables()
    bsz, n_lat, _ = uml_l.shape
    n_ctx = uml_c.shape[1]
    nct = (n_lat + n_ctx) // CHUNK
    kern = functools.partial(_mlstm_kernel, need_ctx_out=need_ctx_out, n_lat=n_lat, n_ctx=n_ctx)
    out_specs = [pl.BlockSpec((None, n_lat, ML_WIDTH), lambda b: (b, 0, 0))]
    out_shape = [jax.ShapeDtypeStruct((bsz, n_lat, ML_WIDTH), BF16)]
    if need_ctx_out:
        out_specs.append(pl.BlockSpec((None, n_ctx, ML_WIDTH), lambda b: (b, 0, 0)))
        out_shape.append(jax.ShapeDtypeStruct((bsz, n_ctx, ML_WIDTH), BF16))
    outs = pl.pallas_call(
        kern,
        grid=(bsz,),
        in_specs=[
            pl.BlockSpec((None, n_lat, U_ML), lambda b: (b, 0, 0)),
            pl.BlockSpec((None, n_ctx, U_ML), lambda b: (b, 0, 0)),
            pl.BlockSpec((None, n_lat, U_GATE), lambda b: (b, 0, 0)),
            pl.BlockSpec((None, n_ctx, U_GATE), lambda b: (b, 0, 0)),
            pl.BlockSpec((8, LANES), lambda b: (0, 0)),
            pl.BlockSpec((1, ML_WIDTH), lambda b: (0, 0)),
            pl.BlockSpec(shifts.shape, lambda b: (0, 0)),
            pl.BlockSpec(expand.shape, lambda b: (0, 0)),
        ],
        out_specs=out_specs,
        out_shape=out_shape,
        scratch_shapes=[
            pltpu.VMEM((nct, LANES, CHUNK), F32),
            pltpu.VMEM((nct, CHUNK, LANES), BF16),
            pltpu.VMEM((nct, 8, LANES), F32),
            pltpu.VMEM((nct, 8, LANES), F32),
            pltpu.VMEM((nct, ML_HEADS, ML_HEAD_DIM, 2 * LANES), BF16),
            pltpu.VMEM((N_DIR, ML_HEADS, ML_HEAD_DIM, 2 * LANES), F32),
        ],
        compiler_params=pltpu.CompilerParams(
            dimension_semantics=("arbitrary",), vmem_limit_bytes=VMEM_LIMIT),
        name="mlstm",
    )(uml_l, uml_c, g_l, g_c, pvec, norm_g, shifts, expand)
    return (outs[0], outs[1]) if need_ctx_out else (outs[0], None)


def _ssd_kernel(ul_ref, uc_ref, gl_ref, gc_ref, pv_ref, cw_ref, cb_ref, dsk_ref, ng_ref, ex_ref, *rest,
                need_ctx_out, n_lat, n_ctx):
    if need_ctx_out:
        yl_ref, yc_ref, xact, acum, wsc, act, dtt, dec, hbk, hrun = rest
    else:
        yl_ref, xact, acum, wsc, act, dtt, dec, hbk, hrun = rest
        yc_ref = None
    ncl, ncc = n_lat // CHUNK, n_ctx // CHUNK
    nh, hg = SSD_HEADS, SSD_HEADS // SSD_GROUPS
    gw = hg * SSD_HEAD_DIM
    gn = SSD_GROUPS * SSD_STATE
    ti = _iota((CHUNK, CHUNK), 0)
    si = _iota((CHUNK, CHUNK), 1)
    low = si <= ti
    upp = si >= ti
    tri_low = jnp.where(low, 1.0, 0.0).astype(BF16)
    tri_upp = jnp.where(upp, 1.0, 0.0).astype(BF16)
    lane = _iota((CHUNK, LANES), 1)
    fwd_lane = (lane >= DT_LANE0) & (lane < DT_LANE0 + nh)
    bias0 = pv_ref[0:1, :]
    a_row = -jnp.exp(pv_ref[2:3, :])
    row_id = _iota((CHUNK, SSD_XBC), 0)
    neg_inf = -jnp.inf
    block_mask = jnp.where((_iota((gn, SSD_WIDTH), 0) < SSD_STATE) == (_iota((gn, SSD_WIDTH), 1) < gw),
                           1.0, 0.0)
    lane_group0 = lane < SSD_STATE

    def prep(u_ref, g_ref, nch, goff, nrows):
        def body(c, carry):
            r = pl.multiple_of(c * CHUNK, CHUNK)
            cur = u_ref[pl.ds(r, CHUNK), SSD_WIDTH:U_SSD].astype(F32)
            rp = pl.multiple_of(jnp.maximum(r - 16, 0), 16)
            rn = pl.multiple_of(jnp.minimum(r + CHUNK, nrows - 16), 16)
            prev = u_ref[pl.ds(rp, 16), SSD_WIDTH:U_SSD].astype(F32)[15:16, :]
            nxt = u_ref[pl.ds(rn, 16), SSD_WIDTH:U_SSD].astype(F32)[0:1, :]
            prev = jnp.where(c > 0, prev, 0.0)
            nxt = jnp.where(c < nch - 1, nxt, 0.0)
            xm = jnp.where(row_id == 0, prev, pltpu.roll(cur, 1, 0))
            xp = jnp.where(row_id == CHUNK - 1, nxt, pltpu.roll(cur, CHUNK - 1, 0))
            conv = cw_ref[0:1, :] * xm + cw_ref[1:2, :] * cur + cw_ref[2:3, :] * xp + cb_ref[...]
            xact[goff + c] = (conv * _sigmoid(conv)).astype(BF16)

            dt = _softplus(g_ref[pl.ds(r, CHUNK), :] + bias0)
            la = dt * a_row
            ac = jnp.where(fwd_lane, _tri_cumsum(tri_low, la), _tri_cumsum(tri_upp, la))
            a_end = jnp.where(fwd_lane[0:1, :], ac[CHUNK - 1:CHUNK, :], ac[0:1, :])
            acum[goff + c] = ac
            wsc[goff + c] = jnp.exp(a_end - ac) * dt
            act[goff + c] = ac.T
            dtt[goff + c] = dt.T
            dec[goff + c] = jnp.broadcast_to(jnp.exp(a_end), (8, LANES))
            return carry
        lax.fori_loop(0, nch, body, 0, unroll=UNROLL)

    prep(uc_ref, gc_ref, ncc, 0, n_ctx)
    prep(ul_ref, gl_ref, ncl, ncc, n_lat)

    def update_state(d, cg):
        onehot = ex_ref[:, d * SSD_WIDTH:(d + 1) * SSD_WIDTH]
        xa = xact[cg]
        wx = (_expand(wsc[cg], onehot) * xa[:, 0:SSD_WIDTH].astype(F32)).astype(BF16)
        upd = _dot_tn(xa[:, SSD_WIDTH:SSD_WIDTH + gn], wx)
        decay = _expand(dec[cg], onehot)[0:1, :]
        hrun[d] = (decay * hrun[d] + upd) * block_mask

    def bwd_pass(nch, goff, keep):
        def body(i, carry):
            cg = goff + nch - 1 - i
            if keep:
                hbk[cg] = hrun[1]
            update_state(1, cg)
            return carry
        lax.fori_loop(0, nch, body, 0, unroll=UNROLL)

    def fwd_pass(u_ref, y_ref, nch, goff):
        def body(c, carry):
            cg = goff + c
            if y_ref is not None:
                rows = _chunk_rows(c)
                xa = xact[cg]
                x = xa[:, 0:SSD_WIDTH]
                b_all = xa[:, SSD_WIDTH:SSD_WIDTH + gn]
                c_all = xa[:, SSD_WIDTH + gn:SSD_XBC]
                ac = acum[cg]
                ac_t = act[cg]
                dt_t = dtt[cg]
                lane_w = _iota((CHUNK, gw), 1)
                y_groups = []
                for g in range(SSD_GROUPS):
                    c_g = jnp.where(lane_group0 == (g == 0), c_all, jnp.zeros_like(c_all))
                    cb = _dot_nt(c_g, b_all)
                    s_heads = []
                    for hh in range(hg):
                        jf = DT_LANE0 + g * hg + hh
                        jb = jf + nh
                        ef = jnp.exp(jnp.where(low, ac[:, jf:jf + 1] - ac_t[jf:jf + 1, :], neg_inf))
                        eb = jnp.exp(jnp.where(upp, ac[:, jb:jb + 1] - ac_t[jb:jb + 1, :], neg_inf))
                        s_heads.append((cb * (ef * dt_t[jf:jf + 1, :] + eb * dt_t[jb:jb + 1, :])).astype(BF16))
                    x_g = x[:, g * gw:(g + 1) * gw]
                    bd = jnp.concatenate(
                        [jnp.where((lane_w >= hh * SSD_HEAD_DIM) & (lane_w < (hh + 1) * SSD_HEAD_DIM),
                                   x_g, jnp.zeros_like(x_g)) for hh in range(hg)], axis=0)
                    y_groups.append(_dot(jnp.concatenate(s_heads, axis=1), bd))
                y = jnp.concatenate(y_groups, axis=1)
                e = jnp.exp(ac)
                y = y + _dot(c_all, hrun[0].astype(BF16)) * _expand(e, ex_ref[:, 0:SSD_WIDTH])
                y = y + _dot(c_all, hbk[cg].astype(BF16)) * _expand(e, ex_ref[:, SSD_WIDTH:2 * SSD_WIDTH])
                y = y + dsk_ref[...] * x.astype(F32)
                z = u_ref[rows, 0:SSD_WIDTH].astype(F32)
                yz = y * (z * _sigmoid(z))
                ms = jnp.mean(yz * yz, axis=1, keepdims=True)
                y_ref[rows, :] = (yz * lax.rsqrt(ms + EPS) * ng_ref[...]).astype(y_ref.dtype)
            update_state(0, cg)
            return carry
        lax.fori_loop(0, nch, body, 0, unroll=UNROLL)

    hrun[...] = jnp.zeros(hrun.shape, F32)
    bwd_pass(ncc, 0, need_ctx_out)
    bwd_pass(ncl, ncc, True)
    fwd_pass(uc_ref, yc_ref, ncc, 0)
    fwd_pass(ul_ref, yl_ref, ncl, ncc)


def _ssd_mixer(ussd_l, ussd_c, g_l, g_c, pvec, conv_w, conv_b, d_skip, norm_g, onehot, need_ctx_out):
    bsz, n_lat, _ = ussd_l.shape
    n_ctx = ussd_c.shape[1]
    nct = (n_lat + n_ctx) // CHUNK
    gn = SSD_GROUPS * SSD_STATE
    kern = functools.partial(_ssd_kernel, need_ctx_out=need_ctx_out, n_lat=n_lat, n_ctx=n_ctx)
    out_specs = [pl.BlockSpec((None, n_lat, SSD_WIDTH), lambda b: (b, 0, 0))]
    out_shape = [jax.ShapeDtypeStruct((bsz, n_lat, SSD_WIDTH), BF16)]
    if need_ctx_out:
        out_specs.append(pl.BlockSpec((None, n_ctx, SSD_WIDTH), lambda b: (b, 0, 0)))
        out_shape.append(jax.ShapeDtypeStruct((bsz, n_ctx, SSD_WIDTH), BF16))
    outs = pl.pallas_call(
        kern,
        grid=(bsz,),
        in_specs=[
            pl.BlockSpec((None, n_lat, U_SSD), lambda b: (b, 0, 0)),
            pl.BlockSpec((None, n_ctx, U_SSD), lambda b: (b, 0, 0)),
            pl.BlockSpec((None, n_lat, LANES), lambda b: (b, 0, 0)),
            pl.BlockSpec((None, n_ctx, LANES), lambda b: (b, 0, 0)),
            pl.BlockSpec((8, LANES), lambda b: (0, 0)),
            pl.BlockSpec((3, SSD_XBC), lambda b: (0, 0)),
            pl.BlockSpec((1, SSD_XBC), lambda b: (0, 0)),
            pl.BlockSpec((1, SSD_WIDTH), lambda b: (0, 0)),
            pl.BlockSpec((1, SSD_WIDTH), lambda b: (0, 0)),
            pl.BlockSpec((LANES, 2 * SSD_WIDTH), lambda b: (0, 0)),
        ],
        out_specs=out_specs,
        out_shape=out_shape,
        scratch_shapes=[
            pltpu.VMEM((nct, CHUNK, SSD_XBC), BF16),
            pltpu.VMEM((nct, CHUNK, LANES), F32),
            pltpu.VMEM((nct, CHUNK, LANES), F32),
            pltpu.VMEM((nct, LANES, CHUNK), F32),
            pltpu.VMEM((nct, LANES, CHUNK), F32),
            pltpu.VMEM((nct, 8, LANES), F32),
            pltpu.VMEM((nct, gn, SSD_WIDTH), F32),
            pltpu.VMEM((N_DIR, gn, SSD_WIDTH), F32),
        ],
        compiler_params=pltpu.CompilerParams(
            dimension_semantics=("arbitrary",), vmem_limit_bytes=VMEM_LIMIT),
        name="ssd",
    )(ussd_l, ussd_c, g_l, g_c, pvec, conv_w, conv_b, d_skip, norm_g, onehot)
    return (outs[0], outs[1]) if need_ctx_out else (outs[0], None)


def _head_norm(x, gain):
    lane_lo = _iota((x.shape[0], LANES), 1) < ATT_HEAD_DIM
    outs = []
    for j in range(x.shape[1] // LANES):
        blk = x[:, j * LANES:(j + 1) * LANES]
        sq = blk * blk
        s0 = jnp.sum(jnp.where(lane_lo, sq, 0.0), axis=1, keepdims=True)
        s1 = jnp.sum(jnp.where(lane_lo, 0.0, sq), axis=1, keepdims=True)
        ms = jnp.where(lane_lo, s0, s1) * (1.0 / ATT_HEAD_DIM)
        outs.append(blk * lax.rsqrt(ms + EPS) * gain)
    return outs[0] if len(outs) == 1 else jnp.concatenate(outs, axis=1)


def _rope(x, cos_t, sin_t):
    w = x.shape[1]
    reps = w // LANES
    nfreq = ATT_HEAD_DIM // 4
    partner = jnp.where((_iota(x.shape, 1) & (2 * nfreq - 1)) < nfreq,
                        pltpu.roll(x, w - nfreq, 1), pltpu.roll(x, nfreq, 1))
    if reps > 1:
        cos_t = jnp.concatenate([cos_t] * reps, axis=1)
        sin_t = jnp.concatenate([sin_t] * reps, axis=1)
    return x * cos_t + partner * sin_t


def _dup_heads(x):
    lane_lo = _iota(x.shape, 1) < ATT_HEAD_DIM
    sw = pltpu.roll(x, ATT_HEAD_DIM, 1)
    return jnp.concatenate([jnp.where(lane_lo, x, sw), jnp.where(lane_lo, sw, x)], axis=1)


def _attn_kernel(ul_ref, uc_ref, cos_ref, sin_ref, qg_ref, kg_ref, sink_ref, *rest,
                 need_ctx_out, n_lat, n_ctx):
    if need_ctx_out:
        yl_ref, yc_ref, qp, kb, vb, kc, vc, qcp = rest
    else:
        yl_ref, qp, kb, vb, kc, vc = rest
        yc_ref = qcp = None
    ncl, ncc = n_lat // CHUNK, n_ctx // CHUNK
    rep = ATT_Q_HEADS // ATT_KV_HEADS
    kv0 = ATT_WIDTH
    lane_lo = _iota((CHUNK, LANES), 1) < ATT_HEAD_DIM
    neg_inf = -jnp.inf
    qg = qg_ref[...]
    kg = kg_ref[...]

    for ref in (kb, vb):
        zeros_blk = jnp.zeros((CHUNK, ref.shape[1]), BF16)
        ref[0:CHUNK, :] = zeros_blk
        ref[CHUNK + n_lat:2 * CHUNK + n_lat, :] = zeros_blk

    def values_with_ones(v):
        d = _dup_heads(v).astype(BF16)
        ones = jnp.ones((v.shape[0], LANES), BF16)
        return jnp.concatenate([d[:, 0:LANES], ones, d[:, LANES:2 * LANES], ones], axis=1)

    def prep_lat(c, carry):
        rows = _chunk_rows(c)
        cos_t = cos_ref[rows, :]
        sin_t = sin_ref[rows, :]
        q = _rope(_head_norm(ul_ref[rows, 0:ATT_WIDTH].astype(F32), qg), cos_t, sin_t)
        qp[rows, :] = (q * ATT_SCALE).astype(BF16)
        k = _rope(_head_norm(ul_ref[rows, kv0:kv0 + LANES].astype(F32), kg), cos_t, sin_t)
        v = ul_ref[rows, kv0 + LANES:kv0 + 2 * LANES].astype(F32)
        dst = pl.ds(pl.multiple_of((c + 1) * CHUNK, CHUNK), CHUNK)
        kb[dst, :] = _dup_heads(k).astype(BF16)
        vb[dst, :] = values_with_ones(v)
        return carry
    lax.fori_loop(0, ncl, prep_lat, 0, unroll=UNROLL)

    def prep_ctx(c, carry):
        rows = _chunk_rows(c)
        k = _head_norm(uc_ref[rows, kv0:kv0 + LANES].astype(F32), kg)
        v = uc_ref[rows, kv0 + LANES:kv0 + 2 * LANES].astype(F32)
        kc[rows, :] = _dup_heads(k).astype(BF16)
        vc[rows, :] = values_with_ones(v)
        if need_ctx_out:
            q = _head_norm(uc_ref[rows, 0:ATT_WIDTH].astype(F32), qg)
            qcp[rows, :] = (q * ATT_SCALE).astype(BF16)
        return carry
    lax.fori_loop(0, ncc, prep_ctx, 0, unroll=UNROLL)

    def attend(q_ref, y_ref, rows, band):
        for g in range(ATT_KV_HEADS):
            kl = slice(g * LANES, (g + 1) * LANES)
            vl = slice(2 * g * LANES, 2 * (g + 1) * LANES)
            qms = []
            for r in range(rep):
                hq = g * rep + r
                qb = q_ref[rows, (hq // 2) * LANES:(hq // 2 + 1) * LANES]
                qms.append(jnp.where(lane_lo != (hq % 2 == 1), qb, jnp.zeros_like(qb)))
            qs = jnp.concatenate(qms, axis=0)
            s_c = _dot_nt(qs, kc[:, kl])
            if band is not None:
                band_rows, valid = band
                s_b = _dot_nt(qs, kb[band_rows, kl])
            ps, sink_terms = [], []
            for r in range(rep):
                sl = slice(r * CHUNK, (r + 1) * CHUNK)
                sk = sink_ref[g * rep + r]
                s = s_c[sl]
                if band is not None:
                    s = jnp.concatenate([s, jnp.where(valid, s_b[sl], neg_inf)], axis=1)
                m = jnp.maximum(jnp.max(s, axis=1, keepdims=True), sk)
                ps.append(jnp.exp(s - m).astype(BF16))
                sink_terms.append(jnp.exp(sk - m))
            p = jnp.concatenate(ps, axis=0)
            acc = _dot(p[:, 0:n_ctx], vc[:, vl])
            if band is not None:
                acc = acc + _dot(p[:, n_ctx:], vb[band_rows, vl])
            for r in range(rep):
                sl = slice(r * CHUNK, (r + 1) * CHUNK)
                o = acc[sl, 0:LANES] / (acc[sl, LANES:2 * LANES] + sink_terms[r])
                if r % 2 == 1:
                    jb = (g * rep + r) // 2
                    y_ref[rows, jb * LANES:(jb + 1) * LANES] = jnp.where(lane_lo, o_even, o).astype(y_ref.dtype)
                o_even = o

    qq = _iota((CHUNK, 3 * CHUNK), 0)
    kk = _iota((CHUNK, 3 * CHUNK), 1)
    in_window = (kk >= qq) & (kk <= qq + 2 * WINDOW)

    def lat_block(i, carry):
        kpos = kk + (i - 1) * CHUNK
        valid = in_window & (kpos >= 0) & (kpos < n_lat)
        band_rows = pl.ds(pl.multiple_of(i * CHUNK, CHUNK), 3 * CHUNK)
        attend(qp, yl_ref, _chunk_rows(i), (band_rows, valid))
        return carry
    lax.fori_loop(0, ncl, lat_block, 0, unroll=UNROLL)

    if need_ctx_out:
        def ctx_block(i, carry):
            attend(qcp, yc_ref, _chunk_rows(i), None)
            return carry
        lax.fori_loop(0, ncc, ctx_block, 0, unroll=UNROLL)


def _attn_mixer(uatt_l, uatt_c, cos_t, sin_t, qgain, kgain, sink, need_ctx_out):
    bsz, n_lat, _ = uatt_l.shape
    n_ctx = uatt_c.shape[1]
    kern = functools.partial(_attn_kernel, need_ctx_out=need_ctx_out, n_lat=n_lat, n_ctx=n_ctx)
    out_specs = [pl.BlockSpec((None, n_lat, ATT_WIDTH), lambda b: (b, 0, 0))]
    out_shape = [jax.ShapeDtypeStruct((bsz, n_lat, ATT_WIDTH), BF16)]
    scratch = [
        pltpu.VMEM((n_lat, ATT_WIDTH), BF16),
        pltpu.VMEM((n_lat + 2 * CHUNK, 2 * LANES), BF16),
        pltpu.VMEM((n_lat + 2 * CHUNK, 4 * LANES), BF16),
        pltpu.VMEM((n_ctx, 2 * LANES), BF16),
        pltpu.VMEM((n_ctx, 4 * LANES), BF16),
    ]
    if need_ctx_out:
        out_specs.append(pl.BlockSpec((None, n_ctx, ATT_WIDTH), lambda b: (b, 0, 0)))
        out_shape.append(jax.ShapeDtypeStruct((bsz, n_ctx, ATT_WIDTH), BF16))
        scratch.append(pltpu.VMEM((n_ctx, ATT_WIDTH), BF16))
    outs = pl.pallas_call(
        kern,
        grid=(bsz,),
        in_specs=[
            pl.BlockSpec((None, n_lat, U_ATT), lambda b: (b, 0, 0)),
            pl.BlockSpec((None, n_ctx, U_ATT), lambda b: (b, 0, 0)),
            pl.BlockSpec((n_lat, LANES), lambda b: (0, 0)),
            pl.BlockSpec((n_lat, LANES), lambda b: (0, 0)),
            pl.BlockSpec((1, LANES), lambda b: (0, 0)),
            pl.BlockSpec((1, LANES), lambda b: (0, 0)),
            pl.BlockSpec(memory_space=pltpu.SMEM),
        ],
        out_specs=out_specs,
        out_shape=out_shape,
        scratch_shapes=scratch,
        compiler_params=pltpu.CompilerParams(
            dimension_semantics=("arbitrary",), vmem_limit_bytes=VMEM_LIMIT),
        name="attention",
    )(uatt_l, uatt_c, cos_t, sin_t, qgain, kgain, sink)
    return (outs[0], outs[1]) if need_ctx_out else (outs[0], None)


def _in_layout():
    names = (('ml_q', ML_WIDTH), ('ml_k', ML_WIDTH), ('ml_v', ML_WIDTH), ('ml_o', ML_WIDTH),
             ('ml_i', N_DIR * ML_HEADS), ('ml_f', N_DIR * ML_HEADS),
             ('ssd_z', SSD_WIDTH), ('ssd_xbc', SSD_XBC), ('ssd_dt', N_DIR * SSD_HEADS),
             ('att_q', ATT_WIDTH), ('att_k', ATT_KV_WIDTH), ('att_v', ATT_KV_WIDTH))
    out, off = {}, 0
    for name, width in names:
        out[name] = (off, off + width)
        off += width
    return out


def _permute_cols(a):
    lay = _in_layout()
    take = lambda name: a[..., lay[name][0]:lay[name][1]]
    zeros = lambda n: jnp.zeros(a.shape[:-1] + (n,), a.dtype)
    n_f, n_dt, n_i = N_DIR * ML_HEADS, N_DIR * SSD_HEADS, N_DIR * ML_HEADS
    return jnp.concatenate(
        [take('ml_q'), take('ml_k'), take('ml_v'), take('ml_o'),
         take('ssd_z'), take('ssd_xbc'),
         take('att_q'), take('att_k'), take('att_v'),
         take('ml_f'), take('ssd_dt'), zeros(LANES - n_f - n_dt),
         take('ml_i'), zeros(LANES - n_i)], axis=-1)


def _gate_rows(ml_b_i, ml_b_f, ssd_dt_bias, ssd_a_log):
    n_f, n_dt = N_DIR * ML_HEADS, N_DIR * SSD_HEADS
    row0 = jnp.concatenate([ml_b_f.reshape(-1), ssd_dt_bias.reshape(-1), jnp.zeros((LANES - n_f - n_dt,), F32)])
    row1 = jnp.concatenate([ml_b_i.reshape(-1), jnp.zeros((LANES - n_f,), F32)])
    row2 = jnp.concatenate([jnp.zeros((n_f,), F32), ssd_a_log.reshape(-1), jnp.zeros((LANES - n_f - n_dt,), F32)])
    return jnp.concatenate([row0[None], row1[None], row2[None], jnp.zeros((5, LANES), F32)], axis=0)


def _ssd_onehot():
    m = np.zeros((LANES, N_DIR * SSD_WIDTH), np.float32)
    for d in range(N_DIR):
        for h in range(SSD_HEADS):
            c0 = d * SSD_WIDTH + h * SSD_HEAD_DIM
            m[DT_LANE0 + d * SSD_HEADS + h, c0:c0 + SSD_HEAD_DIM] = 1.0
    return jnp.asarray(m, BF16)


def _rope_tables(n_lat):
    nfreq = ATT_HEAD_DIM // 4
    pos = np.arange(n_lat)
    rows, cols = pos // GRID_W, pos % GRID_W
    inv = jnp.asarray(ROPE_BASE, F32) ** (-jnp.arange(nfreq, dtype=F32) / nfreq)
    ang_r = jnp.asarray(rows, F32)[:, None] * inv
    ang_c = jnp.asarray(cols, F32)[:, None] * inv
    cos_h = jnp.concatenate([jnp.cos(ang_r)] * 2 + [jnp.cos(ang_c)] * 2, axis=1)
    sin_h = jnp.concatenate([-jnp.sin(ang_r), jnp.sin(ang_r), -jnp.sin(ang_c), jnp.sin(ang_c)], axis=1)
    return jnp.concatenate([cos_h] * 2, axis=1), jnp.concatenate([sin_h] * 2, axis=1)


def _token_mixers(u_l, u_c, p, layer, tables, need_ctx_out):
    uml_l, ussd_l, uatt_l, g_l = u_l
    uml_c, ussd_c, uatt_c, g_c = u_c
    pvec = _gate_rows(p['ml_b_i'][layer], p['ml_b_f'][layer], p['ssd_dt_bias'][layer], p['ssd_a_log'][layer])
    m_l, m_c = _mlstm_mixer(uml_l, uml_c, g_l, g_c, pvec, p['ml_norm_g'][layer][None], need_ctx_out)
    s_l, s_c = _ssd_mixer(ussd_l, ussd_c, g_l, g_c, pvec, p['ssd_conv_w'][layer], p['ssd_conv_b'][layer][None],
                          jnp.repeat(p['ssd_d'][layer], SSD_HEAD_DIM)[None], p['ssd_norm_g'][layer][None],
                          tables['onehot'], need_ctx_out)
    a_l, a_c = _attn_mixer(uatt_l, uatt_c, tables['cos'], tables['sin'],
                           jnp.tile(p['att_qn_g'][layer], 2)[None], jnp.tile(p['att_kn_g'][layer], 2)[None],
                           p['att_sink'][layer], need_ctx_out)
    return (m_l, s_l, a_l), (m_c, s_c, a_c)


def kernel(x, c, ctx, c_ctx, w_mod, b_mod, norm1_g, w_in, ml_b_i, ml_b_f, ml_norm_g, ssd_conv_w, ssd_conv_b,
           ssd_a_log, ssd_dt_bias, ssd_d, ssd_norm_g, att_qn_g, att_kn_g, att_sink, w_out, norm2_g, w_up, w_down):
    p = dict(ml_b_i=ml_b_i, ml_b_f=ml_b_f, ml_norm_g=ml_norm_g, ssd_conv_w=ssd_conv_w, ssd_conv_b=ssd_conv_b,
             ssd_a_log=ssd_a_log, ssd_dt_bias=ssd_dt_bias, ssd_d=ssd_d, ssd_norm_g=ssd_norm_g,
             att_qn_g=att_qn_g, att_kn_g=att_kn_g, att_sink=att_sink)
    bsz, n_lat, d = x.shape
    n_ctx = ctx.shape[1]
    depth = w_mod.shape[0]
    mod_rows = 16
    cc = jnp.concatenate([c, c_ctx[None], jnp.zeros((mod_rows - bsz - 1, d), F32)], axis=0)
    mods = _modulation(cc, w_mod, b_mod).reshape(depth, mod_rows, 6, d)
    cos_t, sin_t = _rope_tables(n_lat)
    tables = dict(cos=cos_t, sin=sin_t, onehot=_ssd_onehot())
    lat_row = lambda b: b
    ctx_row = lambda b: bsz
    h_ctx = ctx
    for layer in range(depth):
        need_ctx_out = layer < depth - 1
        w = _permute_cols(w_in[layer]).astype(BF16)
        g1 = norm1_g[layer][None]
        u_l = _in_proj(x, mods[layer], lat_row, g1, w, tm=256)
        u_c = _in_proj(h_ctx, mods[layer], ctx_row, g1, w, tm=256)
        y_l, y_c = _token_mixers(u_l, u_c, p, layer, tables, need_ctx_out)
        wo, wup, wdn = w_out[layer].astype(BF16), w_up[layer].astype(BF16), w_down[layer].astype(BF16)
        g2 = norm2_g[layer][None]
        x = _post(x, *y_l, mods[layer], lat_row, g2, wo, wup, wdn, tm=512)
        if need_ctx_out:
            h_ctx = _post(h_ctx, *y_c, mods[layer], ctx_row, g2, wo, wup, wdn, tm=256)
    return x
```

```python
import functools
import math

import numpy as np
import jax
import jax.numpy as jnp
from jax import lax
from jax.experimental import pallas as pl
from jax.experimental.pallas import tpu as pltpu

F32 = jnp.float32
BF16 = jnp.bfloat16

EPS = 1e-6
GRID_W = 64
N_DIR = 2
ML_HEADS = 4
ML_HEAD_DIM = 128
ML_WIDTH = ML_HEADS * ML_HEAD_DIM
SSD_HEADS = 8
SSD_HEAD_DIM = 64
SSD_WIDTH = SSD_HEADS * SSD_HEAD_DIM
SSD_STATE = 64
SSD_GROUPS = 2
SSD_XBC = SSD_WIDTH + 2 * SSD_GROUPS * SSD_STATE
ATT_Q_HEADS = 8
ATT_KV_HEADS = 2
ATT_HEAD_DIM = 64
ATT_WIDTH = ATT_Q_HEADS * ATT_HEAD_DIM
ATT_KV_WIDTH = ATT_KV_HEADS * ATT_HEAD_DIM
WINDOW = 128
ROPE_BASE = 10000.0
MIX_WIDTH = ML_WIDTH + SSD_WIDTH + ATT_WIDTH

LANES = 128
CHUNK = 128
VMEM_LIMIT = 56 * 1024 * 1024
UNROLL = 4

U_ML = 4 * ML_WIDTH
U_SSD = SSD_WIDTH + SSD_XBC
U_ATT = ATT_WIDTH + 2 * ATT_KV_WIDTH
U_GATE = 2 * LANES
U_TOTAL = U_ML + U_SSD + U_ATT + U_GATE
DT_LANE0 = N_DIR * ML_HEADS

ML_SCALE = ML_HEAD_DIM ** -0.5
ATT_SCALE = ATT_HEAD_DIM ** -0.5

_NT = (((1,), (1,)), ((), ()))
_TN = (((0,), (0,)), ((), ()))


def _dot(a, b):
    return jnp.dot(a, b, preferred_element_type=F32)


def _dot_nt(a, b):
    return lax.dot_general(a, b, _NT, preferred_element_type=F32)


def _dot_tn(a, b):
    return lax.dot_general(a, b, _TN, preferred_element_type=F32)


def _split2(x):
    hi = x.astype(BF16)
    lo = (x - hi.astype(F32)).astype(BF16)
    return hi, lo


def _split3(x):
    hi = x.astype(BF16)
    r = x - hi.astype(F32)
    mid = r.astype(BF16)
    lo = (r - mid.astype(F32)).astype(BF16)
    return hi, mid, lo


def _tri_cumsum(tri, x):
    hi, mid, lo = _split3(x)
    return _dot(tri, hi) + _dot(tri, mid) + _dot(tri, lo)


def _sigmoid(x):
    return 1.0 / (1.0 + jnp.exp(-x))


def _log_sigmoid(x):
    return jnp.minimum(x, 0.0) - jnp.log1p(jnp.exp(-jnp.abs(x)))


def _softplus(x):
    return jnp.maximum(x, 0.0) + jnp.log1p(jnp.exp(-jnp.abs(x)))


def _iota(shape, dim):
    return lax.broadcasted_iota(jnp.int32, shape, dim)


def _chunk_rows(c):
    return pl.ds(pl.multiple_of(c * CHUNK, CHUNK), CHUNK)


def _mod_kernel(s_ref, w_ref, b_ref, o_ref):
    s = s_ref[...]
    s = s * _sigmoid(s)
    w = w_ref[...]
    s_hi, s_lo = _split2(s)
    w_hi, w_lo = _split2(w)
    o_ref[...] = _dot(s_hi, w_hi) + (_dot(s_hi, w_lo) + _dot(s_lo, w_hi)) + b_ref[...]


def _modulation(cc, w_mod, b_mod):
    depth, d, d6 = w_mod.shape
    r = cc.shape[0]
    bn = 1024
    return pl.pallas_call(
        _mod_kernel,
        grid=(depth, d6 // bn),
        in_specs=[
            pl.BlockSpec((r, d), lambda l, j: (0, 0)),
            pl.BlockSpec((None, d, bn), lambda l, j: (l, 0, j)),
            pl.BlockSpec((None, 1, bn), lambda l, j: (l, 0, j)),
        ],
        out_specs=pl.BlockSpec((None, r, bn), lambda l, j: (l, 0, j)),
        out_shape=jax.ShapeDtypeStruct((depth, r, d6), F32),
        compiler_params=pltpu.CompilerParams(
            dimension_semantics=("arbitrary", "arbitrary"), vmem_limit_bytes=VMEM_LIMIT),
        name="modulation",
    )(cc, w_mod, b_mod.reshape(depth, 1, d6))


def _inproj_kernel(x_ref, mod_ref, g_ref, w_ref, oml_ref, ossd_ref, oatt_ref, og_ref):
    x = x_ref[...]
    ms = jnp.mean(x * x, axis=-1, keepdims=True)
    xn = x * lax.rsqrt(ms + EPS)
    h = xn * (g_ref[...] * (1.0 + mod_ref[1:2, :])) + mod_ref[0:1, :]
    hb = h.astype(BF16)
    step = 512
    off = 0
    for o_ref, width in ((oml_ref, U_ML), (ossd_ref, U_SSD), (oatt_ref, U_ATT), (og_ref, U_GATE)):
        for j in range(0, width, step):
            wj = min(step, width - j)
            o_ref[:, j:j + wj] = _dot(hb, w_ref[:, off + j:off + j + wj]).astype(o_ref.dtype)
        off += width


def _in_proj(x, mods, mod_row, g1, w, tm):
    bsz, t, d = x.shape
    widths = (U_ML, U_SSD, U_ATT, U_GATE)
    dtypes = (BF16, BF16, BF16, F32)
    return pl.pallas_call(
        _inproj_kernel,
        grid=(bsz, t // tm),
        in_specs=[
            pl.BlockSpec((None, tm, d), lambda b, i: (b, i, 0)),
            pl.BlockSpec((None, 6, d), lambda b, i: (mod_row(b), 0, 0)),
            pl.BlockSpec((1, d), lambda b, i: (0, 0)),
            pl.BlockSpec((d, U_TOTAL), lambda b, i: (0, 0), pipeline_mode=pl.Buffered(1)),
        ],
        out_specs=[pl.BlockSpec((None, tm, wd), lambda b, i: (b, i, 0)) for wd in widths],
        out_shape=[jax.ShapeDtypeStruct((bsz, t, wd), dt) for wd, dt in zip(widths, dtypes)],
        compiler_params=pltpu.CompilerParams(
            dimension_semantics=("arbitrary", "arbitrary"), vmem_limit_bytes=VMEM_LIMIT),
        name="in_proj",
    )(x, mods, g1, w)


def _post_kernel(x_ref, yml_ref, yssd_ref, yatt_ref, mod_ref, g_ref, wo_ref, wup_ref, wdn_ref, o_ref):
    proj = (_dot(yml_ref[...], wo_ref[0:ML_WIDTH, :])
            + _dot(yssd_ref[...], wo_ref[ML_WIDTH:ML_WIDTH + SSD_WIDTH, :])
            + _dot(yatt_ref[...], wo_ref[ML_WIDTH + SSD_WIDTH:MIX_WIDTH, :]))
    x1 = x_ref[...] + mod_ref[2:3, :] * proj
    ms = jnp.mean(x1 * x1, axis=-1, keepdims=True)
    xn = x1 * lax.rsqrt(ms + EPS)
    hb = (xn * (g_ref[...] * (1.0 + mod_ref[4:5, :])) + mod_ref[3:4, :]).astype(BF16)
    d_ff = wup_ref.shape[1]
    step = 1024
    acc = None
    for j in range(0, d_ff, step):
        up = jnp.maximum(_dot(hb, wup_ref[:, j:j + step]), 0.0)
        part = _dot((up * up).astype(BF16), wdn_ref[j:j + step, :])
        acc = part if acc is None else acc + part
    o_ref[...] = x1 + mod_ref[5:6, :] * acc


def _post(x, yml, yssd, yatt, mods, mod_row, g2, wo, wup, wdn, tm):
    bsz, t, d = x.shape
    d_ff = wup.shape[1]
    const = dict(pipeline_mode=pl.Buffered(1))
    return pl.pallas_call(
        _post_kernel,
        grid=(bsz, t // tm),
        in_specs=[
            pl.BlockSpec((None, tm, d), lambda b, i: (b, i, 0)),
            pl.BlockSpec((None, tm, ML_WIDTH), lambda b, i: (b, i, 0)),
            pl.BlockSpec((None, tm, SSD_WIDTH), lambda b, i: (b, i, 0)),
            pl.BlockSpec((None, tm, ATT_WIDTH), lambda b, i: (b, i, 0)),
            pl.BlockSpec((None, 6, d), lambda b, i: (mod_row(b), 0, 0)),
            pl.BlockSpec((1, d), lambda b, i: (0, 0)),
            pl.BlockSpec((MIX_WIDTH, d), lambda b, i: (0, 0), **const),
            pl.BlockSpec((d, d_ff), lambda b, i: (0, 0), **const),
            pl.BlockSpec((d_ff, d), lambda b, i: (0, 0), **const),
        ],
        out_specs=pl.BlockSpec((None, tm, d), lambda b, i: (b, i, 0)),
        out_shape=jax.ShapeDtypeStruct((bsz, t, d), F32),
        compiler_params=pltpu.CompilerParams(
            dimension_semantics=("arbitrary", "arbitrary"), vmem_limit_bytes=VMEM_LIMIT),
        name="post",
    )(x, yml, yssd, yatt, mods, g2, wo, wup, wdn)


ML_GATES = N_DIR * ML_HEADS


def _mlstm_kernel(ul_ref, uc_ref, gl_ref, gc_ref, pv_ref, ng_ref, sh_ref, ex_ref, *rest,
                  need_ctx_out, n_lat, n_ctx):
    if need_ctx_out:
        yl_ref, yc_ref, arow, packed, ends, mprev, cbk, crun = rest
    else:
        yl_ref, arow, packed, ends, mprev, cbk, crun = rest
        yc_ref = None
    ncl, ncc = n_lat // CHUNK, n_ctx // CHUNK
    nct = ncl + ncc
    nh = ML_HEADS
    ti = _iota((CHUNK, CHUNK), 0)
    si = _iota((CHUNK, CHUNK), 1)
    low = si <= ti
    upp = si >= ti
    tri_low = jnp.where(low, 1.0, 0.0).astype(BF16)
    tri_upp = jnp.where(upp, 1.0, 0.0).astype(BF16)
    fwd_lane = _iota((CHUNK, LANES), 1) < nh
    row = _iota((CHUNK, LANES), 0)
    bias_f = pv_ref[0:1, :]
    bias_i = pv_ref[1:2, :]
    neg_inf = -jnp.inf
    ones_blk = jnp.ones((CHUNK, LANES), BF16)

    def prep(g_ref, nch, goff):
        def body(c, carry):
            g = g_ref[_chunk_rows(c), :]
            lf = _log_sigmoid(g[:, 0:LANES] + bias_f)
            li = g[:, LANES:2 * LANES] + bias_i
            b = jnp.where(fwd_lane, _tri_cumsum(tri_low, lf), _tri_cumsum(tri_upp, lf))
            a = li - b
            cm = a
            k = 1
            while k < CHUNK:
                before = jnp.where(row >= k, pltpu.roll(cm, k, 0), neg_inf)
                after = jnp.where(row < CHUNK - k, pltpu.roll(cm, CHUNK - k, 0), neg_inf)
                cm = jnp.maximum(cm, jnp.where(fwd_lane, before, after))
                k *= 2
            arow[goff + c] = a.T
            pk = None
            for i, part in enumerate(_split3(cm) + _split3(b)):
                t = _dot(part, sh_ref[:, i * LANES:(i + 1) * LANES])
                pk = t if pk is None else pk + t
            packed[goff + c] = pk.astype(BF16)
            ends[goff + c, 0:1, :] = jnp.where(fwd_lane[0:1], cm[CHUNK - 1:CHUNK], cm[0:1])
            ends[goff + c, 1:2, :] = jnp.where(fwd_lane[0:1], b[CHUNK - 1:CHUNK], b[0:1])
            return carry
        lax.fori_loop(0, nch, body, 0, unroll=UNROLL)

    prep(gc_ref, ncc, 0)
    prep(gl_ref, ncl, ncc)

    def m_scan(lo, hi, reverse, m0):
        def body(i, m):
            cg = hi - 1 - i if reverse else lo + i
            mprev[cg, int(reverse):int(reverse) + 1, :] = m
            return ends[cg, 1:2, :] + jnp.maximum(m, ends[cg, 0:1, :])
        return lax.fori_loop(0, hi - lo, body, m0)

    m_zero = jnp.zeros((1, LANES), F32)
    m_scan(0, nct, False, m_zero)
    m_scan(ncc, nct, True, m_scan(0, ncc, True, m_zero))

    def load_kv(u_ref, rows, h):
        k = u_ref[rows, ML_WIDTH + h * LANES:ML_WIDTH + (h + 1) * LANES]
        v = u_ref[rows, 2 * ML_WIDTH + h * LANES:2 * ML_WIDTH + (h + 1) * LANES]
        return k, jnp.concatenate([v, ones_blk], axis=1)

    def update(d, h, cg, k, v1):
        j = d * nh + h
        m_prev = mprev[cg, d:d + 1, j:j + 1]
        m_end = jnp.maximum(m_prev, ends[cg, 0:1, j:j + 1])
        w_row = jnp.exp(arow[cg, j:j + 1, :] - m_end) * ML_SCALE
        ktw = (k.T.astype(F32) * w_row).astype(BF16)
        crun[d, h] = jnp.exp(m_prev - m_end) * crun[d, h] + _dot(ktw, v1)

    def bwd_pass(u_ref, nch, goff, keep):
        def body(i, carry):
            c = nch - 1 - i
            rows = _chunk_rows(c)
            for h in range(nh):
                if keep:
                    cbk[goff + c, h] = crun[1, h].astype(BF16)
                k, v1 = load_kv(u_ref, rows, h)
                update(1, h, goff + c, k, v1)
            return carry
        lax.fori_loop(0, nch, body, 0, unroll=UNROLL)

    def fwd_pass(u_ref, y_ref, nch, goff):
        def body(c, carry):
            rows = _chunk_rows(c)
            cg = goff + c
            for h in range(nh):
                k, v1 = load_kv(u_ref, rows, h)
                if y_ref is not None:
                    q = u_ref[rows, h * LANES:(h + 1) * LANES]
                    o = u_ref[rows, 3 * ML_WIDTH + h * LANES:3 * ML_WIDTH + (h + 1) * LANES]
                    sqk = _dot_nt(q, k) * ML_SCALE
                    qc = _dot(q, jnp.concatenate([crun[0, h].astype(BF16), cbk[cg, h]], axis=1))
                    s_dirs, stats = [], []
                    for d, mask in ((0, low), (1, upp)):
                        j = d * nh + h
                        rep = _dot(packed[cg], ex_ref[:, 2 * j * LANES:2 * (j + 1) * LANES])
                        m_prev = mprev[cg, d:d + 1, j:j + 1]
                        m_t = jnp.maximum(rep[:, 0:LANES], m_prev)
                        s_dirs.append(sqk * jnp.exp(jnp.where(mask, arow[cg, j:j + 1, :] - m_t, neg_inf)))
                        stats.append((m_prev, m_t, rep[:, LANES:2 * LANES]))
                    sv = _dot(jnp.concatenate(s_dirs, axis=0).astype(BF16), v1)
                    hsum = None
                    for d, (m_prev, m_t, b_t) in enumerate(stats):
                        g = jnp.exp(m_prev - m_t)
                        sv_d = sv[d * CHUNK:(d + 1) * CHUNK, :]
                        qc_d = qc[:, 2 * d * LANES:2 * (d + 1) * LANES]
                        num = sv_d[:, 0:LANES] + g * qc_d[:, 0:LANES]
                        den = sv_d[:, LANES:2 * LANES] + g * qc_d[:, LANES:2 * LANES]
                        hd = num / jnp.maximum(jnp.abs(den), jnp.exp(-(b_t + m_t)))
                        hsum = hd if hsum is None else hsum + hd
                    ms = jnp.mean(hsum * hsum, axis=1, keepdims=True)
                    y = hsum * lax.rsqrt(ms + EPS) * ng_ref[:, h * LANES:(h + 1) * LANES]
                    y = y * _sigmoid(o.astype(F32))
                    y_ref[rows, h * LANES:(h + 1) * LANES] = y.astype(y_ref.dtype)
                update(0, h, cg, k, v1)
            return carry
        lax.fori_loop(0, nch, body, 0, unroll=UNROLL)

    crun[...] = jnp.zeros(crun.shape, F32)
    bwd_pass(uc_ref, ncc, 0, need_ctx_out)
    bwd_pass(ul_ref, ncl, ncc, True)
    fwd_pass(uc_ref, yc_ref, ncc, 0)
    fwd_pass(ul_ref, yl_ref, ncl, ncc)


def _ml_lane_tables():
    shifts = np.zeros((LANES, 6 * LANES), np.float32)
    expand = np.zeros((LANES, ML_GATES * 2 * LANES), np.float32)
    for j in range(ML_GATES):
        for i in range(6):
            shifts[j, i * LANES + ML_GATES * i + j] = 1.0
            half = i // 3
            expand[ML_GATES * i + j, (2 * j + half) * LANES:(2 * j + half + 1) * LANES] = 1.0
    return jnp.asarray(shifts, BF16), jnp.asarray(expand, BF16)


def _mlstm_mixer(uml_l, uml_c, g_l, g_c, pvec, norm_g, need_ctx_out):
    shifts, expand = _ml_lane_tables()
    bsz, n_lat, _ = uml_l.shape
    n_ctx = uml_c.shape[1]
    nct = (n_lat + n_ctx) // CHUNK
    kern = functools.partial(_mlstm_kernel, need_ctx_out=need_ctx_out, n_lat=n_lat, n_ctx=n_ctx)
    out_specs = [pl.BlockSpec((None, n_lat, ML_WIDTH), lambda b: (b, 0, 0))]
    out_shape = [jax.ShapeDtypeStruct((bsz, n_lat, ML_WIDTH), BF16)]
    if need_ctx_out:
        out_specs.append(pl.BlockSpec((None, n_ctx, ML_WIDTH), lambda b: (b, 0, 0)))
        out_shape.append(jax.ShapeDtypeStruct((bsz, n_ctx, ML_WIDTH), BF16))
    outs = pl.pallas_call(
        kern,
        grid=(bsz,),
        in_specs=[
            pl.BlockSpec((None, n_lat, U_ML), lambda b: (b, 0, 0)),
            pl.BlockSpec((None, n_ctx, U_ML), lambda b: (b, 0, 0)),
            pl.BlockSpec((None, n_lat, U_GATE), lambda b: (b, 0, 0)),
            pl.BlockSpec((None, n_ctx, U_GATE), lambda b: (b, 0, 0)),
            pl.BlockSpec((8, LANES), lambda b: (0, 0)),
            pl.BlockSpec((1, ML_WIDTH), lambda b: (0, 0)),
            pl.BlockSpec(shifts.shape, lambda b: (0, 0)),
            pl.BlockSpec(expand.shape, lambda b: (0, 0)),
        ],
        out_specs=out_specs,
        out_shape=out_shape,
        scratch_shapes=[
            pltpu.VMEM((nct, LANES, CHUNK), F32),
            pltpu.VMEM((nct, CHUNK, LANES), BF16),
            pltpu.VMEM((nct, 8, LANES), F32),
            pltpu.VMEM((nct, 8, LANES), F32),
            pltpu.VMEM((nct, ML_HEADS, ML_HEAD_DIM, 2 * LANES), BF16),
            pltpu.VMEM((N_DIR, ML_HEADS, ML_HEAD_DIM, 2 * LANES), F32),
        ],
        compiler_params=pltpu.CompilerParams(
            dimension_semantics=("arbitrary",), vmem_limit_bytes=VMEM_LIMIT),
        name="mlstm",
    )(uml_l, uml_c, g_l, g_c, pvec, norm_g, shifts, expand)
    return (outs[0], outs[1]) if need_ctx_out else (outs[0], None)


def _ssd_kernel(ul_ref, uc_ref, gl_ref, gc_ref, pv_ref, cw_ref, cb_ref, dsk_ref, ng_ref, sh_ref, ex_ref, *rest,
                need_ctx_out, n_lat, n_ctx):
    if need_ctx_out:
        yl_ref, yc_ref, xact, acum, packed, act, dtt, hbk, hrun = rest
    else:
        yl_ref, xact, acum, packed, act, dtt, hbk, hrun = rest
        yc_ref = None
    ncl, ncc = n_lat // CHUNK, n_ctx // CHUNK
    nh, hg = SSD_HEADS, SSD_HEADS // SSD_GROUPS
    gw = hg * SSD_HEAD_DIM
    gn = SSD_GROUPS * SSD_STATE
    ti = _iota((CHUNK, CHUNK), 0)
    si = _iota((CHUNK, CHUNK), 1)
    low_strict = si < ti
    upp_strict = si > ti
    tri_low = jnp.where(si <= ti, 1.0, 0.0).astype(BF16)
    tri_upp = jnp.where(si >= ti, 1.0, 0.0).astype(BF16)
    lane = _iota((CHUNK, LANES), 1)
    fwd_lane = (lane >= DT_LANE0) & (lane < DT_LANE0 + nh)
    bias0 = pv_ref[0:1, :]
    a_row = -jnp.exp(pv_ref[2:3, :])
    row_id = _iota((CHUNK, SSD_XBC), 0)
    lane_group0 = lane < SSD_STATE

    def prep(u_ref, g_ref, nch, goff, nrows):
        def body(c, carry):
            r = pl.multiple_of(c * CHUNK, CHUNK)
            cur = u_ref[pl.ds(r, CHUNK), SSD_WIDTH:U_SSD].astype(F32)
            rp = pl.multiple_of(jnp.maximum(r - 16, 0), 16)
            rn = pl.multiple_of(jnp.minimum(r + CHUNK, nrows - 16), 16)
            prev = u_ref[pl.ds(rp, 16), SSD_WIDTH:U_SSD].astype(F32)[15:16, :]
            nxt = u_ref[pl.ds(rn, 16), SSD_WIDTH:U_SSD].astype(F32)[0:1, :]
            prev = jnp.where(c > 0, prev, 0.0)
            nxt = jnp.where(c < nch - 1, nxt, 0.0)
            xm = jnp.where(row_id == 0, prev, pltpu.roll(cur, 1, 0))
            xp = jnp.where(row_id == CHUNK - 1, nxt, pltpu.roll(cur, CHUNK - 1, 0))
            conv = cw_ref[0:1, :] * xm + cw_ref[1:2, :] * cur + cw_ref[2:3, :] * xp + cb_ref[...]
            xact[goff + c] = (conv * _sigmoid(conv)).astype(BF16)

            dt = _softplus(g_ref[pl.ds(r, CHUNK), :] + bias0)
            la = dt * a_row
            ac = jnp.where(fwd_lane, _tri_cumsum(tri_low, la), _tri_cumsum(tri_upp, la))
            a_end = jnp.where(fwd_lane[0:1, :], ac[CHUNK - 1:CHUNK, :], ac[0:1, :])
            acum[goff + c] = ac
            act[goff + c] = (ac - jnp.log(dt)).T
            dtt[goff + c] = dt.T
            pk = None
            for i, part in enumerate(_split3(jnp.exp(ac)) + _split3(jnp.exp(a_end - ac) * dt)):
                t = _dot(part, sh_ref[:, i * LANES:(i + 1) * LANES])
                pk = t if pk is None else pk + t
            packed[goff + c] = pk.astype(BF16)
            return carry
        lax.fori_loop(0, nch, body, 0, unroll=UNROLL)

    prep(uc_ref, gc_ref, ncc, 0, n_ctx)
    prep(ul_ref, gl_ref, ncl, ncc, n_lat)

    def e_cols(d, g):
        return slice(d * SSD_WIDTH + g * gw, d * SSD_WIDTH + (g + 1) * gw)

    def w_cols(d, g):
        return slice((N_DIR + d) * SSD_WIDTH + g * gw, (N_DIR + d) * SSD_WIDTH + (g + 1) * gw)

    def update_state(d, cg):
        pk = packed[cg]
        xa = xact[cg]
        b_all = xa[:, SSD_WIDTH:SSD_WIDTH + gn]
        zero = jnp.zeros_like(b_all)
        upd = []
        for g in range(SSD_GROUPS):
            wx = (_dot(pk, ex_ref[:, w_cols(d, g)]) * xa[:, g * gw:(g + 1) * gw].astype(F32)).astype(BF16)
            upd.append(_dot_tn(jnp.where(lane_group0 == (g == 0), b_all, zero), wx))
        tail, last = (slice(CHUNK - 16, CHUNK), 15) if d == 0 else (slice(0, 16), 0)
        decay = _dot(pk[tail], ex_ref[:, d * SSD_WIDTH:(d + 1) * SSD_WIDTH])[last:last + 1, :]
        hrun[d] = decay * hrun[d] + jnp.concatenate(upd, axis=1)

    def bwd_pass(nch, goff, keep):
        def body(i, carry):
            cg = goff + nch - 1 - i
            if keep:
                hbk[cg] = hrun[1].astype(BF16)
            update_state(1, cg)
            return carry
        lax.fori_loop(0, nch, body, 0, unroll=UNROLL)

    def fwd_pass(u_ref, y_ref, nch, goff):
        def body(c, carry):
            cg = goff + c
            if y_ref is not None:
                rows = _chunk_rows(c)
                xa = xact[cg]
                x = xa[:, 0:SSD_WIDTH]
                b_all = xa[:, SSD_WIDTH:SSD_WIDTH + gn]
                c_all = xa[:, SSD_WIDTH + gn:SSD_XBC]
                ac = acum[cg]
                ac_t = act[cg]
                dt_t = dtt[cg]
                lane_w = _iota((CHUNK, gw), 1)
                pk = packed[cg]
                yz_groups, ssq = [], None
                for g in range(SSD_GROUPS):
                    cols = slice(g * gw, (g + 1) * gw)
                    c_g = jnp.where(lane_group0 == (g == 0), c_all, jnp.zeros_like(c_all))
                    cb = _dot_nt(c_g, b_all)
                    s_heads = []
                    for hh in range(hg):
                        jf = DT_LANE0 + g * hg + hh
                        jb = jf + nh
                        on_diag = jnp.log(dt_t[jf:jf + 1, :] + dt_t[jb:jb + 1, :])
                        expo = jnp.where(low_strict, ac[:, jf:jf + 1] - ac_t[jf:jf + 1, :],
                                         jnp.where(upp_strict, ac[:, jb:jb + 1] - ac_t[jb:jb + 1, :], on_diag))
                        s_heads.append((cb * jnp.exp(expo)).astype(BF16))
                    x_g = x[:, cols]
                    bd = jnp.concatenate(
                        [jnp.where((lane_w >= hh * SSD_HEAD_DIM) & (lane_w < (hh + 1) * SSD_HEAD_DIM),
                                   x_g, jnp.zeros_like(x_g)) for hh in range(hg)], axis=0)
                    y = _dot(jnp.concatenate(s_heads, axis=1), bd)
                    y = y + _dot(c_all, hrun[0, :, cols].astype(BF16)) * _dot(pk, ex_ref[:, e_cols(0, g)])
                    y = y + _dot(c_all, hbk[cg, :, cols]) * _dot(pk, ex_ref[:, e_cols(1, g)])
                    y = y + dsk_ref[:, cols] * x_g.astype(F32)
                    z = u_ref[rows, cols].astype(F32)
                    yz = y * (z * _sigmoid(z))
                    part = jnp.sum(yz * yz, axis=1, keepdims=True)
                    ssq = part if ssq is None else ssq + part
                    yz_groups.append(yz)
                scale = lax.rsqrt(ssq * (1.0 / SSD_WIDTH) + EPS)
                for g, yz in enumerate(yz_groups):
                    cols = slice(g * gw, (g + 1) * gw)
                    y_ref[rows, cols] = (yz * scale * ng_ref[:, cols]).astype(y_ref.dtype)
            update_state(0, cg)
            return carry
        lax.fori_loop(0, nch, body, 0, unroll=UNROLL)

    hrun[...] = jnp.zeros(hrun.shape, F32)
    bwd_pass(ncc, 0, need_ctx_out)
    bwd_pass(ncl, ncc, True)
    fwd_pass(uc_ref, yc_ref, ncc, 0)
    fwd_pass(ul_ref, yl_ref, ncl, ncc)


def _ssd_lane_tables():
    nj = N_DIR * SSD_HEADS
    shifts = np.zeros((LANES, 6 * LANES), np.float32)
    expand = np.zeros((LANES, 2 * N_DIR * SSD_WIDTH), np.float32)
    for i in range(6):
        for d in range(N_DIR):
            for h in range(SSD_HEADS):
                j = d * SSD_HEADS + h
                shifts[DT_LANE0 + j, i * LANES + nj * i + j] = 1.0
                c0 = ((i // 3) * N_DIR + d) * SSD_WIDTH + h * SSD_HEAD_DIM
                expand[nj * i + j, c0:c0 + SSD_HEAD_DIM] = 1.0
    return jnp.asarray(shifts, BF16), jnp.asarray(expand, BF16)


def _ssd_mixer(ussd_l, ussd_c, g_l, g_c, pvec, conv_w, conv_b, d_skip, norm_g, need_ctx_out):
    shifts, expand = _ssd_lane_tables()
    bsz, n_lat, _ = ussd_l.shape
    n_ctx = ussd_c.shape[1]
    nct = (n_lat + n_ctx) // CHUNK
    gn = SSD_GROUPS * SSD_STATE
    kern = functools.partial(_ssd_kernel, need_ctx_out=need_ctx_out, n_lat=n_lat, n_ctx=n_ctx)
    out_specs = [pl.BlockSpec((None, n_lat, SSD_WIDTH), lambda b: (b, 0, 0))]
    out_shape = [jax.ShapeDtypeStruct((bsz, n_lat, SSD_WIDTH), BF16)]
    if need_ctx_out:
        out_specs.append(pl.BlockSpec((None, n_ctx, SSD_WIDTH), lambda b: (b, 0, 0)))
        out_shape.append(jax.ShapeDtypeStruct((bsz, n_ctx, SSD_WIDTH), BF16))
    outs = pl.pallas_call(
        kern,
        grid=(bsz,),
        in_specs=[
            pl.BlockSpec((None, n_lat, U_SSD), lambda b: (b, 0, 0)),
            pl.BlockSpec((None, n_ctx, U_SSD), lambda b: (b, 0, 0)),
            pl.BlockSpec((None, n_lat, LANES), lambda b: (b, 0, 0)),
            pl.BlockSpec((None, n_ctx, LANES), lambda b: (b, 0, 0)),
            pl.BlockSpec((8, LANES), lambda b: (0, 0)),
            pl.BlockSpec((3, SSD_XBC), lambda b: (0, 0)),
            pl.BlockSpec((1, SSD_XBC), lambda b: (0, 0)),
            pl.BlockSpec((1, SSD_WIDTH), lambda b: (0, 0)),
            pl.BlockSpec((1, SSD_WIDTH), lambda b: (0, 0)),
            pl.BlockSpec(shifts.shape, lambda b: (0, 0)),
            pl.BlockSpec(expand.shape, lambda b: (0, 0)),
        ],
        out_specs=out_specs,
        out_shape=out_shape,
        scratch_shapes=[
            pltpu.VMEM((nct, CHUNK, SSD_XBC), BF16),
            pltpu.VMEM((nct, CHUNK, LANES), F32),
            pltpu.VMEM((nct, CHUNK, LANES), BF16),
            pltpu.VMEM((nct, LANES, CHUNK), F32),
            pltpu.VMEM((nct, LANES, CHUNK), F32),
            pltpu.VMEM((nct, gn, SSD_WIDTH), BF16),
            pltpu.VMEM((N_DIR, gn, SSD_WIDTH), F32),
        ],
        compiler_params=pltpu.CompilerParams(
            dimension_semantics=("arbitrary",), vmem_limit_bytes=VMEM_LIMIT),
        name="ssd",
    )(ussd_l, ussd_c, g_l, g_c, pvec, conv_w, conv_b, d_skip, norm_g, shifts, expand)
    return (outs[0], outs[1]) if need_ctx_out else (outs[0], None)


def _head_norm(x, gain):
    rows = x.shape[0]
    blks = [x[:, j * LANES:(j + 1) * LANES] for j in range(x.shape[1] // LANES)]
    same_head = (_iota((LANES, LANES), 0) < ATT_HEAD_DIM) == (_iota((LANES, LANES), 1) < ATT_HEAD_DIM)
    sq = jnp.concatenate([b * b for b in blks], axis=0).astype(BF16)
    ms = _dot(sq, jnp.where(same_head, 1.0, 0.0).astype(BF16)) * (1.0 / ATT_HEAD_DIM)
    outs = [b * lax.rsqrt(ms[j * rows:(j + 1) * rows] + EPS) * gain for j, b in enumerate(blks)]
    return outs[0] if len(outs) == 1 else jnp.concatenate(outs, axis=1)


def _rope(x, cos_t, sin_t):
    w = x.shape[1]
    reps = w // LANES
    nfreq = ATT_HEAD_DIM // 4
    partner = jnp.where((_iota(x.shape, 1) & (2 * nfreq - 1)) < nfreq,
                        pltpu.roll(x, w - nfreq, 1), pltpu.roll(x, nfreq, 1))
    if reps > 1:
        cos_t = jnp.concatenate([cos_t] * reps, axis=1)
        sin_t = jnp.concatenate([sin_t] * reps, axis=1)
    return x * cos_t + partner * sin_t


def _dup_heads(x):
    lane_lo = _iota(x.shape, 1) < ATT_HEAD_DIM
    sw = pltpu.roll(x, ATT_HEAD_DIM, 1)
    return jnp.concatenate([jnp.where(lane_lo, x, sw), jnp.where(lane_lo, sw, x)], axis=1)


def _attn_kernel(ul_ref, uc_ref, cos_ref, sin_ref, qg_ref, kg_ref, sink_ref, *rest,
                 need_ctx_out, n_lat, n_ctx):
    if need_ctx_out:
        yl_ref, yc_ref, qp, kb, vb, kc, vc, qcp = rest
    else:
        yl_ref, qp, kb, vb, kc, vc = rest
        yc_ref = qcp = None
    ncl, ncc = n_lat // CHUNK, n_ctx // CHUNK
    rep = ATT_Q_HEADS // ATT_KV_HEADS
    kv0 = ATT_WIDTH
    lane_lo = _iota((CHUNK, LANES), 1) < ATT_HEAD_DIM
    neg_inf = -jnp.inf
    qg = qg_ref[...]
    kg = kg_ref[...]

    for ref in (kb, vb):
        zeros_blk = jnp.zeros((CHUNK, ref.shape[1]), BF16)
        ref[0:CHUNK, :] = zeros_blk
        ref[CHUNK + n_lat:2 * CHUNK + n_lat, :] = zeros_blk

    def values_with_ones(v):
        d = _dup_heads(v).astype(BF16)
        ones = jnp.ones((v.shape[0], LANES), BF16)
        return jnp.concatenate([d[:, 0:LANES], ones, d[:, LANES:2 * LANES], ones], axis=1)

    def prep_lat(c, carry):
        rows = _chunk_rows(c)
        cos_t = cos_ref[rows, :]
        sin_t = sin_ref[rows, :]
        q = _rope(_head_norm(ul_ref[rows, 0:ATT_WIDTH].astype(F32), qg), cos_t, sin_t)
        qp[rows, :] = (q * ATT_SCALE).astype(BF16)
        k = _rope(_head_norm(ul_ref[rows, kv0:kv0 + LANES].astype(F32), kg), cos_t, sin_t)
        v = ul_ref[rows, kv0 + LANES:kv0 + 2 * LANES].astype(F32)
        dst = pl.ds(pl.multiple_of((c + 1) * CHUNK, CHUNK), CHUNK)
        kb[dst, :] = _dup_heads(k).astype(BF16)
        vb[dst, :] = values_with_ones(v)
        return carry
    lax.fori_loop(0, ncl, prep_lat, 0, unroll=UNROLL)

    def prep_ctx(c, carry):
        rows = _chunk_rows(c)
        k = _head_norm(uc_ref[rows, kv0:kv0 + LANES].astype(F32), kg)
        v = uc_ref[rows, kv0 + LANES:kv0 + 2 * LANES].astype(F32)
        kc[rows, :] = _dup_heads(k).astype(BF16)
        vc[rows, :] = values_with_ones(v)
        if need_ctx_out:
            q = _head_norm(uc_ref[rows, 0:ATT_WIDTH].astype(F32), qg)
            qcp[rows, :] = (q * ATT_SCALE).astype(BF16)
        return carry
    lax.fori_loop(0, ncc, prep_ctx, 0, unroll=UNROLL)

    def attend(q_ref, y_ref, rows, band):
        for g in range(ATT_KV_HEADS):
            kl = slice(g * LANES, (g + 1) * LANES)
            vl = slice(2 * g * LANES, 2 * (g + 1) * LANES)
            qms = []
            for r in range(rep):
                hq = g * rep + r
                qb = q_ref[rows, (hq // 2) * LANES:(hq // 2 + 1) * LANES]
                qms.append(jnp.where(lane_lo != (hq % 2 == 1), qb, jnp.zeros_like(qb)))
            qs = jnp.concatenate(qms, axis=0)
            s_c = _dot_nt(qs, kc[:, kl])
            if band is not None:
                band_rows, valid = band
                s_b = _dot_nt(qs, kb[band_rows, kl])
            ps, sink_terms = [], []
            for r in range(rep):
                sl = slice(r * CHUNK, (r + 1) * CHUNK)
                sk = sink_ref[g * rep + r]
                s = s_c[sl]
                if band is not None:
                    s = jnp.concatenate([s, jnp.where(valid, s_b[sl], neg_inf)], axis=1)
                m = jnp.maximum(jnp.max(s, axis=1, keepdims=True), sk)
                ps.append(jnp.exp(s - m).astype(BF16))
                sink_terms.append(jnp.exp(sk - m))
            p = jnp.concatenate(ps, axis=0)
            acc = _dot(p[:, 0:n_ctx], vc[:, vl])
            if band is not None:
                acc = acc + _dot(p[:, n_ctx:], vb[band_rows, vl])
            for r in range(rep):
                sl = slice(r * CHUNK, (r + 1) * CHUNK)
                o = acc[sl, 0:LANES] / (acc[sl, LANES:2 * LANES] + sink_terms[r])
                if r % 2 == 1:
                    jb = (g * rep + r) // 2
                    y_ref[rows, jb * LANES:(jb + 1) * LANES] = jnp.where(lane_lo, o_even, o).astype(y_ref.dtype)
                o_even = o

    qq = _iota((CHUNK, 3 * CHUNK), 0)
    kk = _iota((CHUNK, 3 * CHUNK), 1)
    in_window = (kk >= qq) & (kk <= qq + 2 * WINDOW)

    def lat_block(i, carry):
        kpos = kk + (i - 1) * CHUNK
        valid = in_window & (kpos >= 0) & (kpos < n_lat)
        band_rows = pl.ds(pl.multiple_of(i * CHUNK, CHUNK), 3 * CHUNK)
        attend(qp, yl_ref, _chunk_rows(i), (band_rows, valid))
        return carry
    lax.fori_loop(0, ncl, lat_block, 0, unroll=UNROLL)

    if need_ctx_out:
        def ctx_block(i, carry):
            attend(qcp, yc_ref, _chunk_rows(i), None)
            return carry
        lax.fori_loop(0, ncc, ctx_block, 0, unroll=UNROLL)


def _attn_mixer(uatt_l, uatt_c, cos_t, sin_t, qgain, kgain, sink, need_ctx_out):
    bsz, n_lat, _ = uatt_l.shape
    n_ctx = uatt_c.shape[1]
    kern = functools.partial(_attn_kernel, need_ctx_out=need_ctx_out, n_lat=n_lat, n_ctx=n_ctx)
    out_specs = [pl.BlockSpec((None, n_lat, ATT_WIDTH), lambda b: (b, 0, 0))]
    out_shape = [jax.ShapeDtypeStruct((bsz, n_lat, ATT_WIDTH), BF16)]
    scratch = [
        pltpu.VMEM((n_lat, ATT_WIDTH), BF16),
        pltpu.VMEM((n_lat + 2 * CHUNK, 2 * LANES), BF16),
        pltpu.VMEM((n_lat + 2 * CHUNK, 4 * LANES), BF16),
        pltpu.VMEM((n_ctx, 2 * LANES), BF16),
        pltpu.VMEM((n_ctx, 4 * LANES), BF16),
    ]
    if need_ctx_out:
        out_specs.append(pl.BlockSpec((None, n_ctx, ATT_WIDTH), lambda b: (b, 0, 0)))
        out_shape.append(jax.ShapeDtypeStruct((bsz, n_ctx, ATT_WIDTH), BF16))
        scratch.append(pltpu.VMEM((n_ctx, ATT_WIDTH), BF16))
    outs = pl.pallas_call(
        kern,
        grid=(bsz,),
        in_specs=[
            pl.BlockSpec((None, n_lat, U_ATT), lambda b: (b, 0, 0)),
            pl.BlockSpec((None, n_ctx, U_ATT), lambda b: (b, 0, 0)),
            pl.BlockSpec((n_lat, LANES), lambda b: (0, 0)),
            pl.BlockSpec((n_lat, LANES), lambda b: (0, 0)),
            pl.BlockSpec((1, LANES), lambda b: (0, 0)),
            pl.BlockSpec((1, LANES), lambda b: (0, 0)),
            pl.BlockSpec(memory_space=pltpu.SMEM),
        ],
        out_specs=out_specs,
        out_shape=out_shape,
        scratch_shapes=scratch,
        compiler_params=pltpu.CompilerParams(
            dimension_semantics=("arbitrary",), vmem_limit_bytes=VMEM_LIMIT),
        name="attention",
    )(uatt_l, uatt_c, cos_t, sin_t, qgain, kgain, sink)
    return (outs[0], outs[1]) if need_ctx_out else (outs[0], None)


def _in_layout():
    names = (('ml_q', ML_WIDTH), ('ml_k', ML_WIDTH), ('ml_v', ML_WIDTH), ('ml_o', ML_WIDTH),
             ('ml_i', N_DIR * ML_HEADS), ('ml_f', N_DIR * ML_HEADS),
             ('ssd_z', SSD_WIDTH), ('ssd_xbc', SSD_XBC), ('ssd_dt', N_DIR * SSD_HEADS),
             ('att_q', ATT_WIDTH), ('att_k', ATT_KV_WIDTH), ('att_v', ATT_KV_WIDTH))
    out, off = {}, 0
    for name, width in names:
        out[name] = (off, off + width)
        off += width
    return out


def _permute_cols(a):
    lay = _in_layout()
    take = lambda name: a[..., lay[name][0]:lay[name][1]]
    zeros = lambda n: jnp.zeros(a.shape[:-1] + (n,), a.dtype)
    n_f, n_dt, n_i = N_DIR * ML_HEADS, N_DIR * SSD_HEADS, N_DIR * ML_HEADS
    return jnp.concatenate(
        [take('ml_q'), take('ml_k'), take('ml_v'), take('ml_o'),
         take('ssd_z'), take('ssd_xbc'),
         take('att_q'), take('att_k'), take('att_v'),
         take('ml_f'), take('ssd_dt'), zeros(LANES - n_f - n_dt),
         take('ml_i'), zeros(LANES - n_i)], axis=-1)


def _gate_rows(ml_b_i, ml_b_f, ssd_dt_bias, ssd_a_log):
    n_f, n_dt = N_DIR * ML_HEADS, N_DIR * SSD_HEADS
    row0 = jnp.concatenate([ml_b_f.reshape(-1), ssd_dt_bias.reshape(-1), jnp.zeros((LANES - n_f - n_dt,), F32)])
    row1 = jnp.concatenate([ml_b_i.reshape(-1), jnp.zeros((LANES - n_f,), F32)])
    row2 = jnp.concatenate([jnp.zeros((n_f,), F32), ssd_a_log.reshape(-1), jnp.zeros((LANES - n_f - n_dt,), F32)])
    return jnp.concatenate([row0[None], row1[None], row2[None], jnp.zeros((5, LANES), F32)], axis=0)


def _rope_tables(n_lat):
    nfreq = ATT_HEAD_DIM // 4
    pos = np.arange(n_lat)
    rows, cols = pos // GRID_W, pos % GRID_W
    inv = np.float32(ROPE_BASE) ** (-np.arange(nfreq, dtype=np.float32) / np.float32(nfreq))
    ang_r = rows.astype(np.float32)[:, None] * inv
    ang_c = cols.astype(np.float32)[:, None] * inv
    cos_h = np.concatenate([np.cos(ang_r)] * 2 + [np.cos(ang_c)] * 2, axis=1)
    sin_h = np.concatenate([-np.sin(ang_r), np.sin(ang_r), -np.sin(ang_c), np.sin(ang_c)], axis=1)
    return (jnp.asarray(np.concatenate([cos_h] * 2, axis=1), F32),
            jnp.asarray(np.concatenate([sin_h] * 2, axis=1), F32))


def _token_mixers(u_l, u_c, p, layer, tables, need_ctx_out):
    uml_l, ussd_l, uatt_l, g_l = u_l
    uml_c, ussd_c, uatt_c, g_c = u_c
    pvec = _gate_rows(p['ml_b_i'][layer], p['ml_b_f'][layer], p['ssd_dt_bias'][layer], p['ssd_a_log'][layer])
    m_l, m_c = _mlstm_mixer(uml_l, uml_c, g_l, g_c, pvec, p['ml_norm_g'][layer][None], need_ctx_out)
    s_l, s_c = _ssd_mixer(ussd_l, ussd_c, g_l, g_c, pvec, p['ssd_conv_w'][layer], p['ssd_conv_b'][layer][None],
                          jnp.repeat(p['ssd_d'][layer], SSD_HEAD_DIM)[None], p['ssd_norm_g'][layer][None],
                          need_ctx_out)
    a_l, a_c = _attn_mixer(uatt_l, uatt_c, tables['cos'], tables['sin'],
                           jnp.tile(p['att_qn_g'][layer], 2)[None], jnp.tile(p['att_kn_g'][layer], 2)[None],
                           p['att_sink'][layer], need_ctx_out)
    return (m_l, s_l, a_l), (m_c, s_c, a_c)


def kernel(x, c, ctx, c_ctx, w_mod, b_mod, norm1_g, w_in, ml_b_i, ml_b_f, ml_norm_g, ssd_conv_w, ssd_conv_b,
           ssd_a_log, ssd_dt_bias, ssd_d, ssd_norm_g, att_qn_g, att_kn_g, att_sink, w_out, norm2_g, w_up, w_down):
    p = dict(ml_b_i=ml_b_i, ml_b_f=ml_b_f, ml_norm_g=ml_norm_g, ssd_conv_w=ssd_conv_w, ssd_conv_b=ssd_conv_b,
             ssd_a_log=ssd_a_log, ssd_dt_bias=ssd_dt_bias, ssd_d=ssd_d, ssd_norm_g=ssd_norm_g,
             att_qn_g=att_qn_g, att_kn_g=att_kn_g, att_sink=att_sink)
    bsz, n_lat, d = x.shape
    n_ctx = ctx.shape[1]
    depth = w_mod.shape[0]
    mod_rows = 16
    cc = jnp.concatenate([c, c_ctx[None], jnp.zeros((mod_rows - bsz - 1, d), F32)], axis=0)
    mods = _modulation(cc, w_mod, b_mod).reshape(depth, mod_rows, 6, d)
    cos_t, sin_t = _rope_tables(n_lat)
    tables = dict(cos=cos_t, sin=sin_t)
    lat_row = lambda b: b
    ctx_row = lambda b: bsz
    h_ctx = ctx
    for layer in range(depth):
        need_ctx_out = layer < depth - 1
        w = _permute_cols(w_in[layer].astype(BF16))
        g1 = norm1_g[layer][None]
        u_l = _in_proj(x, mods[layer], lat_row, g1, w, tm=256)
        u_c = _in_proj(h_ctx, mods[layer], ctx_row, g1, w, tm=256)
        y_l, y_c = _token_mixers(u_l, u_c, p, layer, tables, need_ctx_out)
        wo, wup, wdn = w_out[layer].astype(BF16), w_up[layer].astype(BF16), w_down[layer].astype(BF16)
        g2 = norm2_g[layer][None]
        x = _post(x, *y_l, mods[layer], lat_row, g2, wo, wup, wdn, tm=512)
        if need_ctx_out:
            h_ctx = _post(h_ctx, *y_c, mods[layer], ctx_row, g2, wo, wup, wdn, tm=256)
    return x
```

```python
import functools
import math

import numpy as np
import jax
import jax.numpy as jnp
from jax import lax
from jax.experimental import pallas as pl
from jax.experimental.pallas import tpu as pltpu

F32 = jnp.float32
BF16 = jnp.bfloat16

EPS = 1e-6
GRID_W = 64
N_DIR = 2
ML_HEADS = 4
ML_HEAD_DIM = 128
ML_WIDTH = ML_HEADS * ML_HEAD_DIM
SSD_HEADS = 8
SSD_HEAD_DIM = 64
SSD_WIDTH = SSD_HEADS * SSD_HEAD_DIM
SSD_STATE = 64
SSD_GROUPS = 2
SSD_XBC = SSD_WIDTH + 2 * SSD_GROUPS * SSD_STATE
ATT_Q_HEADS = 8
ATT_KV_HEADS = 2
ATT_HEAD_DIM = 64
ATT_WIDTH = ATT_Q_HEADS * ATT_HEAD_DIM
ATT_KV_WIDTH = ATT_KV_HEADS * ATT_HEAD_DIM
WINDOW = 128
ROPE_BASE = 10000.0
MIX_WIDTH = ML_WIDTH + SSD_WIDTH + ATT_WIDTH

LANES = 128
CHUNK = 128
VMEM_LIMIT = 56 * 1024 * 1024
UNROLL = 4
UNROLL_LIGHT = 8

U_ML = 4 * ML_WIDTH
U_SSD = SSD_WIDTH + SSD_XBC
U_ATT = ATT_WIDTH + 2 * ATT_KV_WIDTH
U_GATE = 2 * LANES
U_TOTAL = U_ML + U_SSD + U_ATT + U_GATE
DT_LANE0 = N_DIR * ML_HEADS

ML_SCALE = ML_HEAD_DIM ** -0.5
ATT_SCALE = ATT_HEAD_DIM ** -0.5

_NT = (((1,), (1,)), ((), ()))
_TN = (((0,), (0,)), ((), ()))


def _dot(a, b):
    return jnp.dot(a, b, preferred_element_type=F32)


def _dot_nt(a, b):
    return lax.dot_general(a, b, _NT, preferred_element_type=F32)


def _dot_tn(a, b):
    return lax.dot_general(a, b, _TN, preferred_element_type=F32)


def _split2(x):
    hi = x.astype(BF16)
    lo = (x - hi.astype(F32)).astype(BF16)
    return hi, lo


def _split3(x):
    hi = x.astype(BF16)
    r = x - hi.astype(F32)
    mid = r.astype(BF16)
    lo = (r - mid.astype(F32)).astype(BF16)
    return hi, mid, lo


def _tri_cumsums(tris, x):
    parts = _split3(x)
    return [_dot(tri, parts[0]) + _dot(tri, parts[1]) + _dot(tri, parts[2]) for tri in tris]


def _sigmoid(x):
    return 1.0 / (1.0 + jnp.exp(-x))


def _log_sigmoid(x):
    return jnp.minimum(x, 0.0) - jnp.log1p(jnp.exp(-jnp.abs(x)))


def _softplus(x):
    return jnp.maximum(x, 0.0) + jnp.log1p(jnp.exp(-jnp.abs(x)))


def _iota(shape, dim):
    return lax.broadcasted_iota(jnp.int32, shape, dim)


def _chunk_rows(c):
    return pl.ds(pl.multiple_of(c * CHUNK, CHUNK), CHUNK)


def _mod_kernel(s_ref, w_ref, b_ref, o_ref):
    s = s_ref[...]
    s = s * _sigmoid(s)
    w = w_ref[...]
    s_hi, s_lo = _split2(s)
    w_hi, w_lo = _split2(w)
    o_ref[...] = _dot(s_hi, w_hi) + (_dot(s_hi, w_lo) + _dot(s_lo, w_hi)) + b_ref[...]


def _modulation(cc, w_mod, b_mod):
    depth, d, d6 = w_mod.shape
    r = cc.shape[0]
    bn = 1024
    return pl.pallas_call(
        _mod_kernel,
        grid=(depth, d6 // bn),
        in_specs=[
            pl.BlockSpec((r, d), lambda l, j: (0, 0)),
            pl.BlockSpec((None, d, bn), lambda l, j: (l, 0, j)),
            pl.BlockSpec((None, 1, bn), lambda l, j: (l, 0, j)),
        ],
        out_specs=pl.BlockSpec((None, r, bn), lambda l, j: (l, 0, j)),
        out_shape=jax.ShapeDtypeStruct((depth, r, d6), F32),
        compiler_params=pltpu.CompilerParams(
            dimension_semantics=("arbitrary", "arbitrary"), vmem_limit_bytes=VMEM_LIMIT),
        name="modulation",
    )(cc, w_mod, b_mod.reshape(depth, 1, d6))


def _inproj_kernel(x_ref, mod_ref, g_ref, w_ref, oml_ref, ossd_ref, oatt_ref, og_ref):
    x = x_ref[...]
    ms = jnp.mean(x * x, axis=-1, keepdims=True)
    xn = x * lax.rsqrt(ms + EPS)
    h = xn * (g_ref[...] * (1.0 + mod_ref[1:2, :])) + mod_ref[0:1, :]
    hb = h.astype(BF16)
    step = 512
    off = 0
    for o_ref, width in ((oml_ref, U_ML), (ossd_ref, U_SSD), (oatt_ref, U_ATT), (og_ref, U_GATE)):
        for j in range(0, width, step):
            wj = min(step, width - j)
            o_ref[:, j:j + wj] = _dot(hb, w_ref[:, off + j:off + j + wj]).astype(o_ref.dtype)
        off += width


def _in_proj(x, mods, mod_row, g1, w, layer, tm):
    bsz, t, d = x.shape
    widths = (U_ML, U_SSD, U_ATT, U_GATE)
    dtypes = (BF16, BF16, BF16, F32)
    return pl.pallas_call(
        _inproj_kernel,
        grid=(bsz, t // tm),
        in_specs=[
            pl.BlockSpec((None, tm, d), lambda b, i: (b, i, 0)),
            pl.BlockSpec((None, None, 6, d), lambda b, i: (layer, mod_row(b), 0, 0)),
            pl.BlockSpec((None, 1, d), lambda b, i: (layer, 0, 0)),
            pl.BlockSpec((None, d, U_TOTAL), lambda b, i: (layer, 0, 0), pipeline_mode=pl.Buffered(1)),
        ],
        out_specs=[pl.BlockSpec((None, tm, wd), lambda b, i: (b, i, 0)) for wd in widths],
        out_shape=[jax.ShapeDtypeStruct((bsz, t, wd), dt) for wd, dt in zip(widths, dtypes)],
        compiler_params=pltpu.CompilerParams(
            dimension_semantics=("arbitrary", "arbitrary"), vmem_limit_bytes=VMEM_LIMIT),
        name="in_proj",
    )(x, mods, g1, w)


def _post_kernel(x_ref, yml_ref, yssd_ref, yatt_ref, mod_ref, g_ref, wo_ref, wup_ref, wdn_ref, o_ref):
    proj = (_dot(yml_ref[...], wo_ref[0:ML_WIDTH, :])
            + _dot(yssd_ref[...], wo_ref[ML_WIDTH:ML_WIDTH + SSD_WIDTH, :])
            + _dot(yatt_ref[...], wo_ref[ML_WIDTH + SSD_WIDTH:MIX_WIDTH, :]))
    x1 = x_ref[...] + mod_ref[2:3, :] * proj
    ms = jnp.mean(x1 * x1, axis=-1, keepdims=True)
    xn = x1 * lax.rsqrt(ms + EPS)
    hb = (xn * (g_ref[...] * (1.0 + mod_ref[4:5, :])) + mod_ref[3:4, :]).astype(BF16)
    d_ff = wup_ref.shape[1]
    step = 1024
    acc = None
    for j in range(0, d_ff, step):
        up = jnp.maximum(_dot(hb, wup_ref[:, j:j + step]), 0.0)
        part = _dot((up * up).astype(BF16), wdn_ref[j:j + step, :])
        acc = part if acc is None else acc + part
    o_ref[...] = x1 + mod_ref[5:6, :] * acc


def _post(x, yml, yssd, yatt, mods, mod_row, g2, wo, wup, wdn, layer, tm):
    bsz, t, d = x.shape
    d_ff = wup.shape[2]
    const = dict(pipeline_mode=pl.Buffered(1))
    return pl.pallas_call(
        _post_kernel,
        grid=(bsz, t // tm),
        in_specs=[
            pl.BlockSpec((None, tm, d), lambda b, i: (b, i, 0)),
            pl.BlockSpec((None, tm, ML_WIDTH), lambda b, i: (b, i, 0)),
            pl.BlockSpec((None, tm, SSD_WIDTH), lambda b, i: (b, i, 0)),
            pl.BlockSpec((None, tm, ATT_WIDTH), lambda b, i: (b, i, 0)),
            pl.BlockSpec((None, None, 6, d), lambda b, i: (layer, mod_row(b), 0, 0)),
            pl.BlockSpec((None, 1, d), lambda b, i: (layer, 0, 0)),
            pl.BlockSpec((None, MIX_WIDTH, d), lambda b, i: (layer, 0, 0), **const),
            pl.BlockSpec((None, d, d_ff), lambda b, i: (layer, 0, 0), **const),
            pl.BlockSpec((None, d_ff, d), lambda b, i: (layer, 0, 0), **const),
        ],
        out_specs=pl.BlockSpec((None, tm, d), lambda b, i: (b, i, 0)),
        out_shape=jax.ShapeDtypeStruct((bsz, t, d), F32),
        compiler_params=pltpu.CompilerParams(
            dimension_semantics=("arbitrary", "arbitrary"), vmem_limit_bytes=VMEM_LIMIT),
        name="post",
    )(x, yml, yssd, yatt, mods, g2, wo, wup, wdn)


ML_GATES = N_DIR * ML_HEADS


def _mlstm_kernel(ul_ref, uc_ref, gl_ref, gc_ref, pv_ref, ng_ref, sh_ref, ex_ref, *rest,
                  need_ctx_out, n_lat, n_ctx):
    if need_ctx_out:
        yl_ref, yc_ref, arow, packed, ends, mprev, cbk, crun = rest
    else:
        yl_ref, arow, packed, ends, mprev, cbk, crun = rest
        yc_ref = None
    ncl, ncc = n_lat // CHUNK, n_ctx // CHUNK
    nct = ncl + ncc
    nh = ML_HEADS
    ti = _iota((CHUNK, CHUNK), 0)
    si = _iota((CHUNK, CHUNK), 1)
    low = si <= ti
    upp = si >= ti
    tri_low = jnp.where(low, 1.0, 0.0).astype(BF16)
    tri_upp = jnp.where(upp, 1.0, 0.0).astype(BF16)
    fwd_lane = _iota((CHUNK, LANES), 1) < nh
    row = _iota((CHUNK, LANES), 0)
    bias_f = pv_ref[0:1, :]
    bias_i = pv_ref[1:2, :]
    neg_inf = -jnp.inf
    ones_blk = jnp.ones((CHUNK, LANES), BF16)

    def prep(g_ref, nch, goff):
        def body(c, carry):
            g = g_ref[_chunk_rows(c), :]
            lf = _log_sigmoid(g[:, 0:LANES] + bias_f)
            li = g[:, LANES:2 * LANES] + bias_i
            b = jnp.where(fwd_lane, *_tri_cumsums((tri_low, tri_upp), lf))
            a = li - b
            cm = a
            k = 1
            while k < CHUNK:
                before = jnp.where(row >= k, pltpu.roll(cm, k, 0), neg_inf)
                after = jnp.where(row < CHUNK - k, pltpu.roll(cm, CHUNK - k, 0), neg_inf)
                cm = jnp.maximum(cm, jnp.where(fwd_lane, before, after))
                k *= 2
            arow[goff + c] = a.T
            pk = None
            for i, part in enumerate(_split3(cm) + _split3(b)):
                t = _dot(part, sh_ref[:, i * LANES:(i + 1) * LANES])
                pk = t if pk is None else pk + t
            packed[goff + c] = pk.astype(BF16)
            ends[goff + c, 0:1, :] = jnp.where(fwd_lane[0:1], cm[CHUNK - 1:CHUNK], cm[0:1])
            ends[goff + c, 1:2, :] = jnp.where(fwd_lane[0:1], b[CHUNK - 1:CHUNK], b[0:1])
            return carry
        lax.fori_loop(0, nch, body, 0, unroll=UNROLL_LIGHT)

    prep(gc_ref, ncc, 0)
    prep(gl_ref, ncl, ncc)

    def m_scan(lo, hi, reverse, m0):
        def body(i, m):
            cg = hi - 1 - i if reverse else lo + i
            mprev[cg, int(reverse):int(reverse) + 1, :] = m
            return ends[cg, 1:2, :] + jnp.maximum(m, ends[cg, 0:1, :])
        return lax.fori_loop(0, hi - lo, body, m0)

    m_zero = jnp.zeros((1, LANES), F32)
    m_scan(0, nct, False, m_zero)
    m_scan(ncc, nct, True, m_scan(0, ncc, True, m_zero))

    def load_kv(u_ref, rows, h):
        k = u_ref[rows, ML_WIDTH + h * LANES:ML_WIDTH + (h + 1) * LANES]
        v = u_ref[rows, 2 * ML_WIDTH + h * LANES:2 * ML_WIDTH + (h + 1) * LANES]
        return k, jnp.concatenate([v, ones_blk], axis=1)

    def update(d, h, cg, k, v1):
        j = d * nh + h
        m_prev = mprev[cg, d:d + 1, j:j + 1]
        m_end = jnp.maximum(m_prev, ends[cg, 0:1, j:j + 1])
        w_row = jnp.exp(arow[cg, j:j + 1, :] - m_end) * ML_SCALE
        ktw = (k.T.astype(F32) * w_row).astype(BF16)
        crun[d, h] = jnp.exp(m_prev - m_end) * crun[d, h] + _dot(ktw, v1)

    def bwd_pass(u_ref, nch, goff, keep):
        def body(i, carry):
            c = nch - 1 - i
            rows = _chunk_rows(c)
            for h in range(nh):
                if keep:
                    cbk[goff + c, h] = crun[1, h].astype(BF16)
                k, v1 = load_kv(u_ref, rows, h)
                update(1, h, goff + c, k, v1)
            return carry
        lax.fori_loop(0, nch, body, 0, unroll=UNROLL_LIGHT)

    def fwd_pass(u_ref, y_ref, nch, goff):
        def body(c, carry):
            rows = _chunk_rows(c)
            cg = goff + c
            for h in range(nh):
                k, v1 = load_kv(u_ref, rows, h)
                if y_ref is not None:
                    q = u_ref[rows, h * LANES:(h + 1) * LANES]
                    o = u_ref[rows, 3 * ML_WIDTH + h * LANES:3 * ML_WIDTH + (h + 1) * LANES]
                    sqk = _dot_nt(q, k) * ML_SCALE
                    qc = _dot(q, jnp.concatenate([crun[0, h].astype(BF16), cbk[cg, h]], axis=1))
                    s_dirs, stats = [], []
                    for d, mask in ((0, low), (1, upp)):
                        j = d * nh + h
                        rep = _dot(packed[cg], ex_ref[:, 2 * j * LANES:2 * (j + 1) * LANES])
                        m_prev = mprev[cg, d:d + 1, j:j + 1]
                        m_t = jnp.maximum(rep[:, 0:LANES], m_prev)
                        s_dirs.append(sqk * jnp.exp(jnp.where(mask, arow[cg, j:j + 1, :] - m_t, neg_inf)))
                        stats.append((m_prev, m_t, rep[:, LANES:2 * LANES]))
                    sv = _dot(jnp.concatenate(s_dirs, axis=0).astype(BF16), v1)
                    hsum = None
                    for d, (m_prev, m_t, b_t) in enumerate(stats):
                        g = jnp.exp(m_prev - m_t)
                        sv_d = sv[d * CHUNK:(d + 1) * CHUNK, :]
                        qc_d = qc[:, 2 * d * LANES:2 * (d + 1) * LANES]
                        num = sv_d[:, 0:LANES] + g * qc_d[:, 0:LANES]
                        den = sv_d[:, LANES:2 * LANES] + g * qc_d[:, LANES:2 * LANES]
                        hd = num / jnp.maximum(jnp.abs(den), jnp.exp(-(b_t + m_t)))
                        hsum = hd if hsum is None else hsum + hd
                    ms = jnp.mean(hsum * hsum, axis=1, keepdims=True)
                    y = hsum * lax.rsqrt(ms + EPS) * ng_ref[:, h * LANES:(h + 1) * LANES]
                    y = y * _sigmoid(o.astype(F32))
                    y_ref[rows, h * LANES:(h + 1) * LANES] = y.astype(y_ref.dtype)
                update(0, h, cg, k, v1)
            return carry
        lax.fori_loop(0, nch, body, 0, unroll=UNROLL)

    crun[...] = jnp.zeros(crun.shape, F32)
    bwd_pass(uc_ref, ncc, 0, need_ctx_out)
    bwd_pass(ul_ref, ncl, ncc, True)
    fwd_pass(uc_ref, yc_ref, ncc, 0)
    fwd_pass(ul_ref, yl_ref, ncl, ncc)


def _ml_lane_tables():
    shifts = np.zeros((LANES, 6 * LANES), np.float32)
    expand = np.zeros((LANES, ML_GATES * 2 * LANES), np.float32)
    for j in range(ML_GATES):
        for i in range(6):
            shifts[j, i * LANES + ML_GATES * i + j] = 1.0
            half = i // 3
            expand[ML_GATES * i + j, (2 * j + half) * LANES:(2 * j + half + 1) * LANES] = 1.0
    return jnp.asarray(shifts, BF16), jnp.asarray(expand, BF16)


def _mlstm_mixer(uml_l, uml_c, g_l, g_c, pvec, norm_g, need_ctx_out):
    shifts, expand = _ml_lane_tables()
    bsz, n_lat, _ = uml_l.shape
    n_ctx = uml_c.shape[1]
    nct = (n_lat + n_ctx) // CHUNK
    kern = functools.partial(_mlstm_kernel, need_ctx_out=need_ctx_out, n_lat=n_lat, n_ctx=n_ctx)
    out_specs = [pl.BlockSpec((None, n_lat, ML_WIDTH), lambda b: (b, 0, 0))]
    out_shape = [jax.ShapeDtypeStruct((bsz, n_lat, ML_WIDTH), BF16)]
    if need_ctx_out:
        out_specs.append(pl.BlockSpec((None, n_ctx, ML_WIDTH), lambda b: (b, 0, 0)))
        out_shape.append(jax.ShapeDtypeStruct((bsz, n_ctx, ML_WIDTH), BF16))
    outs = pl.pallas_call(
        kern,
        grid=(bsz,),
        in_specs=[
            pl.BlockSpec((None, n_lat, U_ML), lambda b: (b, 0, 0)),
            pl.BlockSpec((None, n_ctx, U_ML), lambda b: (b, 0, 0)),
            pl.BlockSpec((None, n_lat, U_GATE), lambda b: (b, 0, 0)),
            pl.BlockSpec((None, n_ctx, U_GATE), lambda b: (b, 0, 0)),
            pl.BlockSpec((8, LANES), lambda b: (0, 0)),
            pl.BlockSpec((1, ML_WIDTH), lambda b: (0, 0)),
            pl.BlockSpec(shifts.shape, lambda b: (0, 0)),
            pl.BlockSpec(expand.shape, lambda b: (0, 0)),
        ],
        out_specs=out_specs,
        out_shape=out_shape,
        scratch_shapes=[
            pltpu.VMEM((nct, LANES, CHUNK), F32),
            pltpu.VMEM((nct, CHUNK, LANES), BF16),
            pltpu.VMEM((nct, 8, LANES), F32),
            pltpu.VMEM((nct, 8, LANES), F32),
            pltpu.VMEM((nct, ML_HEADS, ML_HEAD_DIM, 2 * LANES), BF16),
            pltpu.VMEM((N_DIR, ML_HEADS, ML_HEAD_DIM, 2 * LANES), F32),
        ],
        compiler_params=pltpu.CompilerParams(
            dimension_semantics=("arbitrary",), vmem_limit_bytes=VMEM_LIMIT),
        name="mlstm",
    )(uml_l, uml_c, g_l, g_c, pvec, norm_g, shifts, expand)
    return (outs[0], outs[1]) if need_ctx_out else (outs[0], None)


def _ssd_kernel(ul_ref, uc_ref, gl_ref, gc_ref, pv_ref, cw_ref, cb_ref, dsk_ref, ng_ref, sh_ref, ex_ref, *rest,
                need_ctx_out, n_lat, n_ctx):
    if need_ctx_out:
        yl_ref, yc_ref, xact, acum, packed, act, dtt, hbk, hrun = rest
    else:
        yl_ref, xact, acum, packed, act, dtt, hbk, hrun = rest
        yc_ref = None
    ncl, ncc = n_lat // CHUNK, n_ctx // CHUNK
    nh, hg = SSD_HEADS, SSD_HEADS // SSD_GROUPS
    gw = hg * SSD_HEAD_DIM
    gn = SSD_GROUPS * SSD_STATE
    ti = _iota((CHUNK, CHUNK), 0)
    si = _iota((CHUNK, CHUNK), 1)
    low_strict = si < ti
    upp_strict = si > ti
    tri_low = jnp.where(si <= ti, 1.0, 0.0).astype(BF16)
    tri_upp = jnp.where(si >= ti, 1.0, 0.0).astype(BF16)
    lane = _iota((CHUNK, LANES), 1)
    fwd_lane = (lane >= DT_LANE0) & (lane < DT_LANE0 + nh)
    bias0 = pv_ref[0:1, :]
    a_row = -jnp.exp(pv_ref[2:3, :])
    shift_down = jnp.where(si == ti - 1, 1.0, 0.0).astype(BF16)
    shift_up = jnp.where(si == ti + 1, 1.0, 0.0).astype(BF16)
    lane_group0 = lane < SSD_STATE

    def prep(u_ref, g_ref, nch, goff, nrows):
        def body(c, carry):
            r = pl.multiple_of(c * CHUNK, CHUNK)
            cur = u_ref[pl.ds(r, CHUNK), SSD_WIDTH:U_SSD]
            rp = pl.multiple_of(jnp.maximum(r - CHUNK, 0), CHUNK)
            rn = pl.multiple_of(jnp.minimum(r + CHUNK, nrows - CHUNK), CHUNK)
            prev = u_ref[pl.ds(rp, CHUNK), SSD_WIDTH:U_SSD]
            nxt = u_ref[pl.ds(rn, CHUNK), SSD_WIDTH:U_SSD]
            from_prev = jnp.where((ti == 0) & (si == CHUNK - 1) & (c > 0), 1.0, 0.0).astype(BF16)
            from_next = jnp.where((ti == CHUNK - 1) & (si == 0) & (c < nch - 1), 1.0, 0.0).astype(BF16)
            xm = _dot(jnp.concatenate([shift_down, from_prev], axis=1), jnp.concatenate([cur, prev], axis=0))
            xp = _dot(jnp.concatenate([shift_up, from_next], axis=1), jnp.concatenate([cur, nxt], axis=0))
            conv = (cw_ref[0:1, :] * xm + cw_ref[1:2, :] * cur.astype(F32) + cw_ref[2:3, :] * xp
                    + cb_ref[...])
            xact[goff + c] = (conv * _sigmoid(conv)).astype(BF16)

            dt = _softplus(g_ref[pl.ds(r, CHUNK), :] + bias0)
            la = dt * a_row
            ac = jnp.where(fwd_lane, *_tri_cumsums((tri_low, tri_upp), la))
            a_end = jnp.where(fwd_lane[0:1, :], ac[CHUNK - 1:CHUNK, :], ac[0:1, :])
            acum[goff + c] = ac
            act[goff + c] = (ac - jnp.log(dt)).T
            dtt[goff + c] = dt.T
            pk = None
            for i, part in enumerate(_split3(jnp.exp(ac)) + _split3(jnp.exp(a_end - ac) * dt)):
                t = _dot(part, sh_ref[:, i * LANES:(i + 1) * LANES])
                pk = t if pk is None else pk + t
            packed[goff + c] = pk.astype(BF16)
            return carry
        lax.fori_loop(0, nch, body, 0, unroll=UNROLL_LIGHT)

    prep(uc_ref, gc_ref, ncc, 0, n_ctx)
    prep(ul_ref, gl_ref, ncl, ncc, n_lat)

    def e_cols(d, g):
        return slice(d * SSD_WIDTH + g * gw, d * SSD_WIDTH + (g + 1) * gw)

    def w_cols(d, g):
        return slice((N_DIR + d) * SSD_WIDTH + g * gw, (N_DIR + d) * SSD_WIDTH + (g + 1) * gw)

    def update_state(d, cg):
        pk = packed[cg]
        xa = xact[cg]
        b_all = xa[:, SSD_WIDTH:SSD_WIDTH + gn]
        zero = jnp.zeros_like(b_all)
        upd = []
        for g in range(SSD_GROUPS):
            wx = (_dot(pk, ex_ref[:, w_cols(d, g)]) * xa[:, g * gw:(g + 1) * gw].astype(F32)).astype(BF16)
            upd.append(_dot_tn(jnp.where(lane_group0 == (g == 0), b_all, zero), wx))
        tail, last = (slice(CHUNK - 16, CHUNK), 15) if d == 0 else (slice(0, 16), 0)
        decay = _dot(pk[tail], ex_ref[:, d * SSD_WIDTH:(d + 1) * SSD_WIDTH])[last:last + 1, :]
        hrun[d] = decay * hrun[d] + jnp.concatenate(upd, axis=1)

    def bwd_pass(nch, goff, keep):
        def body(i, carry):
            cg = goff + nch - 1 - i
            if keep:
                hbk[cg] = hrun[1].astype(BF16)
            update_state(1, cg)
            return carry
        lax.fori_loop(0, nch, body, 0, unroll=UNROLL_LIGHT)

    def fwd_pass(u_ref, y_ref, nch, goff):
        def body(c, carry):
            cg = goff + c
            if y_ref is not None:
                rows = _chunk_rows(c)
                xa = xact[cg]
                x = xa[:, 0:SSD_WIDTH]
                b_all = xa[:, SSD_WIDTH:SSD_WIDTH + gn]
                c_all = xa[:, SSD_WIDTH + gn:SSD_XBC]
                ac = acum[cg]
                ac_t = act[cg]
                dt_t = dtt[cg]
                lane_w = _iota((CHUNK, gw), 1)
                pk = packed[cg]
                yz_groups, ssq = [], None
                for g in range(SSD_GROUPS):
                    cols = slice(g * gw, (g + 1) * gw)
                    c_g = jnp.where(lane_group0 == (g == 0), c_all, jnp.zeros_like(c_all))
                    cb = _dot_nt(c_g, b_all)
                    s_heads = []
                    for hh in range(hg):
                        jf = DT_LANE0 + g * hg + hh
                        jb = jf + nh
                        on_diag = jnp.log(dt_t[jf:jf + 1, :] + dt_t[jb:jb + 1, :])
                        expo = jnp.where(low_strict, ac[:, jf:jf + 1] - ac_t[jf:jf + 1, :],
                                         jnp.where(upp_strict, ac[:, jb:jb + 1] - ac_t[jb:jb + 1, :], on_diag))
                        s_heads.append((cb * jnp.exp(expo)).astype(BF16))
                    x_g = x[:, cols]
                    bd = jnp.concatenate(
                        [jnp.where((lane_w >= hh * SSD_HEAD_DIM) & (lane_w < (hh + 1) * SSD_HEAD_DIM),
                                   x_g, jnp.zeros_like(x_g)) for hh in range(hg)], axis=0)
                    y = _dot(jnp.concatenate(s_heads, axis=1), bd)
                    y = y + _dot(c_all, hrun[0, :, cols].astype(BF16)) * _dot(pk, ex_ref[:, e_cols(0, g)])
                    y = y + _dot(c_all, hbk[cg, :, cols]) * _dot(pk, ex_ref[:, e_cols(1, g)])
                    y = y + dsk_ref[:, cols] * x_g.astype(F32)
                    z = u_ref[rows, cols].astype(F32)
                    yz = y * (z * _sigmoid(z))
                    part = jnp.sum(yz * yz, axis=1, keepdims=True)
                    ssq = part if ssq is None else ssq + part
                    yz_groups.append(yz)
                scale = lax.rsqrt(ssq * (1.0 / SSD_WIDTH) + EPS)
                for g, yz in enumerate(yz_groups):
                    cols = slice(g * gw, (g + 1) * gw)
                    y_ref[rows, cols] = (yz * scale * ng_ref[:, cols]).astype(y_ref.dtype)
            update_state(0, cg)
            return carry
        lax.fori_loop(0, nch, body, 0, unroll=UNROLL)

    hrun[...] = jnp.zeros(hrun.shape, F32)
    bwd_pass(ncc, 0, need_ctx_out)
    bwd_pass(ncl, ncc, True)
    fwd_pass(uc_ref, yc_ref, ncc, 0)
    fwd_pass(ul_ref, yl_ref, ncl, ncc)


def _ssd_lane_tables():
    nj = N_DIR * SSD_HEADS
    shifts = np.zeros((LANES, 6 * LANES), np.float32)
    expand = np.zeros((LANES, 2 * N_DIR * SSD_WIDTH), np.float32)
    for i in range(6):
        for d in range(N_DIR):
            for h in range(SSD_HEADS):
                j = d * SSD_HEADS + h
                shifts[DT_LANE0 + j, i * LANES + nj * i + j] = 1.0
                c0 = ((i // 3) * N_DIR + d) * SSD_WIDTH + h * SSD_HEAD_DIM
                expand[nj * i + j, c0:c0 + SSD_HEAD_DIM] = 1.0
    return jnp.asarray(shifts, BF16), jnp.asarray(expand, BF16)


def _ssd_mixer(ussd_l, ussd_c, g_l, g_c, pvec, conv_w, conv_b, d_skip, norm_g, need_ctx_out):
    shifts, expand = _ssd_lane_tables()
    bsz, n_lat, _ = ussd_l.shape
    n_ctx = ussd_c.shape[1]
    nct = (n_lat + n_ctx) // CHUNK
    gn = SSD_GROUPS * SSD_STATE
    kern = functools.partial(_ssd_kernel, need_ctx_out=need_ctx_out, n_lat=n_lat, n_ctx=n_ctx)
    out_specs = [pl.BlockSpec((None, n_lat, SSD_WIDTH), lambda b: (b, 0, 0))]
    out_shape = [jax.ShapeDtypeStruct((bsz, n_lat, SSD_WIDTH), BF16)]
    if need_ctx_out:
        out_specs.append(pl.BlockSpec((None, n_ctx, SSD_WIDTH), lambda b: (b, 0, 0)))
        out_shape.append(jax.ShapeDtypeStruct((bsz, n_ctx, SSD_WIDTH), BF16))
    outs = pl.pallas_call(
        kern,
        grid=(bsz,),
        in_specs=[
            pl.BlockSpec((None, n_lat, U_SSD), lambda b: (b, 0, 0)),
            pl.BlockSpec((None, n_ctx, U_SSD), lambda b: (b, 0, 0)),
            pl.BlockSpec((None, n_lat, LANES), lambda b: (b, 0, 0)),
            pl.BlockSpec((None, n_ctx, LANES), lambda b: (b, 0, 0)),
            pl.BlockSpec((8, LANES), lambda b: (0, 0)),
            pl.BlockSpec((3, SSD_XBC), lambda b: (0, 0)),
            pl.BlockSpec((1, SSD_XBC), lambda b: (0, 0)),
            pl.BlockSpec((1, SSD_WIDTH), lambda b: (0, 0)),
            pl.BlockSpec((1, SSD_WIDTH), lambda b: (0, 0)),
            pl.BlockSpec(shifts.shape, lambda b: (0, 0)),
            pl.BlockSpec(expand.shape, lambda b: (0, 0)),
        ],
        out_specs=out_specs,
        out_shape=out_shape,
        scratch_shapes=[
            pltpu.VMEM((nct, CHUNK, SSD_XBC), BF16),
            pltpu.VMEM((nct, CHUNK, LANES), F32),
            pltpu.VMEM((nct, CHUNK, LANES), BF16),
            pltpu.VMEM((nct, LANES, CHUNK), F32),
            pltpu.VMEM((nct, LANES, CHUNK), F32),
            pltpu.VMEM((nct, gn, SSD_WIDTH), BF16),
            pltpu.VMEM((N_DIR, gn, SSD_WIDTH), F32),
        ],
        compiler_params=pltpu.CompilerParams(
            dimension_semantics=("arbitrary",), vmem_limit_bytes=VMEM_LIMIT),
        name="ssd",
    )(ussd_l, ussd_c, g_l, g_c, pvec, conv_w, conv_b, d_skip, norm_g, shifts, expand)
    return (outs[0], outs[1]) if need_ctx_out else (outs[0], None)


def _head_norm(x, gain):
    rows = x.shape[0]
    blks = [x[:, j * LANES:(j + 1) * LANES] for j in range(x.shape[1] // LANES)]
    same_head = (_iota((LANES, LANES), 0) < ATT_HEAD_DIM) == (_iota((LANES, LANES), 1) < ATT_HEAD_DIM)
    sq = jnp.concatenate([b * b for b in blks], axis=0).astype(BF16)
    ms = _dot(sq, jnp.where(same_head, 1.0, 0.0).astype(BF16)) * (1.0 / ATT_HEAD_DIM)
    outs = [b * lax.rsqrt(ms[j * rows:(j + 1) * rows] + EPS) * gain for j, b in enumerate(blks)]
    return outs[0] if len(outs) == 1 else jnp.concatenate(outs, axis=1)


def _rope(x, cos_t, sin_t):
    w = x.shape[1]
    reps = w // LANES
    nfreq = ATT_HEAD_DIM // 4
    partner = jnp.where((_iota(x.shape, 1) & (2 * nfreq - 1)) < nfreq,
                        pltpu.roll(x, w - nfreq, 1), pltpu.roll(x, nfreq, 1))
    if reps > 1:
        cos_t = jnp.concatenate([cos_t] * reps, axis=1)
        sin_t = jnp.concatenate([sin_t] * reps, axis=1)
    return x * cos_t + partner * sin_t


def _dup_heads(x):
    lane_lo = _iota(x.shape, 1) < ATT_HEAD_DIM
    sw = pltpu.roll(x, ATT_HEAD_DIM, 1)
    return jnp.concatenate([jnp.where(lane_lo, x, sw), jnp.where(lane_lo, sw, x)], axis=1)


def _attn_kernel(ul_ref, uc_ref, cos_ref, sin_ref, qg_ref, kg_ref, sink_ref, *rest,
                 need_ctx_out, n_lat, n_ctx):
    if need_ctx_out:
        yl_ref, yc_ref, qp, kb, vb, kc, vc, qcp = rest
    else:
        yl_ref, qp, kb, vb, kc, vc = rest
        yc_ref = qcp = None
    ncl, ncc = n_lat // CHUNK, n_ctx // CHUNK
    rep = ATT_Q_HEADS // ATT_KV_HEADS
    kv0 = ATT_WIDTH
    lane_lo = _iota((CHUNK, LANES), 1) < ATT_HEAD_DIM
    neg_inf = -jnp.inf
    qg = qg_ref[...]
    kg = kg_ref[...]

    for ref in (kb, vb):
        zeros_blk = jnp.zeros((CHUNK, ref.shape[1]), BF16)
        ref[0:CHUNK, :] = zeros_blk
        ref[CHUNK + n_lat:2 * CHUNK + n_lat, :] = zeros_blk

    def values_with_ones(v):
        d = _dup_heads(v).astype(BF16)
        ones = jnp.ones((v.shape[0], LANES), BF16)
        return jnp.concatenate([d[:, 0:LANES], ones, d[:, LANES:2 * LANES], ones], axis=1)

    def prep_lat(c, carry):
        rows = _chunk_rows(c)
        cos_t = cos_ref[rows, :]
        sin_t = sin_ref[rows, :]
        q = _rope(_head_norm(ul_ref[rows, 0:ATT_WIDTH].astype(F32), qg), cos_t, sin_t)
        qp[rows, :] = (q * ATT_SCALE).astype(BF16)
        k = _rope(_head_norm(ul_ref[rows, kv0:kv0 + LANES].astype(F32), kg), cos_t, sin_t)
        v = ul_ref[rows, kv0 + LANES:kv0 + 2 * LANES].astype(F32)
        dst = pl.ds(pl.multiple_of((c + 1) * CHUNK, CHUNK), CHUNK)
        kb[dst, :] = _dup_heads(k).astype(BF16)
        vb[dst, :] = values_with_ones(v)
        return carry
    lax.fori_loop(0, ncl, prep_lat, 0, unroll=UNROLL_LIGHT)

    def prep_ctx(c, carry):
        rows = _chunk_rows(c)
        k = _head_norm(uc_ref[rows, kv0:kv0 + LANES].astype(F32), kg)
        v = uc_ref[rows, kv0 + LANES:kv0 + 2 * LANES].astype(F32)
        kc[rows, :] = _dup_heads(k).astype(BF16)
        vc[rows, :] = values_with_ones(v)
        if need_ctx_out:
            q = _head_norm(uc_ref[rows, 0:ATT_WIDTH].astype(F32), qg)
            qcp[rows, :] = (q * ATT_SCALE).astype(BF16)
        return carry
    lax.fori_loop(0, ncc, prep_ctx, 0, unroll=UNROLL)

    def attend(q_ref, y_ref, rows, band):
        for g in range(ATT_KV_HEADS):
            kl = slice(g * LANES, (g + 1) * LANES)
            vl = slice(2 * g * LANES, 2 * (g + 1) * LANES)
            qms = []
            for r in range(rep):
                hq = g * rep + r
                qb = q_ref[rows, (hq // 2) * LANES:(hq // 2 + 1) * LANES]
                qms.append(jnp.where(lane_lo != (hq % 2 == 1), qb, jnp.zeros_like(qb)))
            qs = jnp.concatenate(qms, axis=0)
            s_c = _dot_nt(qs, kc[:, kl])
            if band is not None:
                band_rows, valid = band
                s_b = _dot_nt(qs, kb[band_rows, kl])
            ps, sink_terms = [], []
            for r in range(rep):
                sl = slice(r * CHUNK, (r + 1) * CHUNK)
                sk = sink_ref[g * rep + r]
                s = s_c[sl]
                if band is not None:
                    s = jnp.concatenate([s, jnp.where(valid, s_b[sl], neg_inf)], axis=1)
                m = jnp.maximum(jnp.max(s, axis=1, keepdims=True), sk)
                ps.append(jnp.exp(s - m).astype(BF16))
                sink_terms.append(jnp.exp(sk - m))
            p = jnp.concatenate(ps, axis=0)
            acc = _dot(p[:, 0:n_ctx], vc[:, vl])
            if band is not None:
                acc = acc + _dot(p[:, n_ctx:], vb[band_rows, vl])
            for r in range(rep):
                sl = slice(r * CHUNK, (r + 1) * CHUNK)
                o = acc[sl, 0:LANES] / (acc[sl, LANES:2 * LANES] + sink_terms[r])
                if r % 2 == 1:
                    jb = (g * rep + r) // 2
                    y_ref[rows, jb * LANES:(jb + 1) * LANES] = jnp.where(lane_lo, o_even, o).astype(y_ref.dtype)
                o_even = o

    qq = _iota((CHUNK, 3 * CHUNK), 0)
    kk = _iota((CHUNK, 3 * CHUNK), 1)
    in_window = (kk >= qq) & (kk <= qq + 2 * WINDOW)

    def lat_block(i, carry):
        kpos = kk + (i - 1) * CHUNK
        valid = in_window & (kpos >= 0) & (kpos < n_lat)
        band_rows = pl.ds(pl.multiple_of(i * CHUNK, CHUNK), 3 * CHUNK)
        attend(qp, yl_ref, _chunk_rows(i), (band_rows, valid))
        return carry
    lax.fori_loop(0, ncl, lat_block, 0, unroll=UNROLL)

    if need_ctx_out:
        def ctx_block(i, carry):
            attend(qcp, yc_ref, _chunk_rows(i), None)
            return carry
        lax.fori_loop(0, ncc, ctx_block, 0, unroll=UNROLL)


def _attn_mixer(uatt_l, uatt_c, cos_t, sin_t, qgain, kgain, sink, need_ctx_out):
    bsz, n_lat, _ = uatt_l.shape
    n_ctx = uatt_c.shape[1]
    kern = functools.partial(_attn_kernel, need_ctx_out=need_ctx_out, n_lat=n_lat, n_ctx=n_ctx)
    out_specs = [pl.BlockSpec((None, n_lat, ATT_WIDTH), lambda b: (b, 0, 0))]
    out_shape = [jax.ShapeDtypeStruct((bsz, n_lat, ATT_WIDTH), BF16)]
    scratch = [
        pltpu.VMEM((n_lat, ATT_WIDTH), BF16),
        pltpu.VMEM((n_lat + 2 * CHUNK, 2 * LANES), BF16),
        pltpu.VMEM((n_lat + 2 * CHUNK, 4 * LANES), BF16),
        pltpu.VMEM((n_ctx, 2 * LANES), BF16),
        pltpu.VMEM((n_ctx, 4 * LANES), BF16),
    ]
    if need_ctx_out:
        out_specs.append(pl.BlockSpec((None, n_ctx, ATT_WIDTH), lambda b: (b, 0, 0)))
        out_shape.append(jax.ShapeDtypeStruct((bsz, n_ctx, ATT_WIDTH), BF16))
        scratch.append(pltpu.VMEM((n_ctx, ATT_WIDTH), BF16))
    outs = pl.pallas_call(
        kern,
        grid=(bsz,),
        in_specs=[
            pl.BlockSpec((None, n_lat, U_ATT), lambda b: (b, 0, 0)),
            pl.BlockSpec((None, n_ctx, U_ATT), lambda b: (b, 0, 0)),
            pl.BlockSpec((n_lat, LANES), lambda b: (0, 0)),
            pl.BlockSpec((n_lat, LANES), lambda b: (0, 0)),
            pl.BlockSpec((1, LANES), lambda b: (0, 0)),
            pl.BlockSpec((1, LANES), lambda b: (0, 0)),
            pl.BlockSpec(memory_space=pltpu.SMEM),
        ],
        out_specs=out_specs,
        out_shape=out_shape,
        scratch_shapes=scratch,
        compiler_params=pltpu.CompilerParams(
            dimension_semantics=("arbitrary",), vmem_limit_bytes=VMEM_LIMIT),
        name="attention",
    )(uatt_l, uatt_c, cos_t, sin_t, qgain, kgain, sink)
    return (outs[0], outs[1]) if need_ctx_out else (outs[0], None)


def _in_layout():
    names = (('ml_q', ML_WIDTH), ('ml_k', ML_WIDTH), ('ml_v', ML_WIDTH), ('ml_o', ML_WIDTH),
             ('ml_i', N_DIR * ML_HEADS), ('ml_f', N_DIR * ML_HEADS),
             ('ssd_z', SSD_WIDTH), ('ssd_xbc', SSD_XBC), ('ssd_dt', N_DIR * SSD_HEADS),
             ('att_q', ATT_WIDTH), ('att_k', ATT_KV_WIDTH), ('att_v', ATT_KV_WIDTH))
    out, off = {}, 0
    for name, width in names:
        out[name] = (off, off + width)
        off += width
    return out


def _permute_cols(a):
    lay = _in_layout()
    take = lambda name: a[..., lay[name][0]:lay[name][1]]
    zeros = lambda n: jnp.zeros(a.shape[:-1] + (n,), a.dtype)
    n_f, n_dt, n_i = N_DIR * ML_HEADS, N_DIR * SSD_HEADS, N_DIR * ML_HEADS
    return jnp.concatenate(
        [take('ml_q'), take('ml_k'), take('ml_v'), take('ml_o'),
         take('ssd_z'), take('ssd_xbc'),
         take('att_q'), take('att_k'), take('att_v'),
         take('ml_f'), take('ssd_dt'), zeros(LANES - n_f - n_dt),
         take('ml_i'), zeros(LANES - n_i)], axis=-1)


def _gate_rows(ml_b_i, ml_b_f, ssd_dt_bias, ssd_a_log):
    n_f, n_dt = N_DIR * ML_HEADS, N_DIR * SSD_HEADS
    row0 = jnp.concatenate([ml_b_f.reshape(-1), ssd_dt_bias.reshape(-1), jnp.zeros((LANES - n_f - n_dt,), F32)])
    row1 = jnp.concatenate([ml_b_i.reshape(-1), jnp.zeros((LANES - n_f,), F32)])
    row2 = jnp.concatenate([jnp.zeros((n_f,), F32), ssd_a_log.reshape(-1), jnp.zeros((LANES - n_f - n_dt,), F32)])
    return jnp.concatenate([row0[None], row1[None], row2[None], jnp.zeros((5, LANES), F32)], axis=0)


def _rope_tables(n_lat):
    nfreq = ATT_HEAD_DIM // 4
    pos = np.arange(n_lat)
    rows, cols = pos // GRID_W, pos % GRID_W
    inv = np.float32(ROPE_BASE) ** (-np.arange(nfreq, dtype=np.float32) / np.float32(nfreq))
    ang_r = rows.astype(np.float32)[:, None] * inv
    ang_c = cols.astype(np.float32)[:, None] * inv
    cos_h = np.concatenate([np.cos(ang_r)] * 2 + [np.cos(ang_c)] * 2, axis=1)
    sin_h = np.concatenate([-np.sin(ang_r), np.sin(ang_r), -np.sin(ang_c), np.sin(ang_c)], axis=1)
    return (jnp.asarray(np.concatenate([cos_h] * 2, axis=1), F32),
            jnp.asarray(np.concatenate([sin_h] * 2, axis=1), F32))


def _token_mixers(u_l, u_c, p, layer, tables, need_ctx_out):
    uml_l, ussd_l, uatt_l, g_l = u_l
    uml_c, ussd_c, uatt_c, g_c = u_c
    pvec = _gate_rows(p['ml_b_i'][layer], p['ml_b_f'][layer], p['ssd_dt_bias'][layer], p['ssd_a_log'][layer])
    m_l, m_c = _mlstm_mixer(uml_l, uml_c, g_l, g_c, pvec, p['ml_norm_g'][layer][None], need_ctx_out)
    s_l, s_c = _ssd_mixer(ussd_l, ussd_c, g_l, g_c, pvec, p['ssd_conv_w'][layer], p['ssd_conv_b'][layer][None],
                          jnp.repeat(p['ssd_d'][layer], SSD_HEAD_DIM)[None], p['ssd_norm_g'][layer][None],
                          need_ctx_out)
    a_l, a_c = _attn_mixer(uatt_l, uatt_c, tables['cos'], tables['sin'],
                           jnp.tile(p['att_qn_g'][layer], 2)[None], jnp.tile(p['att_kn_g'][layer], 2)[None],
                           p['att_sink'][layer], need_ctx_out)
    return (m_l, s_l, a_l), (m_c, s_c, a_c)


def kernel(x, c, ctx, c_ctx, w_mod, b_mod, norm1_g, w_in, ml_b_i, ml_b_f, ml_norm_g, ssd_conv_w, ssd_conv_b,
           ssd_a_log, ssd_dt_bias, ssd_d, ssd_norm_g, att_qn_g, att_kn_g, att_sink, w_out, norm2_g, w_up, w_down):
    p = dict(ml_b_i=ml_b_i, ml_b_f=ml_b_f, ml_norm_g=ml_norm_g, ssd_conv_w=ssd_conv_w, ssd_conv_b=ssd_conv_b,
             ssd_a_log=ssd_a_log, ssd_dt_bias=ssd_dt_bias, ssd_d=ssd_d, ssd_norm_g=ssd_norm_g,
             att_qn_g=att_qn_g, att_kn_g=att_kn_g, att_sink=att_sink)
    bsz, n_lat, d = x.shape
    n_ctx = ctx.shape[1]
    depth = w_mod.shape[0]
    mod_rows = 16
    cc = jnp.concatenate([c, c_ctx[None], jnp.zeros((mod_rows - bsz - 1, d), F32)], axis=0)
    mods = _modulation(cc, w_mod, b_mod).reshape(depth, mod_rows, 6, d)
    cos_t, sin_t = _rope_tables(n_lat)
    tables = dict(cos=cos_t, sin=sin_t)
    lat_row = lambda b: b
    ctx_row = lambda b: bsz
    h_ctx = ctx
    w = _permute_cols(w_in.astype(BF16))
    wo, wup, wdn = w_out.astype(BF16), w_up.astype(BF16), w_down.astype(BF16)
    g1, g2 = norm1_g[:, None, :], norm2_g[:, None, :]
    for layer in range(depth):
        need_ctx_out = layer < depth - 1
        u_l = _in_proj(x, mods, lat_row, g1, w, layer, tm=512)
        u_c = _in_proj(h_ctx, mods, ctx_row, g1, w, layer, tm=256)
        y_l, y_c = _token_mixers(u_l, u_c, p, layer, tables, need_ctx_out)
        x = _post(x, *y_l, mods, lat_row, g2, wo, wup, wdn, layer, tm=512)
        if need_ctx_out:
            h_ctx = _post(h_ctx, *y_c, mods, ctx_row, g2, wo, wup, wdn, layer, tm=256)
    return x
```

```python
import functools
import math

import numpy as np
import jax
import jax.numpy as jnp
from jax import lax
from jax.experimental import pallas as pl
from jax.experimental.pallas import tpu as pltpu

F32 = jnp.float32
BF16 = jnp.bfloat16

EPS = 1e-6
GRID_W = 64
N_DIR = 2
ML_HEADS = 4
ML_HEAD_DIM = 128
ML_WIDTH = ML_HEADS * ML_HEAD_DIM
SSD_HEADS = 8
SSD_HEAD_DIM = 64
SSD_WIDTH = SSD_HEADS * SSD_HEAD_DIM
SSD_STATE = 64
SSD_GROUPS = 2
SSD_XBC = SSD_WIDTH + 2 * SSD_GROUPS * SSD_STATE
ATT_Q_HEADS = 8
ATT_KV_HEADS = 2
ATT_HEAD_DIM = 64
ATT_WIDTH = ATT_Q_HEADS * ATT_HEAD_DIM
ATT_KV_WIDTH = ATT_KV_HEADS * ATT_HEAD_DIM
WINDOW = 128
ROPE_BASE = 10000.0
MIX_WIDTH = ML_WIDTH + SSD_WIDTH + ATT_WIDTH

LANES = 128
CHUNK = 128
VMEM_LIMIT = 56 * 1024 * 1024
UNROLL = 8
UNROLL_LIGHT = 8

U_ML = 4 * ML_WIDTH
U_SSD = SSD_WIDTH + SSD_XBC
U_ATT = ATT_WIDTH + 2 * ATT_KV_WIDTH
U_GATE = 2 * LANES
U_TOTAL = U_ML + U_SSD + U_ATT + U_GATE
DT_LANE0 = N_DIR * ML_HEADS

ML_SCALE = ML_HEAD_DIM ** -0.5
ATT_SCALE = ATT_HEAD_DIM ** -0.5

_NT = (((1,), (1,)), ((), ()))
_TN = (((0,), (0,)), ((), ()))


def _dot(a, b):
    return jnp.dot(a, b, preferred_element_type=F32)


def _dot_nt(a, b):
    return lax.dot_general(a, b, _NT, preferred_element_type=F32)


def _dot_tn(a, b):
    return lax.dot_general(a, b, _TN, preferred_element_type=F32)


def _split2(x):
    hi = x.astype(BF16)
    lo = (x - hi.astype(F32)).astype(BF16)
    return hi, lo


def _split3(x):
    hi = x.astype(BF16)
    r = x - hi.astype(F32)
    mid = r.astype(BF16)
    lo = (r - mid.astype(F32)).astype(BF16)
    return hi, mid, lo


def _tri_cumsums(tris, x):
    parts = _split3(x)
    return [_dot(tri, parts[0]) + _dot(tri, parts[1]) + _dot(tri, parts[2]) for tri in tris]


def _sigmoid(x):
    return 1.0 / (1.0 + jnp.exp2(x * (-1.0 / math.log(2.0))))


def _log_sigmoid(x):
    return jnp.minimum(x, 0.0) - jnp.log1p(jnp.exp(-jnp.abs(x)))


def _softplus(x):
    return jnp.maximum(x, 0.0) + jnp.log1p(jnp.exp(-jnp.abs(x)))


def _iota(shape, dim):
    return lax.broadcasted_iota(jnp.int32, shape, dim)


def _chunk_rows(c):
    return pl.ds(pl.multiple_of(c * CHUNK, CHUNK), CHUNK)


def _mod_kernel(s_ref, w_ref, b_ref, o_ref):
    s = s_ref[...]
    s = s * _sigmoid(s)
    w = w_ref[...]
    s_hi, s_lo = _split2(s)
    w_hi, w_lo = _split2(w)
    o_ref[...] = _dot(s_hi, w_hi) + (_dot(s_hi, w_lo) + _dot(s_lo, w_hi)) + b_ref[...]


def _modulation(cc, w_mod, b_mod):
    depth, d, d6 = w_mod.shape
    r = cc.shape[0]
    bn = 1024
    return pl.pallas_call(
        _mod_kernel,
        grid=(depth, d6 // bn),
        in_specs=[
            pl.BlockSpec((r, d), lambda l, j: (0, 0)),
            pl.BlockSpec((None, d, bn), lambda l, j: (l, 0, j)),
            pl.BlockSpec((None, 1, bn), lambda l, j: (l, 0, j)),
        ],
        out_specs=pl.BlockSpec((None, r, bn), lambda l, j: (l, 0, j)),
        out_shape=jax.ShapeDtypeStruct((depth, r, d6), F32),
        compiler_params=pltpu.CompilerParams(
            dimension_semantics=("arbitrary", "arbitrary"), vmem_limit_bytes=VMEM_LIMIT),
        name="modulation",
    )(cc, w_mod, b_mod.reshape(depth, 1, d6))


def _inproj_kernel(x_ref, mod_ref, g_ref, w_ref, oml_ref, ossd_ref, oatt_ref, og_ref):
    x = x_ref[...]
    ms = jnp.mean(x * x, axis=-1, keepdims=True)
    xn = x * lax.rsqrt(ms + EPS)
    h = xn * (g_ref[...] * (1.0 + mod_ref[1:2, :])) + mod_ref[0:1, :]
    hb = h.astype(BF16)
    step = 512
    off = 0
    for o_ref, width in ((oml_ref, U_ML), (ossd_ref, U_SSD), (oatt_ref, U_ATT), (og_ref, U_GATE)):
        for j in range(0, width, step):
            wj = min(step, width - j)
            o_ref[:, j:j + wj] = _dot(hb, w_ref[:, off + j:off + j + wj]).astype(o_ref.dtype)
        off += width


def _in_proj(x, mods, mod_row, g1, w, layer, tm):
    bsz, t, d = x.shape
    widths = (U_ML, U_SSD, U_ATT, U_GATE)
    dtypes = (BF16, BF16, BF16, F32)
    return pl.pallas_call(
        _inproj_kernel,
        grid=(bsz, t // tm),
        in_specs=[
            pl.BlockSpec((None, tm, d), lambda b, i: (b, i, 0)),
            pl.BlockSpec((None, None, 6, d), lambda b, i: (layer, mod_row(b), 0, 0)),
            pl.BlockSpec((None, 1, d), lambda b, i: (layer, 0, 0)),
            pl.BlockSpec((None, d, U_TOTAL), lambda b, i: (layer, 0, 0), pipeline_mode=pl.Buffered(1)),
        ],
        out_specs=[pl.BlockSpec((None, tm, wd), lambda b, i: (b, i, 0)) for wd in widths],
        out_shape=[jax.ShapeDtypeStruct((bsz, t, wd), dt) for wd, dt in zip(widths, dtypes)],
        compiler_params=pltpu.CompilerParams(
            dimension_semantics=("arbitrary", "arbitrary"), vmem_limit_bytes=VMEM_LIMIT),
        name="in_proj",
    )(x, mods, g1, w)


def _post_kernel(x_ref, yml_ref, yssd_ref, yatt_ref, mod_ref, g_ref, wo_ref, wup_ref, wdn_ref, o_ref):
    proj = (_dot(yml_ref[...], wo_ref[0:ML_WIDTH, :])
            + _dot(yssd_ref[...], wo_ref[ML_WIDTH:ML_WIDTH + SSD_WIDTH, :])
            + _dot(yatt_ref[...], wo_ref[ML_WIDTH + SSD_WIDTH:MIX_WIDTH, :]))
    x1 = x_ref[...] + mod_ref[2:3, :] * proj
    ms = jnp.mean(x1 * x1, axis=-1, keepdims=True)
    xn = x1 * lax.rsqrt(ms + EPS)
    hb = (xn * (g_ref[...] * (1.0 + mod_ref[4:5, :])) + mod_ref[3:4, :]).astype(BF16)
    d_ff = wup_ref.shape[1]
    step = 1024
    acc = None
    for j in range(0, d_ff, step):
        up = jnp.maximum(_dot(hb, wup_ref[:, j:j + step]), 0.0)
        part = _dot((up * up).astype(BF16), wdn_ref[j:j + step, :])
        acc = part if acc is None else acc + part
    o_ref[...] = x1 + mod_ref[5:6, :] * acc


def _post(x, yml, yssd, yatt, mods, mod_row, g2, wo, wup, wdn, layer, tm):
    bsz, t, d = x.shape
    d_ff = wup.shape[2]
    const = dict(pipeline_mode=pl.Buffered(1))
    return pl.pallas_call(
        _post_kernel,
        grid=(bsz, t // tm),
        in_specs=[
            pl.BlockSpec((None, tm, d), lambda b, i: (b, i, 0)),
            pl.BlockSpec((None, tm, ML_WIDTH), lambda b, i: (b, i, 0)),
            pl.BlockSpec((None, tm, SSD_WIDTH), lambda b, i: (b, i, 0)),
            pl.BlockSpec((None, tm, ATT_WIDTH), lambda b, i: (b, i, 0)),
            pl.BlockSpec((None, None, 6, d), lambda b, i: (layer, mod_row(b), 0, 0)),
            pl.BlockSpec((None, 1, d), lambda b, i: (layer, 0, 0)),
            pl.BlockSpec((None, MIX_WIDTH, d), lambda b, i: (layer, 0, 0), **const),
            pl.BlockSpec((None, d, d_ff), lambda b, i: (layer, 0, 0), **const),
            pl.BlockSpec((None, d_ff, d), lambda b, i: (layer, 0, 0), **const),
        ],
        out_specs=pl.BlockSpec((None, tm, d), lambda b, i: (b, i, 0)),
        out_shape=jax.ShapeDtypeStruct((bsz, t, d), F32),
        compiler_params=pltpu.CompilerParams(
            dimension_semantics=("arbitrary", "arbitrary"), vmem_limit_bytes=VMEM_LIMIT),
        name="post",
    )(x, yml, yssd, yatt, mods, g2, wo, wup, wdn)


ML_GATES = N_DIR * ML_HEADS


def _mlstm_kernel(ul_ref, uc_ref, gl_ref, gc_ref, pv_ref, ng_ref, sh_ref, ex_ref, *rest,
                  need_ctx_out, n_lat, n_ctx):
    if need_ctx_out:
        yl_ref, yc_ref, arow, packed, ends, mprev, cbk, crun = rest
    else:
        yl_ref, arow, packed, ends, mprev, cbk, crun = rest
        yc_ref = None
    ncl, ncc = n_lat // CHUNK, n_ctx // CHUNK
    nct = ncl + ncc
    nh = ML_HEADS
    ti = _iota((CHUNK, CHUNK), 0)
    si = _iota((CHUNK, CHUNK), 1)
    low = si <= ti
    upp = si >= ti
    tri_low = jnp.where(low, 1.0, 0.0).astype(BF16)
    tri_upp = jnp.where(upp, 1.0, 0.0).astype(BF16)
    fwd_lane = _iota((CHUNK, LANES), 1) < nh
    row = _iota((CHUNK, LANES), 0)
    bias_f = pv_ref[0:1, :]
    bias_i = pv_ref[1:2, :]
    neg_inf = -jnp.inf
    ones_blk = jnp.ones((CHUNK, LANES), BF16)

    def prep(g_ref, nch, goff):
        def body(c, carry):
            g = g_ref[_chunk_rows(c), :]
            lf = _log_sigmoid(g[:, 0:LANES] + bias_f)
            li = g[:, LANES:2 * LANES] + bias_i
            b = jnp.where(fwd_lane, *_tri_cumsums((tri_low, tri_upp), lf))
            a = li - b
            cm = a
            k = 1
            while k < CHUNK:
                before = jnp.where(row >= k, pltpu.roll(cm, k, 0), neg_inf)
                after = jnp.where(row < CHUNK - k, pltpu.roll(cm, CHUNK - k, 0), neg_inf)
                cm = jnp.maximum(cm, jnp.where(fwd_lane, before, after))
                k *= 2
            arow[goff + c] = a.T
            pk = None
            for i, part in enumerate(_split3(cm) + _split3(b)):
                t = _dot(part, sh_ref[:, i * LANES:(i + 1) * LANES])
                pk = t if pk is None else pk + t
            packed[goff + c] = pk.astype(BF16)
            ends[goff + c, 0:1, :] = jnp.where(fwd_lane[0:1], cm[CHUNK - 1:CHUNK], cm[0:1])
            ends[goff + c, 1:2, :] = jnp.where(fwd_lane[0:1], b[CHUNK - 1:CHUNK], b[0:1])
            return carry
        lax.fori_loop(0, nch, body, 0, unroll=UNROLL_LIGHT)

    prep(gc_ref, ncc, 0)
    prep(gl_ref, ncl, ncc)

    def m_scan(lo, hi, reverse, m0):
        def body(i, m):
            cg = hi - 1 - i if reverse else lo + i
            mprev[cg, int(reverse):int(reverse) + 1, :] = m
            return ends[cg, 1:2, :] + jnp.maximum(m, ends[cg, 0:1, :])
        return lax.fori_loop(0, hi - lo, body, m0)

    m_zero = jnp.zeros((1, LANES), F32)
    m_scan(0, nct, False, m_zero)
    m_scan(ncc, nct, True, m_scan(0, ncc, True, m_zero))

    def load_kv(u_ref, rows, h):
        k = u_ref[rows, ML_WIDTH + h * LANES:ML_WIDTH + (h + 1) * LANES]
        v = u_ref[rows, 2 * ML_WIDTH + h * LANES:2 * ML_WIDTH + (h + 1) * LANES]
        return k, jnp.concatenate([v, ones_blk], axis=1)

    def update(d, h, cg, k, v1):
        j = d * nh + h
        m_prev = mprev[cg, d:d + 1, j:j + 1]
        m_end = jnp.maximum(m_prev, ends[cg, 0:1, j:j + 1])
        w_row = jnp.exp(arow[cg, j:j + 1, :] - m_end) * ML_SCALE
        ktw = (k.T.astype(F32) * w_row).astype(BF16)
        crun[d, h] = jnp.exp(m_prev - m_end) * crun[d, h] + _dot(ktw, v1)

    def bwd_pass(u_ref, nch, goff, keep):
        def body(i, carry):
            c = nch - 1 - i
            rows = _chunk_rows(c)
            for h in range(nh):
                if keep:
                    cbk[goff + c, h] = crun[1, h].astype(BF16)
                k, v1 = load_kv(u_ref, rows, h)
                update(1, h, goff + c, k, v1)
            return carry
        lax.fori_loop(0, nch, body, 0, unroll=UNROLL_LIGHT)

    def fwd_pass(u_ref, y_ref, nch, goff):
        def body(c, carry):
            rows = _chunk_rows(c)
            cg = goff + c
            for h in range(nh):
                k, v1 = load_kv(u_ref, rows, h)
                if y_ref is not None:
                    q = u_ref[rows, h * LANES:(h + 1) * LANES]
                    o = u_ref[rows, 3 * ML_WIDTH + h * LANES:3 * ML_WIDTH + (h + 1) * LANES]
                    sqk = _dot_nt(q, k) * ML_SCALE
                    qc = _dot(q, jnp.concatenate([crun[0, h].astype(BF16), cbk[cg, h]], axis=1))
                    s_dirs, stats = [], []
                    for d, mask in ((0, low), (1, upp)):
                        j = d * nh + h
                        rep = _dot(packed[cg], ex_ref[:, 2 * j * LANES:2 * (j + 1) * LANES])
                        m_prev = mprev[cg, d:d + 1, j:j + 1]
                        m_t = jnp.maximum(rep[:, 0:LANES], m_prev)
                        s_dirs.append(sqk * jnp.exp(jnp.where(mask, arow[cg, j:j + 1, :] - m_t, neg_inf)))
                        stats.append((m_prev, m_t, rep[:, LANES:2 * LANES]))
                    sv = _dot(jnp.concatenate(s_dirs, axis=0).astype(BF16), v1)
                    hsum = None
                    for d, (m_prev, m_t, b_t) in enumerate(stats):
                        g = jnp.exp(m_prev - m_t)
                        sv_d = sv[d * CHUNK:(d + 1) * CHUNK, :]
                        qc_d = qc[:, 2 * d * LANES:2 * (d + 1) * LANES]
                        num = sv_d[:, 0:LANES] + g * qc_d[:, 0:LANES]
                        den = sv_d[:, LANES:2 * LANES] + g * qc_d[:, LANES:2 * LANES]
                        hd = num / jnp.maximum(jnp.abs(den), jnp.exp(-(b_t + m_t)))
                        hsum = hd if hsum is None else hsum + hd
                    ms = jnp.mean(hsum * hsum, axis=1, keepdims=True)
                    y = hsum * lax.rsqrt(ms + EPS) * ng_ref[:, h * LANES:(h + 1) * LANES]
                    y = y * _sigmoid(o.astype(F32))
                    y_ref[rows, h * LANES:(h + 1) * LANES] = y.astype(y_ref.dtype)
                update(0, h, cg, k, v1)
            return carry
        lax.fori_loop(0, nch, body, 0, unroll=UNROLL)

    crun[...] = jnp.zeros(crun.shape, F32)
    bwd_pass(uc_ref, ncc, 0, need_ctx_out)
    bwd_pass(ul_ref, ncl, ncc, True)
    fwd_pass(uc_ref, yc_ref, ncc, 0)
    fwd_pass(ul_ref, yl_ref, ncl, ncc)


def _ml_lane_tables():
    shifts = np.zeros((LANES, 6 * LANES), np.float32)
    expand = np.zeros((LANES, ML_GATES * 2 * LANES), np.float32)
    for j in range(ML_GATES):
        for i in range(6):
            shifts[j, i * LANES + ML_GATES * i + j] = 1.0
            half = i // 3
            expand[ML_GATES * i + j, (2 * j + half) * LANES:(2 * j + half + 1) * LANES] = 1.0
    return jnp.asarray(shifts, BF16), jnp.asarray(expand, BF16)


def _mlstm_mixer(uml_l, uml_c, g_l, g_c, pvec, norm_g, need_ctx_out):
    shifts, expand = _ml_lane_tables()
    bsz, n_lat, _ = uml_l.shape
    n_ctx = uml_c.shape[1]
    nct = (n_lat + n_ctx) // CHUNK
    kern = functools.partial(_mlstm_kernel, need_ctx_out=need_ctx_out, n_lat=n_lat, n_ctx=n_ctx)
    out_specs = [pl.BlockSpec((None, n_lat, ML_WIDTH), lambda b: (b, 0, 0))]
    out_shape = [jax.ShapeDtypeStruct((bsz, n_lat, ML_WIDTH), BF16)]
    if need_ctx_out:
        out_specs.append(pl.BlockSpec((None, n_ctx, ML_WIDTH), lambda b: (b, 0, 0)))
        out_shape.append(jax.ShapeDtypeStruct((bsz, n_ctx, ML_WIDTH), BF16))
    outs = pl.pallas_call(
        kern,
        grid=(bsz,),
        in_specs=[
            pl.BlockSpec((None, n_lat, U_ML), lambda b: (b, 0, 0)),
            pl.BlockSpec((None, n_ctx, U_ML), lambda b: (b, 0, 0)),
            pl.BlockSpec((None, n_lat, U_GATE), lambda b: (b, 0, 0)),
            pl.BlockSpec((None, n_ctx, U_GATE), lambda b: (b, 0, 0)),
            pl.BlockSpec((8, LANES), lambda b: (0, 0)),
            pl.BlockSpec((1, ML_WIDTH), lambda b: (0, 0)),
            pl.BlockSpec(shifts.shape, lambda b: (0, 0)),
            pl.BlockSpec(expand.shape, lambda b: (0, 0)),
        ],
        out_specs=out_specs,
        out_shape=out_shape,
        scratch_shapes=[
            pltpu.VMEM((nct, LANES, CHUNK), F32),
            pltpu.VMEM((nct, CHUNK, LANES), BF16),
            pltpu.VMEM((nct, 8, LANES), F32),
            pltpu.VMEM((nct, 8, LANES), F32),
            pltpu.VMEM((nct, ML_HEADS, ML_HEAD_DIM, 2 * LANES), BF16),
            pltpu.VMEM((N_DIR, ML_HEADS, ML_HEAD_DIM, 2 * LANES), F32),
        ],
        compiler_params=pltpu.CompilerParams(
            dimension_semantics=("arbitrary",), vmem_limit_bytes=VMEM_LIMIT),
        name="mlstm",
    )(uml_l, uml_c, g_l, g_c, pvec, norm_g, shifts, expand)
    return (outs[0], outs[1]) if need_ctx_out else (outs[0], None)


def _ssd_kernel(ul_ref, uc_ref, gl_ref, gc_ref, pv_ref, cw_ref, cb_ref, dsk_ref, ng_ref, sh_ref, ex_ref, *rest,
                need_ctx_out, n_lat, n_ctx):
    if need_ctx_out:
        yl_ref, yc_ref, xact, acum, packed, act, dtt, hbk, hrun = rest
    else:
        yl_ref, xact, acum, packed, act, dtt, hbk, hrun = rest
        yc_ref = None
    ncl, ncc = n_lat // CHUNK, n_ctx // CHUNK
    nh, hg = SSD_HEADS, SSD_HEADS // SSD_GROUPS
    gw = hg * SSD_HEAD_DIM
    gn = SSD_GROUPS * SSD_STATE
    ti = _iota((CHUNK, CHUNK), 0)
    si = _iota((CHUNK, CHUNK), 1)
    low_strict = si < ti
    upp_strict = si > ti
    tri_low = jnp.where(si <= ti, 1.0, 0.0).astype(BF16)
    tri_upp = jnp.where(si >= ti, 1.0, 0.0).astype(BF16)
    lane = _iota((CHUNK, LANES), 1)
    fwd_lane = (lane >= DT_LANE0) & (lane < DT_LANE0 + nh)
    bias0 = pv_ref[0:1, :]
    a_row = -jnp.exp(pv_ref[2:3, :])
    shift_down = jnp.where(si == ti - 1, 1.0, 0.0).astype(BF16)
    shift_up = jnp.where(si == ti + 1, 1.0, 0.0).astype(BF16)
    lane_group0 = lane < SSD_STATE

    def prep(u_ref, g_ref, nch, goff, nrows):
        def body(c, carry):
            r = pl.multiple_of(c * CHUNK, CHUNK)
            cur = u_ref[pl.ds(r, CHUNK), SSD_WIDTH:U_SSD]
            rp = pl.multiple_of(jnp.maximum(r - CHUNK, 0), CHUNK)
            rn = pl.multiple_of(jnp.minimum(r + CHUNK, nrows - CHUNK), CHUNK)
            prev = u_ref[pl.ds(rp, CHUNK), SSD_WIDTH:U_SSD]
            nxt = u_ref[pl.ds(rn, CHUNK), SSD_WIDTH:U_SSD]
            from_prev = jnp.where((ti == 0) & (si == CHUNK - 1) & (c > 0), 1.0, 0.0).astype(BF16)
            from_next = jnp.where((ti == CHUNK - 1) & (si == 0) & (c < nch - 1), 1.0, 0.0).astype(BF16)
            xm = _dot(jnp.concatenate([shift_down, from_prev], axis=1), jnp.concatenate([cur, prev], axis=0))
            xp = _dot(jnp.concatenate([shift_up, from_next], axis=1), jnp.concatenate([cur, nxt], axis=0))
            conv = (cw_ref[0:1, :] * xm + cw_ref[1:2, :] * cur.astype(F32) + cw_ref[2:3, :] * xp
                    + cb_ref[...])
            xact[goff + c] = (conv * _sigmoid(conv)).astype(BF16)

            dt = _softplus(g_ref[pl.ds(r, CHUNK), :] + bias0)
            la = dt * a_row
            ac = jnp.where(fwd_lane, *_tri_cumsums((tri_low, tri_upp), la))
            a_end = jnp.where(fwd_lane[0:1, :], ac[CHUNK - 1:CHUNK, :], ac[0:1, :])
            acum[goff + c] = ac
            act[goff + c] = (ac - jnp.log(dt)).T
            dtt[goff + c] = dt.T
            pk = None
            for i, part in enumerate(_split3(jnp.exp(ac)) + _split3(jnp.exp(a_end - ac) * dt)):
                t = _dot(part, sh_ref[:, i * LANES:(i + 1) * LANES])
                pk = t if pk is None else pk + t
            packed[goff + c] = pk.astype(BF16)
            return carry
        lax.fori_loop(0, nch, body, 0, unroll=UNROLL_LIGHT)

    prep(uc_ref, gc_ref, ncc, 0, n_ctx)
    prep(ul_ref, gl_ref, ncl, ncc, n_lat)

    def e_cols(d, g):
        return slice(d * SSD_WIDTH + g * gw, d * SSD_WIDTH + (g + 1) * gw)

    def w_cols(d, g):
        return slice((N_DIR + d) * SSD_WIDTH + g * gw, (N_DIR + d) * SSD_WIDTH + (g + 1) * gw)

    def update_state(d, cg):
        pk = packed[cg]
        xa = xact[cg]
        b_all = xa[:, SSD_WIDTH:SSD_WIDTH + gn]
        zero = jnp.zeros_like(b_all)
        upd = []
        for g in range(SSD_GROUPS):
            wx = (_dot(pk, ex_ref[:, w_cols(d, g)]) * xa[:, g * gw:(g + 1) * gw].astype(F32)).astype(BF16)
            upd.append(_dot_tn(jnp.where(lane_group0 == (g == 0), b_all, zero), wx))
        tail, last = (slice(CHUNK - 16, CHUNK), 15) if d == 0 else (slice(0, 16), 0)
        decay = _dot(pk[tail], ex_ref[:, d * SSD_WIDTH:(d + 1) * SSD_WIDTH])[last:last + 1, :]
        hrun[d] = decay * hrun[d] + jnp.concatenate(upd, axis=1)

    def bwd_pass(nch, goff, keep):
        def body(i, carry):
            cg = goff + nch - 1 - i
            if keep:
                hbk[cg] = hrun[1].astype(BF16)
            update_state(1, cg)
            return carry
        lax.fori_loop(0, nch, body, 0, unroll=UNROLL_LIGHT)

    def fwd_pass(u_ref, y_ref, nch, goff):
        def body(c, carry):
            cg = goff + c
            if y_ref is not None:
                rows = _chunk_rows(c)
                xa = xact[cg]
                x = xa[:, 0:SSD_WIDTH]
                b_all = xa[:, SSD_WIDTH:SSD_WIDTH + gn]
                c_all = xa[:, SSD_WIDTH + gn:SSD_XBC]
                ac = acum[cg]
                ac_t = act[cg]
                dt_t = dtt[cg]
                lane_w = _iota((CHUNK, gw), 1)
                pk = packed[cg]
                yz_groups, ssq = [], None
                for g in range(SSD_GROUPS):
                    cols = slice(g * gw, (g + 1) * gw)
                    c_g = jnp.where(lane_group0 == (g == 0), c_all, jnp.zeros_like(c_all))
                    cb = _dot_nt(c_g, b_all)
                    s_heads = []
                    for hh in range(hg):
                        jf = DT_LANE0 + g * hg + hh
                        jb = jf + nh
                        on_diag = jnp.log(dt_t[jf:jf + 1, :] + dt_t[jb:jb + 1, :])
                        expo = jnp.where(low_strict, ac[:, jf:jf + 1] - ac_t[jf:jf + 1, :],
                                         jnp.where(upp_strict, ac[:, jb:jb + 1] - ac_t[jb:jb + 1, :], on_diag))
                        s_heads.append((cb * jnp.exp(expo)).astype(BF16))
                    x_g = x[:, cols]
                    bd = jnp.concatenate(
                        [jnp.where((lane_w >= hh * SSD_HEAD_DIM) & (lane_w < (hh + 1) * SSD_HEAD_DIM),
                                   x_g, jnp.zeros_like(x_g)) for hh in range(hg)], axis=0)
                    y = _dot(jnp.concatenate(s_heads, axis=1), bd)
                    y = y + _dot(c_all, hrun[0, :, cols].astype(BF16)) * _dot(pk, ex_ref[:, e_cols(0, g)])
                    y = y + _dot(c_all, hbk[cg, :, cols]) * _dot(pk, ex_ref[:, e_cols(1, g)])
                    y = y + dsk_ref[:, cols] * x_g.astype(F32)
                    z = u_ref[rows, cols].astype(F32)
                    yz = y * (z * _sigmoid(z))
                    part = jnp.sum(yz * yz, axis=1, keepdims=True)
                    ssq = part if ssq is None else ssq + part
                    yz_groups.append(yz)
                scale = lax.rsqrt(ssq * (1.0 / SSD_WIDTH) + EPS)
                for g, yz in enumerate(yz_groups):
                    cols = slice(g * gw, (g + 1) * gw)
                    y_ref[rows, cols] = (yz * scale * ng_ref[:, cols]).astype(y_ref.dtype)
            update_state(0, cg)
            return carry
        lax.fori_loop(0, nch, body, 0, unroll=UNROLL)

    hrun[...] = jnp.zeros(hrun.shape, F32)
    bwd_pass(ncc, 0, need_ctx_out)
    bwd_pass(ncl, ncc, True)
    fwd_pass(uc_ref, yc_ref, ncc, 0)
    fwd_pass(ul_ref, yl_ref, ncl, ncc)


def _ssd_lane_tables():
    nj = N_DIR * SSD_HEADS
    shifts = np.zeros((LANES, 6 * LANES), np.float32)
    expand = np.zeros((LANES, 2 * N_DIR * SSD_WIDTH), np.float32)
    for i in range(6):
        for d in range(N_DIR):
            for h in range(SSD_HEADS):
                j = d * SSD_HEADS + h
                shifts[DT_LANE0 + j, i * LANES + nj * i + j] = 1.0
                c0 = ((i // 3) * N_DIR + d) * SSD_WIDTH + h * SSD_HEAD_DIM
                expand[nj * i + j, c0:c0 + SSD_HEAD_DIM] = 1.0
    return jnp.asarray(shifts, BF16), jnp.asarray(expand, BF16)


def _ssd_mixer(ussd_l, ussd_c, g_l, g_c, pvec, conv_w, conv_b, d_skip, norm_g, need_ctx_out):
    shifts, expand = _ssd_lane_tables()
    bsz, n_lat, _ = ussd_l.shape
    n_ctx = ussd_c.shape[1]
    nct = (n_lat + n_ctx) // CHUNK
    gn = SSD_GROUPS * SSD_STATE
    kern = functools.partial(_ssd_kernel, need_ctx_out=need_ctx_out, n_lat=n_lat, n_ctx=n_ctx)
    out_specs = [pl.BlockSpec((None, n_lat, SSD_WIDTH), lambda b: (b, 0, 0))]
    out_shape = [jax.ShapeDtypeStruct((bsz, n_lat, SSD_WIDTH), BF16)]
    if need_ctx_out:
        out_specs.append(pl.BlockSpec((None, n_ctx, SSD_WIDTH), lambda b: (b, 0, 0)))
        out_shape.append(jax.ShapeDtypeStruct((bsz, n_ctx, SSD_WIDTH), BF16))
    outs = pl.pallas_call(
        kern,
        grid=(bsz,),
        in_specs=[
            pl.BlockSpec((None, n_lat, U_SSD), lambda b: (b, 0, 0)),
            pl.BlockSpec((None, n_ctx, U_SSD), lambda b: (b, 0, 0)),
            pl.BlockSpec((None, n_lat, LANES), lambda b: (b, 0, 0)),
            pl.BlockSpec((None, n_ctx, LANES), lambda b: (b, 0, 0)),
            pl.BlockSpec((8, LANES), lambda b: (0, 0)),
            pl.BlockSpec((3, SSD_XBC), lambda b: (0, 0)),
            pl.BlockSpec((1, SSD_XBC), lambda b: (0, 0)),
            pl.BlockSpec((1, SSD_WIDTH), lambda b: (0, 0)),
            pl.BlockSpec((1, SSD_WIDTH), lambda b: (0, 0)),
            pl.BlockSpec(shifts.shape, lambda b: (0, 0)),
            pl.BlockSpec(expand.shape, lambda b: (0, 0)),
        ],
        out_specs=out_specs,
        out_shape=out_shape,
        scratch_shapes=[
            pltpu.VMEM((nct, CHUNK, SSD_XBC), BF16),
            pltpu.VMEM((nct, CHUNK, LANES), F32),
            pltpu.VMEM((nct, CHUNK, LANES), BF16),
            pltpu.VMEM((nct, LANES, CHUNK), F32),
            pltpu.VMEM((nct, LANES, CHUNK), F32),
            pltpu.VMEM((nct, gn, SSD_WIDTH), BF16),
            pltpu.VMEM((N_DIR, gn, SSD_WIDTH), F32),
        ],
        compiler_params=pltpu.CompilerParams(
            dimension_semantics=("arbitrary",), vmem_limit_bytes=VMEM_LIMIT),
        name="ssd",
    )(ussd_l, ussd_c, g_l, g_c, pvec, conv_w, conv_b, d_skip, norm_g, shifts, expand)
    return (outs[0], outs[1]) if need_ctx_out else (outs[0], None)


def _head_norm(x, gain):
    rows = x.shape[0]
    blks = [x[:, j * LANES:(j + 1) * LANES] for j in range(x.shape[1] // LANES)]
    same_head = (_iota((LANES, LANES), 0) < ATT_HEAD_DIM) == (_iota((LANES, LANES), 1) < ATT_HEAD_DIM)
    sq = jnp.concatenate([b * b for b in blks], axis=0).astype(BF16)
    ms = _dot(sq, jnp.where(same_head, 1.0, 0.0).astype(BF16)) * (1.0 / ATT_HEAD_DIM)
    outs = [b * lax.rsqrt(ms[j * rows:(j + 1) * rows] + EPS) * gain for j, b in enumerate(blks)]
    return outs[0] if len(outs) == 1 else jnp.concatenate(outs, axis=1)


def _rope(x, cos_t, sin_t):
    w = x.shape[1]
    reps = w // LANES
    nfreq = ATT_HEAD_DIM // 4
    partner = jnp.where((_iota(x.shape, 1) & (2 * nfreq - 1)) < nfreq,
                        pltpu.roll(x, w - nfreq, 1), pltpu.roll(x, nfreq, 1))
    if reps > 1:
        cos_t = jnp.concatenate([cos_t] * reps, axis=1)
        sin_t = jnp.concatenate([sin_t] * reps, axis=1)
    return x * cos_t + partner * sin_t


def _dup_heads(x):
    lane_lo = _iota(x.shape, 1) < ATT_HEAD_DIM
    sw = pltpu.roll(x, ATT_HEAD_DIM, 1)
    return jnp.concatenate([jnp.where(lane_lo, x, sw), jnp.where(lane_lo, sw, x)], axis=1)


def _attn_kernel(ul_ref, uc_ref, cos_ref, sin_ref, qg_ref, kg_ref, sink_ref, *rest,
                 need_ctx_out, n_lat, n_ctx):
    if need_ctx_out:
        yl_ref, yc_ref, qp, kb, vb, kc, vc, qcp = rest
    else:
        yl_ref, qp, kb, vb, kc, vc = rest
        yc_ref = qcp = None
    ncl, ncc = n_lat // CHUNK, n_ctx // CHUNK
    rep = ATT_Q_HEADS // ATT_KV_HEADS
    kv0 = ATT_WIDTH
    lane_lo = _iota((CHUNK, LANES), 1) < ATT_HEAD_DIM
    neg_inf = -jnp.inf
    qg = qg_ref[...]
    kg = kg_ref[...]

    for ref in (kb, vb):
        zeros_blk = jnp.zeros((CHUNK, ref.shape[1]), BF16)
        ref[0:CHUNK, :] = zeros_blk
        ref[CHUNK + n_lat:2 * CHUNK + n_lat, :] = zeros_blk

    def values_with_ones(v):
        d = _dup_heads(v).astype(BF16)
        ones = jnp.ones((v.shape[0], LANES), BF16)
        return jnp.concatenate([d[:, 0:LANES], ones, d[:, LANES:2 * LANES], ones], axis=1)

    def prep_lat(c, carry):
        rows = _chunk_rows(c)
        cos_t = cos_ref[rows, :]
        sin_t = sin_ref[rows, :]
        q = _rope(_head_norm(ul_ref[rows, 0:ATT_WIDTH].astype(F32), qg), cos_t, sin_t)
        qp[rows, :] = (q * ATT_SCALE).astype(BF16)
        k = _rope(_head_norm(ul_ref[rows, kv0:kv0 + LANES].astype(F32), kg), cos_t, sin_t)
        v = ul_ref[rows, kv0 + LANES:kv0 + 2 * LANES].astype(F32)
        dst = pl.ds(pl.multiple_of((c + 1) * CHUNK, CHUNK), CHUNK)
        kb[dst, :] = _dup_heads(k).astype(BF16)
        vb[dst, :] = values_with_ones(v)
        return carry
    lax.fori_loop(0, ncl, prep_lat, 0, unroll=UNROLL_LIGHT)

    def prep_ctx(c, carry):
        rows = _chunk_rows(c)
        k = _head_norm(uc_ref[rows, kv0:kv0 + LANES].astype(F32), kg)
        v = uc_ref[rows, kv0 + LANES:kv0 + 2 * LANES].astype(F32)
        kc[rows, :] = _dup_heads(k).astype(BF16)
        vc[rows, :] = values_with_ones(v)
        if need_ctx_out:
            q = _head_norm(uc_ref[rows, 0:ATT_WIDTH].astype(F32), qg)
            qcp[rows, :] = (q * ATT_SCALE).astype(BF16)
        return carry
    lax.fori_loop(0, ncc, prep_ctx, 0, unroll=UNROLL)

    def attend(q_ref, y_ref, rows, band):
        for g in range(ATT_KV_HEADS):
            kl = slice(g * LANES, (g + 1) * LANES)
            vl = slice(2 * g * LANES, 2 * (g + 1) * LANES)
            qms = []
            for r in range(rep):
                hq = g * rep + r
                qb = q_ref[rows, (hq // 2) * LANES:(hq // 2 + 1) * LANES]
                qms.append(jnp.where(lane_lo != (hq % 2 == 1), qb, jnp.zeros_like(qb)))
            qs = jnp.concatenate(qms, axis=0)
            s_c = _dot_nt(qs, kc[:, kl])
            if band is not None:
                band_rows, valid = band
                s_b = _dot_nt(qs, kb[band_rows, kl])
            ps, sink_terms = [], []
            for r in range(rep):
                sl = slice(r * CHUNK, (r + 1) * CHUNK)
                sk = sink_ref[g * rep + r]
                s = s_c[sl]
                if band is not None:
                    s = jnp.concatenate([s, jnp.where(valid, s_b[sl], neg_inf)], axis=1)
                m = jnp.maximum(jnp.max(s, axis=1, keepdims=True), sk)
                ps.append(jnp.exp(s - m).astype(BF16))
                sink_terms.append(jnp.exp(sk - m))
            p = jnp.concatenate(ps, axis=0)
            acc = _dot(p[:, 0:n_ctx], vc[:, vl])
            if band is not None:
                acc = acc + _dot(p[:, n_ctx:], vb[band_rows, vl])
            for r in range(rep):
                sl = slice(r * CHUNK, (r + 1) * CHUNK)
                o = acc[sl, 0:LANES] / (acc[sl, LANES:2 * LANES] + sink_terms[r])
                if r % 2 == 1:
                    jb = (g * rep + r) // 2
                    y_ref[rows, jb * LANES:(jb + 1) * LANES] = jnp.where(lane_lo, o_even, o).astype(y_ref.dtype)
                o_even = o

    qq = _iota((CHUNK, 3 * CHUNK), 0)
    kk = _iota((CHUNK, 3 * CHUNK), 1)
    in_window = (kk >= qq) & (kk <= qq + 2 * WINDOW)

    def lat_block(i, carry):
        kpos = kk + (i - 1) * CHUNK
        valid = in_window & (kpos >= 0) & (kpos < n_lat)
        band_rows = pl.ds(pl.multiple_of(i * CHUNK, CHUNK), 3 * CHUNK)
        attend(qp, yl_ref, _chunk_rows(i), (band_rows, valid))
        return carry
    lax.fori_loop(0, ncl, lat_block, 0, unroll=UNROLL)

    if need_ctx_out:
        def ctx_block(i, carry):
            attend(qcp, yc_ref, _chunk_rows(i), None)
            return carry
        lax.fori_loop(0, ncc, ctx_block, 0, unroll=UNROLL)


def _attn_mixer(uatt_l, uatt_c, cos_t, sin_t, qgain, kgain, sink, need_ctx_out):
    bsz, n_lat, _ = uatt_l.shape
    n_ctx = uatt_c.shape[1]
    kern = functools.partial(_attn_kernel, need_ctx_out=need_ctx_out, n_lat=n_lat, n_ctx=n_ctx)
    out_specs = [pl.BlockSpec((None, n_lat, ATT_WIDTH), lambda b: (b, 0, 0))]
    out_shape = [jax.ShapeDtypeStruct((bsz, n_lat, ATT_WIDTH), BF16)]
    scratch = [
        pltpu.VMEM((n_lat, ATT_WIDTH), BF16),
        pltpu.VMEM((n_lat + 2 * CHUNK, 2 * LANES), BF16),
        pltpu.VMEM((n_lat + 2 * CHUNK, 4 * LANES), BF16),
        pltpu.VMEM((n_ctx, 2 * LANES), BF16),
        pltpu.VMEM((n_ctx, 4 * LANES), BF16),
    ]
    if need_ctx_out:
        out_specs.append(pl.BlockSpec((None, n_ctx, ATT_WIDTH), lambda b: (b, 0, 0)))
        out_shape.append(jax.ShapeDtypeStruct((bsz, n_ctx, ATT_WIDTH), BF16))
        scratch.append(pltpu.VMEM((n_ctx, ATT_WIDTH), BF16))
    outs = pl.pallas_call(
        kern,
        grid=(bsz,),
        in_specs=[
            pl.BlockSpec((None, n_lat, U_ATT), lambda b: (b, 0, 0)),
            pl.BlockSpec((None, n_ctx, U_ATT), lambda b: (b, 0, 0)),
            pl.BlockSpec((n_lat, LANES), lambda b: (0, 0)),
            pl.BlockSpec((n_lat, LANES), lambda b: (0, 0)),
            pl.BlockSpec((1, LANES), lambda b: (0, 0)),
            pl.BlockSpec((1, LANES), lambda b: (0, 0)),
            pl.BlockSpec(memory_space=pltpu.SMEM),
        ],
        out_specs=out_specs,
        out_shape=out_shape,
        scratch_shapes=scratch,
        compiler_params=pltpu.CompilerParams(
            dimension_semantics=("arbitrary",), vmem_limit_bytes=VMEM_LIMIT),
        name="attention",
    )(uatt_l, uatt_c, cos_t, sin_t, qgain, kgain, sink)
    return (outs[0], outs[1]) if need_ctx_out else (outs[0], None)


def _in_layout():
    names = (('ml_q', ML_WIDTH), ('ml_k', ML_WIDTH), ('ml_v', ML_WIDTH), ('ml_o', ML_WIDTH),
             ('ml_i', N_DIR * ML_HEADS), ('ml_f', N_DIR * ML_HEADS),
             ('ssd_z', SSD_WIDTH), ('ssd_xbc', SSD_XBC), ('ssd_dt', N_DIR * SSD_HEADS),
             ('att_q', ATT_WIDTH), ('att_k', ATT_KV_WIDTH), ('att_v', ATT_KV_WIDTH))
    out, off = {}, 0
    for name, width in names:
        out[name] = (off, off + width)
        off += width
    return out


def _col_map():
    lay = _in_layout()
    take = lambda name: np.arange(lay[name][0], lay[name][1])
    zeros = lambda n: np.full((n,), -1)
    n_f, n_dt, n_i = N_DIR * ML_HEADS, N_DIR * SSD_HEADS, N_DIR * ML_HEADS
    return np.concatenate(
        [take('ml_q'), take('ml_k'), take('ml_v'), take('ml_o'),
         take('ssd_z'), take('ssd_xbc'),
         take('att_q'), take('att_k'), take('att_v'),
         take('ml_f'), take('ssd_dt'), zeros(LANES - n_f - n_dt),
         take('ml_i'), zeros(LANES - n_i)])


def _permute_plan():
    src = _col_map()
    plan, mats = [], []
    for t in range(U_TOTAL // LANES):
        cols = src[t * LANES:(t + 1) * LANES]
        if cols[0] >= 0 and cols[0] % LANES == 0 and np.array_equal(cols, cols[0] + np.arange(LANES)):
            plan.append(('copy', int(cols[0]) // LANES))
            continue
        tiles = sorted({int(c) // LANES for c in cols if c >= 0})
        m = np.zeros((2 * LANES, LANES), np.float32)
        for lane, c in enumerate(cols):
            if c >= 0:
                m[tiles.index(int(c) // LANES) * LANES + int(c) % LANES, lane] = 1.0
        assert len(tiles) <= 2
        plan.append(('mix', tuple(tiles), len(mats)))
        mats.append(m)
    return tuple(plan), np.stack(mats)


def _wprep_kernel(w_ref, p_ref, o_ref, *, plan):
    rows, in_width = w_ref.shape

    def src_tile(a):
        lo, hi = a * LANES, min((a + 1) * LANES, in_width)
        t = w_ref[:, lo:hi].astype(BF16)
        if hi - lo < LANES:
            t = jnp.concatenate([t, jnp.zeros((rows, LANES - (hi - lo)), BF16)], axis=1)
        return t

    for t, step in enumerate(plan):
        if step[0] == 'copy':
            o_ref[:, t * LANES:(t + 1) * LANES] = src_tile(step[1])
        else:
            tiles, mi = step[1], step[2]
            lhs = jnp.concatenate([src_tile(a) for a in tiles], axis=1) if len(tiles) > 1 else src_tile(tiles[0])
            o_ref[:, t * LANES:(t + 1) * LANES] = _dot(lhs, p_ref[mi, 0:len(tiles) * LANES, :]).astype(BF16)


def _permute_weights(w_in):
    depth, d, in_width = w_in.shape
    plan, mats = _permute_plan()
    tr = 256
    return pl.pallas_call(
        functools.partial(_wprep_kernel, plan=plan),
        grid=(depth, d // tr),
        in_specs=[pl.BlockSpec((None, tr, in_width), lambda l, i: (l, i, 0)),
                  pl.BlockSpec(mats.shape, lambda l, i: (0, 0, 0))],
        out_specs=pl.BlockSpec((None, tr, U_TOTAL), lambda l, i: (l, i, 0)),
        out_shape=jax.ShapeDtypeStruct((depth, d, U_TOTAL), BF16),
        compiler_params=pltpu.CompilerParams(
            dimension_semantics=("arbitrary", "arbitrary"), vmem_limit_bytes=VMEM_LIMIT),
        name="permute_weights",
    )(w_in, jnp.asarray(mats, BF16))


def _gate_rows(ml_b_i, ml_b_f, ssd_dt_bias, ssd_a_log):
    n_f, n_dt = N_DIR * ML_HEADS, N_DIR * SSD_HEADS
    row0 = jnp.concatenate([ml_b_f.reshape(-1), ssd_dt_bias.reshape(-1), jnp.zeros((LANES - n_f - n_dt,), F32)])
    row1 = jnp.concatenate([ml_b_i.reshape(-1), jnp.zeros((LANES - n_f,), F32)])
    row2 = jnp.concatenate([jnp.zeros((n_f,), F32), ssd_a_log.reshape(-1), jnp.zeros((LANES - n_f - n_dt,), F32)])
    return jnp.concatenate([row0[None], row1[None], row2[None], jnp.zeros((5, LANES), F32)], axis=0)


def _rope_tables(n_lat):
    nfreq = ATT_HEAD_DIM // 4
    pos = np.arange(n_lat)
    rows, cols = pos // GRID_W, pos % GRID_W
    inv = np.float32(ROPE_BASE) ** (-np.arange(nfreq, dtype=np.float32) / np.float32(nfreq))
    ang_r = rows.astype(np.float32)[:, None] * inv
    ang_c = cols.astype(np.float32)[:, None] * inv
    cos_h = np.concatenate([np.cos(ang_r)] * 2 + [np.cos(ang_c)] * 2, axis=1)
    sin_h = np.concatenate([-np.sin(ang_r), np.sin(ang_r), -np.sin(ang_c), np.sin(ang_c)], axis=1)
    return (jnp.asarray(np.concatenate([cos_h] * 2, axis=1), F32),
            jnp.asarray(np.concatenate([sin_h] * 2, axis=1), F32))


def _token_mixers(u_l, u_c, p, layer, tables, need_ctx_out):
    uml_l, ussd_l, uatt_l, g_l = u_l
    uml_c, ussd_c, uatt_c, g_c = u_c
    pvec = _gate_rows(p['ml_b_i'][layer], p['ml_b_f'][layer], p['ssd_dt_bias'][layer], p['ssd_a_log'][layer])
    m_l, m_c = _mlstm_mixer(uml_l, uml_c, g_l, g_c, pvec, p['ml_norm_g'][layer][None], need_ctx_out)
    s_l, s_c = _ssd_mixer(ussd_l, ussd_c, g_l, g_c, pvec, p['ssd_conv_w'][layer], p['ssd_conv_b'][layer][None],
                          jnp.repeat(p['ssd_d'][layer], SSD_HEAD_DIM)[None], p['ssd_norm_g'][layer][None],
                          need_ctx_out)
    a_l, a_c = _attn_mixer(uatt_l, uatt_c, tables['cos'], tables['sin'],
                           jnp.tile(p['att_qn_g'][layer], 2)[None], jnp.tile(p['att_kn_g'][layer], 2)[None],
                           p['att_sink'][layer], need_ctx_out)
    return (m_l, s_l, a_l), (m_c, s_c, a_c)


def kernel(x, c, ctx, c_ctx, w_mod, b_mod, norm1_g, w_in, ml_b_i, ml_b_f, ml_norm_g, ssd_conv_w, ssd_conv_b,
           ssd_a_log, ssd_dt_bias, ssd_d, ssd_norm_g, att_qn_g, att_kn_g, att_sink, w_out, norm2_g, w_up, w_down):
    p = dict(ml_b_i=ml_b_i, ml_b_f=ml_b_f, ml_norm_g=ml_norm_g, ssd_conv_w=ssd_conv_w, ssd_conv_b=ssd_conv_b,
             ssd_a_log=ssd_a_log, ssd_dt_bias=ssd_dt_bias, ssd_d=ssd_d, ssd_norm_g=ssd_norm_g,
             att_qn_g=att_qn_g, att_kn_g=att_kn_g, att_sink=att_sink)
    bsz, n_lat, d = x.shape
    n_ctx = ctx.shape[1]
    depth = w_mod.shape[0]
    mod_rows = 16
    cc = jnp.concatenate([c, c_ctx[None], jnp.zeros((mod_rows - bsz - 1, d), F32)], axis=0)
    mods = _modulation(cc, w_mod, b_mod).reshape(depth, mod_rows, 6, d)
    cos_t, sin_t = _rope_tables(n_lat)
    tables = dict(cos=cos_t, sin=sin_t)
    lat_row = lambda b: b
    ctx_row = lambda b: bsz
    h_ctx = ctx
    w = _permute_weights(w_in)
    wo, wup, wdn = w_out.astype(BF16), w_up.astype(BF16), w_down.astype(BF16)
    g1, g2 = norm1_g[:, None, :], norm2_g[:, None, :]
    for layer in range(depth):
        need_ctx_out = layer < depth - 1
        u_l = _in_proj(x, mods, lat_row, g1, w, layer, tm=512)
        u_c = _in_proj(h_ctx, mods, ctx_row, g1, w, layer, tm=256)
        y_l, y_c = _token_mixers(u_l, u_c, p, layer, tables, need_ctx_out)
        x = _post(x, *y_l, mods, lat_row, g2, wo, wup, wdn, layer, tm=512)
        if need_ctx_out:
            h_ctx = _post(h_ctx, *y_c, mods, ctx_row, g2, wo, wup, wdn, layer, tm=256)
    return x
```

```python
import functools
import math

import numpy as np
import jax
import jax.numpy as jnp
from jax import lax
from jax.experimental import pallas as pl
from jax.experimental.pallas import tpu as pltpu

F32 = jnp.float32
BF16 = jnp.bfloat16

EPS = 1e-6
GRID_W = 64
N_DIR = 2
ML_HEADS = 4
ML_HEAD_DIM = 128
ML_WIDTH = ML_HEADS * ML_HEAD_DIM
SSD_HEADS = 8
SSD_HEAD_DIM = 64
SSD_WIDTH = SSD_HEADS * SSD_HEAD_DIM
SSD_STATE = 64
SSD_GROUPS = 2
SSD_XBC = SSD_WIDTH + 2 * SSD_GROUPS * SSD_STATE
ATT_Q_HEADS = 8
ATT_KV_HEADS = 2
ATT_HEAD_DIM = 64
ATT_WIDTH = ATT_Q_HEADS * ATT_HEAD_DIM
ATT_KV_WIDTH = ATT_KV_HEADS * ATT_HEAD_DIM
WINDOW = 128
ROPE_BASE = 10000.0
MIX_WIDTH = ML_WIDTH + SSD_WIDTH + ATT_WIDTH

LANES = 128
CHUNK = 128
VMEM_LIMIT = 56 * 1024 * 1024
UNROLL = 8
UNROLL_LIGHT = 8

U_ML = 4 * ML_WIDTH
U_SSD = SSD_WIDTH + SSD_XBC
U_ATT = ATT_WIDTH + 2 * ATT_KV_WIDTH
U_GATE = LANES
U_TOTAL = U_ML + U_SSD + U_ATT + U_GATE
DT_LANE0 = N_DIR * ML_HEADS
LI_LANE0 = DT_LANE0 + N_DIR * SSD_HEADS

ML_SCALE = ML_HEAD_DIM ** -0.5
ATT_SCALE = ATT_HEAD_DIM ** -0.5

_NT = (((1,), (1,)), ((), ()))
_TN = (((0,), (0,)), ((), ()))


def _dot(a, b):
    return jnp.dot(a, b, preferred_element_type=F32)


def _dot_nt(a, b):
    return lax.dot_general(a, b, _NT, preferred_element_type=F32)


def _dot_tn(a, b):
    return lax.dot_general(a, b, _TN, preferred_element_type=F32)


def _split2(x):
    hi = x.astype(BF16)
    lo = (x - hi.astype(F32)).astype(BF16)
    return hi, lo


def _split3(x):
    hi = x.astype(BF16)
    r = x - hi.astype(F32)
    mid = r.astype(BF16)
    lo = (r - mid.astype(F32)).astype(BF16)
    return hi, mid, lo


def _tri_cumsums(tris, x):
    parts = _split3(x)
    return [_dot(tri, parts[0]) + _dot(tri, parts[1]) + _dot(tri, parts[2]) for tri in tris]


def _sigmoid(x):
    return 1.0 / (1.0 + jnp.exp2(x * (-1.0 / math.log(2.0))))


def _log_sigmoid(x):
    return jnp.minimum(x, 0.0) - jnp.log1p(jnp.exp(-jnp.abs(x)))


def _softplus(x):
    return jnp.maximum(x, 0.0) + jnp.log1p(jnp.exp(-jnp.abs(x)))


def _iota(shape, dim):
    return lax.broadcasted_iota(jnp.int32, shape, dim)


def _chunk_rows(c):
    return pl.ds(pl.multiple_of(c * CHUNK, CHUNK), CHUNK)


def _mod_kernel(s_ref, w_ref, b_ref, o_ref):
    s = s_ref[...]
    s = s * _sigmoid(s)
    w = w_ref[...]
    s_hi, s_lo = _split2(s)
    w_hi, w_lo = _split2(w)
    o_ref[...] = _dot(s_hi, w_hi) + (_dot(s_hi, w_lo) + _dot(s_lo, w_hi)) + b_ref[...]


def _modulation(cc, w_mod, b_mod):
    depth, d, d6 = w_mod.shape
    r = cc.shape[0]
    bn = 2048
    return pl.pallas_call(
        _mod_kernel,
        grid=(depth, d6 // bn),
        in_specs=[
            pl.BlockSpec((r, d), lambda l, j: (0, 0)),
            pl.BlockSpec((None, d, bn), lambda l, j: (l, 0, j)),
            pl.BlockSpec((None, 1, bn), lambda l, j: (l, 0, j)),
        ],
        out_specs=pl.BlockSpec((None, r, bn), lambda l, j: (l, 0, j)),
        out_shape=jax.ShapeDtypeStruct((depth, r, d6), F32),
        compiler_params=pltpu.CompilerParams(
            dimension_semantics=("arbitrary", "arbitrary"), vmem_limit_bytes=VMEM_LIMIT),
        name="modulation",
    )(cc, w_mod, b_mod.reshape(depth, 1, d6))


def _inproj_kernel(x_ref, mod_ref, g_ref, w_ref, oml_ref, ossd_ref, oatt_ref, og_ref):
    x = x_ref[...]
    ms = jnp.mean(x * x, axis=-1, keepdims=True)
    xn = x * lax.rsqrt(ms + EPS)
    h = xn * (g_ref[...] * (1.0 + mod_ref[1:2, :])) + mod_ref[0:1, :]
    hb = h.astype(BF16)
    step = 512
    off = 0
    for o_ref, width in ((oml_ref, U_ML), (ossd_ref, U_SSD), (oatt_ref, U_ATT), (og_ref, U_GATE)):
        for j in range(0, width, step):
            wj = min(step, width - j)
            o_ref[:, j:j + wj] = _dot(hb, w_ref[:, off + j:off + j + wj]).astype(o_ref.dtype)
        off += width


def _in_proj(x, mods, mod_row, g1, w, layer, tm):
    bsz, t, d = x.shape
    widths = (U_ML, U_SSD, U_ATT, U_GATE)
    dtypes = (BF16, BF16, BF16, F32)
    return pl.pallas_call(
        _inproj_kernel,
        grid=(bsz, t // tm),
        in_specs=[
            pl.BlockSpec((None, tm, d), lambda b, i: (b, i, 0)),
            pl.BlockSpec((None, None, 6, d), lambda b, i: (layer, mod_row(b), 0, 0)),
            pl.BlockSpec((None, 1, d), lambda b, i: (layer, 0, 0)),
            pl.BlockSpec((None, d, U_TOTAL), lambda b, i: (layer, 0, 0), pipeline_mode=pl.Buffered(1)),
        ],
        out_specs=[pl.BlockSpec((None, tm, wd), lambda b, i: (b, i, 0)) for wd in widths],
        out_shape=[jax.ShapeDtypeStruct((bsz, t, wd), dt) for wd, dt in zip(widths, dtypes)],
        compiler_params=pltpu.CompilerParams(
            dimension_semantics=("arbitrary", "arbitrary"), vmem_limit_bytes=VMEM_LIMIT),
        name="in_proj",
    )(x, mods, g1, w)


def _post_kernel(x_ref, yml_ref, yssd_ref, yatt_ref, mod_ref, g_ref, wo_ref, wup_ref, wdn_ref, o_ref):
    proj = (_dot(yml_ref[...], wo_ref[0:ML_WIDTH, :])
            + _dot(yssd_ref[...], wo_ref[ML_WIDTH:ML_WIDTH + SSD_WIDTH, :])
            + _dot(yatt_ref[...], wo_ref[ML_WIDTH + SSD_WIDTH:MIX_WIDTH, :]))
    x1 = x_ref[...] + mod_ref[2:3, :] * proj
    ms = jnp.mean(x1 * x1, axis=-1, keepdims=True)
    xn = x1 * lax.rsqrt(ms + EPS)
    hb = (xn * (g_ref[...] * (1.0 + mod_ref[4:5, :])) + mod_ref[3:4, :]).astype(BF16)
    d_ff = wup_ref.shape[1]
    step = 1024
    acc = None
    for j in range(0, d_ff, step):
        up = jnp.maximum(_dot(hb, wup_ref[:, j:j + step]), 0.0)
        part = _dot((up * up).astype(BF16), wdn_ref[j:j + step, :])
        acc = part if acc is None else acc + part
    o_ref[...] = x1 + mod_ref[5:6, :] * acc


def _post(x, yml, yssd, yatt, mods, mod_row, g2, wo, wup, wdn, layer, tm):
    bsz, t, d = x.shape
    d_ff = wup.shape[2]
    const = dict(pipeline_mode=pl.Buffered(1))
    return pl.pallas_call(
        _post_kernel,
        grid=(bsz, t // tm),
        in_specs=[
            pl.BlockSpec((None, tm, d), lambda b, i: (b, i, 0)),
            pl.BlockSpec((None, tm, ML_WIDTH), lambda b, i: (b, i, 0)),
            pl.BlockSpec((None, tm, SSD_WIDTH), lambda b, i: (b, i, 0)),
            pl.BlockSpec((None, tm, ATT_WIDTH), lambda b, i: (b, i, 0)),
            pl.BlockSpec((None, None, 6, d), lambda b, i: (layer, mod_row(b), 0, 0)),
            pl.BlockSpec((None, 1, d), lambda b, i: (layer, 0, 0)),
            pl.BlockSpec((None, MIX_WIDTH, d), lambda b, i: (layer, 0, 0), **const),
            pl.BlockSpec((None, d, d_ff), lambda b, i: (layer, 0, 0), **const),
            pl.BlockSpec((None, d_ff, d), lambda b, i: (layer, 0, 0), **const),
        ],
        out_specs=pl.BlockSpec((None, tm, d), lambda b, i: (b, i, 0)),
        out_shape=jax.ShapeDtypeStruct((bsz, t, d), F32),
        compiler_params=pltpu.CompilerParams(
            dimension_semantics=("arbitrary", "arbitrary"), vmem_limit_bytes=VMEM_LIMIT),
        name="post",
    )(x, yml, yssd, yatt, mods, g2, wo, wup, wdn)


ML_GATES = N_DIR * ML_HEADS


def _mlstm_kernel(ul_ref, uc_ref, gl_ref, gc_ref, pv_ref, ng_ref, sh_ref, ex_ref, *rest,
                  need_ctx_out, n_lat, n_ctx):
    if need_ctx_out:
        yl_ref, yc_ref, arow, packed, ends, mprev, cbk, crun = rest
    else:
        yl_ref, arow, packed, ends, mprev, cbk, crun = rest
        yc_ref = None
    ncl, ncc = n_lat // CHUNK, n_ctx // CHUNK
    nct = ncl + ncc
    nh = ML_HEADS
    ti = _iota((CHUNK, CHUNK), 0)
    si = _iota((CHUNK, CHUNK), 1)
    low = si <= ti
    upp = si >= ti
    tri_low = jnp.where(low, 1.0, 0.0).astype(BF16)
    tri_upp = jnp.where(upp, 1.0, 0.0).astype(BF16)
    fwd_lane = _iota((CHUNK, LANES), 1) < nh
    row = _iota((CHUNK, LANES), 0)
    bias = pv_ref[0:1, :]
    neg_inf = -jnp.inf
    ones_blk = jnp.ones((CHUNK, LANES), BF16)

    def prep(g_ref, nch, goff):
        def body(c, carry):
            g = g_ref[_chunk_rows(c), :] + bias
            lf = _log_sigmoid(g)
            li = pltpu.roll(g, LANES - LI_LANE0, 1)
            b = jnp.where(fwd_lane, *_tri_cumsums((tri_low, tri_upp), lf))
            a = li - b
            cm = a
            k = 1
            while k < CHUNK:
                before = jnp.where(row >= k, pltpu.roll(cm, k, 0), neg_inf)
                after = jnp.where(row < CHUNK - k, pltpu.roll(cm, CHUNK - k, 0), neg_inf)
                cm = jnp.maximum(cm, jnp.where(fwd_lane, before, after))
                k *= 2
            arow[goff + c] = a.T
            pk = None
            for i, part in enumerate(_split3(cm) + _split3(b)):
                t = _dot(part, sh_ref[:, i * LANES:(i + 1) * LANES])
                pk = t if pk is None else pk + t
            packed[goff + c] = pk.astype(BF16)
            ends[goff + c, 0:1, :] = jnp.where(fwd_lane[0:1], cm[CHUNK - 1:CHUNK], cm[0:1])
            ends[goff + c, 1:2, :] = jnp.where(fwd_lane[0:1], b[CHUNK - 1:CHUNK], b[0:1])
            return carry
        lax.fori_loop(0, nch, body, 0, unroll=UNROLL_LIGHT)

    prep(gc_ref, ncc, 0)
    prep(gl_ref, ncl, ncc)

    def m_scan(lo, hi, reverse, m0):
        def body(i, m):
            cg = hi - 1 - i if reverse else lo + i
            mprev[cg, int(reverse):int(reverse) + 1, :] = m
            return ends[cg, 1:2, :] + jnp.maximum(m, ends[cg, 0:1, :])
        return lax.fori_loop(0, hi - lo, body, m0)

    m_zero = jnp.zeros((1, LANES), F32)
    m_scan(0, nct, False, m_zero)
    m_scan(ncc, nct, True, m_scan(0, ncc, True, m_zero))

    def load_kv(u_ref, rows, h):
        k = u_ref[rows, ML_WIDTH + h * LANES:ML_WIDTH + (h + 1) * LANES]
        v = u_ref[rows, 2 * ML_WIDTH + h * LANES:2 * ML_WIDTH + (h + 1) * LANES]
        return k, jnp.concatenate([v, ones_blk], axis=1)

    def update(d, h, cg, k, v1):
        j = d * nh + h
        m_prev = mprev[cg, d:d + 1, j:j + 1]
        m_end = jnp.maximum(m_prev, ends[cg, 0:1, j:j + 1])
        w_row = jnp.exp(arow[cg, j:j + 1, :] - m_end) * ML_SCALE
        ktw = (k.T.astype(F32) * w_row).astype(BF16)
        crun[d, h] = jnp.exp(m_prev - m_end) * crun[d, h] + _dot(ktw, v1)

    def bwd_pass(u_ref, nch, goff, keep):
        def body(i, carry):
            c = nch - 1 - i
            rows = _chunk_rows(c)
            for h in range(nh):
                if keep:
                    cbk[goff + c, h] = crun[1, h].astype(BF16)
                k, v1 = load_kv(u_ref, rows, h)
                update(1, h, goff + c, k, v1)
            return carry
        lax.fori_loop(0, nch, body, 0, unroll=UNROLL_LIGHT)

    def fwd_pass(u_ref, y_ref, nch, goff):
        def body(c, carry):
            rows = _chunk_rows(c)
            cg = goff + c
            for h in range(nh):
                k, v1 = load_kv(u_ref, rows, h)
                if y_ref is not None:
                    q = u_ref[rows, h * LANES:(h + 1) * LANES]
                    o = u_ref[rows, 3 * ML_WIDTH + h * LANES:3 * ML_WIDTH + (h + 1) * LANES]
                    sqk = _dot_nt(q, k) * ML_SCALE
                    qc = _dot(q, jnp.concatenate([crun[0, h].astype(BF16), cbk[cg, h]], axis=1))
                    s_dirs, stats = [], []
                    for d, mask in ((0, low), (1, upp)):
                        j = d * nh + h
                        rep = _dot(packed[cg], ex_ref[:, 2 * j * LANES:2 * (j + 1) * LANES])
                        m_prev = mprev[cg, d:d + 1, j:j + 1]
                        m_t = jnp.maximum(rep[:, 0:LANES], m_prev)
                        s_dirs.append(sqk * jnp.exp(jnp.where(mask, arow[cg, j:j + 1, :] - m_t, neg_inf)))
                        stats.append((m_prev, m_t, rep[:, LANES:2 * LANES]))
                    sv = _dot(jnp.concatenate(s_dirs, axis=0).astype(BF16), v1)
                    hsum = None
                    for d, (m_prev, m_t, b_t) in enumerate(stats):
                        g = jnp.exp(m_prev - m_t)
                        sv_d = sv[d * CHUNK:(d + 1) * CHUNK, :]
                        qc_d = qc[:, 2 * d * LANES:2 * (d + 1) * LANES]
                        num = sv_d[:, 0:LANES] + g * qc_d[:, 0:LANES]
                        den = sv_d[:, LANES:2 * LANES] + g * qc_d[:, LANES:2 * LANES]
                        hd = num / jnp.maximum(jnp.abs(den), jnp.exp(-(b_t + m_t)))
                        hsum = hd if hsum is None else hsum + hd
                    ms = jnp.mean(hsum * hsum, axis=1, keepdims=True)
                    y = hsum * lax.rsqrt(ms + EPS) * ng_ref[:, h * LANES:(h + 1) * LANES]
                    y = y * _sigmoid(o.astype(F32))
                    y_ref[rows, h * LANES:(h + 1) * LANES] = y.astype(y_ref.dtype)
                update(0, h, cg, k, v1)
            return carry
        lax.fori_loop(0, nch, body, 0, unroll=UNROLL)

    crun[...] = jnp.zeros(crun.shape, F32)
    bwd_pass(uc_ref, ncc, 0, need_ctx_out)
    bwd_pass(ul_ref, ncl, ncc, True)
    fwd_pass(uc_ref, yc_ref, ncc, 0)
    fwd_pass(ul_ref, yl_ref, ncl, ncc)


def _ml_lane_tables():
    shifts = np.zeros((LANES, 6 * LANES), np.float32)
    expand = np.zeros((LANES, ML_GATES * 2 * LANES), np.float32)
    for j in range(ML_GATES):
        for i in range(6):
            shifts[j, i * LANES + ML_GATES * i + j] = 1.0
            half = i // 3
            expand[ML_GATES * i + j, (2 * j + half) * LANES:(2 * j + half + 1) * LANES] = 1.0
    return jnp.asarray(shifts, BF16), jnp.asarray(expand, BF16)


def _mlstm_mixer(uml_l, uml_c, g_l, g_c, pvec, norm_g, need_ctx_out):
    shifts, expand = _ml_lane_tables()
    bsz, n_lat, _ = uml_l.shape
    n_ctx = uml_c.shape[1]
    nct = (n_lat + n_ctx) // CHUNK
    kern = functools.partial(_mlstm_kernel, need_ctx_out=need_ctx_out, n_lat=n_lat, n_ctx=n_ctx)
    out_specs = [pl.BlockSpec((None, n_lat, ML_WIDTH), lambda b: (b, 0, 0))]
    out_shape = [jax.ShapeDtypeStruct((bsz, n_lat, ML_WIDTH), BF16)]
    if need_ctx_out:
        out_specs.append(pl.BlockSpec((None, n_ctx, ML_WIDTH), lambda b: (b, 0, 0)))
        out_shape.append(jax.ShapeDtypeStruct((bsz, n_ctx, ML_WIDTH), BF16))
    outs = pl.pallas_call(
        kern,
        grid=(bsz,),
        in_specs=[
            pl.BlockSpec((None, n_lat, U_ML), lambda b: (b, 0, 0)),
            pl.BlockSpec((None, n_ctx, U_ML), lambda b: (b, 0, 0)),
            pl.BlockSpec((None, n_lat, U_GATE), lambda b: (b, 0, 0)),
            pl.BlockSpec((None, n_ctx, U_GATE), lambda b: (b, 0, 0)),
            pl.BlockSpec((8, LANES), lambda b: (0, 0)),
            pl.BlockSpec((1, ML_WIDTH), lambda b: (0, 0)),
            pl.BlockSpec(shifts.shape, lambda b: (0, 0)),
            pl.BlockSpec(expand.shape, lambda b: (0, 0)),
        ],
        out_specs=out_specs,
        out_shape=out_shape,
        scratch_shapes=[
            pltpu.VMEM((nct, LANES, CHUNK), F32),
            pltpu.VMEM((nct, CHUNK, LANES), BF16),
            pltpu.VMEM((nct, 8, LANES), F32),
            pltpu.VMEM((nct, 8, LANES), F32),
            pltpu.VMEM((nct, ML_HEADS, ML_HEAD_DIM, 2 * LANES), BF16),
            pltpu.VMEM((N_DIR, ML_HEADS, ML_HEAD_DIM, 2 * LANES), F32),
        ],
        compiler_params=pltpu.CompilerParams(
            dimension_semantics=("arbitrary",), vmem_limit_bytes=VMEM_LIMIT),
        name="mlstm",
    )(uml_l, uml_c, g_l, g_c, pvec, norm_g, shifts, expand)
    return (outs[0], outs[1]) if need_ctx_out else (outs[0], None)


def _ssd_kernel(ul_ref, uc_ref, gl_ref, gc_ref, pv_ref, cw_ref, cb_ref, dsk_ref, ng_ref, sh_ref, ex_ref, *rest,
                need_ctx_out, n_lat, n_ctx):
    if need_ctx_out:
        yl_ref, yc_ref, xact, acum, packed, act, dtt, hbk, hrun = rest
    else:
        yl_ref, xact, acum, packed, act, dtt, hbk, hrun = rest
        yc_ref = None
    ncl, ncc = n_lat // CHUNK, n_ctx // CHUNK
    nh, hg = SSD_HEADS, SSD_HEADS // SSD_GROUPS
    gw = hg * SSD_HEAD_DIM
    gn = SSD_GROUPS * SSD_STATE
    ti = _iota((CHUNK, CHUNK), 0)
    si = _iota((CHUNK, CHUNK), 1)
    low_strict = si < ti
    upp_strict = si > ti
    tri_low = jnp.where(si <= ti, 1.0, 0.0).astype(BF16)
    tri_upp = jnp.where(si >= ti, 1.0, 0.0).astype(BF16)
    lane = _iota((CHUNK, LANES), 1)
    fwd_lane = (lane >= DT_LANE0) & (lane < DT_LANE0 + nh)
    bias0 = pv_ref[0:1, :]
    a_row = -jnp.exp(pv_ref[2:3, :])
    shift_down = jnp.where(si == ti - 1, 1.0, 0.0).astype(BF16)
    shift_up = jnp.where(si == ti + 1, 1.0, 0.0).astype(BF16)
    lane_group0 = lane < SSD_STATE

    def prep(u_ref, g_ref, nch, goff, nrows):
        def body(c, carry):
            r = pl.multiple_of(c * CHUNK, CHUNK)
            cur = u_ref[pl.ds(r, CHUNK), SSD_WIDTH:U_SSD]
            rp = pl.multiple_of(jnp.maximum(r - CHUNK, 0), CHUNK)
            rn = pl.multiple_of(jnp.minimum(r + CHUNK, nrows - CHUNK), CHUNK)
            prev = u_ref[pl.ds(rp, CHUNK), SSD_WIDTH:U_SSD]
            nxt = u_ref[pl.ds(rn, CHUNK), SSD_WIDTH:U_SSD]
            from_prev = jnp.where((ti == 0) & (si == CHUNK - 1) & (c > 0), 1.0, 0.0).astype(BF16)
            from_next = jnp.where((ti == CHUNK - 1) & (si == 0) & (c < nch - 1), 1.0, 0.0).astype(BF16)
            xm = _dot(jnp.concatenate([shift_down, from_prev], axis=1), jnp.concatenate([cur, prev], axis=0))
            xp = _dot(jnp.concatenate([shift_up, from_next], axis=1), jnp.concatenate([cur, nxt], axis=0))
            conv = (cw_ref[0:1, :] * xm + cw_ref[1:2, :] * cur.astype(F32) + cw_ref[2:3, :] * xp
                    + cb_ref[...])
            xact[goff + c] = (conv * _sigmoid(conv)).astype(BF16)

            dt = _softplus(g_ref[pl.ds(r, CHUNK), :] + bias0)
            la = dt * a_row
            ac = jnp.where(fwd_lane, *_tri_cumsums((tri_low, tri_upp), la))
            a_end = jnp.where(fwd_lane[0:1, :], ac[CHUNK - 1:CHUNK, :], ac[0:1, :])
            acum[goff + c] = ac
            act[goff + c] = (ac - jnp.log(dt)).T
            dtt[goff + c] = dt.T
            pk = None
            for i, part in enumerate(_split3(jnp.exp(ac)) + _split3(jnp.exp(a_end - ac) * dt)):
                t = _dot(part, sh_ref[:, i * LANES:(i + 1) * LANES])
                pk = t if pk is None else pk + t
            packed[goff + c] = pk.astype(BF16)
            return carry
        lax.fori_loop(0, nch, body, 0, unroll=UNROLL_LIGHT)

    prep(uc_ref, gc_ref, ncc, 0, n_ctx)
    prep(ul_ref, gl_ref, ncl, ncc, n_lat)

    def e_cols(d, g):
        return slice(d * SSD_WIDTH + g * gw, d * SSD_WIDTH + (g + 1) * gw)

    def w_cols(d, g):
        return slice((N_DIR + d) * SSD_WIDTH + g * gw, (N_DIR + d) * SSD_WIDTH + (g + 1) * gw)

    def update_state(d, cg):
        pk = packed[cg]
        xa = xact[cg]
        b_all = xa[:, SSD_WIDTH:SSD_WIDTH + gn]
        zero = jnp.zeros_like(b_all)
        upd = []
        for g in range(SSD_GROUPS):
            wx = (_dot(pk, ex_ref[:, w_cols(d, g)]) * xa[:, g * gw:(g + 1) * gw].astype(F32)).astype(BF16)
            upd.append(_dot_tn(jnp.where(lane_group0 == (g == 0), b_all, zero), wx))
        tail, last = (slice(CHUNK - 16, CHUNK), 15) if d == 0 else (slice(0, 16), 0)
        decay = _dot(pk[tail], ex_ref[:, d * SSD_WIDTH:(d + 1) * SSD_WIDTH])[last:last + 1, :]
        hrun[d] = decay * hrun[d] + jnp.concatenate(upd, axis=1)

    def bwd_pass(nch, goff, keep):
        def body(i, carry):
            cg = goff + nch - 1 - i
            if keep:
                hbk[cg] = hrun[1].astype(BF16)
            update_state(1, cg)
            return carry
        lax.fori_loop(0, nch, body, 0, unroll=UNROLL_LIGHT)

    def fwd_pass(u_ref, y_ref, nch, goff):
        def body(c, carry):
            cg = goff + c
            if y_ref is not None:
                rows = _chunk_rows(c)
                xa = xact[cg]
                x = xa[:, 0:SSD_WIDTH]
                b_all = xa[:, SSD_WIDTH:SSD_WIDTH + gn]
                c_all = xa[:, SSD_WIDTH + gn:SSD_XBC]
                ac = acum[cg]
                ac_t = act[cg]
                dt_t = dtt[cg]
                lane_w = _iota((CHUNK, gw), 1)
                pk = packed[cg]
                yz_groups, ssq = [], None
                for g in range(SSD_GROUPS):
                    cols = slice(g * gw, (g + 1) * gw)
                    c_g = jnp.where(lane_group0 == (g == 0), c_all, jnp.zeros_like(c_all))
                    cb = _dot_nt(c_g, b_all)
                    s_heads = []
                    for hh in range(hg):
                        jf = DT_LANE0 + g * hg + hh
                        jb = jf + nh
                        on_diag = jnp.log(dt_t[jf:jf + 1, :] + dt_t[jb:jb + 1, :])
                        expo = jnp.where(low_strict, ac[:, jf:jf + 1] - ac_t[jf:jf + 1, :],
                                         jnp.where(upp_strict, ac[:, jb:jb + 1] - ac_t[jb:jb + 1, :], on_diag))
                        s_heads.append((cb * jnp.exp(expo)).astype(BF16))
                    x_g = x[:, cols]
                    bd = jnp.concatenate(
                        [jnp.where((lane_w >= hh * SSD_HEAD_DIM) & (lane_w < (hh + 1) * SSD_HEAD_DIM),
                                   x_g, jnp.zeros_like(x_g)) for hh in range(hg)], axis=0)
                    y = _dot(jnp.concatenate(s_heads, axis=1), bd)
                    y = y + _dot(c_all, hrun[0, :, cols].astype(BF16)) * _dot(pk, ex_ref[:, e_cols(0, g)])
                    y = y + _dot(c_all, hbk[cg, :, cols]) * _dot(pk, ex_ref[:, e_cols(1, g)])
                    y = y + dsk_ref[:, cols] * x_g.astype(F32)
                    z = u_ref[rows, cols].astype(F32)
                    yz = y * (z * _sigmoid(z))
                    part = jnp.sum(yz * yz, axis=1, keepdims=True)
                    ssq = part if ssq is None else ssq + part
                    yz_groups.append(yz)
                scale = lax.rsqrt(ssq * (1.0 / SSD_WIDTH) + EPS)
                for g, yz in enumerate(yz_groups):
                    cols = slice(g * gw, (g + 1) * gw)
                    y_ref[rows, cols] = (yz * scale * ng_ref[:, cols]).astype(y_ref.dtype)
            update_state(0, cg)
            return carry
        lax.fori_loop(0, nch, body, 0, unroll=UNROLL)

    hrun[...] = jnp.zeros(hrun.shape, F32)
    bwd_pass(ncc, 0, need_ctx_out)
    bwd_pass(ncl, ncc, True)
    fwd_pass(uc_ref, yc_ref, ncc, 0)
    fwd_pass(ul_ref, yl_ref, ncl, ncc)


def _ssd_lane_tables():
    nj = N_DIR * SSD_HEADS
    shifts = np.zeros((LANES, 6 * LANES), np.float32)
    expand = np.zeros((LANES, 2 * N_DIR * SSD_WIDTH), np.float32)
    for i in range(6):
        for d in range(N_DIR):
            for h in range(SSD_HEADS):
                j = d * SSD_HEADS + h
                shifts[DT_LANE0 + j, i * LANES + nj * i + j] = 1.0
                c0 = ((i // 3) * N_DIR + d) * SSD_WIDTH + h * SSD_HEAD_DIM
                expand[nj * i + j, c0:c0 + SSD_HEAD_DIM] = 1.0
    return jnp.asarray(shifts, BF16), jnp.asarray(expand, BF16)


def _ssd_mixer(ussd_l, ussd_c, g_l, g_c, pvec, conv_w, conv_b, d_skip, norm_g, need_ctx_out):
    shifts, expand = _ssd_lane_tables()
    bsz, n_lat, _ = ussd_l.shape
    n_ctx = ussd_c.shape[1]
    nct = (n_lat + n_ctx) // CHUNK
    gn = SSD_GROUPS * SSD_STATE
    kern = functools.partial(_ssd_kernel, need_ctx_out=need_ctx_out, n_lat=n_lat, n_ctx=n_ctx)
    out_specs = [pl.BlockSpec((None, n_lat, SSD_WIDTH), lambda b: (b, 0, 0))]
    out_shape = [jax.ShapeDtypeStruct((bsz, n_lat, SSD_WIDTH), BF16)]
    if need_ctx_out:
        out_specs.append(pl.BlockSpec((None, n_ctx, SSD_WIDTH), lambda b: (b, 0, 0)))
        out_shape.append(jax.ShapeDtypeStruct((bsz, n_ctx, SSD_WIDTH), BF16))
    outs = pl.pallas_call(
        kern,
        grid=(bsz,),
        in_specs=[
            pl.BlockSpec((None, n_lat, U_SSD), lambda b: (b, 0, 0)),
            pl.BlockSpec((None, n_ctx, U_SSD), lambda b: (b, 0, 0)),
            pl.BlockSpec((None, n_lat, LANES), lambda b: (b, 0, 0)),
            pl.BlockSpec((None, n_ctx, LANES), lambda b: (b, 0, 0)),
            pl.BlockSpec((8, LANES), lambda b: (0, 0)),
            pl.BlockSpec((3, SSD_XBC), lambda b: (0, 0)),
            pl.BlockSpec((1, SSD_XBC), lambda b: (0, 0)),
            pl.BlockSpec((1, SSD_WIDTH), lambda b: (0, 0)),
            pl.BlockSpec((1, SSD_WIDTH), lambda b: (0, 0)),
            pl.BlockSpec(shifts.shape, lambda b: (0, 0)),
            pl.BlockSpec(expand.shape, lambda b: (0, 0)),
        ],
        out_specs=out_specs,
        out_shape=out_shape,
        scratch_shapes=[
            pltpu.VMEM((nct, CHUNK, SSD_XBC), BF16),
            pltpu.VMEM((nct, CHUNK, LANES), F32),
            pltpu.VMEM((nct, CHUNK, LANES), BF16),
            pltpu.VMEM((nct, LANES, CHUNK), F32),
            pltpu.VMEM((nct, LANES, CHUNK), F32),
            pltpu.VMEM((nct, gn, SSD_WIDTH), BF16),
            pltpu.VMEM((N_DIR, gn, SSD_WIDTH), F32),
        ],
        compiler_params=pltpu.CompilerParams(
            dimension_semantics=("arbitrary",), vmem_limit_bytes=VMEM_LIMIT),
        name="ssd",
    )(ussd_l, ussd_c, g_l, g_c, pvec, conv_w, conv_b, d_skip, norm_g, shifts, expand)
    return (outs[0], outs[1]) if need_ctx_out else (outs[0], None)


def _head_norm(x, gain):
    rows = x.shape[0]
    blks = [x[:, j * LANES:(j + 1) * LANES] for j in range(x.shape[1] // LANES)]
    same_head = (_iota((LANES, LANES), 0) < ATT_HEAD_DIM) == (_iota((LANES, LANES), 1) < ATT_HEAD_DIM)
    sq = jnp.concatenate([b * b for b in blks], axis=0).astype(BF16)
    ms = _dot(sq, jnp.where(same_head, 1.0, 0.0).astype(BF16)) * (1.0 / ATT_HEAD_DIM)
    outs = [b * lax.rsqrt(ms[j * rows:(j + 1) * rows] + EPS) * gain for j, b in enumerate(blks)]
    return outs[0] if len(outs) == 1 else jnp.concatenate(outs, axis=1)


def _rope(x, cos_t, sin_t):
    w = x.shape[1]
    reps = w // LANES
    nfreq = ATT_HEAD_DIM // 4
    partner = jnp.where((_iota(x.shape, 1) & (2 * nfreq - 1)) < nfreq,
                        pltpu.roll(x, w - nfreq, 1), pltpu.roll(x, nfreq, 1))
    if reps > 1:
        cos_t = jnp.concatenate([cos_t] * reps, axis=1)
        sin_t = jnp.concatenate([sin_t] * reps, axis=1)
    return x * cos_t + partner * sin_t


def _dup_heads(x):
    lane_lo = _iota(x.shape, 1) < ATT_HEAD_DIM
    sw = pltpu.roll(x, ATT_HEAD_DIM, 1)
    return jnp.concatenate([jnp.where(lane_lo, x, sw), jnp.where(lane_lo, sw, x)], axis=1)


def _attn_kernel(ul_ref, uc_ref, cos_ref, sin_ref, qg_ref, kg_ref, sink_ref, *rest,
                 need_ctx_out, n_lat, n_ctx):
    if need_ctx_out:
        yl_ref, yc_ref, qp, kb, vb, kc, vc, qcp = rest
    else:
        yl_ref, qp, kb, vb, kc, vc = rest
        yc_ref = qcp = None
    ncl, ncc = n_lat // CHUNK, n_ctx // CHUNK
    rep = ATT_Q_HEADS // ATT_KV_HEADS
    kv0 = ATT_WIDTH
    lane_lo = _iota((CHUNK, LANES), 1) < ATT_HEAD_DIM
    neg_inf = -jnp.inf
    qg = qg_ref[...]
    kg = kg_ref[...]

    for ref in (kb, vb):
        zeros_blk = jnp.zeros((CHUNK, ref.shape[1]), BF16)
        ref[0:CHUNK, :] = zeros_blk
        ref[CHUNK + n_lat:2 * CHUNK + n_lat, :] = zeros_blk

    def values_with_ones(v):
        d = _dup_heads(v).astype(BF16)
        ones = jnp.ones((v.shape[0], LANES), BF16)
        return jnp.concatenate([d[:, 0:LANES], ones, d[:, LANES:2 * LANES], ones], axis=1)

    def prep_lat(c, carry):
        rows = _chunk_rows(c)
        cos_t = cos_ref[rows, :]
        sin_t = sin_ref[rows, :]
        q = _rope(_head_norm(ul_ref[rows, 0:ATT_WIDTH].astype(F32), qg), cos_t, sin_t)
        qp[rows, :] = (q * ATT_SCALE).astype(BF16)
        k = _rope(_head_norm(ul_ref[rows, kv0:kv0 + LANES].astype(F32), kg), cos_t, sin_t)
        v = ul_ref[rows, kv0 + LANES:kv0 + 2 * LANES].astype(F32)
        dst = pl.ds(pl.multiple_of((c + 1) * CHUNK, CHUNK), CHUNK)
        kb[dst, :] = _dup_heads(k).astype(BF16)
        vb[dst, :] = values_with_ones(v)
        return carry
    lax.fori_loop(0, ncl, prep_lat, 0, unroll=UNROLL_LIGHT)

    def prep_ctx(c, carry):
        rows = _chunk_rows(c)
        k = _head_norm(uc_ref[rows, kv0:kv0 + LANES].astype(F32), kg)
        v = uc_ref[rows, kv0 + LANES:kv0 + 2 * LANES].astype(F32)
        kc[rows, :] = _dup_heads(k).astype(BF16)
        vc[rows, :] = values_with_ones(v)
        if need_ctx_out:
            q = _head_norm(uc_ref[rows, 0:ATT_WIDTH].astype(F32), qg)
            qcp[rows, :] = (q * ATT_SCALE).astype(BF16)
        return carry
    lax.fori_loop(0, ncc, prep_ctx, 0, unroll=UNROLL)

    def attend(q_ref, y_ref, rows, band):
        for g in range(ATT_KV_HEADS):
            kl = slice(g * LANES, (g + 1) * LANES)
            vl = slice(2 * g * LANES, 2 * (g + 1) * LANES)
            qms = []
            for r in range(rep):
                hq = g * rep + r
                qb = q_ref[rows, (hq // 2) * LANES:(hq // 2 + 1) * LANES]
                qms.append(jnp.where(lane_lo != (hq % 2 == 1), qb, jnp.zeros_like(qb)))
            qs = jnp.concatenate(qms, axis=0)
            s_c = _dot_nt(qs, kc[:, kl])
            if band is not None:
                band_rows, valid = band
                s_b = _dot_nt(qs, kb[band_rows, kl])
            ps, sink_terms = [], []
            for r in range(rep):
                sl = slice(r * CHUNK, (r + 1) * CHUNK)
                sk = sink_ref[g * rep + r]
                s = s_c[sl]
                if band is not None:
                    s = jnp.concatenate([s, jnp.where(valid, s_b[sl], neg_inf)], axis=1)
                m = jnp.maximum(jnp.max(s, axis=1, keepdims=True), sk)
                ps.append(jnp.exp(s - m).astype(BF16))
                sink_terms.append(jnp.exp(sk - m))
            p = jnp.concatenate(ps, axis=0)
            acc = _dot(p[:, 0:n_ctx], vc[:, vl])
            if band is not None:
                acc = acc + _dot(p[:, n_ctx:], vb[band_rows, vl])
            for r in range(rep):
                sl = slice(r * CHUNK, (r + 1) * CHUNK)
                o = acc[sl, 0:LANES] / (acc[sl, LANES:2 * LANES] + sink_terms[r])
                if r % 2 == 1:
                    jb = (g * rep + r) // 2
                    y_ref[rows, jb * LANES:(jb + 1) * LANES] = jnp.where(lane_lo, o_even, o).astype(y_ref.dtype)
                o_even = o

    qq = _iota((CHUNK, 3 * CHUNK), 0)
    kk = _iota((CHUNK, 3 * CHUNK), 1)
    in_window = (kk >= qq) & (kk <= qq + 2 * WINDOW)

    def lat_block(i, carry):
        kpos = kk + (i - 1) * CHUNK
        valid = in_window & (kpos >= 0) & (kpos < n_lat)
        band_rows = pl.ds(pl.multiple_of(i * CHUNK, CHUNK), 3 * CHUNK)
        attend(qp, yl_ref, _chunk_rows(i), (band_rows, valid))
        return carry
    lax.fori_loop(0, ncl, lat_block, 0, unroll=UNROLL)

    if need_ctx_out:
        def ctx_block(i, carry):
            attend(qcp, yc_ref, _chunk_rows(i), None)
            return carry
        lax.fori_loop(0, ncc, ctx_block, 0, unroll=UNROLL)


def _attn_mixer(uatt_l, uatt_c, cos_t, sin_t, qgain, kgain, sink, need_ctx_out):
    bsz, n_lat, _ = uatt_l.shape
    n_ctx = uatt_c.shape[1]
    kern = functools.partial(_attn_kernel, need_ctx_out=need_ctx_out, n_lat=n_lat, n_ctx=n_ctx)
    out_specs = [pl.BlockSpec((None, n_lat, ATT_WIDTH), lambda b: (b, 0, 0))]
    out_shape = [jax.ShapeDtypeStruct((bsz, n_lat, ATT_WIDTH), BF16)]
    scratch = [
        pltpu.VMEM((n_lat, ATT_WIDTH), BF16),
        pltpu.VMEM((n_lat + 2 * CHUNK, 2 * LANES), BF16),
        pltpu.VMEM((n_lat + 2 * CHUNK, 4 * LANES), BF16),
        pltpu.VMEM((n_ctx, 2 * LANES), BF16),
        pltpu.VMEM((n_ctx, 4 * LANES), BF16),
    ]
    if need_ctx_out:
        out_specs.append(pl.BlockSpec((None, n_ctx, ATT_WIDTH), lambda b: (b, 0, 0)))
        out_shape.append(jax.ShapeDtypeStruct((bsz, n_ctx, ATT_WIDTH), BF16))
        scratch.append(pltpu.VMEM((n_ctx, ATT_WIDTH), BF16))
    outs = pl.pallas_call(
        kern,
        grid=(bsz,),
        in_specs=[
            pl.BlockSpec((None, n_lat, U_ATT), lambda b: (b, 0, 0)),
            pl.BlockSpec((None, n_ctx, U_ATT), lambda b: (b, 0, 0)),
            pl.BlockSpec((n_lat, LANES), lambda b: (0, 0)),
            pl.BlockSpec((n_lat, LANES), lambda b: (0, 0)),
            pl.BlockSpec((1, LANES), lambda b: (0, 0)),
            pl.BlockSpec((1, LANES), lambda b: (0, 0)),
            pl.BlockSpec(memory_space=pltpu.SMEM),
        ],
        out_specs=out_specs,
        out_shape=out_shape,
        scratch_shapes=scratch,
        compiler_params=pltpu.CompilerParams(
            dimension_semantics=("arbitrary",), vmem_limit_bytes=VMEM_LIMIT),
        name="attention",
    )(uatt_l, uatt_c, cos_t, sin_t, qgain, kgain, sink)
    return (outs[0], outs[1]) if need_ctx_out else (outs[0], None)


def _in_layout():
    names = (('ml_q', ML_WIDTH), ('ml_k', ML_WIDTH), ('ml_v', ML_WIDTH), ('ml_o', ML_WIDTH),
             ('ml_i', N_DIR * ML_HEADS), ('ml_f', N_DIR * ML_HEADS),
             ('ssd_z', SSD_WIDTH), ('ssd_xbc', SSD_XBC), ('ssd_dt', N_DIR * SSD_HEADS),
             ('att_q', ATT_WIDTH), ('att_k', ATT_KV_WIDTH), ('att_v', ATT_KV_WIDTH))
    out, off = {}, 0
    for name, width in names:
        out[name] = (off, off + width)
        off += width
    return out


def _col_map():
    lay = _in_layout()
    take = lambda name: np.arange(lay[name][0], lay[name][1])
    zeros = lambda n: np.full((n,), -1)
    n_f, n_dt, n_i = N_DIR * ML_HEADS, N_DIR * SSD_HEADS, N_DIR * ML_HEADS
    return np.concatenate(
        [take('ml_q'), take('ml_k'), take('ml_v'), take('ml_o'),
         take('ssd_z'), take('ssd_xbc'),
         take('att_q'), take('att_k'), take('att_v'),
         take('ml_f'), take('ssd_dt'), take('ml_i'), zeros(LANES - n_f - n_dt - n_i)])


def _permute_plan():
    src = _col_map()
    plan = []
    for t in range(U_TOTAL // LANES):
        cols = src[t * LANES:(t + 1) * LANES]
        runs, i = [], 0
        while i < LANES:
            j = i
            while j + 1 < LANES and (cols[j + 1] == cols[j] + 1 if cols[j] >= 0 else cols[j + 1] < 0):
                j += 1
            runs.append((int(cols[i]), j - i + 1))
            i = j + 1
        plan.append(tuple(runs))
    return tuple(plan)


def _wprep_kernel(wt_ref, o_ref, *, plan):
    kb = wt_ref.shape[1]
    for t, runs in enumerate(plan):
        pieces = [wt_ref[s:s + n, :] if s >= 0 else jnp.zeros((n, kb), F32) for s, n in runs]
        blk = pieces[0] if len(pieces) == 1 else jnp.concatenate(pieces, axis=0)
        o_ref[:, t * LANES:(t + 1) * LANES] = blk.T.astype(BF16)


def _permute_weights(w_in):
    depth, d, in_width = w_in.shape
    plan = _permute_plan()
    kb = 256
    return pl.pallas_call(
        functools.partial(_wprep_kernel, plan=plan),
        grid=(depth, d // kb),
        in_specs=[pl.BlockSpec((None, in_width, kb), lambda l, i: (l, 0, i))],
        out_specs=pl.BlockSpec((None, kb, U_TOTAL), lambda l, i: (l, i, 0)),
        out_shape=jax.ShapeDtypeStruct((depth, d, U_TOTAL), BF16),
        compiler_params=pltpu.CompilerParams(
            dimension_semantics=("arbitrary", "arbitrary"), vmem_limit_bytes=VMEM_LIMIT),
        name="permute_weights",
    )(jnp.swapaxes(w_in, 1, 2))


def _gate_rows(ml_b_i, ml_b_f, ssd_dt_bias, ssd_a_log):
    n_f, n_dt = N_DIR * ML_HEADS, N_DIR * SSD_HEADS
    row0 = jnp.concatenate([ml_b_f.reshape(-1), ssd_dt_bias.reshape(-1), ml_b_i.reshape(-1),
                            jnp.zeros((LANES - 2 * n_f - n_dt,), F32)])
    row2 = jnp.concatenate([jnp.zeros((n_f,), F32), ssd_a_log.reshape(-1), jnp.zeros((LANES - n_f - n_dt,), F32)])
    zero = jnp.zeros((1, LANES), F32)
    return jnp.concatenate([row0[None], zero, row2[None], jnp.zeros((5, LANES), F32)], axis=0)


def _rope_tables(n_lat):
    nfreq = ATT_HEAD_DIM // 4
    pos = np.arange(n_lat)
    rows, cols = pos // GRID_W, pos % GRID_W
    inv = np.float32(ROPE_BASE) ** (-np.arange(nfreq, dtype=np.float32) / np.float32(nfreq))
    ang_r = rows.astype(np.float32)[:, None] * inv
    ang_c = cols.astype(np.float32)[:, None] * inv
    cos_h = np.concatenate([np.cos(ang_r)] * 2 + [np.cos(ang_c)] * 2, axis=1)
    sin_h = np.concatenate([-np.sin(ang_r), np.sin(ang_r), -np.sin(ang_c), np.sin(ang_c)], axis=1)
    return (jnp.asarray(np.concatenate([cos_h] * 2, axis=1), F32),
            jnp.asarray(np.concatenate([sin_h] * 2, axis=1), F32))


def _token_mixers(u_l, u_c, p, layer, tables, need_ctx_out):
    uml_l, ussd_l, uatt_l, g_l = u_l
    uml_c, ussd_c, uatt_c, g_c = u_c
    pvec = _gate_rows(p['ml_b_i'][layer], p['ml_b_f'][layer], p['ssd_dt_bias'][layer], p['ssd_a_log'][layer])
    m_l, m_c = _mlstm_mixer(uml_l, uml_c, g_l, g_c, pvec, p['ml_norm_g'][layer][None], need_ctx_out)
    s_l, s_c = _ssd_mixer(ussd_l, ussd_c, g_l, g_c, pvec, p['ssd_conv_w'][layer], p['ssd_conv_b'][layer][None],
                          jnp.repeat(p['ssd_d'][layer], SSD_HEAD_DIM)[None], p['ssd_norm_g'][layer][None],
                          need_ctx_out)
    a_l, a_c = _attn_mixer(uatt_l, uatt_c, tables['cos'], tables['sin'],
                           jnp.tile(p['att_qn_g'][layer], 2)[None], jnp.tile(p['att_kn_g'][layer], 2)[None],
                           p['att_sink'][layer], need_ctx_out)
    return (m_l, s_l, a_l), (m_c, s_c, a_c)


def kernel(x, c, ctx, c_ctx, w_mod, b_mod, norm1_g, w_in, ml_b_i, ml_b_f, ml_norm_g, ssd_conv_w, ssd_conv_b,
           ssd_a_log, ssd_dt_bias, ssd_d, ssd_norm_g, att_qn_g, att_kn_g, att_sink, w_out, norm2_g, w_up, w_down):
    p = dict(ml_b_i=ml_b_i, ml_b_f=ml_b_f, ml_norm_g=ml_norm_g, ssd_conv_w=ssd_conv_w, ssd_conv_b=ssd_conv_b,
             ssd_a_log=ssd_a_log, ssd_dt_bias=ssd_dt_bias, ssd_d=ssd_d, ssd_norm_g=ssd_norm_g,
             att_qn_g=att_qn_g, att_kn_g=att_kn_g, att_sink=att_sink)
    bsz, n_lat, d = x.shape
    n_ctx = ctx.shape[1]
    depth = w_mod.shape[0]
    mod_rows = 16
    cc = jnp.concatenate([c, c_ctx[None], jnp.zeros((mod_rows - bsz - 1, d), F32)], axis=0)
    mods = _modulation(cc, w_mod, b_mod).reshape(depth, mod_rows, 6, d)
    cos_t, sin_t = _rope_tables(n_lat)
    tables = dict(cos=cos_t, sin=sin_t)
    lat_row = lambda b: b
    ctx_row = lambda b: bsz
    h_ctx = ctx
    w = _permute_weights(w_in)
    wo, wup, wdn = w_out.astype(BF16), w_up.astype(BF16), w_down.astype(BF16)
    g1, g2 = norm1_g[:, None, :], norm2_g[:, None, :]
    for layer in range(depth):
        need_ctx_out = layer < depth - 1
        u_l = _in_proj(x, mods, lat_row, g1, w, layer, tm=512)
        u_c = _in_proj(h_ctx, mods, ctx_row, g1, w, layer, tm=256)
        y_l, y_c = _token_mixers(u_l, u_c, p, layer, tables, need_ctx_out)
        x = _post(x, *y_l, mods, lat_row, g2, wo, wup, wdn, layer, tm=512)
        if need_ctx_out:
            h_ctx = _post(h_ctx, *y_c, mods, ctx_row, g2, wo, wup, wdn, layer, tm=256)
    return x
```

```python
import functools
import math

import numpy as np
import jax
import jax.numpy as jnp
from jax import lax
from jax.experimental import pallas as pl
from jax.experimental.pallas import tpu as pltpu

F32 = jnp.float32
BF16 = jnp.bfloat16

EPS = 1e-6
GRID_W = 64
N_DIR = 2
ML_HEADS = 4
ML_HEAD_DIM = 128
ML_WIDTH = ML_HEADS * ML_HEAD_DIM
SSD_HEADS = 8
SSD_HEAD_DIM = 64
SSD_WIDTH = SSD_HEADS * SSD_HEAD_DIM
SSD_STATE = 64
SSD_GROUPS = 2
SSD_XBC = SSD_WIDTH + 2 * SSD_GROUPS * SSD_STATE
ATT_Q_HEADS = 8
ATT_KV_HEADS = 2
ATT_HEAD_DIM = 64
ATT_WIDTH = ATT_Q_HEADS * ATT_HEAD_DIM
ATT_KV_WIDTH = ATT_KV_HEADS * ATT_HEAD_DIM
WINDOW = 128
ROPE_BASE = 10000.0
MIX_WIDTH = ML_WIDTH + SSD_WIDTH + ATT_WIDTH

LANES = 128
CHUNK = 128
VMEM_LIMIT = 56 * 1024 * 1024
UNROLL = 8
UNROLL_LIGHT = 8
UNROLL_GATES = 16

U_ML = 4 * ML_WIDTH
U_SSD = SSD_WIDTH + SSD_XBC
U_ATT = ATT_WIDTH + 2 * ATT_KV_WIDTH
U_GATE = LANES
U_TOTAL = U_ML + U_SSD + U_ATT + U_GATE
DT_LANE0 = N_DIR * ML_HEADS
LI_LANE0 = DT_LANE0 + N_DIR * SSD_HEADS

ML_SCALE = ML_HEAD_DIM ** -0.5
ATT_SCALE = ATT_HEAD_DIM ** -0.5

_NT = (((1,), (1,)), ((), ()))
_TN = (((0,), (0,)), ((), ()))


def _dot(a, b):
    return jnp.dot(a, b, preferred_element_type=F32)


def _dot_nt(a, b):
    return lax.dot_general(a, b, _NT, preferred_element_type=F32)


def _dot_tn(a, b):
    return lax.dot_general(a, b, _TN, preferred_element_type=F32)


def _split2(x):
    hi = x.astype(BF16)
    lo = (x - hi.astype(F32)).astype(BF16)
    return hi, lo


def _split3(x):
    hi = x.astype(BF16)
    r = x - hi.astype(F32)
    mid = r.astype(BF16)
    lo = (r - mid.astype(F32)).astype(BF16)
    return hi, mid, lo


def _row_cumsums(tris, x):
    rows = x.shape[0]
    parts = _split3(jnp.concatenate([x, jnp.zeros_like(x)], axis=0))
    return [(_dot(parts[0], tri) + _dot(parts[1], tri) + _dot(parts[2], tri))[0:rows] for tri in tris]


def _sigmoid(x):
    return 1.0 / (1.0 + jnp.exp2(x * (-1.0 / math.log(2.0))))


def _log_sigmoid(x):
    return jnp.minimum(x, 0.0) - jnp.log1p(jnp.exp(-jnp.abs(x)))


def _softplus(x):
    return jnp.maximum(x, 0.0) + jnp.log1p(jnp.exp(-jnp.abs(x)))


def _iota(shape, dim):
    return lax.broadcasted_iota(jnp.int32, shape, dim)


def _chunk_rows(c):
    return pl.ds(pl.multiple_of(c * CHUNK, CHUNK), CHUNK)


def _mod_kernel(s_ref, w_ref, b_ref, o_ref):
    s = s_ref[...]
    s = s * _sigmoid(s)
    w = w_ref[...]
    s_hi, s_lo = _split2(s)
    w_hi, w_lo = _split2(w)
    o_ref[...] = _dot(s_hi, w_hi) + (_dot(s_hi, w_lo) + _dot(s_lo, w_hi)) + b_ref[...]


def _modulation(cc, w_mod, b_mod):
    depth, d, d6 = w_mod.shape
    r = cc.shape[0]
    bn = 2048
    return pl.pallas_call(
        _mod_kernel,
        grid=(depth, d6 // bn),
        in_specs=[
            pl.BlockSpec((r, d), lambda l, j: (0, 0)),
            pl.BlockSpec((None, d, bn), lambda l, j: (l, 0, j)),
            pl.BlockSpec((None, 1, bn), lambda l, j: (l, 0, j)),
        ],
        out_specs=pl.BlockSpec((None, r, bn), lambda l, j: (l, 0, j)),
        out_shape=jax.ShapeDtypeStruct((depth, r, d6), F32),
        compiler_params=pltpu.CompilerParams(
            dimension_semantics=("arbitrary", "arbitrary"), vmem_limit_bytes=VMEM_LIMIT),
        name="modulation",
    )(cc, w_mod, b_mod.reshape(depth, 1, d6))


def _inproj_kernel(x_ref, mod_ref, g_ref, w_ref, oml_ref, ossd_ref, oatt_ref, og_ref):
    x = x_ref[...]
    ms = jnp.mean(x * x, axis=-1, keepdims=True)
    xn = x * lax.rsqrt(ms + EPS)
    h = xn * (g_ref[...] * (1.0 + mod_ref[1:2, :])) + mod_ref[0:1, :]
    hb = h.astype(BF16)
    step = 512
    off = 0
    for o_ref, width in ((oml_ref, U_ML), (ossd_ref, U_SSD), (oatt_ref, U_ATT), (og_ref, U_GATE)):
        for j in range(0, width, step):
            wj = min(step, width - j)
            o_ref[:, j:j + wj] = _dot(hb, w_ref[:, off + j:off + j + wj]).astype(o_ref.dtype)
        off += width


def _in_proj(x, mods, mod_row, g1, w, layer, tm):
    bsz, t, d = x.shape
    widths = (U_ML, U_SSD, U_ATT, U_GATE)
    dtypes = (BF16, BF16, BF16, F32)
    return pl.pallas_call(
        _inproj_kernel,
        grid=(bsz, t // tm),
        in_specs=[
            pl.BlockSpec((None, tm, d), lambda b, i: (b, i, 0)),
            pl.BlockSpec((None, None, 6, d), lambda b, i: (layer, mod_row(b), 0, 0)),
            pl.BlockSpec((None, 1, d), lambda b, i: (layer, 0, 0)),
            pl.BlockSpec((None, d, U_TOTAL), lambda b, i: (layer, 0, 0), pipeline_mode=pl.Buffered(1)),
        ],
        out_specs=[pl.BlockSpec((None, tm, wd), lambda b, i: (b, i, 0)) for wd in widths],
        out_shape=[jax.ShapeDtypeStruct((bsz, t, wd), dt) for wd, dt in zip(widths, dtypes)],
        compiler_params=pltpu.CompilerParams(
            dimension_semantics=("arbitrary", "arbitrary"), vmem_limit_bytes=VMEM_LIMIT),
        name="in_proj",
    )(x, mods, g1, w)


def _post_kernel(x_ref, yml_ref, yssd_ref, yatt_ref, mod_ref, g_ref, wo_ref, wup_ref, wdn_ref, o_ref):
    proj = (_dot(yml_ref[...], wo_ref[0:ML_WIDTH, :])
            + _dot(yssd_ref[...], wo_ref[ML_WIDTH:ML_WIDTH + SSD_WIDTH, :])
            + _dot(yatt_ref[...], wo_ref[ML_WIDTH + SSD_WIDTH:MIX_WIDTH, :]))
    x1 = x_ref[...] + mod_ref[2:3, :] * proj
    ms = jnp.mean(x1 * x1, axis=-1, keepdims=True)
    xn = x1 * lax.rsqrt(ms + EPS)
    hb = (xn * (g_ref[...] * (1.0 + mod_ref[4:5, :])) + mod_ref[3:4, :]).astype(BF16)
    d_ff = wup_ref.shape[1]
    step = 1024
    acc = None
    for j in range(0, d_ff, step):
        up = jnp.maximum(_dot(hb, wup_ref[:, j:j + step]), 0.0)
        part = _dot((up * up).astype(BF16), wdn_ref[j:j + step, :])
        acc = part if acc is None else acc + part
    o_ref[...] = x1 + mod_ref[5:6, :] * acc


def _post(x, yml, yssd, yatt, mods, mod_row, g2, wo, wup, wdn, layer, tm):
    bsz, t, d = x.shape
    d_ff = wup.shape[2]
    const = dict(pipeline_mode=pl.Buffered(1))
    return pl.pallas_call(
        _post_kernel,
        grid=(bsz, t // tm),
        in_specs=[
            pl.BlockSpec((None, tm, d), lambda b, i: (b, i, 0)),
            pl.BlockSpec((None, tm, ML_WIDTH), lambda b, i: (b, i, 0)),
            pl.BlockSpec((None, tm, SSD_WIDTH), lambda b, i: (b, i, 0)),
            pl.BlockSpec((None, tm, ATT_WIDTH), lambda b, i: (b, i, 0)),
            pl.BlockSpec((None, None, 6, d), lambda b, i: (layer, mod_row(b), 0, 0)),
            pl.BlockSpec((None, 1, d), lambda b, i: (layer, 0, 0)),
            pl.BlockSpec((None, MIX_WIDTH, d), lambda b, i: (layer, 0, 0), **const),
            pl.BlockSpec((None, d, d_ff), lambda b, i: (layer, 0, 0), **const),
            pl.BlockSpec((None, d_ff, d), lambda b, i: (layer, 0, 0), **const),
        ],
        out_specs=pl.BlockSpec((None, tm, d), lambda b, i: (b, i, 0)),
        out_shape=jax.ShapeDtypeStruct((bsz, t, d), F32),
        compiler_params=pltpu.CompilerParams(
            dimension_semantics=("arbitrary", "arbitrary"), vmem_limit_bytes=VMEM_LIMIT),
        name="post",
    )(x, yml, yssd, yatt, mods, g2, wo, wup, wdn)


ML_GATES = N_DIR * ML_HEADS


def _mlstm_kernel(ul_ref, uc_ref, gl_ref, gc_ref, pv_ref, ng_ref, ex_ref, *rest,
                  need_ctx_out, n_lat, n_ctx):
    if need_ctx_out:
        yl_ref, yc_ref, arow, packed, ends, mprev, cbk, crun = rest
    else:
        yl_ref, arow, packed, ends, mprev, cbk, crun = rest
        yc_ref = None
    ncl, ncc = n_lat // CHUNK, n_ctx // CHUNK
    nct = ncl + ncc
    nh = ML_HEADS
    ti = _iota((CHUNK, CHUNK), 0)
    si = _iota((CHUNK, CHUNK), 1)
    low = si <= ti
    upp = si >= ti
    tri_low = jnp.where(low, 1.0, 0.0).astype(BF16)
    tri_upp = jnp.where(upp, 1.0, 0.0).astype(BF16)
    fwd_row = _iota((ML_GATES, CHUNK), 0) < nh
    tok = _iota((ML_GATES, CHUNK), 1)
    bias = pv_ref[0:1, :]
    neg_inf = -jnp.inf
    ones_blk = jnp.ones((CHUNK, LANES), BF16)

    def prep(g_ref, nch, goff):
        def body(c, carry):
            gt = (g_ref[_chunk_rows(c), :] + bias).T
            lf = _log_sigmoid(gt[0:ML_GATES])
            li = gt[LI_LANE0:LI_LANE0 + ML_GATES]
            prefix, suffix = _row_cumsums((tri_upp, tri_low), lf)
            b = jnp.where(fwd_row, prefix, suffix)
            a = li - b
            cm = a
            k = 1
            while k < CHUNK:
                before = jnp.where(tok >= k, pltpu.roll(cm, k, 1), neg_inf)
                after = jnp.where(tok < CHUNK - k, pltpu.roll(cm, CHUNK - k, 1), neg_inf)
                cm = jnp.maximum(cm, jnp.where(fwd_row, before, after))
                k *= 2
            arow[goff + c] = a
            parts = [p.astype(F32) for p in _split3(cm) + _split3(b)]
            parts.append(jnp.zeros((CHUNK - len(parts) * ML_GATES, CHUNK), F32))
            packed[goff + c] = jnp.concatenate(parts, axis=0).T.astype(BF16)
            for i, x in enumerate((cm, b)):
                end = jnp.where(fwd_row[:, 0:1], x[:, CHUNK - 1:CHUNK], x[:, 0:1])
                ends[goff + c, i] = jnp.broadcast_to(end, (ML_GATES, CHUNK))
            return carry
        lax.fori_loop(0, nch, body, 0, unroll=UNROLL_GATES)

    prep(gc_ref, ncc, 0)
    prep(gl_ref, ncl, ncc)

    def m_scan(lo, hi, reverse, m0):
        def body(i, m):
            cg = hi - 1 - i if reverse else lo + i
            mprev[cg, int(reverse)] = m
            return ends[cg, 1] + jnp.maximum(m, ends[cg, 0])
        return lax.fori_loop(0, hi - lo, body, m0)

    m_zero = jnp.zeros((ML_GATES, CHUNK), F32)
    m_scan(0, nct, False, m_zero)
    m_scan(ncc, nct, True, m_scan(0, ncc, True, m_zero))

    def load_kv(u_ref, rows, h):
        k = u_ref[rows, ML_WIDTH + h * LANES:ML_WIDTH + (h + 1) * LANES]
        v = u_ref[rows, 2 * ML_WIDTH + h * LANES:2 * ML_WIDTH + (h + 1) * LANES]
        return k, jnp.concatenate([v, ones_blk], axis=1)

    def update(d, h, cg, k, v1):
        j = d * nh + h
        m_prev = mprev[cg, d, j:j + 1, 0:1]
        m_end = jnp.maximum(m_prev, ends[cg, 0, j:j + 1, 0:1])
        w_row = jnp.exp(arow[cg, j:j + 1, :] - m_end) * ML_SCALE
        ktw = (k.T.astype(F32) * w_row).astype(BF16)
        crun[d, h] = jnp.exp(m_prev - m_end) * crun[d, h] + _dot(ktw, v1)

    def bwd_pass(u_ref, nch, goff, keep):
        def body(i, carry):
            c = nch - 1 - i
            rows = _chunk_rows(c)
            for h in range(nh):
                if keep:
                    cbk[goff + c, h] = crun[1, h].astype(BF16)
                k, v1 = load_kv(u_ref, rows, h)
                update(1, h, goff + c, k, v1)
            return carry
        lax.fori_loop(0, nch, body, 0, unroll=UNROLL_LIGHT)

    def fwd_pass(u_ref, y_ref, nch, goff):
        def body(c, carry):
            rows = _chunk_rows(c)
            cg = goff + c
            for h in range(nh):
                k, v1 = load_kv(u_ref, rows, h)
                if y_ref is not None:
                    q = u_ref[rows, h * LANES:(h + 1) * LANES]
                    o = u_ref[rows, 3 * ML_WIDTH + h * LANES:3 * ML_WIDTH + (h + 1) * LANES]
                    sqk = _dot_nt(q, k) * ML_SCALE
                    qc = _dot(q, jnp.concatenate([crun[0, h].astype(BF16), cbk[cg, h]], axis=1))
                    s_dirs, stats = [], []
                    for d, mask in ((0, low), (1, upp)):
                        j = d * nh + h
                        rep = _dot(packed[cg], ex_ref[:, 2 * j * LANES:2 * (j + 1) * LANES])
                        m_prev = mprev[cg, d, j:j + 1, 0:1]
                        m_t = jnp.maximum(rep[:, 0:LANES], m_prev)
                        s_dirs.append(sqk * jnp.exp(jnp.where(mask, arow[cg, j:j + 1, :] - m_t, neg_inf)))
                        stats.append((m_prev, m_t, rep[:, LANES:2 * LANES]))
                    sv = _dot(jnp.concatenate(s_dirs, axis=0).astype(BF16), v1)
                    hsum = None
                    for d, (m_prev, m_t, b_t) in enumerate(stats):
                        g = jnp.exp(m_prev - m_t)
                        sv_d = sv[d * CHUNK:(d + 1) * CHUNK, :]
                        qc_d = qc[:, 2 * d * LANES:2 * (d + 1) * LANES]
                        num = sv_d[:, 0:LANES] + g * qc_d[:, 0:LANES]
                        den = sv_d[:, LANES:2 * LANES] + g * qc_d[:, LANES:2 * LANES]
                        hd = num / jnp.maximum(jnp.abs(den), jnp.exp(-(b_t + m_t)))
                        hsum = hd if hsum is None else hsum + hd
                    ms = jnp.mean(hsum * hsum, axis=1, keepdims=True)
                    y = hsum * lax.rsqrt(ms + EPS) * ng_ref[:, h * LANES:(h + 1) * LANES]
                    y = y * _sigmoid(o.astype(F32))
                    y_ref[rows, h * LANES:(h + 1) * LANES] = y.astype(y_ref.dtype)
                update(0, h, cg, k, v1)
            return carry
        lax.fori_loop(0, nch, body, 0, unroll=UNROLL)

    crun[...] = jnp.zeros(crun.shape, F32)
    bwd_pass(uc_ref, ncc, 0, need_ctx_out)
    bwd_pass(ul_ref, ncl, ncc, True)
    fwd_pass(uc_ref, yc_ref, ncc, 0)
    fwd_pass(ul_ref, yl_ref, ncl, ncc)


def _ml_expand_table():
    expand = np.zeros((LANES, ML_GATES * 2 * LANES), np.float32)
    for j in range(ML_GATES):
        for i in range(6):
            half = i // 3
            expand[ML_GATES * i + j, (2 * j + half) * LANES:(2 * j + half + 1) * LANES] = 1.0
    return jnp.asarray(expand, BF16)


def _mlstm_mixer(uml_l, uml_c, g_l, g_c, pvec, norm_g, need_ctx_out):
    expand = _ml_expand_table()
    bsz, n_lat, _ = uml_l.shape
    n_ctx = uml_c.shape[1]
    nct = (n_lat + n_ctx) // CHUNK
    kern = functools.partial(_mlstm_kernel, need_ctx_out=need_ctx_out, n_lat=n_lat, n_ctx=n_ctx)
    out_specs = [pl.BlockSpec((None, n_lat, ML_WIDTH), lambda b: (b, 0, 0))]
    out_shape = [jax.ShapeDtypeStruct((bsz, n_lat, ML_WIDTH), BF16)]
    if need_ctx_out:
        out_specs.append(pl.BlockSpec((None, n_ctx, ML_WIDTH), lambda b: (b, 0, 0)))
        out_shape.append(jax.ShapeDtypeStruct((bsz, n_ctx, ML_WIDTH), BF16))
    outs = pl.pallas_call(
        kern,
        grid=(bsz,),
        in_specs=[
            pl.BlockSpec((None, n_lat, U_ML), lambda b: (b, 0, 0)),
            pl.BlockSpec((None, n_ctx, U_ML), lambda b: (b, 0, 0)),
            pl.BlockSpec((None, n_lat, U_GATE), lambda b: (b, 0, 0)),
            pl.BlockSpec((None, n_ctx, U_GATE), lambda b: (b, 0, 0)),
            pl.BlockSpec((8, LANES), lambda b: (0, 0)),
            pl.BlockSpec((1, ML_WIDTH), lambda b: (0, 0)),
            pl.BlockSpec(expand.shape, lambda b: (0, 0)),
        ],
        out_specs=out_specs,
        out_shape=out_shape,
        scratch_shapes=[
            pltpu.VMEM((nct, ML_GATES, CHUNK), F32),
            pltpu.VMEM((nct, CHUNK, LANES), BF16),
            pltpu.VMEM((nct, 2, ML_GATES, CHUNK), F32),
            pltpu.VMEM((nct, N_DIR, ML_GATES, CHUNK), F32),
            pltpu.VMEM((nct, ML_HEADS, ML_HEAD_DIM, 2 * LANES), BF16),
            pltpu.VMEM((N_DIR, ML_HEADS, ML_HEAD_DIM, 2 * LANES), F32),
        ],
        compiler_params=pltpu.CompilerParams(
            dimension_semantics=("arbitrary",), vmem_limit_bytes=VMEM_LIMIT),
        name="mlstm",
    )(uml_l, uml_c, g_l, g_c, pvec, norm_g, expand)
    return (outs[0], outs[1]) if need_ctx_out else (outs[0], None)


def _ssd_kernel(ul_ref, uc_ref, gl_ref, gc_ref, pv_ref, cw_ref, cb_ref, dsk_ref, ng_ref, ex_ref, *rest,
                need_ctx_out, n_lat, n_ctx):
    if need_ctx_out:
        yl_ref, yc_ref, xact, acum, packed, act, dtt, hbk, hrun = rest
    else:
        yl_ref, xact, acum, packed, act, dtt, hbk, hrun = rest
        yc_ref = None
    ncl, ncc = n_lat // CHUNK, n_ctx // CHUNK
    nh, hg = SSD_HEADS, SSD_HEADS // SSD_GROUPS
    gw = hg * SSD_HEAD_DIM
    gn = SSD_GROUPS * SSD_STATE
    ti = _iota((CHUNK, CHUNK), 0)
    si = _iota((CHUNK, CHUNK), 1)
    low_strict = si < ti
    upp_strict = si > ti
    tri_low = jnp.where(si <= ti, 1.0, 0.0).astype(BF16)
    tri_upp = jnp.where(si >= ti, 1.0, 0.0).astype(BF16)
    lane = _iota((CHUNK, LANES), 1)
    nj = N_DIR * nh
    ac_lane0 = 6 * nj
    fwd_row = _iota((nj, CHUNK), 0) < nh
    bias0 = pv_ref[0:1, :]
    a_rows = jnp.broadcast_to(-jnp.exp(pv_ref[2:3, :]), (LANES, LANES)).T[DT_LANE0:DT_LANE0 + nj]
    shift_down = jnp.where(si == ti - 1, 1.0, 0.0).astype(BF16)
    shift_up = jnp.where(si == ti + 1, 1.0, 0.0).astype(BF16)
    lane_group0 = lane < SSD_STATE

    def prep(u_ref, g_ref, nch, goff, nrows):
        def body(c, carry):
            r = pl.multiple_of(c * CHUNK, CHUNK)
            cur = u_ref[pl.ds(r, CHUNK), SSD_WIDTH:U_SSD]
            rp = pl.multiple_of(jnp.maximum(r - CHUNK, 0), CHUNK)
            rn = pl.multiple_of(jnp.minimum(r + CHUNK, nrows - CHUNK), CHUNK)
            prev = u_ref[pl.ds(rp, CHUNK), SSD_WIDTH:U_SSD]
            nxt = u_ref[pl.ds(rn, CHUNK), SSD_WIDTH:U_SSD]
            from_prev = jnp.where((ti == 0) & (si == CHUNK - 1) & (c > 0), 1.0, 0.0).astype(BF16)
            from_next = jnp.where((ti == CHUNK - 1) & (si == 0) & (c < nch - 1), 1.0, 0.0).astype(BF16)
            xm = _dot(jnp.concatenate([shift_down, from_prev], axis=1), jnp.concatenate([cur, prev], axis=0))
            xp = _dot(jnp.concatenate([shift_up, from_next], axis=1), jnp.concatenate([cur, nxt], axis=0))
            conv = (cw_ref[0:1, :] * xm + cw_ref[1:2, :] * cur.astype(F32) + cw_ref[2:3, :] * xp
                    + cb_ref[...])
            xact[goff + c] = (conv * _sigmoid(conv)).astype(BF16)

            gt = (g_ref[pl.ds(r, CHUNK), :] + bias0).T
            dt = _softplus(gt[DT_LANE0:DT_LANE0 + nj])
            prefix, suffix = _row_cumsums((tri_upp, tri_low), dt * a_rows)
            ac = jnp.where(fwd_row, prefix, suffix)
            a_end = jnp.where(fwd_row[:, 0:1], ac[:, CHUNK - 1:CHUNK], ac[:, 0:1])
            act[goff + c] = ac - jnp.log(dt)
            dtt[goff + c] = dt
            parts = [p.astype(F32) for p in _split3(jnp.exp(ac)) + _split3(jnp.exp(a_end - ac) * dt)]
            parts += [ac, jnp.zeros((CHUNK - 7 * nj, CHUNK), F32)]
            tile = jnp.concatenate(parts, axis=0).T
            acum[goff + c] = tile
            packed[goff + c] = tile.astype(BF16)
            return carry
        lax.fori_loop(0, nch, body, 0, unroll=UNROLL_LIGHT)

    prep(uc_ref, gc_ref, ncc, 0, n_ctx)
    prep(ul_ref, gl_ref, ncl, ncc, n_lat)

    def e_cols(d, g):
        return slice(d * SSD_WIDTH + g * gw, d * SSD_WIDTH + (g + 1) * gw)

    def w_cols(d, g):
        return slice((N_DIR + d) * SSD_WIDTH + g * gw, (N_DIR + d) * SSD_WIDTH + (g + 1) * gw)

    def update_state(d, cg):
        pk = packed[cg]
        xa = xact[cg]
        b_all = xa[:, SSD_WIDTH:SSD_WIDTH + gn]
        zero = jnp.zeros_like(b_all)
        upd = []
        for g in range(SSD_GROUPS):
            wx = (_dot(pk, ex_ref[:, w_cols(d, g)]) * xa[:, g * gw:(g + 1) * gw].astype(F32)).astype(BF16)
            upd.append(_dot_tn(jnp.where(lane_group0 == (g == 0), b_all, zero), wx))
        tail, last = (slice(CHUNK - 16, CHUNK), 15) if d == 0 else (slice(0, 16), 0)
        decay = _dot(pk[tail], ex_ref[:, d * SSD_WIDTH:(d + 1) * SSD_WIDTH])[last:last + 1, :]
        hrun[d] = decay * hrun[d] + jnp.concatenate(upd, axis=1)

    def bwd_pass(nch, goff, keep):
        def body(i, carry):
            cg = goff + nch - 1 - i
            if keep:
                hbk[cg] = hrun[1].astype(BF16)
            update_state(1, cg)
            return carry
        lax.fori_loop(0, nch, body, 0, unroll=UNROLL_LIGHT)

    def fwd_pass(u_ref, y_ref, nch, goff):
        def body(c, carry):
            cg = goff + c
            if y_ref is not None:
                rows = _chunk_rows(c)
                xa = xact[cg]
                x = xa[:, 0:SSD_WIDTH]
                b_all = xa[:, SSD_WIDTH:SSD_WIDTH + gn]
                c_all = xa[:, SSD_WIDTH + gn:SSD_XBC]
                ac = acum[cg]
                ac_t = act[cg]
                dt_t = dtt[cg]
                lane_w = _iota((CHUNK, gw), 1)
                pk = packed[cg]
                yz_groups, ssq = [], None
                for g in range(SSD_GROUPS):
                    cols = slice(g * gw, (g + 1) * gw)
                    c_g = jnp.where(lane_group0 == (g == 0), c_all, jnp.zeros_like(c_all))
                    cb = _dot_nt(c_g, b_all)
                    s_heads = []
                    for hh in range(hg):
                        jf = g * hg + hh
                        jb = jf + nh
                        on_diag = jnp.log(dt_t[jf:jf + 1, :] + dt_t[jb:jb + 1, :])
                        ac_f = ac[:, ac_lane0 + jf:ac_lane0 + jf + 1]
                        ac_b = ac[:, ac_lane0 + jb:ac_lane0 + jb + 1]
                        expo = jnp.where(low_strict, ac_f - ac_t[jf:jf + 1, :],
                                         jnp.where(upp_strict, ac_b - ac_t[jb:jb + 1, :], on_diag))
                        s_heads.append((cb * jnp.exp(expo)).astype(BF16))
                    x_g = x[:, cols]
                    bd = jnp.concatenate(
                        [jnp.where((lane_w >= hh * SSD_HEAD_DIM) & (lane_w < (hh + 1) * SSD_HEAD_DIM),
                                   x_g, jnp.zeros_like(x_g)) for hh in range(hg)], axis=0)
                    y = _dot(jnp.concatenate(s_heads, axis=1), bd)
                    y = y + _dot(c_all, hrun[0, :, cols].astype(BF16)) * _dot(pk, ex_ref[:, e_cols(0, g)])
                    y = y + _dot(c_all, hbk[cg, :, cols]) * _dot(pk, ex_ref[:, e_cols(1, g)])
                    y = y + dsk_ref[:, cols] * x_g.astype(F32)
                    z = u_ref[rows, cols].astype(F32)
                    yz = y * (z * _sigmoid(z))
                    part = jnp.sum(yz * yz, axis=1, keepdims=True)
                    ssq = part if ssq is None else ssq + part
                    yz_groups.append(yz)
                scale = lax.rsqrt(ssq * (1.0 / SSD_WIDTH) + EPS)
                for g, yz in enumerate(yz_groups):
                    cols = slice(g * gw, (g + 1) * gw)
                    y_ref[rows, cols] = (yz * scale * ng_ref[:, cols]).astype(y_ref.dtype)
            update_state(0, cg)
            return carry
        lax.fori_loop(0, nch, body, 0, unroll=UNROLL)

    hrun[...] = jnp.zeros(hrun.shape, F32)
    bwd_pass(ncc, 0, need_ctx_out)
    bwd_pass(ncl, ncc, True)
    fwd_pass(uc_ref, yc_ref, ncc, 0)
    fwd_pass(ul_ref, yl_ref, ncl, ncc)


def _ssd_expand_table():
    nj = N_DIR * SSD_HEADS
    expand = np.zeros((LANES, 2 * N_DIR * SSD_WIDTH), np.float32)
    for i in range(6):
        for d in range(N_DIR):
            for h in range(SSD_HEADS):
                j = d * SSD_HEADS + h
                c0 = ((i // 3) * N_DIR + d) * SSD_WIDTH + h * SSD_HEAD_DIM
                expand[nj * i + j, c0:c0 + SSD_HEAD_DIM] = 1.0
    return jnp.asarray(expand, BF16)


def _ssd_mixer(ussd_l, ussd_c, g_l, g_c, pvec, conv_w, conv_b, d_skip, norm_g, need_ctx_out):
    expand = _ssd_expand_table()
    bsz, n_lat, _ = ussd_l.shape
    n_ctx = ussd_c.shape[1]
    nct = (n_lat + n_ctx) // CHUNK
    gn = SSD_GROUPS * SSD_STATE
    kern = functools.partial(_ssd_kernel, need_ctx_out=need_ctx_out, n_lat=n_lat, n_ctx=n_ctx)
    out_specs = [pl.BlockSpec((None, n_lat, SSD_WIDTH), lambda b: (b, 0, 0))]
    out_shape = [jax.ShapeDtypeStruct((bsz, n_lat, SSD_WIDTH), BF16)]
    if need_ctx_out:
        out_specs.append(pl.BlockSpec((None, n_ctx, SSD_WIDTH), lambda b: (b, 0, 0)))
        out_shape.append(jax.ShapeDtypeStruct((bsz, n_ctx, SSD_WIDTH), BF16))
    outs = pl.pallas_call(
        kern,
        grid=(bsz,),
        in_specs=[
            pl.BlockSpec((None, n_lat, U_SSD), lambda b: (b, 0, 0)),
            pl.BlockSpec((None, n_ctx, U_SSD), lambda b: (b, 0, 0)),
            pl.BlockSpec((None, n_lat, LANES), lambda b: (b, 0, 0)),
            pl.BlockSpec((None, n_ctx, LANES), lambda b: (b, 0, 0)),
            pl.BlockSpec((8, LANES), lambda b: (0, 0)),
            pl.BlockSpec((3, SSD_XBC), lambda b: (0, 0)),
            pl.BlockSpec((1, SSD_XBC), lambda b: (0, 0)),
            pl.BlockSpec((1, SSD_WIDTH), lambda b: (0, 0)),
            pl.BlockSpec((1, SSD_WIDTH), lambda b: (0, 0)),
            pl.BlockSpec(expand.shape, lambda b: (0, 0)),
        ],
        out_specs=out_specs,
        out_shape=out_shape,
        scratch_shapes=[
            pltpu.VMEM((nct, CHUNK, SSD_XBC), BF16),
            pltpu.VMEM((nct, CHUNK, LANES), F32),
            pltpu.VMEM((nct, CHUNK, LANES), BF16),
            pltpu.VMEM((nct, N_DIR * SSD_HEADS, CHUNK), F32),
            pltpu.VMEM((nct, N_DIR * SSD_HEADS, CHUNK), F32),
            pltpu.VMEM((nct, gn, SSD_WIDTH), BF16),
            pltpu.VMEM((N_DIR, gn, SSD_WIDTH), F32),
        ],
        compiler_params=pltpu.CompilerParams(
            dimension_semantics=("arbitrary",), vmem_limit_bytes=VMEM_LIMIT),
        name="ssd",
    )(ussd_l, ussd_c, g_l, g_c, pvec, conv_w, conv_b, d_skip, norm_g, expand)
    return (outs[0], outs[1]) if need_ctx_out else (outs[0], None)


def _head_norm(x, gain):
    rows = x.shape[0]
    blks = [x[:, j * LANES:(j + 1) * LANES] for j in range(x.shape[1] // LANES)]
    same_head = (_iota((LANES, LANES), 0) < ATT_HEAD_DIM) == (_iota((LANES, LANES), 1) < ATT_HEAD_DIM)
    sq = jnp.concatenate([b * b for b in blks], axis=0).astype(BF16)
    ms = _dot(sq, jnp.where(same_head, 1.0, 0.0).astype(BF16)) * (1.0 / ATT_HEAD_DIM)
    outs = [b * lax.rsqrt(ms[j * rows:(j + 1) * rows] + EPS) * gain for j, b in enumerate(blks)]
    return outs[0] if len(outs) == 1 else jnp.concatenate(outs, axis=1)


def _rope(x, cos_t, sin_t):
    w = x.shape[1]
    reps = w // LANES
    nfreq = ATT_HEAD_DIM // 4
    partner = jnp.where((_iota(x.shape, 1) & (2 * nfreq - 1)) < nfreq,
                        pltpu.roll(x, w - nfreq, 1), pltpu.roll(x, nfreq, 1))
    if reps > 1:
        cos_t = jnp.concatenate([cos_t] * reps, axis=1)
        sin_t = jnp.concatenate([sin_t] * reps, axis=1)
    return x * cos_t + partner * sin_t


def _dup_heads(x):
    lane_lo = _iota(x.shape, 1) < ATT_HEAD_DIM
    sw = pltpu.roll(x, ATT_HEAD_DIM, 1)
    return jnp.concatenate([jnp.where(lane_lo, x, sw), jnp.where(lane_lo, sw, x)], axis=1)


def _attn_kernel(ul_ref, uc_ref, cos_ref, sin_ref, qg_ref, kg_ref, sink_ref, *rest,
                 need_ctx_out, n_lat, n_ctx):
    if need_ctx_out:
        yl_ref, yc_ref, qp, kb, vb, kc, vc, qcp = rest
    else:
        yl_ref, qp, kb, vb, kc, vc = rest
        yc_ref = qcp = None
    ncl, ncc = n_lat // CHUNK, n_ctx // CHUNK
    rep = ATT_Q_HEADS // ATT_KV_HEADS
    kv0 = ATT_WIDTH
    lane_lo = _iota((CHUNK, LANES), 1) < ATT_HEAD_DIM
    neg_inf = -jnp.inf
    qg = qg_ref[...]
    kg = kg_ref[...]

    for ref in (kb, vb):
        zeros_blk = jnp.zeros((CHUNK, ref.shape[1]), BF16)
        ref[0:CHUNK, :] = zeros_blk
        ref[CHUNK + n_lat:2 * CHUNK + n_lat, :] = zeros_blk

    def values_with_ones(v):
        d = _dup_heads(v).astype(BF16)
        ones = jnp.ones((v.shape[0], LANES), BF16)
        return jnp.concatenate([d[:, 0:LANES], ones, d[:, LANES:2 * LANES], ones], axis=1)

    def prep_lat(c, carry):
        rows = _chunk_rows(c)
        cos_t = cos_ref[rows, :]
        sin_t = sin_ref[rows, :]
        q = _rope(_head_norm(ul_ref[rows, 0:ATT_WIDTH].astype(F32), qg), cos_t, sin_t)
        qp[rows, :] = (q * ATT_SCALE).astype(BF16)
        k = _rope(_head_norm(ul_ref[rows, kv0:kv0 + LANES].astype(F32), kg), cos_t, sin_t)
        v = ul_ref[rows, kv0 + LANES:kv0 + 2 * LANES].astype(F32)
        dst = pl.ds(pl.multiple_of((c + 1) * CHUNK, CHUNK), CHUNK)
        kb[dst, :] = _dup_heads(k).astype(BF16)
        vb[dst, :] = values_with_ones(v)
        return carry
    lax.fori_loop(0, ncl, prep_lat, 0, unroll=UNROLL_LIGHT)

    def prep_ctx(c, carry):
        rows = _chunk_rows(c)
        k = _head_norm(uc_ref[rows, kv0:kv0 + LANES].astype(F32), kg)
        v = uc_ref[rows, kv0 + LANES:kv0 + 2 * LANES].astype(F32)
        kc[rows, :] = _dup_heads(k).astype(BF16)
        vc[rows, :] = values_with_ones(v)
        if need_ctx_out:
            q = _head_norm(uc_ref[rows, 0:ATT_WIDTH].astype(F32), qg)
            qcp[rows, :] = (q * ATT_SCALE).astype(BF16)
        return carry
    lax.fori_loop(0, ncc, prep_ctx, 0, unroll=UNROLL)

    def attend(q_ref, y_ref, rows, band):
        for g in range(ATT_KV_HEADS):
            kl = slice(g * LANES, (g + 1) * LANES)
            vl = slice(2 * g * LANES, 2 * (g + 1) * LANES)
            qms = []
            for r in range(rep):
                hq = g * rep + r
                qb = q_ref[rows, (hq // 2) * LANES:(hq // 2 + 1) * LANES]
                qms.append(jnp.where(lane_lo != (hq % 2 == 1), qb, jnp.zeros_like(qb)))
            qs = jnp.concatenate(qms, axis=0)
            s_c = _dot_nt(qs, kc[:, kl])
            if band is not None:
                band_rows, valid = band
                s_b = _dot_nt(qs, kb[band_rows, kl])
            ps, sink_terms = [], []
            for r in range(rep):
                sl = slice(r * CHUNK, (r + 1) * CHUNK)
                sk = sink_ref[g * rep + r]
                s = s_c[sl]
                if band is not None:
                    s = jnp.concatenate([s, jnp.where(valid, s_b[sl], neg_inf)], axis=1)
                m = jnp.maximum(jnp.max(s, axis=1, keepdims=True), sk)
                ps.append(jnp.exp(s - m).astype(BF16))
                sink_terms.append(jnp.exp(sk - m))
            p = jnp.concatenate(ps, axis=0)
            acc = _dot(p[:, 0:n_ctx], vc[:, vl])
            if band is not None:
                acc = acc + _dot(p[:, n_ctx:], vb[band_rows, vl])
            for r in range(rep):
                sl = slice(r * CHUNK, (r + 1) * CHUNK)
                o = acc[sl, 0:LANES] / (acc[sl, LANES:2 * LANES] + sink_terms[r])
                if r % 2 == 1:
                    jb = (g * rep + r) // 2
                    y_ref[rows, jb * LANES:(jb + 1) * LANES] = jnp.where(lane_lo, o_even, o).astype(y_ref.dtype)
                o_even = o

    qq = _iota((CHUNK, 3 * CHUNK), 0)
    kk = _iota((CHUNK, 3 * CHUNK), 1)
    in_window = (kk >= qq) & (kk <= qq + 2 * WINDOW)

    def lat_block(i, carry):
        kpos = kk + (i - 1) * CHUNK
        valid = in_window & (kpos >= 0) & (kpos < n_lat)
        band_rows = pl.ds(pl.multiple_of(i * CHUNK, CHUNK), 3 * CHUNK)
        attend(qp, yl_ref, _chunk_rows(i), (band_rows, valid))
        return carry
    lax.fori_loop(0, ncl, lat_block, 0, unroll=UNROLL)

    if need_ctx_out:
        def ctx_block(i, carry):
            attend(qcp, yc_ref, _chunk_rows(i), None)
            return carry
        lax.fori_loop(0, ncc, ctx_block, 0, unroll=UNROLL)


def _attn_mixer(uatt_l, uatt_c, cos_t, sin_t, qgain, kgain, sink, need_ctx_out):
    bsz, n_lat, _ = uatt_l.shape
    n_ctx = uatt_c.shape[1]
    kern = functools.partial(_attn_kernel, need_ctx_out=need_ctx_out, n_lat=n_lat, n_ctx=n_ctx)
    out_specs = [pl.BlockSpec((None, n_lat, ATT_WIDTH), lambda b: (b, 0, 0))]
    out_shape = [jax.ShapeDtypeStruct((bsz, n_lat, ATT_WIDTH), BF16)]
    scratch = [
        pltpu.VMEM((n_lat, ATT_WIDTH), BF16),
        pltpu.VMEM((n_lat + 2 * CHUNK, 2 * LANES), BF16),
        pltpu.VMEM((n_lat + 2 * CHUNK, 4 * LANES), BF16),
        pltpu.VMEM((n_ctx, 2 * LANES), BF16),
        pltpu.VMEM((n_ctx, 4 * LANES), BF16),
    ]
    if need_ctx_out:
        out_specs.append(pl.BlockSpec((None, n_ctx, ATT_WIDTH), lambda b: (b, 0, 0)))
        out_shape.append(jax.ShapeDtypeStruct((bsz, n_ctx, ATT_WIDTH), BF16))
        scratch.append(pltpu.VMEM((n_ctx, ATT_WIDTH), BF16))
    outs = pl.pallas_call(
        kern,
        grid=(bsz,),
        in_specs=[
            pl.BlockSpec((None, n_lat, U_ATT), lambda b: (b, 0, 0)),
            pl.BlockSpec((None, n_ctx, U_ATT), lambda b: (b, 0, 0)),
            pl.BlockSpec((n_lat, LANES), lambda b: (0, 0)),
            pl.BlockSpec((n_lat, LANES), lambda b: (0, 0)),
            pl.BlockSpec((1, LANES), lambda b: (0, 0)),
            pl.BlockSpec((1, LANES), lambda b: (0, 0)),
            pl.BlockSpec(memory_space=pltpu.SMEM),
        ],
        out_specs=out_specs,
        out_shape=out_shape,
        scratch_shapes=scratch,
        compiler_params=pltpu.CompilerParams(
            dimension_semantics=("arbitrary",), vmem_limit_bytes=VMEM_LIMIT),
        name="attention",
    )(uatt_l, uatt_c, cos_t, sin_t, qgain, kgain, sink)
    return (outs[0], outs[1]) if need_ctx_out else (outs[0], None)


def _in_layout():
    names = (('ml_q', ML_WIDTH), ('ml_k', ML_WIDTH), ('ml_v', ML_WIDTH), ('ml_o', ML_WIDTH),
             ('ml_i', N_DIR * ML_HEADS), ('ml_f', N_DIR * ML_HEADS),
             ('ssd_z', SSD_WIDTH), ('ssd_xbc', SSD_XBC), ('ssd_dt', N_DIR * SSD_HEADS),
             ('att_q', ATT_WIDTH), ('att_k', ATT_KV_WIDTH), ('att_v', ATT_KV_WIDTH))
    out, off = {}, 0
    for name, width in names:
        out[name] = (off, off + width)
        off += width
    return out


def _col_map():
    lay = _in_layout()
    take = lambda name: np.arange(lay[name][0], lay[name][1])
    zeros = lambda n: np.full((n,), -1)
    n_f, n_dt, n_i = N_DIR * ML_HEADS, N_DIR * SSD_HEADS, N_DIR * ML_HEADS
    return np.concatenate(
        [take('ml_q'), take('ml_k'), take('ml_v'), take('ml_o'),
         take('ssd_z'), take('ssd_xbc'),
         take('att_q'), take('att_k'), take('att_v'),
         take('ml_f'), take('ssd_dt'), take('ml_i'), zeros(LANES - n_f - n_dt - n_i)])


def _permute_plan():
    src = _col_map()
    plan = []
    for t in range(U_TOTAL // LANES):
        cols = src[t * LANES:(t + 1) * LANES]
        runs, i = [], 0
        while i < LANES:
            j = i
            while j + 1 < LANES and (cols[j + 1] == cols[j] + 1 if cols[j] >= 0 else cols[j + 1] < 0):
                j += 1
            runs.append((int(cols[i]), j - i + 1))
            i = j + 1
        plan.append(tuple(runs))
    return tuple(plan)


def _wprep_kernel(wt_ref, o_ref, *, plan):
    kb = wt_ref.shape[1]
    for t, runs in enumerate(plan):
        pieces = [wt_ref[s:s + n, :] if s >= 0 else jnp.zeros((n, kb), F32) for s, n in runs]
        blk = pieces[0] if len(pieces) == 1 else jnp.concatenate(pieces, axis=0)
        o_ref[:, t * LANES:(t + 1) * LANES] = blk.T.astype(BF16)


def _permute_weights(w_in):
    depth, d, in_width = w_in.shape
    plan = _permute_plan()
    kb = 256
    return pl.pallas_call(
        functools.partial(_wprep_kernel, plan=plan),
        grid=(depth, d // kb),
        in_specs=[pl.BlockSpec((None, in_width, kb), lambda l, i: (l, 0, i))],
        out_specs=pl.BlockSpec((None, kb, U_TOTAL), lambda l, i: (l, i, 0)),
        out_shape=jax.ShapeDtypeStruct((depth, d, U_TOTAL), BF16),
        compiler_params=pltpu.CompilerParams(
            dimension_semantics=("arbitrary", "arbitrary"), vmem_limit_bytes=VMEM_LIMIT),
        name="permute_weights",
    )(jnp.swapaxes(w_in, 1, 2))


def _gate_rows(ml_b_i, ml_b_f, ssd_dt_bias, ssd_a_log):
    n_f, n_dt = N_DIR * ML_HEADS, N_DIR * SSD_HEADS
    row0 = jnp.concatenate([ml_b_f.reshape(-1), ssd_dt_bias.reshape(-1), ml_b_i.reshape(-1),
                            jnp.zeros((LANES - 2 * n_f - n_dt,), F32)])
    row2 = jnp.concatenate([jnp.zeros((n_f,), F32), ssd_a_log.reshape(-1), jnp.zeros((LANES - n_f - n_dt,), F32)])
    zero = jnp.zeros((1, LANES), F32)
    return jnp.concatenate([row0[None], zero, row2[None], jnp.zeros((5, LANES), F32)], axis=0)


def _rope_tables(n_lat):
    nfreq = ATT_HEAD_DIM // 4
    pos = np.arange(n_lat)
    rows, cols = pos // GRID_W, pos % GRID_W
    inv = np.float32(ROPE_BASE) ** (-np.arange(nfreq, dtype=np.float32) / np.float32(nfreq))
    ang_r = rows.astype(np.float32)[:, None] * inv
    ang_c = cols.astype(np.float32)[:, None] * inv
    cos_h = np.concatenate([np.cos(ang_r)] * 2 + [np.cos(ang_c)] * 2, axis=1)
    sin_h = np.concatenate([-np.sin(ang_r), np.sin(ang_r), -np.sin(ang_c), np.sin(ang_c)], axis=1)
    return (jnp.asarray(np.concatenate([cos_h] * 2, axis=1), F32),
            jnp.asarray(np.concatenate([sin_h] * 2, axis=1), F32))


def _token_mixers(u_l, u_c, p, layer, tables, need_ctx_out):
    uml_l, ussd_l, uatt_l, g_l = u_l
    uml_c, ussd_c, uatt_c, g_c = u_c
    pvec = _gate_rows(p['ml_b_i'][layer], p['ml_b_f'][layer], p['ssd_dt_bias'][layer], p['ssd_a_log'][layer])
    m_l, m_c = _mlstm_mixer(uml_l, uml_c, g_l, g_c, pvec, p['ml_norm_g'][layer][None], need_ctx_out)
    s_l, s_c = _ssd_mixer(ussd_l, ussd_c, g_l, g_c, pvec, p['ssd_conv_w'][layer], p['ssd_conv_b'][layer][None],
                          jnp.repeat(p['ssd_d'][layer], SSD_HEAD_DIM)[None], p['ssd_norm_g'][layer][None],
                          need_ctx_out)
    a_l, a_c = _attn_mixer(uatt_l, uatt_c, tables['cos'], tables['sin'],
                           jnp.tile(p['att_qn_g'][layer], 2)[None], jnp.tile(p['att_kn_g'][layer], 2)[None],
                           p['att_sink'][layer], need_ctx_out)
    return (m_l, s_l, a_l), (m_c, s_c, a_c)


def kernel(x, c, ctx, c_ctx, w_mod, b_mod, norm1_g, w_in, ml_b_i, ml_b_f, ml_norm_g, ssd_conv_w, ssd_conv_b,
           ssd_a_log, ssd_dt_bias, ssd_d, ssd_norm_g, att_qn_g, att_kn_g, att_sink, w_out, norm2_g, w_up, w_down):
    p = dict(ml_b_i=ml_b_i, ml_b_f=ml_b_f, ml_norm_g=ml_norm_g, ssd_conv_w=ssd_conv_w, ssd_conv_b=ssd_conv_b,
             ssd_a_log=ssd_a_log, ssd_dt_bias=ssd_dt_bias, ssd_d=ssd_d, ssd_norm_g=ssd_norm_g,
             att_qn_g=att_qn_g, att_kn_g=att_kn_g, att_sink=att_sink)
    bsz, n_lat, d = x.shape
    n_ctx = ctx.shape[1]
    depth = w_mod.shape[0]
    mod_rows = 16
    cc = jnp.concatenate([c, c_ctx[None], jnp.zeros((mod_rows - bsz - 1, d), F32)], axis=0)
    mods = _modulation(cc, w_mod, b_mod).reshape(depth, mod_rows, 6, d)
    cos_t, sin_t = _rope_tables(n_lat)
    tables = dict(cos=cos_t, sin=sin_t)
    lat_row = lambda b: b
    ctx_row = lambda b: bsz
    h_ctx = ctx
    w = _permute_weights(w_in)
    wo, wup, wdn = w_out.astype(BF16), w_up.astype(BF16), w_down.astype(BF16)
    g1, g2 = norm1_g[:, None, :], norm2_g[:, None, :]
    for layer in range(depth):
        need_ctx_out = layer < depth - 1
        u_l = _in_proj(x, mods, lat_row, g1, w, layer, tm=512)
        u_c = _in_proj(h_ctx, mods, ctx_row, g1, w, layer, tm=256)
        y_l, y_c = _token_mixers(u_l, u_c, p, layer, tables, need_ctx_out)
        x = _post(x, *y_l, mods, lat_row, g2, wo, wup, wdn, layer, tm=512)
        if need_ctx_out:
            h_ctx = _post(h_ctx, *y_c, mods, ctx_row, g2, wo, wup, wdn, layer, tm=256)
    return x
```

```python
import functools
import math

import numpy as np
import jax
import jax.numpy as jnp
from jax import lax
from jax.experimental import pallas as pl
from jax.experimental.pallas import tpu as pltpu

F32 = jnp.float32
BF16 = jnp.bfloat16

EPS = 1e-6
GRID_W = 64
N_DIR = 2
ML_HEADS = 4
ML_HEAD_DIM = 128
ML_WIDTH = ML_HEADS * ML_HEAD_DIM
SSD_HEADS = 8
SSD_HEAD_DIM = 64
SSD_WIDTH = SSD_HEADS * SSD_HEAD_DIM
SSD_STATE = 64
SSD_GROUPS = 2
SSD_XBC = SSD_WIDTH + 2 * SSD_GROUPS * SSD_STATE
ATT_Q_HEADS = 8
ATT_KV_HEADS = 2
ATT_HEAD_DIM = 64
ATT_WIDTH = ATT_Q_HEADS * ATT_HEAD_DIM
ATT_KV_WIDTH = ATT_KV_HEADS * ATT_HEAD_DIM
WINDOW = 128
ROPE_BASE = 10000.0
MIX_WIDTH = ML_WIDTH + SSD_WIDTH + ATT_WIDTH

LANES = 128
CHUNK = 128
VMEM_LIMIT = 56 * 1024 * 1024
UNROLL = 16
UNROLL_LIGHT = 8
PREP_GROUP = 8

U_ML = 4 * ML_WIDTH
U_SSD = SSD_WIDTH + SSD_XBC
U_ATT = ATT_WIDTH + 2 * ATT_KV_WIDTH
U_GATE = LANES
U_TOTAL = U_ML + U_SSD + U_ATT + U_GATE
DT_LANE0 = N_DIR * ML_HEADS
LI_LANE0 = DT_LANE0 + N_DIR * SSD_HEADS

ML_SCALE = ML_HEAD_DIM ** -0.5
ATT_SCALE = ATT_HEAD_DIM ** -0.5

_NT = (((1,), (1,)), ((), ()))
_TN = (((0,), (0,)), ((), ()))


def _dot(a, b):
    return jnp.dot(a, b, preferred_element_type=F32)


def _dot_nt(a, b):
    return lax.dot_general(a, b, _NT, preferred_element_type=F32)


def _dot_tn(a, b):
    return lax.dot_general(a, b, _TN, preferred_element_type=F32)


def _split2(x):
    hi = x.astype(BF16)
    lo = (x - hi.astype(F32)).astype(BF16)
    return hi, lo


def _split3(x):
    hi = x.astype(BF16)
    r = x - hi.astype(F32)
    mid = r.astype(BF16)
    lo = (r - mid.astype(F32)).astype(BF16)
    return hi, mid, lo


def _row_cumsums(tris, x):
    rows = x.shape[0]
    parts = _split3(jnp.concatenate([x, jnp.zeros_like(x)], axis=0))
    return [(_dot(parts[0], tri) + _dot(parts[1], tri) + _dot(parts[2], tri))[0:rows] for tri in tris]


def _sigmoid(x):
    return 1.0 / (1.0 + jnp.exp2(x * (-1.0 / math.log(2.0))))


def _log_sigmoid(x):
    return jnp.minimum(x, 0.0) - jnp.log1p(jnp.exp(-jnp.abs(x)))


def _softplus(x):
    return jnp.maximum(x, 0.0) + jnp.log1p(jnp.exp(-jnp.abs(x)))


def _iota(shape, dim):
    return lax.broadcasted_iota(jnp.int32, shape, dim)


def _chunk_rows(c):
    return pl.ds(pl.multiple_of(c * CHUNK, CHUNK), CHUNK)


def _mod_kernel(s_ref, w_ref, b_ref, o_ref):
    s = s_ref[...]
    s = s * _sigmoid(s)
    w = w_ref[...]
    s_hi, s_lo = _split2(s)
    w_hi, w_lo = _split2(w)
    o_ref[...] = _dot(s_hi, w_hi) + (_dot(s_hi, w_lo) + _dot(s_lo, w_hi)) + b_ref[...]


def _modulation(cc, w_mod, b_mod):
    depth, d, d6 = w_mod.shape
    r = cc.shape[0]
    bn = 2048
    return pl.pallas_call(
        _mod_kernel,
        grid=(depth, d6 // bn),
        in_specs=[
            pl.BlockSpec((r, d), lambda l, j: (0, 0)),
            pl.BlockSpec((None, d, bn), lambda l, j: (l, 0, j)),
            pl.BlockSpec((None, 1, bn), lambda l, j: (l, 0, j)),
        ],
        out_specs=pl.BlockSpec((None, r, bn), lambda l, j: (l, 0, j)),
        out_shape=jax.ShapeDtypeStruct((depth, r, d6), F32),
        compiler_params=pltpu.CompilerParams(
            dimension_semantics=("arbitrary", "arbitrary"), vmem_limit_bytes=VMEM_LIMIT),
        name="modulation",
    )(cc, w_mod, b_mod.reshape(depth, 1, d6))


def _inproj_kernel(x_ref, mod_ref, g_ref, w_ref, oml_ref, ossd_ref, oatt_ref, og_ref):
    x = x_ref[...]
    ms = jnp.mean(x * x, axis=-1, keepdims=True)
    xn = x * lax.rsqrt(ms + EPS)
    h = xn * (g_ref[...] * (1.0 + mod_ref[1:2, :])) + mod_ref[0:1, :]
    hb = h.astype(BF16)
    step = 512
    off = 0
    for o_ref, width in ((oml_ref, U_ML), (ossd_ref, U_SSD), (oatt_ref, U_ATT), (og_ref, U_GATE)):
        for j in range(0, width, step):
            wj = min(step, width - j)
            o_ref[:, j:j + wj] = _dot(hb, w_ref[:, off + j:off + j + wj]).astype(o_ref.dtype)
        off += width


def _in_proj(x, mods, mod_row, g1, w, layer, tm):
    bsz, t, d = x.shape
    widths = (U_ML, U_SSD, U_ATT, U_GATE)
    dtypes = (BF16, BF16, BF16, F32)
    return pl.pallas_call(
        _inproj_kernel,
        grid=(bsz, t // tm),
        in_specs=[
            pl.BlockSpec((None, tm, d), lambda b, i: (b, i, 0)),
            pl.BlockSpec((None, None, 6, d), lambda b, i: (layer, mod_row(b), 0, 0)),
            pl.BlockSpec((None, 1, d), lambda b, i: (layer, 0, 0)),
            pl.BlockSpec((None, d, U_TOTAL), lambda b, i: (layer, 0, 0), pipeline_mode=pl.Buffered(1)),
        ],
        out_specs=[pl.BlockSpec((None, tm, wd), lambda b, i: (b, i, 0)) for wd in widths],
        out_shape=[jax.ShapeDtypeStruct((bsz, t, wd), dt) for wd, dt in zip(widths, dtypes)],
        compiler_params=pltpu.CompilerParams(
            dimension_semantics=("arbitrary", "arbitrary"), vmem_limit_bytes=VMEM_LIMIT),
        name="in_proj",
    )(x, mods, g1, w)


def _post_kernel(x_ref, yml_ref, yssd_ref, yatt_ref, mod_ref, g_ref, wo_ref, wup_ref, wdn_ref, o_ref):
    proj = (_dot(yml_ref[...], wo_ref[0:ML_WIDTH, :])
            + _dot(yssd_ref[...], wo_ref[ML_WIDTH:ML_WIDTH + SSD_WIDTH, :])
            + _dot(yatt_ref[...], wo_ref[ML_WIDTH + SSD_WIDTH:MIX_WIDTH, :]))
    x1 = x_ref[...] + mod_ref[2:3, :] * proj
    ms = jnp.mean(x1 * x1, axis=-1, keepdims=True)
    xn = x1 * lax.rsqrt(ms + EPS)
    hb = (xn * (g_ref[...] * (1.0 + mod_ref[4:5, :])) + mod_ref[3:4, :]).astype(BF16)
    d_ff = wup_ref.shape[1]
    step = 1024
    acc = None
    for j in range(0, d_ff, step):
        up = jnp.maximum(_dot(hb, wup_ref[:, j:j + step]), 0.0)
        part = _dot((up * up).astype(BF16), wdn_ref[j:j + step, :])
        acc = part if acc is None else acc + part
    o_ref[...] = x1 + mod_ref[5:6, :] * acc


def _post(x, yml, yssd, yatt, mods, mod_row, g2, wo, wup, wdn, layer, tm):
    bsz, t, d = x.shape
    d_ff = wup.shape[2]
    const = dict(pipeline_mode=pl.Buffered(1))
    return pl.pallas_call(
        _post_kernel,
        grid=(bsz, t // tm),
        in_specs=[
            pl.BlockSpec((None, tm, d), lambda b, i: (b, i, 0)),
            pl.BlockSpec((None, tm, ML_WIDTH), lambda b, i: (b, i, 0)),
            pl.BlockSpec((None, tm, SSD_WIDTH), lambda b, i: (b, i, 0)),
            pl.BlockSpec((None, tm, ATT_WIDTH), lambda b, i: (b, i, 0)),
            pl.BlockSpec((None, None, 6, d), lambda b, i: (layer, mod_row(b), 0, 0)),
            pl.BlockSpec((None, 1, d), lambda b, i: (layer, 0, 0)),
            pl.BlockSpec((None, MIX_WIDTH, d), lambda b, i: (layer, 0, 0), **const),
            pl.BlockSpec((None, d, d_ff), lambda b, i: (layer, 0, 0), **const),
            pl.BlockSpec((None, d_ff, d), lambda b, i: (layer, 0, 0), **const),
        ],
        out_specs=pl.BlockSpec((None, tm, d), lambda b, i: (b, i, 0)),
        out_shape=jax.ShapeDtypeStruct((bsz, t, d), F32),
        compiler_params=pltpu.CompilerParams(
            dimension_semantics=("arbitrary", "arbitrary"), vmem_limit_bytes=VMEM_LIMIT),
        name="post",
    )(x, yml, yssd, yatt, mods, g2, wo, wup, wdn)


ML_GATES = N_DIR * ML_HEADS


def _mlstm_kernel(ul_ref, uc_ref, gl_ref, gc_ref, pv_ref, ng_ref, ex_ref, *rest,
                  need_ctx_out, n_lat, n_ctx):
    if need_ctx_out:
        yl_ref, yc_ref, arow, packed, ends, mprev, cbk, crun = rest
    else:
        yl_ref, arow, packed, ends, mprev, cbk, crun = rest
        yc_ref = None
    ncl, ncc = n_lat // CHUNK, n_ctx // CHUNK
    nct = ncl + ncc
    nh = ML_HEADS
    ti = _iota((CHUNK, CHUNK), 0)
    si = _iota((CHUNK, CHUNK), 1)
    low = si <= ti
    upp = si >= ti
    tri_low = jnp.where(low, 1.0, 0.0).astype(BF16)
    tri_upp = jnp.where(upp, 1.0, 0.0).astype(BF16)
    fwd_row = _iota((ML_GATES, CHUNK), 0) < nh
    bias = pv_ref[0:1, :]
    neg_inf = -jnp.inf
    ones_blk = jnp.ones((CHUNK, LANES), BF16)

    def prep_group(g_ref, c0, n, goff):
        def per_chunk(x):
            return [x[:, i * CHUNK:(i + 1) * CHUNK] for i in range(n)]

        def wide(xs):
            return xs[0] if n == 1 else jnp.concatenate(xs, axis=1)

        gts = [(g_ref[(c0 + i) * CHUNK:(c0 + i + 1) * CHUNK, :] + bias).T for i in range(n)]
        lf = _log_sigmoid(wide([gt[0:ML_GATES] for gt in gts]))
        li = wide([gt[LI_LANE0:LI_LANE0 + ML_GATES] for gt in gts])
        sums = [_row_cumsums((tri_upp, tri_low), x) for x in per_chunk(lf)]
        fwd = _iota((ML_GATES, n * CHUNK), 0) < nh
        tok = _iota((ML_GATES, n * CHUNK), 1) & (CHUNK - 1)
        b = jnp.where(fwd, wide([s[0] for s in sums]), wide([s[1] for s in sums]))
        a = li - b
        cm = a
        k = 1
        while k < CHUNK:
            before = jnp.where(tok >= k, pltpu.roll(cm, k, 1), neg_inf)
            after = jnp.where(tok < CHUNK - k, pltpu.roll(cm, n * CHUNK - k, 1), neg_inf)
            cm = jnp.maximum(cm, jnp.where(fwd, before, after))
            k *= 2
        parts = [p.astype(F32) for p in _split3(cm) + _split3(b)]
        parts.append(jnp.zeros((CHUNK - len(parts) * ML_GATES, n * CHUNK), F32))
        stacked = jnp.concatenate(parts, axis=0)
        for i in range(n):
            cg = goff + c0 + i
            cols = slice(i * CHUNK, (i + 1) * CHUNK)
            arow[cg] = a[:, cols]
            packed[cg] = stacked[:, cols].T.astype(BF16)
            for plane, x in enumerate((cm, b)):
                last = x[:, (i + 1) * CHUNK - 1:(i + 1) * CHUNK]
                first = x[:, i * CHUNK:i * CHUNK + 1]
                ends[cg, plane] = jnp.broadcast_to(jnp.where(fwd_row[:, 0:1], last, first), (ML_GATES, CHUNK))

    def prep(g_ref, nch, goff):
        for c0 in range(0, nch, PREP_GROUP):
            prep_group(g_ref, c0, min(PREP_GROUP, nch - c0), goff)

    prep(gc_ref, ncc, 0)
    prep(gl_ref, ncl, ncc)

    def m_scan(lo, hi, reverse, m0):
        def body(i, m):
            cg = hi - 1 - i if reverse else lo + i
            mprev[cg, int(reverse)] = m
            return ends[cg, 1] + jnp.maximum(m, ends[cg, 0])
        return lax.fori_loop(0, hi - lo, body, m0)

    m_zero = jnp.zeros((ML_GATES, CHUNK), F32)
    m_scan(0, nct, False, m_zero)
    m_scan(ncc, nct, True, m_scan(0, ncc, True, m_zero))

    def load_kv(u_ref, rows, h):
        k = u_ref[rows, ML_WIDTH + h * LANES:ML_WIDTH + (h + 1) * LANES]
        v = u_ref[rows, 2 * ML_WIDTH + h * LANES:2 * ML_WIDTH + (h + 1) * LANES]
        return k, jnp.concatenate([v, ones_blk], axis=1)

    def update(d, h, cg, k, v1):
        j = d * nh + h
        m_prev = mprev[cg, d, j:j + 1, 0:1]
        m_end = jnp.maximum(m_prev, ends[cg, 0, j:j + 1, 0:1])
        w_row = jnp.exp(arow[cg, j:j + 1, :] - m_end) * ML_SCALE
        ktw = (k.T.astype(F32) * w_row).astype(BF16)
        crun[d, h] = jnp.exp(m_prev - m_end) * crun[d, h] + _dot(ktw, v1)

    def bwd_pass(u_ref, nch, goff, keep):
        def body(i, carry):
            c = nch - 1 - i
            rows = _chunk_rows(c)
            for h in range(nh):
                if keep:
                    cbk[goff + c, h] = crun[1, h].astype(BF16)
                k, v1 = load_kv(u_ref, rows, h)
                update(1, h, goff + c, k, v1)
            return carry
        lax.fori_loop(0, nch, body, 0, unroll=UNROLL_LIGHT)

    def fwd_pass(u_ref, y_ref, nch, goff):
        def body(c, carry):
            rows = _chunk_rows(c)
            cg = goff + c
            for h in range(nh):
                k, v1 = load_kv(u_ref, rows, h)
                if y_ref is not None:
                    q = u_ref[rows, h * LANES:(h + 1) * LANES]
                    o = u_ref[rows, 3 * ML_WIDTH + h * LANES:3 * ML_WIDTH + (h + 1) * LANES]
                    sqk = _dot_nt(q, k) * ML_SCALE
                    qc = _dot(q, jnp.concatenate([crun[0, h].astype(BF16), cbk[cg, h]], axis=1))
                    s_dirs, stats = [], []
                    for d, mask in ((0, low), (1, upp)):
                        j = d * nh + h
                        rep = _dot(packed[cg], ex_ref[:, 2 * j * LANES:2 * (j + 1) * LANES])
                        m_prev = mprev[cg, d, j:j + 1, 0:1]
                        m_t = jnp.maximum(rep[:, 0:LANES], m_prev)
                        s_dirs.append(sqk * jnp.exp(jnp.where(mask, arow[cg, j:j + 1, :] - m_t, neg_inf)))
                        stats.append((m_prev, m_t, rep[:, LANES:2 * LANES]))
                    sv = _dot(jnp.concatenate(s_dirs, axis=0).astype(BF16), v1)
                    hsum = None
                    for d, (m_prev, m_t, b_t) in enumerate(stats):
                        g = jnp.exp(m_prev - m_t)
                        sv_d = sv[d * CHUNK:(d + 1) * CHUNK, :]
                        qc_d = qc[:, 2 * d * LANES:2 * (d + 1) * LANES]
                        num = sv_d[:, 0:LANES] + g * qc_d[:, 0:LANES]
                        den = sv_d[:, LANES:2 * LANES] + g * qc_d[:, LANES:2 * LANES]
                        hd = num / jnp.maximum(jnp.abs(den), jnp.exp(-(b_t + m_t)))
                        hsum = hd if hsum is None else hsum + hd
                    ms = jnp.mean(hsum * hsum, axis=1, keepdims=True)
                    y = hsum * lax.rsqrt(ms + EPS) * ng_ref[:, h * LANES:(h + 1) * LANES]
                    y = y * _sigmoid(o.astype(F32))
                    y_ref[rows, h * LANES:(h + 1) * LANES] = y.astype(y_ref.dtype)
                update(0, h, cg, k, v1)
            return carry
        lax.fori_loop(0, nch, body, 0, unroll=UNROLL)

    crun[...] = jnp.zeros(crun.shape, F32)
    bwd_pass(uc_ref, ncc, 0, need_ctx_out)
    bwd_pass(ul_ref, ncl, ncc, True)
    fwd_pass(uc_ref, yc_ref, ncc, 0)
    fwd_pass(ul_ref, yl_ref, ncl, ncc)


def _ml_expand_table():
    expand = np.zeros((LANES, ML_GATES * 2 * LANES), np.float32)
    for j in range(ML_GATES):
        for i in range(6):
            half = i // 3
            expand[ML_GATES * i + j, (2 * j + half) * LANES:(2 * j + half + 1) * LANES] = 1.0
    return jnp.asarray(expand, BF16)


def _mlstm_mixer(uml_l, uml_c, g_l, g_c, pvec, norm_g, need_ctx_out):
    expand = _ml_expand_table()
    bsz, n_lat, _ = uml_l.shape
    n_ctx = uml_c.shape[1]
    nct = (n_lat + n_ctx) // CHUNK
    kern = functools.partial(_mlstm_kernel, need_ctx_out=need_ctx_out, n_lat=n_lat, n_ctx=n_ctx)
    out_specs = [pl.BlockSpec((None, n_lat, ML_WIDTH), lambda b: (b, 0, 0))]
    out_shape = [jax.ShapeDtypeStruct((bsz, n_lat, ML_WIDTH), BF16)]
    if need_ctx_out:
        out_specs.append(pl.BlockSpec((None, n_ctx, ML_WIDTH), lambda b: (b, 0, 0)))
        out_shape.append(jax.ShapeDtypeStruct((bsz, n_ctx, ML_WIDTH), BF16))
    outs = pl.pallas_call(
        kern,
        grid=(bsz,),
        in_specs=[
            pl.BlockSpec((None, n_lat, U_ML), lambda b: (b, 0, 0)),
            pl.BlockSpec((None, n_ctx, U_ML), lambda b: (b, 0, 0)),
            pl.BlockSpec((None, n_lat, U_GATE), lambda b: (b, 0, 0)),
            pl.BlockSpec((None, n_ctx, U_GATE), lambda b: (b, 0, 0)),
            pl.BlockSpec((8, LANES), lambda b: (0, 0)),
            pl.BlockSpec((1, ML_WIDTH), lambda b: (0, 0)),
            pl.BlockSpec(expand.shape, lambda b: (0, 0)),
        ],
        out_specs=out_specs,
        out_shape=out_shape,
        scratch_shapes=[
            pltpu.VMEM((nct, ML_GATES, CHUNK), F32),
            pltpu.VMEM((nct, CHUNK, LANES), BF16),
            pltpu.VMEM((nct, 2, ML_GATES, CHUNK), F32),
            pltpu.VMEM((nct, N_DIR, ML_GATES, CHUNK), F32),
            pltpu.VMEM((nct, ML_HEADS, ML_HEAD_DIM, 2 * LANES), BF16),
            pltpu.VMEM((N_DIR, ML_HEADS, ML_HEAD_DIM, 2 * LANES), F32),
        ],
        compiler_params=pltpu.CompilerParams(
            dimension_semantics=("arbitrary",), vmem_limit_bytes=VMEM_LIMIT),
        name="mlstm",
    )(uml_l, uml_c, g_l, g_c, pvec, norm_g, expand)
    return (outs[0], outs[1]) if need_ctx_out else (outs[0], None)


def _ssd_kernel(ul_ref, uc_ref, gl_ref, gc_ref, pv_ref, cw_ref, cb_ref, dsk_ref, ng_ref, ex_ref, *rest,
                need_ctx_out, n_lat, n_ctx):
    if need_ctx_out:
        yl_ref, yc_ref, xact, acum, packed, act, dtt, hbk, hrun = rest
    else:
        yl_ref, xact, acum, packed, act, dtt, hbk, hrun = rest
        yc_ref = None
    ncl, ncc = n_lat // CHUNK, n_ctx // CHUNK
    nh, hg = SSD_HEADS, SSD_HEADS // SSD_GROUPS
    gw = hg * SSD_HEAD_DIM
    gn = SSD_GROUPS * SSD_STATE
    ti = _iota((CHUNK, CHUNK), 0)
    si = _iota((CHUNK, CHUNK), 1)
    low_strict = si < ti
    upp_strict = si > ti
    tri_low = jnp.where(si <= ti, 1.0, 0.0).astype(BF16)
    tri_upp = jnp.where(si >= ti, 1.0, 0.0).astype(BF16)
    lane = _iota((CHUNK, LANES), 1)
    nj = N_DIR * nh
    ac_lane0 = 6 * nj
    fwd_row = _iota((nj, CHUNK), 0) < nh
    bias0 = pv_ref[0:1, :]
    a_rows = jnp.broadcast_to(-jnp.exp(pv_ref[2:3, :]), (LANES, LANES)).T[DT_LANE0:DT_LANE0 + nj]
    shift_down = jnp.where(si == ti - 1, 1.0, 0.0).astype(BF16)
    shift_up = jnp.where(si == ti + 1, 1.0, 0.0).astype(BF16)
    lane_group0 = lane < SSD_STATE

    def prep(u_ref, g_ref, nch, goff, nrows):
        def body(c, carry):
            r = pl.multiple_of(c * CHUNK, CHUNK)
            cur = u_ref[pl.ds(r, CHUNK), SSD_WIDTH:U_SSD]
            rp = pl.multiple_of(jnp.maximum(r - CHUNK, 0), CHUNK)
            rn = pl.multiple_of(jnp.minimum(r + CHUNK, nrows - CHUNK), CHUNK)
            prev = u_ref[pl.ds(rp, CHUNK), SSD_WIDTH:U_SSD]
            nxt = u_ref[pl.ds(rn, CHUNK), SSD_WIDTH:U_SSD]
            from_prev = jnp.where((ti == 0) & (si == CHUNK - 1) & (c > 0), 1.0, 0.0).astype(BF16)
            from_next = jnp.where((ti == CHUNK - 1) & (si == 0) & (c < nch - 1), 1.0, 0.0).astype(BF16)
            xm = _dot(jnp.concatenate([shift_down, from_prev], axis=1), jnp.concatenate([cur, prev], axis=0))
            xp = _dot(jnp.concatenate([shift_up, from_next], axis=1), jnp.concatenate([cur, nxt], axis=0))
            conv = (cw_ref[0:1, :] * xm + cw_ref[1:2, :] * cur.astype(F32) + cw_ref[2:3, :] * xp
                    + cb_ref[...])
            xact[goff + c] = (conv * _sigmoid(conv)).astype(BF16)

            gt = (g_ref[pl.ds(r, CHUNK), :] + bias0).T
            dt = _softplus(gt[DT_LANE0:DT_LANE0 + nj])
            prefix, suffix = _row_cumsums((tri_upp, tri_low), dt * a_rows)
            ac = jnp.where(fwd_row, prefix, suffix)
            a_end = jnp.where(fwd_row[:, 0:1], ac[:, CHUNK - 1:CHUNK], ac[:, 0:1])
            act[goff + c] = ac - jnp.log(dt)
            dtt[goff + c] = dt
            parts = [p.astype(F32) for p in _split3(jnp.exp(ac)) + _split3(jnp.exp(a_end - ac) * dt)]
            parts += [ac, jnp.zeros((CHUNK - 7 * nj, CHUNK), F32)]
            tile = jnp.concatenate(parts, axis=0).T
            acum[goff + c] = tile
            packed[goff + c] = tile.astype(BF16)
            return carry
        lax.fori_loop(0, nch, body, 0, unroll=UNROLL_LIGHT)

    prep(uc_ref, gc_ref, ncc, 0, n_ctx)
    prep(ul_ref, gl_ref, ncl, ncc, n_lat)

    def e_cols(d, g):
        return slice(d * SSD_WIDTH + g * gw, d * SSD_WIDTH + (g + 1) * gw)

    def w_cols(d, g):
        return slice((N_DIR + d) * SSD_WIDTH + g * gw, (N_DIR + d) * SSD_WIDTH + (g + 1) * gw)

    def update_state(d, cg):
        pk = packed[cg]
        xa = xact[cg]
        b_all = xa[:, SSD_WIDTH:SSD_WIDTH + gn]
        zero = jnp.zeros_like(b_all)
        upd = []
        for g in range(SSD_GROUPS):
            wx = (_dot(pk, ex_ref[:, w_cols(d, g)]) * xa[:, g * gw:(g + 1) * gw].astype(F32)).astype(BF16)
            upd.append(_dot_tn(jnp.where(lane_group0 == (g == 0), b_all, zero), wx))
        tail, last = (slice(CHUNK - 16, CHUNK), 15) if d == 0 else (slice(0, 16), 0)
        decay = _dot(pk[tail], ex_ref[:, d * SSD_WIDTH:(d + 1) * SSD_WIDTH])[last:last + 1, :]
        hrun[d] = decay * hrun[d] + jnp.concatenate(upd, axis=1)

    def bwd_pass(nch, goff, keep):
        def body(i, carry):
            cg = goff + nch - 1 - i
            if keep:
                hbk[cg] = hrun[1].astype(BF16)
            update_state(1, cg)
            return carry
        lax.fori_loop(0, nch, body, 0, unroll=UNROLL_LIGHT)

    def fwd_pass(u_ref, y_ref, nch, goff):
        def body(c, carry):
            cg = goff + c
            if y_ref is not None:
                rows = _chunk_rows(c)
                xa = xact[cg]
                x = xa[:, 0:SSD_WIDTH]
                b_all = xa[:, SSD_WIDTH:SSD_WIDTH + gn]
                c_all = xa[:, SSD_WIDTH + gn:SSD_XBC]
                ac = acum[cg]
                ac_t = act[cg]
                dt_t = dtt[cg]
                lane_w = _iota((CHUNK, gw), 1)
                pk = packed[cg]
                yz_groups, ssq = [], None
                for g in range(SSD_GROUPS):
                    cols = slice(g * gw, (g + 1) * gw)
                    c_g = jnp.where(lane_group0 == (g == 0), c_all, jnp.zeros_like(c_all))
                    cb = _dot_nt(c_g, b_all)
                    s_heads = []
                    for hh in range(hg):
                        jf = g * hg + hh
                        jb = jf + nh
                        on_diag = jnp.log(dt_t[jf:jf + 1, :] + dt_t[jb:jb + 1, :])
                        ac_f = ac[:, ac_lane0 + jf:ac_lane0 + jf + 1]
                        ac_b = ac[:, ac_lane0 + jb:ac_lane0 + jb + 1]
                        expo = jnp.where(low_strict, ac_f - ac_t[jf:jf + 1, :],
                                         jnp.where(upp_strict, ac_b - ac_t[jb:jb + 1, :], on_diag))
                        s_heads.append((cb * jnp.exp(expo)).astype(BF16))
                    x_g = x[:, cols]
                    bd = jnp.concatenate(
                        [jnp.where((lane_w >= hh * SSD_HEAD_DIM) & (lane_w < (hh + 1) * SSD_HEAD_DIM),
                                   x_g, jnp.zeros_like(x_g)) for hh in range(hg)], axis=0)
                    y = _dot(jnp.concatenate(s_heads, axis=1), bd)
                    y = y + _dot(c_all, hrun[0, :, cols].astype(BF16)) * _dot(pk, ex_ref[:, e_cols(0, g)])
                    y = y + _dot(c_all, hbk[cg, :, cols]) * _dot(pk, ex_ref[:, e_cols(1, g)])
                    y = y + dsk_ref[:, cols] * x_g.astype(F32)
                    z = u_ref[rows, cols].astype(F32)
                    yz = y * (z * _sigmoid(z))
                    part = jnp.sum(yz * yz, axis=1, keepdims=True)
                    ssq = part if ssq is None else ssq + part
                    yz_groups.append(yz)
                scale = lax.rsqrt(ssq * (1.0 / SSD_WIDTH) + EPS)
                for g, yz in enumerate(yz_groups):
                    cols = slice(g * gw, (g + 1) * gw)
                    y_ref[rows, cols] = (yz * scale * ng_ref[:, cols]).astype(y_ref.dtype)
            update_state(0, cg)
            return carry
        lax.fori_loop(0, nch, body, 0, unroll=UNROLL)

    hrun[...] = jnp.zeros(hrun.shape, F32)
    bwd_pass(ncc, 0, need_ctx_out)
    bwd_pass(ncl, ncc, True)
    fwd_pass(uc_ref, yc_ref, ncc, 0)
    fwd_pass(ul_ref, yl_ref, ncl, ncc)


def _ssd_expand_table():
    nj = N_DIR * SSD_HEADS
    expand = np.zeros((LANES, 2 * N_DIR * SSD_WIDTH), np.float32)
    for i in range(6):
        for d in range(N_DIR):
            for h in range(SSD_HEADS):
                j = d * SSD_HEADS + h
                c0 = ((i // 3) * N_DIR + d) * SSD_WIDTH + h * SSD_HEAD_DIM
                expand[nj * i + j, c0:c0 + SSD_HEAD_DIM] = 1.0
    return jnp.asarray(expand, BF16)


def _ssd_mixer(ussd_l, ussd_c, g_l, g_c, pvec, conv_w, conv_b, d_skip, norm_g, need_ctx_out):
    expand = _ssd_expand_table()
    bsz, n_lat, _ = ussd_l.shape
    n_ctx = ussd_c.shape[1]
    nct = (n_lat + n_ctx) // CHUNK
    gn = SSD_GROUPS * SSD_STATE
    kern = functools.partial(_ssd_kernel, need_ctx_out=need_ctx_out, n_lat=n_lat, n_ctx=n_ctx)
    out_specs = [pl.BlockSpec((None, n_lat, SSD_WIDTH), lambda b: (b, 0, 0))]
    out_shape = [jax.ShapeDtypeStruct((bsz, n_lat, SSD_WIDTH), BF16)]
    if need_ctx_out:
        out_specs.append(pl.BlockSpec((None, n_ctx, SSD_WIDTH), lambda b: (b, 0, 0)))
        out_shape.append(jax.ShapeDtypeStruct((bsz, n_ctx, SSD_WIDTH), BF16))
    outs = pl.pallas_call(
        kern,
        grid=(bsz,),
        in_specs=[
            pl.BlockSpec((None, n_lat, U_SSD), lambda b: (b, 0, 0)),
            pl.BlockSpec((None, n_ctx, U_SSD), lambda b: (b, 0, 0)),
            pl.BlockSpec((None, n_lat, LANES), lambda b: (b, 0, 0)),
            pl.BlockSpec((None, n_ctx, LANES), lambda b: (b, 0, 0)),
            pl.BlockSpec((8, LANES), lambda b: (0, 0)),
            pl.BlockSpec((3, SSD_XBC), lambda b: (0, 0)),
            pl.BlockSpec((1, SSD_XBC), lambda b: (0, 0)),
            pl.BlockSpec((1, SSD_WIDTH), lambda b: (0, 0)),
            pl.BlockSpec((1, SSD_WIDTH), lambda b: (0, 0)),
            pl.BlockSpec(expand.shape, lambda b: (0, 0)),
        ],
        out_specs=out_specs,
        out_shape=out_shape,
        scratch_shapes=[
            pltpu.VMEM((nct, CHUNK, SSD_XBC), BF16),
            pltpu.VMEM((nct, CHUNK, LANES), F32),
            pltpu.VMEM((nct, CHUNK, LANES), BF16),
            pltpu.VMEM((nct, N_DIR * SSD_HEADS, CHUNK), F32),
            pltpu.VMEM((nct, N_DIR * SSD_HEADS, CHUNK), F32),
            pltpu.VMEM((nct, gn, SSD_WIDTH), BF16),
            pltpu.VMEM((N_DIR, gn, SSD_WIDTH), F32),
        ],
        compiler_params=pltpu.CompilerParams(
            dimension_semantics=("arbitrary",), vmem_limit_bytes=VMEM_LIMIT),
        name="ssd",
    )(ussd_l, ussd_c, g_l, g_c, pvec, conv_w, conv_b, d_skip, norm_g, expand)
    return (outs[0], outs[1]) if need_ctx_out else (outs[0], None)


def _head_norm(x, gain):
    rows = x.shape[0]
    blks = [x[:, j * LANES:(j + 1) * LANES] for j in range(x.shape[1] // LANES)]
    same_head = (_iota((LANES, LANES), 0) < ATT_HEAD_DIM) == (_iota((LANES, LANES), 1) < ATT_HEAD_DIM)
    sq = jnp.concatenate([b * b for b in blks], axis=0).astype(BF16)
    ms = _dot(sq, jnp.where(same_head, 1.0, 0.0).astype(BF16)) * (1.0 / ATT_HEAD_DIM)
    outs = [b * lax.rsqrt(ms[j * rows:(j + 1) * rows] + EPS) * gain for j, b in enumerate(blks)]
    return outs[0] if len(outs) == 1 else jnp.concatenate(outs, axis=1)


def _rope(x, cos_t, sin_t):
    w = x.shape[1]
    reps = w // LANES
    nfreq = ATT_HEAD_DIM // 4
    partner = jnp.where((_iota(x.shape, 1) & (2 * nfreq - 1)) < nfreq,
                        pltpu.roll(x, w - nfreq, 1), pltpu.roll(x, nfreq, 1))
    if reps > 1:
        cos_t = jnp.concatenate([cos_t] * reps, axis=1)
        sin_t = jnp.concatenate([sin_t] * reps, axis=1)
    return x * cos_t + partner * sin_t


def _dup_heads(x):
    lane_lo = _iota(x.shape, 1) < ATT_HEAD_DIM
    sw = pltpu.roll(x, ATT_HEAD_DIM, 1)
    return jnp.concatenate([jnp.where(lane_lo, x, sw), jnp.where(lane_lo, sw, x)], axis=1)


def _attn_kernel(ul_ref, uc_ref, cos_ref, sin_ref, qg_ref, kg_ref, sink_ref, *rest,
                 need_ctx_out, n_lat, n_ctx):
    if need_ctx_out:
        yl_ref, yc_ref, qp, kb, vb, kc, vc, qcp = rest
    else:
        yl_ref, qp, kb, vb, kc, vc = rest
        yc_ref = qcp = None
    ncl, ncc = n_lat // CHUNK, n_ctx // CHUNK
    rep = ATT_Q_HEADS // ATT_KV_HEADS
    kv0 = ATT_WIDTH
    lane_lo = _iota((CHUNK, LANES), 1) < ATT_HEAD_DIM
    neg_inf = -jnp.inf
    qg = qg_ref[...]
    kg = kg_ref[...]

    for ref in (kb, vb):
        zeros_blk = jnp.zeros((CHUNK, ref.shape[1]), BF16)
        ref[0:CHUNK, :] = zeros_blk
        ref[CHUNK + n_lat:2 * CHUNK + n_lat, :] = zeros_blk

    def values_with_ones(v):
        d = _dup_heads(v).astype(BF16)
        ones = jnp.ones((v.shape[0], LANES), BF16)
        return jnp.concatenate([d[:, 0:LANES], ones, d[:, LANES:2 * LANES], ones], axis=1)

    def prep_lat(c, carry):
        rows = _chunk_rows(c)
        cos_t = cos_ref[rows, :]
        sin_t = sin_ref[rows, :]
        q = _rope(_head_norm(ul_ref[rows, 0:ATT_WIDTH].astype(F32), qg), cos_t, sin_t)
        qp[rows, :] = (q * ATT_SCALE).astype(BF16)
        k = _rope(_head_norm(ul_ref[rows, kv0:kv0 + LANES].astype(F32), kg), cos_t, sin_t)
        v = ul_ref[rows, kv0 + LANES:kv0 + 2 * LANES].astype(F32)
        dst = pl.ds(pl.multiple_of((c + 1) * CHUNK, CHUNK), CHUNK)
        kb[dst, :] = _dup_heads(k).astype(BF16)
        vb[dst, :] = values_with_ones(v)
        return carry
    lax.fori_loop(0, ncl, prep_lat, 0, unroll=UNROLL_LIGHT)

    def prep_ctx(c, carry):
        rows = _chunk_rows(c)
        k = _head_norm(uc_ref[rows, kv0:kv0 + LANES].astype(F32), kg)
        v = uc_ref[rows, kv0 + LANES:kv0 + 2 * LANES].astype(F32)
        kc[rows, :] = _dup_heads(k).astype(BF16)
        vc[rows, :] = values_with_ones(v)
        if need_ctx_out:
            q = _head_norm(uc_ref[rows, 0:ATT_WIDTH].astype(F32), qg)
            qcp[rows, :] = (q * ATT_SCALE).astype(BF16)
        return carry
    lax.fori_loop(0, ncc, prep_ctx, 0, unroll=UNROLL)

    def attend(q_ref, y_ref, rows, band):
        for g in range(ATT_KV_HEADS):
            kl = slice(g * LANES, (g + 1) * LANES)
            vl = slice(2 * g * LANES, 2 * (g + 1) * LANES)
            qms = []
            for r in range(rep):
                hq = g * rep + r
                qb = q_ref[rows, (hq // 2) * LANES:(hq // 2 + 1) * LANES]
                qms.append(jnp.where(lane_lo != (hq % 2 == 1), qb, jnp.zeros_like(qb)))
            qs = jnp.concatenate(qms, axis=0)
            s_c = _dot_nt(qs, kc[:, kl])
            if band is not None:
                band_rows, valid = band
                s_b = _dot_nt(qs, kb[band_rows, kl])
            ps, sink_terms = [], []
            for r in range(rep):
                sl = slice(r * CHUNK, (r + 1) * CHUNK)
                sk = sink_ref[g * rep + r]
                s = s_c[sl]
                if band is not None:
                    s = jnp.concatenate([s, jnp.where(valid, s_b[sl], neg_inf)], axis=1)
                m = jnp.maximum(jnp.max(s, axis=1, keepdims=True), sk)
                ps.append(jnp.exp(s - m).astype(BF16))
                sink_terms.append(jnp.exp(sk - m))
            p = jnp.concatenate(ps, axis=0)
            acc = _dot(p[:, 0:n_ctx], vc[:, vl])
            if band is not None:
                acc = acc + _dot(p[:, n_ctx:], vb[band_rows, vl])
            for r in range(rep):
                sl = slice(r * CHUNK, (r + 1) * CHUNK)
                o = acc[sl, 0:LANES] / (acc[sl, LANES:2 * LANES] + sink_terms[r])
                if r % 2 == 1:
                    jb = (g * rep + r) // 2
                    y_ref[rows, jb * LANES:(jb + 1) * LANES] = jnp.where(lane_lo, o_even, o).astype(y_ref.dtype)
                o_even = o

    qq = _iota((CHUNK, 3 * CHUNK), 0)
    kk = _iota((CHUNK, 3 * CHUNK), 1)
    in_window = (kk >= qq) & (kk <= qq + 2 * WINDOW)

    def lat_block(i, carry):
        kpos = kk + (i - 1) * CHUNK
        valid = in_window & (kpos >= 0) & (kpos < n_lat)
        band_rows = pl.ds(pl.multiple_of(i * CHUNK, CHUNK), 3 * CHUNK)
        attend(qp, yl_ref, _chunk_rows(i), (band_rows, valid))
        return carry
    lax.fori_loop(0, ncl, lat_block, 0, unroll=UNROLL)

    if need_ctx_out:
        def ctx_block(i, carry):
            attend(qcp, yc_ref, _chunk_rows(i), None)
            return carry
        lax.fori_loop(0, ncc, ctx_block, 0, unroll=UNROLL)


def _attn_mixer(uatt_l, uatt_c, cos_t, sin_t, qgain, kgain, sink, need_ctx_out):
    bsz, n_lat, _ = uatt_l.shape
    n_ctx = uatt_c.shape[1]
    kern = functools.partial(_attn_kernel, need_ctx_out=need_ctx_out, n_lat=n_lat, n_ctx=n_ctx)
    out_specs = [pl.BlockSpec((None, n_lat, ATT_WIDTH), lambda b: (b, 0, 0))]
    out_shape = [jax.ShapeDtypeStruct((bsz, n_lat, ATT_WIDTH), BF16)]
    scratch = [
        pltpu.VMEM((n_lat, ATT_WIDTH), BF16),
        pltpu.VMEM((n_lat + 2 * CHUNK, 2 * LANES), BF16),
        pltpu.VMEM((n_lat + 2 * CHUNK, 4 * LANES), BF16),
        pltpu.VMEM((n_ctx, 2 * LANES), BF16),
        pltpu.VMEM((n_ctx, 4 * LANES), BF16),
    ]
    if need_ctx_out:
        out_specs.append(pl.BlockSpec((None, n_ctx, ATT_WIDTH), lambda b: (b, 0, 0)))
        out_shape.append(jax.ShapeDtypeStruct((bsz, n_ctx, ATT_WIDTH), BF16))
        scratch.append(pltpu.VMEM((n_ctx, ATT_WIDTH), BF16))
    outs = pl.pallas_call(
        kern,
        grid=(bsz,),
        in_specs=[
            pl.BlockSpec((None, n_lat, U_ATT), lambda b: (b, 0, 0)),
            pl.BlockSpec((None, n_ctx, U_ATT), lambda b: (b, 0, 0)),
            pl.BlockSpec((n_lat, LANES), lambda b: (0, 0)),
            pl.BlockSpec((n_lat, LANES), lambda b: (0, 0)),
            pl.BlockSpec((1, LANES), lambda b: (0, 0)),
            pl.BlockSpec((1, LANES), lambda b: (0, 0)),
            pl.BlockSpec(memory_space=pltpu.SMEM),
        ],
        out_specs=out_specs,
        out_shape=out_shape,
        scratch_shapes=scratch,
        compiler_params=pltpu.CompilerParams(
            dimension_semantics=("arbitrary",), vmem_limit_bytes=VMEM_LIMIT),
        name="attention",
    )(uatt_l, uatt_c, cos_t, sin_t, qgain, kgain, sink)
    return (outs[0], outs[1]) if need_ctx_out else (outs[0], None)


def _in_layout():
    names = (('ml_q', ML_WIDTH), ('ml_k', ML_WIDTH), ('ml_v', ML_WIDTH), ('ml_o', ML_WIDTH),
             ('ml_i', N_DIR * ML_HEADS), ('ml_f', N_DIR * ML_HEADS),
             ('ssd_z', SSD_WIDTH), ('ssd_xbc', SSD_XBC), ('ssd_dt', N_DIR * SSD_HEADS),
             ('att_q', ATT_WIDTH), ('att_k', ATT_KV_WIDTH), ('att_v', ATT_KV_WIDTH))
    out, off = {}, 0
    for name, width in names:
        out[name] = (off, off + width)
        off += width
    return out


def _col_map():
    lay = _in_layout()
    take = lambda name: np.arange(lay[name][0], lay[name][1])
    zeros = lambda n: np.full((n,), -1)
    n_f, n_dt, n_i = N_DIR * ML_HEADS, N_DIR * SSD_HEADS, N_DIR * ML_HEADS
    return np.concatenate(
        [take('ml_q'), take('ml_k'), take('ml_v'), take('ml_o'),
         take('ssd_z'), take('ssd_xbc'),
         take('att_q'), take('att_k'), take('att_v'),
         take('ml_f'), take('ssd_dt'), take('ml_i'), zeros(LANES - n_f - n_dt - n_i)])


def _permute_plan():
    src = _col_map()
    plan = []
    for t in range(U_TOTAL // LANES):
        cols = src[t * LANES:(t + 1) * LANES]
        runs, i = [], 0
        while i < LANES:
            j = i
            while j + 1 < LANES and (cols[j + 1] == cols[j] + 1 if cols[j] >= 0 else cols[j + 1] < 0):
                j += 1
            runs.append((int(cols[i]), j - i + 1))
            i = j + 1
        plan.append(tuple(runs))
    return tuple(plan)


def _wprep_kernel(wt_ref, o_ref, *, plan):
    kb = wt_ref.shape[1]
    for t, runs in enumerate(plan):
        pieces = [wt_ref[s:s + n, :] if s >= 0 else jnp.zeros((n, kb), F32) for s, n in runs]
        blk = pieces[0] if len(pieces) == 1 else jnp.concatenate(pieces, axis=0)
        o_ref[:, t * LANES:(t + 1) * LANES] = blk.T.astype(BF16)


def _permute_weights(w_in):
    depth, d, in_width = w_in.shape
    plan = _permute_plan()
    kb = 256
    return pl.pallas_call(
        functools.partial(_wprep_kernel, plan=plan),
        grid=(depth, d // kb),
        in_specs=[pl.BlockSpec((None, in_width, kb), lambda l, i: (l, 0, i))],
        out_specs=pl.BlockSpec((None, kb, U_TOTAL), lambda l, i: (l, i, 0)),
        out_shape=jax.ShapeDtypeStruct((depth, d, U_TOTAL), BF16),
        compiler_params=pltpu.CompilerParams(
            dimension_semantics=("arbitrary", "arbitrary"), vmem_limit_bytes=VMEM_LIMIT),
        name="permute_weights",
    )(jnp.swapaxes(w_in, 1, 2))


def _gate_rows(ml_b_i, ml_b_f, ssd_dt_bias, ssd_a_log):
    n_f, n_dt = N_DIR * ML_HEADS, N_DIR * SSD_HEADS
    row0 = jnp.concatenate([ml_b_f.reshape(-1), ssd_dt_bias.reshape(-1), ml_b_i.reshape(-1),
                            jnp.zeros((LANES - 2 * n_f - n_dt,), F32)])
    row2 = jnp.concatenate([jnp.zeros((n_f,), F32), ssd_a_log.reshape(-1), jnp.zeros((LANES - n_f - n_dt,), F32)])
    zero = jnp.zeros((1, LANES), F32)
    return jnp.concatenate([row0[None], zero, row2[None], jnp.zeros((5, LANES), F32)], axis=0)


def _rope_tables(n_lat):
    nfreq = ATT_HEAD_DIM // 4
    pos = np.arange(n_lat)
    rows, cols = pos // GRID_W, pos % GRID_W
    inv = np.float32(ROPE_BASE) ** (-np.arange(nfreq, dtype=np.float32) / np.float32(nfreq))
    ang_r = rows.astype(np.float32)[:, None] * inv
    ang_c = cols.astype(np.float32)[:, None] * inv
    cos_h = np.concatenate([np.cos(ang_r)] * 2 + [np.cos(ang_c)] * 2, axis=1)
    sin_h = np.concatenate([-np.sin(ang_r), np.sin(ang_r), -np.sin(ang_c), np.sin(ang_c)], axis=1)
    return (jnp.asarray(np.concatenate([cos_h] * 2, axis=1), F32),
            jnp.asarray(np.concatenate([sin_h] * 2, axis=1), F32))


def _token_mixers(u_l, u_c, p, layer, tables, need_ctx_out):
    uml_l, ussd_l, uatt_l, g_l = u_l
    uml_c, ussd_c, uatt_c, g_c = u_c
    pvec = _gate_rows(p['ml_b_i'][layer], p['ml_b_f'][layer], p['ssd_dt_bias'][layer], p['ssd_a_log'][layer])
    m_l, m_c = _mlstm_mixer(uml_l, uml_c, g_l, g_c, pvec, p['ml_norm_g'][layer][None], need_ctx_out)
    s_l, s_c = _ssd_mixer(ussd_l, ussd_c, g_l, g_c, pvec, p['ssd_conv_w'][layer], p['ssd_conv_b'][layer][None],
                          jnp.repeat(p['ssd_d'][layer], SSD_HEAD_DIM)[None], p['ssd_norm_g'][layer][None],
                          need_ctx_out)
    a_l, a_c = _attn_mixer(uatt_l, uatt_c, tables['cos'], tables['sin'],
                           jnp.tile(p['att_qn_g'][layer], 2)[None], jnp.tile(p['att_kn_g'][layer], 2)[None],
                           p['att_sink'][layer], need_ctx_out)
    return (m_l, s_l, a_l), (m_c, s_c, a_c)


def kernel(x, c, ctx, c_ctx, w_mod, b_mod, norm1_g, w_in, ml_b_i, ml_b_f, ml_norm_g, ssd_conv_w, ssd_conv_b,
           ssd_a_log, ssd_dt_bias, ssd_d, ssd_norm_g, att_qn_g, att_kn_g, att_sink, w_out, norm2_g, w_up, w_down):
    p = dict(ml_b_i=ml_b_i, ml_b_f=ml_b_f, ml_norm_g=ml_norm_g, ssd_conv_w=ssd_conv_w, ssd_conv_b=ssd_conv_b,
             ssd_a_log=ssd_a_log, ssd_dt_bias=ssd_dt_bias, ssd_d=ssd_d, ssd_norm_g=ssd_norm_g,
             att_qn_g=att_qn_g, att_kn_g=att_kn_g, att_sink=att_sink)
    bsz, n_lat, d = x.shape
    n_ctx = ctx.shape[1]
    depth = w_mod.shape[0]
    mod_rows = 16
    cc = jnp.concatenate([c, c_ctx[None], jnp.zeros((mod_rows - bsz - 1, d), F32)], axis=0)
    mods = _modulation(cc, w_mod, b_mod).reshape(depth, mod_rows, 6, d)
    cos_t, sin_t = _rope_tables(n_lat)
    tables = dict(cos=cos_t, sin=sin_t)
    lat_row = lambda b: b
    ctx_row = lambda b: bsz
    h_ctx = ctx
    w = _permute_weights(w_in)
    wo, wup, wdn = w_out.astype(BF16), w_up.astype(BF16), w_down.astype(BF16)
    g1, g2 = norm1_g[:, None, :], norm2_g[:, None, :]
    for layer in range(depth):
        need_ctx_out = layer < depth - 1
        u_l = _in_proj(x, mods, lat_row, g1, w, layer, tm=512)
        u_c = _in_proj(h_ctx, mods, ctx_row, g1, w, layer, tm=256)
        y_l, y_c = _token_mixers(u_l, u_c, p, layer, tables, need_ctx_out)
        x = _post(x, *y_l, mods, lat_row, g2, wo, wup, wdn, layer, tm=512)
        if need_ctx_out:
            h_ctx = _post(h_ctx, *y_c, mods, ctx_row, g2, wo, wup, wdn, layer, tm=256)
    return x
```

```python
import functools
import math

import numpy as np
import jax
import jax.numpy as jnp
from jax import lax
from jax.experimental import pallas as pl
from jax.experimental.pallas import tpu as pltpu

F32 = jnp.float32
BF16 = jnp.bfloat16

EPS = 1e-6
GRID_W = 64
N_DIR = 2
ML_HEADS = 4
ML_HEAD_DIM = 128
ML_WIDTH = ML_HEADS * ML_HEAD_DIM
SSD_HEADS = 8
SSD_HEAD_DIM = 64
SSD_WIDTH = SSD_HEADS * SSD_HEAD_DIM
SSD_STATE = 64
SSD_GROUPS = 2
SSD_XBC = SSD_WIDTH + 2 * SSD_GROUPS * SSD_STATE
ATT_Q_HEADS = 8
ATT_KV_HEADS = 2
ATT_HEAD_DIM = 64
ATT_WIDTH = ATT_Q_HEADS * ATT_HEAD_DIM
ATT_KV_WIDTH = ATT_KV_HEADS * ATT_HEAD_DIM
WINDOW = 128
ROPE_BASE = 10000.0
MIX_WIDTH = ML_WIDTH + SSD_WIDTH + ATT_WIDTH

LANES = 128
CHUNK = 128
VMEM_LIMIT = 56 * 1024 * 1024
UNROLL = 16
UNROLL_LIGHT = 8
PREP_GROUP = 8

U_ML = 4 * ML_WIDTH
U_SSD = SSD_WIDTH + SSD_XBC
U_ATT = ATT_WIDTH + 2 * ATT_KV_WIDTH
U_GATE = LANES
U_TOTAL = U_ML + U_SSD + U_ATT + U_GATE
DT_LANE0 = N_DIR * ML_HEADS
LI_LANE0 = DT_LANE0 + N_DIR * SSD_HEADS

ML_SCALE = ML_HEAD_DIM ** -0.5
ATT_SCALE = ATT_HEAD_DIM ** -0.5


def _dot(a, b):
    return jnp.dot(a, b, preferred_element_type=F32)


def _split2(x):
    hi = x.astype(BF16)
    lo = (x - hi.astype(F32)).astype(BF16)
    return hi, lo


def _split3(x):
    hi = x.astype(BF16)
    r = x - hi.astype(F32)
    mid = r.astype(BF16)
    lo = (r - mid.astype(F32)).astype(BF16)
    return hi, mid, lo


def _row_cumsums(tris, x):
    rows = x.shape[0]
    parts = _split3(jnp.concatenate([x, jnp.zeros_like(x)], axis=0))
    return [(_dot(parts[0], tri) + _dot(parts[1], tri) + _dot(parts[2], tri))[0:rows] for tri in tris]


def _sigmoid(x):
    return 1.0 / (1.0 + jnp.exp2(x * (-1.0 / math.log(2.0))))


def _log_sigmoid(x):
    return jnp.minimum(x, 0.0) - jnp.log1p(jnp.exp(-jnp.abs(x)))


def _softplus(x):
    return jnp.maximum(x, 0.0) + jnp.log1p(jnp.exp(-jnp.abs(x)))


def _iota(shape, dim):
    return lax.broadcasted_iota(jnp.int32, shape, dim)


def _chunk_rows(c):
    return pl.ds(pl.multiple_of(c * CHUNK, CHUNK), CHUNK)


def _mod_kernel(s_ref, w_ref, b_ref, o_ref):
    s = s_ref[...]
    s = s * _sigmoid(s)
    w = w_ref[...]
    s_hi, s_lo = _split2(s)
    w_hi, w_lo = _split2(w)
    o_ref[...] = _dot(s_hi, w_hi) + (_dot(s_hi, w_lo) + _dot(s_lo, w_hi)) + b_ref[...]


def _modulation(cc, w_mod, b_mod):
    depth, d, d6 = w_mod.shape
    r = cc.shape[0]
    bn = 2048
    return pl.pallas_call(
        _mod_kernel,
        grid=(depth, d6 // bn),
        in_specs=[
            pl.BlockSpec((r, d), lambda l, j: (0, 0)),
            pl.BlockSpec((None, d, bn), lambda l, j: (l, 0, j)),
            pl.BlockSpec((None, 1, bn), lambda l, j: (l, 0, j)),
        ],
        out_specs=pl.BlockSpec((None, r, bn), lambda l, j: (l, 0, j)),
        out_shape=jax.ShapeDtypeStruct((depth, r, d6), F32),
        compiler_params=pltpu.CompilerParams(
            dimension_semantics=("arbitrary", "arbitrary"), vmem_limit_bytes=VMEM_LIMIT),
        name="modulation",
    )(cc, w_mod, b_mod.reshape(depth, 1, d6))


def _inproj_kernel(x_ref, mod_ref, g_ref, w_ref, oml_ref, ossd_ref, oatt_ref, og_ref):
    x = x_ref[...]
    ms = jnp.mean(x * x, axis=-1, keepdims=True)
    xn = x * lax.rsqrt(ms + EPS)
    h = xn * (g_ref[...] * (1.0 + mod_ref[1:2, :])) + mod_ref[0:1, :]
    hb = h.astype(BF16)
    step = 512
    off = 0
    for o_ref, width in ((oml_ref, U_ML), (ossd_ref, U_SSD), (oatt_ref, U_ATT), (og_ref, U_GATE)):
        for j in range(0, width, step):
            wj = min(step, width - j)
            o_ref[:, j:j + wj] = _dot(hb, w_ref[:, off + j:off + j + wj]).astype(o_ref.dtype)
        off += width


def _in_proj(x, mods, mod_row, g1, w, layer, tm):
    bsz, t, d = x.shape
    widths = (U_ML, U_SSD, U_ATT, U_GATE)
    dtypes = (BF16, BF16, BF16, F32)
    return pl.pallas_call(
        _inproj_kernel,
        grid=(bsz, t // tm),
        in_specs=[
            pl.BlockSpec((None, tm, d), lambda b, i: (b, i, 0)),
            pl.BlockSpec((None, None, 6, d), lambda b, i: (layer, mod_row(b), 0, 0)),
            pl.BlockSpec((None, 1, d), lambda b, i: (layer, 0, 0)),
            pl.BlockSpec((None, d, U_TOTAL), lambda b, i: (layer, 0, 0), pipeline_mode=pl.Buffered(1)),
        ],
        out_specs=[pl.BlockSpec((None, tm, wd), lambda b, i: (b, i, 0)) for wd in widths],
        out_shape=[jax.ShapeDtypeStruct((bsz, t, wd), dt) for wd, dt in zip(widths, dtypes)],
        compiler_params=pltpu.CompilerParams(
            dimension_semantics=("arbitrary", "arbitrary"), vmem_limit_bytes=VMEM_LIMIT),
        name="in_proj",
    )(x, mods, g1, w)


def _post_kernel(x_ref, yml_ref, yssd_ref, yatt_ref, mod_ref, g_ref, wo_ref, wup_ref, wdn_ref, o_ref):
    proj = (_dot(yml_ref[...], wo_ref[0:ML_WIDTH, :])
            + _dot(yssd_ref[...], wo_ref[ML_WIDTH:ML_WIDTH + SSD_WIDTH, :])
            + _dot(yatt_ref[...], wo_ref[ML_WIDTH + SSD_WIDTH:MIX_WIDTH, :]))
    x1 = x_ref[...] + mod_ref[2:3, :] * proj
    ms = jnp.mean(x1 * x1, axis=-1, keepdims=True)
    xn = x1 * lax.rsqrt(ms + EPS)
    hb = (xn * (g_ref[...] * (1.0 + mod_ref[4:5, :])) + mod_ref[3:4, :]).astype(BF16)
    d_ff = wup_ref.shape[1]
    step = 1024
    acc = None
    for j in range(0, d_ff, step):
        up = jnp.maximum(_dot(hb, wup_ref[:, j:j + step]), 0.0)
        part = _dot((up * up).astype(BF16), wdn_ref[j:j + step, :])
        acc = part if acc is None else acc + part
    o_ref[...] = x1 + mod_ref[5:6, :] * acc


def _post(x, yml, yssd, yatt, mods, mod_row, g2, wo, wup, wdn, layer, tm):
    bsz, t, d = x.shape
    d_ff = wup.shape[2]
    const = dict(pipeline_mode=pl.Buffered(1))
    return pl.pallas_call(
        _post_kernel,
        grid=(bsz, t // tm),
        in_specs=[
            pl.BlockSpec((None, tm, d), lambda b, i: (b, i, 0)),
            pl.BlockSpec((None, tm, ML_WIDTH), lambda b, i: (b, i, 0)),
            pl.BlockSpec((None, tm, SSD_WIDTH), lambda b, i: (b, i, 0)),
            pl.BlockSpec((None, tm, ATT_WIDTH), lambda b, i: (b, i, 0)),
            pl.BlockSpec((None, None, 6, d), lambda b, i: (layer, mod_row(b), 0, 0)),
            pl.BlockSpec((None, 1, d), lambda b, i: (layer, 0, 0)),
            pl.BlockSpec((None, MIX_WIDTH, d), lambda b, i: (layer, 0, 0), **const),
            pl.BlockSpec((None, d, d_ff), lambda b, i: (layer, 0, 0), **const),
            pl.BlockSpec((None, d_ff, d), lambda b, i: (layer, 0, 0), **const),
        ],
        out_specs=pl.BlockSpec((None, tm, d), lambda b, i: (b, i, 0)),
        out_shape=jax.ShapeDtypeStruct((bsz, t, d), F32),
        compiler_params=pltpu.CompilerParams(
            dimension_semantics=("arbitrary", "arbitrary"), vmem_limit_bytes=VMEM_LIMIT),
        name="post",
    )(x, yml, yssd, yatt, mods, g2, wo, wup, wdn)


ML_GATES = N_DIR * ML_HEADS


def _mlstm_kernel(ul_ref, uc_ref, gl_ref, gc_ref, pv_ref, ng_ref, ex_ref, *rest,
                  need_ctx_out, n_lat, n_ctx):
    if need_ctx_out:
        yl_ref, yc_ref, arow, packed, ends, mprev, cbk, crun = rest
    else:
        yl_ref, arow, packed, ends, mprev, cbk, crun = rest
        yc_ref = None
    ncl, ncc = n_lat // CHUNK, n_ctx // CHUNK
    nct = ncl + ncc
    nh = ML_HEADS
    ti = _iota((CHUNK, CHUNK), 0)
    si = _iota((CHUNK, CHUNK), 1)
    low = si <= ti
    upp = si >= ti
    tri_low = jnp.where(low, 1.0, 0.0).astype(BF16)
    tri_upp = jnp.where(upp, 1.0, 0.0).astype(BF16)
    fwd_row = _iota((ML_GATES, CHUNK), 0) < nh
    bias = pv_ref[0:1, :]
    neg_inf = -jnp.inf
    ones_blk = jnp.ones((CHUNK, LANES), BF16)

    def prep_group(g_ref, c0, n, goff):
        def per_chunk(x):
            return [x[:, i * CHUNK:(i + 1) * CHUNK] for i in range(n)]

        def wide(xs):
            return xs[0] if n == 1 else jnp.concatenate(xs, axis=1)

        gts = [(g_ref[(c0 + i) * CHUNK:(c0 + i + 1) * CHUNK, :] + bias).T for i in range(n)]
        lf = _log_sigmoid(wide([gt[0:ML_GATES] for gt in gts]))
        li = wide([gt[LI_LANE0:LI_LANE0 + ML_GATES] for gt in gts])
        sums = [_row_cumsums((tri_upp, tri_low), x) for x in per_chunk(lf)]
        fwd = _iota((ML_GATES, n * CHUNK), 0) < nh
        tok = _iota((ML_GATES, n * CHUNK), 1) & (CHUNK - 1)
        b = jnp.where(fwd, wide([s[0] for s in sums]), wide([s[1] for s in sums]))
        a = li - b
        cm = a
        k = 1
        while k < CHUNK:
            before = jnp.where(tok >= k, pltpu.roll(cm, k, 1), neg_inf)
            after = jnp.where(tok < CHUNK - k, pltpu.roll(cm, n * CHUNK - k, 1), neg_inf)
            cm = jnp.maximum(cm, jnp.where(fwd, before, after))
            k *= 2
        parts = [p.astype(F32) for p in _split3(cm) + _split3(b)]
        parts.append(jnp.zeros((CHUNK - len(parts) * ML_GATES, n * CHUNK), F32))
        stacked = jnp.concatenate(parts, axis=0)
        for i in range(n):
            cg = goff + c0 + i
            cols = slice(i * CHUNK, (i + 1) * CHUNK)
            arow[cg] = a[:, cols]
            packed[cg] = stacked[:, cols].T.astype(BF16)
            for plane, x in enumerate((cm, b)):
                last = x[:, (i + 1) * CHUNK - 1:(i + 1) * CHUNK]
                first = x[:, i * CHUNK:i * CHUNK + 1]
                ends[cg, plane] = jnp.broadcast_to(jnp.where(fwd_row[:, 0:1], last, first), (ML_GATES, CHUNK))

    def prep(g_ref, nch, goff):
        for c0 in range(0, nch, PREP_GROUP):
            prep_group(g_ref, c0, min(PREP_GROUP, nch - c0), goff)

    prep(gc_ref, ncc, 0)
    prep(gl_ref, ncl, ncc)

    def m_scan(lo, hi, reverse, m0):
        def body(i, m):
            cg = hi - 1 - i if reverse else lo + i
            mprev[cg, int(reverse)] = m
            return ends[cg, 1] + jnp.maximum(m, ends[cg, 0])
        return lax.fori_loop(0, hi - lo, body, m0)

    m_zero = jnp.zeros((ML_GATES, CHUNK), F32)
    m_scan(0, nct, False, m_zero)
    m_scan(ncc, nct, True, m_scan(0, ncc, True, m_zero))

    def load_kv(u_ref, rows, h):
        k = u_ref[rows, ML_WIDTH + h * LANES:ML_WIDTH + (h + 1) * LANES]
        v = u_ref[rows, 2 * ML_WIDTH + h * LANES:2 * ML_WIDTH + (h + 1) * LANES]
        return k.T, jnp.concatenate([v, ones_blk], axis=1)

    def update(d, h, cg, kt, v1):
        j = d * nh + h
        m_prev = mprev[cg, d, j:j + 1, 0:1]
        m_end = jnp.maximum(m_prev, ends[cg, 0, j:j + 1, 0:1])
        w_row = jnp.exp(arow[cg, j:j + 1, :] - m_end) * ML_SCALE
        ktw = (kt.astype(F32) * w_row).astype(BF16)
        crun[d, h] = jnp.exp(m_prev - m_end) * crun[d, h] + _dot(ktw, v1)

    def bwd_pass(u_ref, nch, goff, keep):
        def body(i, carry):
            c = nch - 1 - i
            rows = _chunk_rows(c)
            for h in range(nh):
                if keep:
                    cbk[goff + c, h] = crun[1, h].astype(BF16)
                kt, v1 = load_kv(u_ref, rows, h)
                update(1, h, goff + c, kt, v1)
            return carry
        lax.fori_loop(0, nch, body, 0, unroll=UNROLL_LIGHT)

    def fwd_pass(u_ref, y_ref, nch, goff):
        def body(c, carry):
            rows = _chunk_rows(c)
            cg = goff + c
            for h in range(nh):
                kt, v1 = load_kv(u_ref, rows, h)
                if y_ref is not None:
                    q = u_ref[rows, h * LANES:(h + 1) * LANES]
                    o = u_ref[rows, 3 * ML_WIDTH + h * LANES:3 * ML_WIDTH + (h + 1) * LANES]
                    sqk = _dot(q, kt) * ML_SCALE
                    qc = _dot(q, jnp.concatenate([crun[0, h].astype(BF16), cbk[cg, h]], axis=1))
                    s_dirs, stats = [], []
                    for d, mask in ((0, low), (1, upp)):
                        j = d * nh + h
                        rep = _dot(packed[cg], ex_ref[:, 2 * j * LANES:2 * (j + 1) * LANES])
                        m_prev = mprev[cg, d, j:j + 1, 0:1]
                        m_t = jnp.maximum(rep[:, 0:LANES], m_prev)
                        s_dirs.append(sqk * jnp.exp(jnp.where(mask, arow[cg, j:j + 1, :] - m_t, neg_inf)))
                        stats.append((m_prev, m_t, rep[:, LANES:2 * LANES]))
                    sv = _dot(jnp.concatenate(s_dirs, axis=0).astype(BF16), v1)
                    hsum = None
                    for d, (m_prev, m_t, b_t) in enumerate(stats):
                        g = jnp.exp(m_prev - m_t)
                        sv_d = sv[d * CHUNK:(d + 1) * CHUNK, :]
                        qc_d = qc[:, 2 * d * LANES:2 * (d + 1) * LANES]
                        num = sv_d[:, 0:LANES] + g * qc_d[:, 0:LANES]
                        den = sv_d[:, LANES:2 * LANES] + g * qc_d[:, LANES:2 * LANES]
                        hd = num / jnp.maximum(jnp.abs(den), jnp.exp(-(b_t + m_t)))
                        hsum = hd if hsum is None else hsum + hd
                    ms = jnp.mean(hsum * hsum, axis=1, keepdims=True)
                    y = hsum * lax.rsqrt(ms + EPS) * ng_ref[:, h * LANES:(h + 1) * LANES]
                    y = y * _sigmoid(o.astype(F32))
                    y_ref[rows, h * LANES:(h + 1) * LANES] = y.astype(y_ref.dtype)
                update(0, h, cg, kt, v1)
            return carry
        lax.fori_loop(0, nch, body, 0, unroll=UNROLL)

    crun[...] = jnp.zeros(crun.shape, F32)
    bwd_pass(uc_ref, ncc, 0, need_ctx_out)
    bwd_pass(ul_ref, ncl, ncc, True)
    fwd_pass(uc_ref, yc_ref, ncc, 0)
    fwd_pass(ul_ref, yl_ref, ncl, ncc)


def _ml_expand_table():
    expand = np.zeros((LANES, ML_GATES * 2 * LANES), np.float32)
    for j in range(ML_GATES):
        for i in range(6):
            half = i // 3
            expand[ML_GATES * i + j, (2 * j + half) * LANES:(2 * j + half + 1) * LANES] = 1.0
    return jnp.asarray(expand, BF16)


def _mlstm_mixer(uml_l, uml_c, g_l, g_c, pvec, norm_g, need_ctx_out):
    expand = _ml_expand_table()
    bsz, n_lat, _ = uml_l.shape
    n_ctx = uml_c.shape[1]
    nct = (n_lat + n_ctx) // CHUNK
    kern = functools.partial(_mlstm_kernel, need_ctx_out=need_ctx_out, n_lat=n_lat, n_ctx=n_ctx)
    out_specs = [pl.BlockSpec((None, n_lat, ML_WIDTH), lambda b: (b, 0, 0))]
    out_shape = [jax.ShapeDtypeStruct((bsz, n_lat, ML_WIDTH), BF16)]
    if need_ctx_out:
        out_specs.append(pl.BlockSpec((None, n_ctx, ML_WIDTH), lambda b: (b, 0, 0)))
        out_shape.append(jax.ShapeDtypeStruct((bsz, n_ctx, ML_WIDTH), BF16))
    outs = pl.pallas_call(
        kern,
        grid=(bsz,),
        in_specs=[
            pl.BlockSpec((None, n_lat, U_ML), lambda b: (b, 0, 0)),
            pl.BlockSpec((None, n_ctx, U_ML), lambda b: (b, 0, 0)),
            pl.BlockSpec((None, n_lat, U_GATE), lambda b: (b, 0, 0)),
            pl.BlockSpec((None, n_ctx, U_GATE), lambda b: (b, 0, 0)),
            pl.BlockSpec((8, LANES), lambda b: (0, 0)),
            pl.BlockSpec((1, ML_WIDTH), lambda b: (0, 0)),
            pl.BlockSpec(expand.shape, lambda b: (0, 0)),
        ],
        out_specs=out_specs,
        out_shape=out_shape,
        scratch_shapes=[
            pltpu.VMEM((nct, ML_GATES, CHUNK), F32),
            pltpu.VMEM((nct, CHUNK, LANES), BF16),
            pltpu.VMEM((nct, 2, ML_GATES, CHUNK), F32),
            pltpu.VMEM((nct, N_DIR, ML_GATES, CHUNK), F32),
            pltpu.VMEM((nct, ML_HEADS, ML_HEAD_DIM, 2 * LANES), BF16),
            pltpu.VMEM((N_DIR, ML_HEADS, ML_HEAD_DIM, 2 * LANES), F32),
        ],
        compiler_params=pltpu.CompilerParams(
            dimension_semantics=("arbitrary",), vmem_limit_bytes=VMEM_LIMIT),
        name="mlstm",
    )(uml_l, uml_c, g_l, g_c, pvec, norm_g, expand)
    return (outs[0], outs[1]) if need_ctx_out else (outs[0], None)


def _ssd_kernel(ul_ref, uc_ref, gl_ref, gc_ref, pv_ref, cw_ref, cb_ref, dsk_ref, ng_ref, ex_ref, *rest,
                need_ctx_out, n_lat, n_ctx):
    if need_ctx_out:
        yl_ref, yc_ref, xact, bt, acum, packed, act, dtt, hbk, hrun = rest
    else:
        yl_ref, xact, bt, acum, packed, act, dtt, hbk, hrun = rest
        yc_ref = None
    ncl, ncc = n_lat // CHUNK, n_ctx // CHUNK
    nh, hg = SSD_HEADS, SSD_HEADS // SSD_GROUPS
    gw = hg * SSD_HEAD_DIM
    gn = SSD_GROUPS * SSD_STATE
    ti = _iota((CHUNK, CHUNK), 0)
    si = _iota((CHUNK, CHUNK), 1)
    low_strict = si < ti
    upp_strict = si > ti
    tri_low = jnp.where(si <= ti, 1.0, 0.0).astype(BF16)
    tri_upp = jnp.where(si >= ti, 1.0, 0.0).astype(BF16)
    lane = _iota((CHUNK, LANES), 1)
    nj = N_DIR * nh
    ac_lane0 = 6 * nj
    fwd_row = _iota((nj, CHUNK), 0) < nh
    bias0 = pv_ref[0:1, :]
    a_rows = jnp.broadcast_to(-jnp.exp(pv_ref[2:3, :]), (LANES, LANES)).T[DT_LANE0:DT_LANE0 + nj]
    shift_down = jnp.where(si == ti - 1, 1.0, 0.0).astype(BF16)
    shift_up = jnp.where(si == ti + 1, 1.0, 0.0).astype(BF16)
    lane_group0 = lane < SSD_STATE
    row_group0 = _iota((gn, CHUNK), 0) < SSD_STATE

    def prep(u_ref, g_ref, nch, goff, nrows):
        def body(c, carry):
            r = pl.multiple_of(c * CHUNK, CHUNK)
            cur = u_ref[pl.ds(r, CHUNK), SSD_WIDTH:U_SSD]
            rp = pl.multiple_of(jnp.maximum(r - CHUNK, 0), CHUNK)
            rn = pl.multiple_of(jnp.minimum(r + CHUNK, nrows - CHUNK), CHUNK)
            prev = u_ref[pl.ds(rp, CHUNK), SSD_WIDTH:U_SSD]
            nxt = u_ref[pl.ds(rn, CHUNK), SSD_WIDTH:U_SSD]
            from_prev = jnp.where((ti == 0) & (si == CHUNK - 1) & (c > 0), 1.0, 0.0).astype(BF16)
            from_next = jnp.where((ti == CHUNK - 1) & (si == 0) & (c < nch - 1), 1.0, 0.0).astype(BF16)
            xm = _dot(jnp.concatenate([shift_down, from_prev], axis=1), jnp.concatenate([cur, prev], axis=0))
            xp = _dot(jnp.concatenate([shift_up, from_next], axis=1), jnp.concatenate([cur, nxt], axis=0))
            conv = (cw_ref[0:1, :] * xm + cw_ref[1:2, :] * cur.astype(F32) + cw_ref[2:3, :] * xp
                    + cb_ref[...])
            xa = (conv * _sigmoid(conv)).astype(BF16)
            xact[goff + c] = xa
            bt[goff + c] = xa[:, SSD_WIDTH:SSD_WIDTH + gn].T

            gt = (g_ref[pl.ds(r, CHUNK), :] + bias0).T
            dt = _softplus(gt[DT_LANE0:DT_LANE0 + nj])
            prefix, suffix = _row_cumsums((tri_upp, tri_low), dt * a_rows)
            ac = jnp.where(fwd_row, prefix, suffix)
            a_end = jnp.where(fwd_row[:, 0:1], ac[:, CHUNK - 1:CHUNK], ac[:, 0:1])
            act[goff + c] = ac - jnp.log(dt)
            dtt[goff + c] = dt
            parts = [p.astype(F32) for p in _split3(jnp.exp(ac)) + _split3(jnp.exp(a_end - ac) * dt)]
            parts += [ac, jnp.zeros((CHUNK - 7 * nj, CHUNK), F32)]
            tile = jnp.concatenate(parts, axis=0).T
            acum[goff + c] = tile
            packed[goff + c] = tile.astype(BF16)
            return carry
        lax.fori_loop(0, nch, body, 0, unroll=UNROLL_LIGHT)

    prep(uc_ref, gc_ref, ncc, 0, n_ctx)
    prep(ul_ref, gl_ref, ncl, ncc, n_lat)

    def e_cols(d, g):
        return slice(d * SSD_WIDTH + g * gw, d * SSD_WIDTH + (g + 1) * gw)

    def w_cols(d, g):
        return slice((N_DIR + d) * SSD_WIDTH + g * gw, (N_DIR + d) * SSD_WIDTH + (g + 1) * gw)

    def update_state(d, cg):
        pk = packed[cg]
        xa = xact[cg]
        b_t = bt[cg]
        zero = jnp.zeros_like(b_t)
        upd = []
        for g in range(SSD_GROUPS):
            wx = (_dot(pk, ex_ref[:, w_cols(d, g)]) * xa[:, g * gw:(g + 1) * gw].astype(F32)).astype(BF16)
            upd.append(_dot(jnp.where(row_group0 == (g == 0), b_t, zero), wx))
        tail, last = (slice(CHUNK - 16, CHUNK), 15) if d == 0 else (slice(0, 16), 0)
        decay = _dot(pk[tail], ex_ref[:, d * SSD_WIDTH:(d + 1) * SSD_WIDTH])[last:last + 1, :]
        hrun[d] = decay * hrun[d] + jnp.concatenate(upd, axis=1)

    def bwd_pass(nch, goff, keep):
        def body(i, carry):
            cg = goff + nch - 1 - i
            if keep:
                hbk[cg] = hrun[1].astype(BF16)
            update_state(1, cg)
            return carry
        lax.fori_loop(0, nch, body, 0, unroll=UNROLL_LIGHT)

    def fwd_pass(u_ref, y_ref, nch, goff):
        def body(c, carry):
            cg = goff + c
            if y_ref is not None:
                rows = _chunk_rows(c)
                xa = xact[cg]
                x = xa[:, 0:SSD_WIDTH]
                b_t = bt[cg]
                c_all = xa[:, SSD_WIDTH + gn:SSD_XBC]
                ac = acum[cg]
                ac_t = act[cg]
                dt_t = dtt[cg]
                lane_w = _iota((CHUNK, gw), 1)
                pk = packed[cg]
                yz_groups, ssq = [], None
                for g in range(SSD_GROUPS):
                    cols = slice(g * gw, (g + 1) * gw)
                    c_g = jnp.where(lane_group0 == (g == 0), c_all, jnp.zeros_like(c_all))
                    cb = _dot(c_g, b_t)
                    s_heads = []
                    for hh in range(hg):
                        jf = g * hg + hh
                        jb = jf + nh
                        on_diag = jnp.log(dt_t[jf:jf + 1, :] + dt_t[jb:jb + 1, :])
                        ac_f = ac[:, ac_lane0 + jf:ac_lane0 + jf + 1]
                        ac_b = ac[:, ac_lane0 + jb:ac_lane0 + jb + 1]
                        expo = jnp.where(low_strict, ac_f - ac_t[jf:jf + 1, :],
                                         jnp.where(upp_strict, ac_b - ac_t[jb:jb + 1, :], on_diag))
                        s_heads.append((cb * jnp.exp(expo)).astype(BF16))
                    x_g = x[:, cols]
                    bd = jnp.concatenate(
                        [jnp.where((lane_w >= hh * SSD_HEAD_DIM) & (lane_w < (hh + 1) * SSD_HEAD_DIM),
                                   x_g, jnp.zeros_like(x_g)) for hh in range(hg)], axis=0)
                    y = _dot(jnp.concatenate(s_heads, axis=1), bd)
                    y = y + _dot(c_all, hrun[0, :, cols].astype(BF16)) * _dot(pk, ex_ref[:, e_cols(0, g)])
                    y = y + _dot(c_all, hbk[cg, :, cols]) * _dot(pk, ex_ref[:, e_cols(1, g)])
                    y = y + dsk_ref[:, cols] * x_g.astype(F32)
                    z = u_ref[rows, cols].astype(F32)
                    yz = y * (z * _sigmoid(z))
                    part = jnp.sum(yz * yz, axis=1, keepdims=True)
                    ssq = part if ssq is None else ssq + part
                    yz_groups.append(yz)
                scale = lax.rsqrt(ssq * (1.0 / SSD_WIDTH) + EPS)
                for g, yz in enumerate(yz_groups):
                    cols = slice(g * gw, (g + 1) * gw)
                    y_ref[rows, cols] = (yz * scale * ng_ref[:, cols]).astype(y_ref.dtype)
            update_state(0, cg)
            return carry
        lax.fori_loop(0, nch, body, 0, unroll=UNROLL)

    hrun[...] = jnp.zeros(hrun.shape, F32)
    bwd_pass(ncc, 0, need_ctx_out)
    bwd_pass(ncl, ncc, True)
    fwd_pass(uc_ref, yc_ref, ncc, 0)
    fwd_pass(ul_ref, yl_ref, ncl, ncc)


def _ssd_expand_table():
    nj = N_DIR * SSD_HEADS
    expand = np.zeros((LANES, 2 * N_DIR * SSD_WIDTH), np.float32)
    for i in range(6):
        for d in range(N_DIR):
            for h in range(SSD_HEADS):
                j = d * SSD_HEADS + h
                c0 = ((i // 3) * N_DIR + d) * SSD_WIDTH + h * SSD_HEAD_DIM
                expand[nj * i + j, c0:c0 + SSD_HEAD_DIM] = 1.0
    return jnp.asarray(expand, BF16)


def _ssd_mixer(ussd_l, ussd_c, g_l, g_c, pvec, conv_w, conv_b, d_skip, norm_g, need_ctx_out):
    expand = _ssd_expand_table()
    bsz, n_lat, _ = ussd_l.shape
    n_ctx = ussd_c.shape[1]
    nct = (n_lat + n_ctx) // CHUNK
    gn = SSD_GROUPS * SSD_STATE
    kern = functools.partial(_ssd_kernel, need_ctx_out=need_ctx_out, n_lat=n_lat, n_ctx=n_ctx)
    out_specs = [pl.BlockSpec((None, n_lat, SSD_WIDTH), lambda b: (b, 0, 0))]
    out_shape = [jax.ShapeDtypeStruct((bsz, n_lat, SSD_WIDTH), BF16)]
    if need_ctx_out:
        out_specs.append(pl.BlockSpec((None, n_ctx, SSD_WIDTH), lambda b: (b, 0, 0)))
        out_shape.append(jax.ShapeDtypeStruct((bsz, n_ctx, SSD_WIDTH), BF16))
    outs = pl.pallas_call(
        kern,
        grid=(bsz,),
        in_specs=[
            pl.BlockSpec((None, n_lat, U_SSD), lambda b: (b, 0, 0)),
            pl.BlockSpec((None, n_ctx, U_SSD), lambda b: (b, 0, 0)),
            pl.BlockSpec((None, n_lat, LANES), lambda b: (b, 0, 0)),
            pl.BlockSpec((None, n_ctx, LANES), lambda b: (b, 0, 0)),
            pl.BlockSpec((8, LANES), lambda b: (0, 0)),
            pl.BlockSpec((3, SSD_XBC), lambda b: (0, 0)),
            pl.BlockSpec((1, SSD_XBC), lambda b: (0, 0)),
            pl.BlockSpec((1, SSD_WIDTH), lambda b: (0, 0)),
            pl.BlockSpec((1, SSD_WIDTH), lambda b: (0, 0)),
            pl.BlockSpec(expand.shape, lambda b: (0, 0)),
        ],
        out_specs=out_specs,
        out_shape=out_shape,
        scratch_shapes=[
            pltpu.VMEM((nct, CHUNK, SSD_XBC), BF16),
            pltpu.VMEM((nct, gn, CHUNK), BF16),
            pltpu.VMEM((nct, CHUNK, LANES), F32),
            pltpu.VMEM((nct, CHUNK, LANES), BF16),
            pltpu.VMEM((nct, N_DIR * SSD_HEADS, CHUNK), F32),
            pltpu.VMEM((nct, N_DIR * SSD_HEADS, CHUNK), F32),
            pltpu.VMEM((nct, gn, SSD_WIDTH), BF16),
            pltpu.VMEM((N_DIR, gn, SSD_WIDTH), F32),
        ],
        compiler_params=pltpu.CompilerParams(
            dimension_semantics=("arbitrary",), vmem_limit_bytes=VMEM_LIMIT),
        name="ssd",
    )(ussd_l, ussd_c, g_l, g_c, pvec, conv_w, conv_b, d_skip, norm_g, expand)
    return (outs[0], outs[1]) if need_ctx_out else (outs[0], None)


def _head_norm(x, gain):
    rows = x.shape[0]
    blks = [x[:, j * LANES:(j + 1) * LANES] for j in range(x.shape[1] // LANES)]
    same_head = (_iota((LANES, LANES), 0) < ATT_HEAD_DIM) == (_iota((LANES, LANES), 1) < ATT_HEAD_DIM)
    sq = jnp.concatenate([b * b for b in blks], axis=0).astype(BF16)
    ms = _dot(sq, jnp.where(same_head, 1.0, 0.0).astype(BF16)) * (1.0 / ATT_HEAD_DIM)
    outs = [b * lax.rsqrt(ms[j * rows:(j + 1) * rows] + EPS) * gain for j, b in enumerate(blks)]
    return outs[0] if len(outs) == 1 else jnp.concatenate(outs, axis=1)


def _rope(x, cos_t, sin_t):
    w = x.shape[1]
    reps = w // LANES
    nfreq = ATT_HEAD_DIM // 4
    partner = jnp.where((_iota(x.shape, 1) & (2 * nfreq - 1)) < nfreq,
                        pltpu.roll(x, w - nfreq, 1), pltpu.roll(x, nfreq, 1))
    if reps > 1:
        cos_t = jnp.concatenate([cos_t] * reps, axis=1)
        sin_t = jnp.concatenate([sin_t] * reps, axis=1)
    return x * cos_t + partner * sin_t


def _dup_heads(x):
    lane_lo = _iota(x.shape, 1) < ATT_HEAD_DIM
    sw = pltpu.roll(x, ATT_HEAD_DIM, 1)
    return jnp.concatenate([jnp.where(lane_lo, x, sw), jnp.where(lane_lo, sw, x)], axis=1)


def _attn_kernel(ul_ref, uc_ref, cos_ref, sin_ref, qg_ref, kg_ref, sink_ref, *rest,
                 need_ctx_out, n_lat, n_ctx):
    if need_ctx_out:
        yl_ref, yc_ref, qp, kb, vb, kc, vc, qcp = rest
    else:
        yl_ref, qp, kb, vb, kc, vc = rest
        yc_ref = qcp = None
    ncl, ncc = n_lat // CHUNK, n_ctx // CHUNK
    rep = ATT_Q_HEADS // ATT_KV_HEADS
    kv0 = ATT_WIDTH
    lane_lo = _iota((CHUNK, LANES), 1) < ATT_HEAD_DIM
    neg_inf = -jnp.inf
    qg = qg_ref[...]
    kg = kg_ref[...]

    zeros_blk = jnp.zeros((CHUNK, vb.shape[1]), BF16)
    vb[0:CHUNK, :] = zeros_blk
    vb[CHUNK + n_lat:2 * CHUNK + n_lat, :] = zeros_blk
    kb[0] = jnp.zeros(kb.shape[1:], BF16)
    kb[ncl + 1] = jnp.zeros(kb.shape[1:], BF16)

    def store_keys_transposed(ref, blk, k):
        kd = _dup_heads(k)
        for g in range(ATT_KV_HEADS):
            ref[blk, g * LANES:(g + 1) * LANES, :] = kd[:, g * LANES:(g + 1) * LANES].T.astype(BF16)

    def values_with_ones(v):
        d = _dup_heads(v).astype(BF16)
        ones = jnp.ones((v.shape[0], LANES), BF16)
        return jnp.concatenate([d[:, 0:LANES], ones, d[:, LANES:2 * LANES], ones], axis=1)

    def prep_lat(c, carry):
        rows = _chunk_rows(c)
        cos_t = cos_ref[rows, :]
        sin_t = sin_ref[rows, :]
        q = _rope(_head_norm(ul_ref[rows, 0:ATT_WIDTH].astype(F32), qg), cos_t, sin_t)
        qp[rows, :] = (q * ATT_SCALE).astype(BF16)
        k = _rope(_head_norm(ul_ref[rows, kv0:kv0 + LANES].astype(F32), kg), cos_t, sin_t)
        v = ul_ref[rows, kv0 + LANES:kv0 + 2 * LANES].astype(F32)
        dst = pl.ds(pl.multiple_of((c + 1) * CHUNK, CHUNK), CHUNK)
        store_keys_transposed(kb, c + 1, k)
        vb[dst, :] = values_with_ones(v)
        return carry
    lax.fori_loop(0, ncl, prep_lat, 0, unroll=UNROLL_LIGHT)

    def prep_ctx(c, carry):
        rows = _chunk_rows(c)
        k = _head_norm(uc_ref[rows, kv0:kv0 + LANES].astype(F32), kg)
        v = uc_ref[rows, kv0 + LANES:kv0 + 2 * LANES].astype(F32)
        store_keys_transposed(kc, c, k)
        vc[rows, :] = values_with_ones(v)
        if need_ctx_out:
            q = _head_norm(uc_ref[rows, 0:ATT_WIDTH].astype(F32), qg)
            qcp[rows, :] = (q * ATT_SCALE).astype(BF16)
        return carry
    lax.fori_loop(0, ncc, prep_ctx, 0, unroll=UNROLL)

    def attend(q_ref, y_ref, rows, band):
        for g in range(ATT_KV_HEADS):
            kl = slice(g * LANES, (g + 1) * LANES)
            vl = slice(2 * g * LANES, 2 * (g + 1) * LANES)
            qms = []
            for r in range(rep):
                hq = g * rep + r
                qb = q_ref[rows, (hq // 2) * LANES:(hq // 2 + 1) * LANES]
                qms.append(jnp.where(lane_lo != (hq % 2 == 1), qb, jnp.zeros_like(qb)))
            qs = jnp.concatenate(qms, axis=0)
            s_c = _dot(qs, jnp.concatenate([kc[j, kl, :] for j in range(ncc)], axis=1))
            if band is not None:
                blk, band_rows, valid = band
                s_b = _dot(qs, jnp.concatenate([kb[blk + j, kl, :] for j in range(3)], axis=1))
            ps, sink_terms = [], []
            for r in range(rep):
                sl = slice(r * CHUNK, (r + 1) * CHUNK)
                sk = sink_ref[g * rep + r]
                s = s_c[sl]
                if band is not None:
                    s = jnp.concatenate([s, jnp.where(valid, s_b[sl], neg_inf)], axis=1)
                m = jnp.maximum(jnp.max(s, axis=1, keepdims=True), sk)
                ps.append(jnp.exp(s - m).astype(BF16))
                sink_terms.append(jnp.exp(sk - m))
            p = jnp.concatenate(ps, axis=0)
            acc = _dot(p[:, 0:n_ctx], vc[:, vl])
            if band is not None:
                acc = acc + _dot(p[:, n_ctx:], vb[band_rows, vl])
            for r in range(rep):
                sl = slice(r * CHUNK, (r + 1) * CHUNK)
                o = acc[sl, 0:LANES] / (acc[sl, LANES:2 * LANES] + sink_terms[r])
                if r % 2 == 1:
                    jb = (g * rep + r) // 2
                    y_ref[rows, jb * LANES:(jb + 1) * LANES] = jnp.where(lane_lo, o_even, o).astype(y_ref.dtype)
                o_even = o

    qq = _iota((CHUNK, 3 * CHUNK), 0)
    kk = _iota((CHUNK, 3 * CHUNK), 1)
    in_window = (kk >= qq) & (kk <= qq + 2 * WINDOW)

    def lat_block(i, carry):
        kpos = kk + (i - 1) * CHUNK
        valid = in_window & (kpos >= 0) & (kpos < n_lat)
        band_rows = pl.ds(pl.multiple_of(i * CHUNK, CHUNK), 3 * CHUNK)
        attend(qp, yl_ref, _chunk_rows(i), (i, band_rows, valid))
        return carry
    lax.fori_loop(0, ncl, lat_block, 0, unroll=UNROLL)

    if need_ctx_out:
        def ctx_block(i, carry):
            attend(qcp, yc_ref, _chunk_rows(i), None)
            return carry
        lax.fori_loop(0, ncc, ctx_block, 0, unroll=UNROLL)


def _attn_mixer(uatt_l, uatt_c, cos_t, sin_t, qgain, kgain, sink, need_ctx_out):
    bsz, n_lat, _ = uatt_l.shape
    n_ctx = uatt_c.shape[1]
    kern = functools.partial(_attn_kernel, need_ctx_out=need_ctx_out, n_lat=n_lat, n_ctx=n_ctx)
    out_specs = [pl.BlockSpec((None, n_lat, ATT_WIDTH), lambda b: (b, 0, 0))]
    out_shape = [jax.ShapeDtypeStruct((bsz, n_lat, ATT_WIDTH), BF16)]
    scratch = [
        pltpu.VMEM((n_lat, ATT_WIDTH), BF16),
        pltpu.VMEM((n_lat // CHUNK + 2, 2 * LANES, CHUNK), BF16),
        pltpu.VMEM((n_lat + 2 * CHUNK, 4 * LANES), BF16),
        pltpu.VMEM((n_ctx // CHUNK, 2 * LANES, CHUNK), BF16),
        pltpu.VMEM((n_ctx, 4 * LANES), BF16),
    ]
    if need_ctx_out:
        out_specs.append(pl.BlockSpec((None, n_ctx, ATT_WIDTH), lambda b: (b, 0, 0)))
        out_shape.append(jax.ShapeDtypeStruct((bsz, n_ctx, ATT_WIDTH), BF16))
        scratch.append(pltpu.VMEM((n_ctx, ATT_WIDTH), BF16))
    outs = pl.pallas_call(
        kern,
        grid=(bsz,),
        in_specs=[
            pl.BlockSpec((None, n_lat, U_ATT), lambda b: (b, 0, 0)),
            pl.BlockSpec((None, n_ctx, U_ATT), lambda b: (b, 0, 0)),
            pl.BlockSpec((n_lat, LANES), lambda b: (0, 0)),
            pl.BlockSpec((n_lat, LANES), lambda b: (0, 0)),
            pl.BlockSpec((1, LANES), lambda b: (0, 0)),
            pl.BlockSpec((1, LANES), lambda b: (0, 0)),
            pl.BlockSpec(memory_space=pltpu.SMEM),
        ],
        out_specs=out_specs,
        out_shape=out_shape,
        scratch_shapes=scratch,
        compiler_params=pltpu.CompilerParams(
            dimension_semantics=("arbitrary",), vmem_limit_bytes=VMEM_LIMIT),
        name="attention",
    )(uatt_l, uatt_c, cos_t, sin_t, qgain, kgain, sink)
    return (outs[0], outs[1]) if need_ctx_out else (outs[0], None)


def _in_layout():
    names = (('ml_q', ML_WIDTH), ('ml_k', ML_WIDTH), ('ml_v', ML_WIDTH), ('ml_o', ML_WIDTH),
             ('ml_i', N_DIR * ML_HEADS), ('ml_f', N_DIR * ML_HEADS),
             ('ssd_z', SSD_WIDTH), ('ssd_xbc', SSD_XBC), ('ssd_dt', N_DIR * SSD_HEADS),
             ('att_q', ATT_WIDTH), ('att_k', ATT_KV_WIDTH), ('att_v', ATT_KV_WIDTH))
    out, off = {}, 0
    for name, width in names:
        out[name] = (off, off + width)
        off += width
    return out


def _col_map():
    lay = _in_layout()
    take = lambda name: np.arange(lay[name][0], lay[name][1])
    zeros = lambda n: np.full((n,), -1)
    n_f, n_dt, n_i = N_DIR * ML_HEADS, N_DIR * SSD_HEADS, N_DIR * ML_HEADS
    return np.concatenate(
        [take('ml_q'), take('ml_k'), take('ml_v'), take('ml_o'),
         take('ssd_z'), take('ssd_xbc'),
         take('att_q'), take('att_k'), take('att_v'),
         take('ml_f'), take('ssd_dt'), take('ml_i'), zeros(LANES - n_f - n_dt - n_i)])


def _permute_plan():
    src = _col_map()
    plan = []
    for t in range(U_TOTAL // LANES):
        cols = src[t * LANES:(t + 1) * LANES]
        runs, i = [], 0
        while i < LANES:
            j = i
            while j + 1 < LANES and (cols[j + 1] == cols[j] + 1 if cols[j] >= 0 else cols[j + 1] < 0):
                j += 1
            runs.append((int(cols[i]), j - i + 1))
            i = j + 1
        plan.append(tuple(runs))
    return tuple(plan)


def _wprep_kernel(wt_ref, o_ref, *, plan):
    kb = wt_ref.shape[1]
    for t, runs in enumerate(plan):
        pieces = [wt_ref[s:s + n, :] if s >= 0 else jnp.zeros((n, kb), F32) for s, n in runs]
        blk = pieces[0] if len(pieces) == 1 else jnp.concatenate(pieces, axis=0)
        o_ref[:, t * LANES:(t + 1) * LANES] = blk.T.astype(BF16)


def _permute_weights(w_in):
    depth, d, in_width = w_in.shape
    plan = _permute_plan()
    kb = 256
    return pl.pallas_call(
        functools.partial(_wprep_kernel, plan=plan),
        grid=(depth, d // kb),
        in_specs=[pl.BlockSpec((None, in_width, kb), lambda l, i: (l, 0, i))],
        out_specs=pl.BlockSpec((None, kb, U_TOTAL), lambda l, i: (l, i, 0)),
        out_shape=jax.ShapeDtypeStruct((depth, d, U_TOTAL), BF16),
        compiler_params=pltpu.CompilerParams(
            dimension_semantics=("arbitrary", "arbitrary"), vmem_limit_bytes=VMEM_LIMIT),
        name="permute_weights",
    )(jnp.swapaxes(w_in, 1, 2))


def _gate_rows(ml_b_i, ml_b_f, ssd_dt_bias, ssd_a_log):
    n_f, n_dt = N_DIR * ML_HEADS, N_DIR * SSD_HEADS
    row0 = jnp.concatenate([ml_b_f.reshape(-1), ssd_dt_bias.reshape(-1), ml_b_i.reshape(-1),
                            jnp.zeros((LANES - 2 * n_f - n_dt,), F32)])
    row2 = jnp.concatenate([jnp.zeros((n_f,), F32), ssd_a_log.reshape(-1), jnp.zeros((LANES - n_f - n_dt,), F32)])
    zero = jnp.zeros((1, LANES), F32)
    return jnp.concatenate([row0[None], zero, row2[None], jnp.zeros((5, LANES), F32)], axis=0)


def _rope_tables(n_lat):
    nfreq = ATT_HEAD_DIM // 4
    pos = np.arange(n_lat)
    rows, cols = pos // GRID_W, pos % GRID_W
    inv = np.float32(ROPE_BASE) ** (-np.arange(nfreq, dtype=np.float32) / np.float32(nfreq))
    ang_r = rows.astype(np.float32)[:, None] * inv
    ang_c = cols.astype(np.float32)[:, None] * inv
    cos_h = np.concatenate([np.cos(ang_r)] * 2 + [np.cos(ang_c)] * 2, axis=1)
    sin_h = np.concatenate([-np.sin(ang_r), np.sin(ang_r), -np.sin(ang_c), np.sin(ang_c)], axis=1)
    return (jnp.asarray(np.concatenate([cos_h] * 2, axis=1), F32),
            jnp.asarray(np.concatenate([sin_h] * 2, axis=1), F32))


def _token_mixers(u_l, u_c, p, layer, tables, need_ctx_out):
    uml_l, ussd_l, uatt_l, g_l = u_l
    uml_c, ussd_c, uatt_c, g_c = u_c
    pvec = _gate_rows(p['ml_b_i'][layer], p['ml_b_f'][layer], p['ssd_dt_bias'][layer], p['ssd_a_log'][layer])
    m_l, m_c = _mlstm_mixer(uml_l, uml_c, g_l, g_c, pvec, p['ml_norm_g'][layer][None], need_ctx_out)
    s_l, s_c = _ssd_mixer(ussd_l, ussd_c, g_l, g_c, pvec, p['ssd_conv_w'][layer], p['ssd_conv_b'][layer][None],
                          jnp.repeat(p['ssd_d'][layer], SSD_HEAD_DIM)[None], p['ssd_norm_g'][layer][None],
                          need_ctx_out)
    a_l, a_c = _attn_mixer(uatt_l, uatt_c, tables['cos'], tables['sin'],
                           jnp.tile(p['att_qn_g'][layer], 2)[None], jnp.tile(p['att_kn_g'][layer], 2)[None],
                           p['att_sink'][layer], need_ctx_out)
    return (m_l, s_l, a_l), (m_c, s_c, a_c)


def kernel(x, c, ctx, c_ctx, w_mod, b_mod, norm1_g, w_in, ml_b_i, ml_b_f, ml_norm_g, ssd_conv_w, ssd_conv_b,
           ssd_a_log, ssd_dt_bias, ssd_d, ssd_norm_g, att_qn_g, att_kn_g, att_sink, w_out, norm2_g, w_up, w_down):
    p = dict(ml_b_i=ml_b_i, ml_b_f=ml_b_f, ml_norm_g=ml_norm_g, ssd_conv_w=ssd_conv_w, ssd_conv_b=ssd_conv_b,
             ssd_a_log=ssd_a_log, ssd_dt_bias=ssd_dt_bias, ssd_d=ssd_d, ssd_norm_g=ssd_norm_g,
             att_qn_g=att_qn_g, att_kn_g=att_kn_g, att_sink=att_sink)
    bsz, n_lat, d = x.shape
    n_ctx = ctx.shape[1]
    depth = w_mod.shape[0]
    mod_rows = 16
    cc = jnp.concatenate([c, c_ctx[None], jnp.zeros((mod_rows - bsz - 1, d), F32)], axis=0)
    mods = _modulation(cc, w_mod, b_mod).reshape(depth, mod_rows, 6, d)
    cos_t, sin_t = _rope_tables(n_lat)
    tables = dict(cos=cos_t, sin=sin_t)
    lat_row = lambda b: b
    ctx_row = lambda b: bsz
    h_ctx = ctx
    w = _permute_weights(w_in)
    wo, wup, wdn = w_out.astype(BF16), w_up.astype(BF16), w_down.astype(BF16)
    g1, g2 = norm1_g[:, None, :], norm2_g[:, None, :]
    for layer in range(depth):
        need_ctx_out = layer < depth - 1
        u_l = _in_proj(x, mods, lat_row, g1, w, layer, tm=512)
        u_c = _in_proj(h_ctx, mods, ctx_row, g1, w, layer, tm=256)
        y_l, y_c = _token_mixers(u_l, u_c, p, layer, tables, need_ctx_out)
        x = _post(x, *y_l, mods, lat_row, g2, wo, wup, wdn, layer, tm=512)
        if need_ctx_out:
            h_ctx = _post(h_ctx, *y_c, mods, ctx_row, g2, wo, wup, wdn, layer, tm=256)
    return x
```

```python
import functools
import math

import numpy as np
import jax
import jax.numpy as jnp
from jax import lax
from jax.experimental import pallas as pl
from jax.experimental.pallas import tpu as pltpu

F32 = jnp.float32
BF16 = jnp.bfloat16

EPS = 1e-6
GRID_W = 64
N_DIR = 2
ML_HEADS = 4
ML_HEAD_DIM = 128
ML_WIDTH = ML_HEADS * ML_HEAD_DIM
SSD_HEADS = 8
SSD_HEAD_DIM = 64
SSD_WIDTH = SSD_HEADS * SSD_HEAD_DIM
SSD_STATE = 64
SSD_GROUPS = 2
SSD_XBC = SSD_WIDTH + 2 * SSD_GROUPS * SSD_STATE
ATT_Q_HEADS = 8
ATT_KV_HEADS = 2
ATT_HEAD_DIM = 64
ATT_WIDTH = ATT_Q_HEADS * ATT_HEAD_DIM
ATT_KV_WIDTH = ATT_KV_HEADS * ATT_HEAD_DIM
WINDOW = 128
ROPE_BASE = 10000.0
MIX_WIDTH = ML_WIDTH + SSD_WIDTH + ATT_WIDTH

LANES = 128
CHUNK = 128
VMEM_LIMIT = 56 * 1024 * 1024
UNROLL = 16
UNROLL_LIGHT = 8
ATT_STACK = 4
PREP_GROUP = 8

U_ML = 4 * ML_WIDTH
U_SSD = SSD_WIDTH + SSD_XBC
U_ATT = ATT_WIDTH + 2 * ATT_KV_WIDTH
U_GATE = LANES
U_TOTAL = U_ML + U_SSD + U_ATT + U_GATE
DT_LANE0 = N_DIR * ML_HEADS
LI_LANE0 = DT_LANE0 + N_DIR * SSD_HEADS

ML_SCALE = ML_HEAD_DIM ** -0.5
ATT_SCALE = ATT_HEAD_DIM ** -0.5


def _dot(a, b):
    return jnp.dot(a, b, preferred_element_type=F32)


def _split2(x):
    hi = x.astype(BF16)
    lo = (x - hi.astype(F32)).astype(BF16)
    return hi, lo


def _split3(x):
    hi = x.astype(BF16)
    r = x - hi.astype(F32)
    mid = r.astype(BF16)
    lo = (r - mid.astype(F32)).astype(BF16)
    return hi, mid, lo


def _row_cumsums(tris, x):
    rows = x.shape[0]
    parts = _split3(jnp.concatenate([x, jnp.zeros_like(x)], axis=0))
    return [(_dot(parts[0], tri) + _dot(parts[1], tri) + _dot(parts[2], tri))[0:rows] for tri in tris]


def _sigmoid(x):
    return 1.0 / (1.0 + jnp.exp2(x * (-1.0 / math.log(2.0))))


def _log_sigmoid(x):
    return jnp.minimum(x, 0.0) - jnp.log1p(jnp.exp(-jnp.abs(x)))


def _softplus(x):
    return jnp.maximum(x, 0.0) + jnp.log1p(jnp.exp(-jnp.abs(x)))


def _iota(shape, dim):
    return lax.broadcasted_iota(jnp.int32, shape, dim)


def _chunk_rows(c):
    return pl.ds(pl.multiple_of(c * CHUNK, CHUNK), CHUNK)


def _mod_kernel(s_ref, w_ref, b_ref, o_ref):
    s = s_ref[...]
    s = s * _sigmoid(s)
    w = w_ref[...]
    s_hi, s_lo = _split2(s)
    w_hi, w_lo = _split2(w)
    o_ref[...] = _dot(s_hi, w_hi) + (_dot(s_hi, w_lo) + _dot(s_lo, w_hi)) + b_ref[...]


def _modulation(cc, w_mod, b_mod):
    depth, d, d6 = w_mod.shape
    r = cc.shape[0]
    bn = 2048
    return pl.pallas_call(
        _mod_kernel,
        grid=(depth, d6 // bn),
        in_specs=[
            pl.BlockSpec((r, d), lambda l, j: (0, 0)),
            pl.BlockSpec((None, d, bn), lambda l, j: (l, 0, j)),
            pl.BlockSpec((None, 1, bn), lambda l, j: (l, 0, j)),
        ],
        out_specs=pl.BlockSpec((None, r, bn), lambda l, j: (l, 0, j)),
        out_shape=jax.ShapeDtypeStruct((depth, r, d6), F32),
        compiler_params=pltpu.CompilerParams(
            dimension_semantics=("arbitrary", "arbitrary"), vmem_limit_bytes=VMEM_LIMIT),
        name="modulation",
    )(cc, w_mod, b_mod.reshape(depth, 1, d6))


def _inproj_kernel(x_ref, mod_ref, g_ref, w_ref, oml_ref, ossd_ref, oatt_ref, og_ref):
    x = x_ref[...]
    ms = jnp.mean(x * x, axis=-1, keepdims=True)
    xn = x * lax.rsqrt(ms + EPS)
    h = xn * (g_ref[...] * (1.0 + mod_ref[1:2, :])) + mod_ref[0:1, :]
    hb = h.astype(BF16)
    step = 512
    off = 0
    for o_ref, width in ((oml_ref, U_ML), (ossd_ref, U_SSD), (oatt_ref, U_ATT), (og_ref, U_GATE)):
        for j in range(0, width, step):
            wj = min(step, width - j)
            o_ref[:, j:j + wj] = _dot(hb, w_ref[:, off + j:off + j + wj]).astype(o_ref.dtype)
        off += width


def _in_proj(x, mods, mod_row, g1, w, layer, tm):
    bsz, t, d = x.shape
    widths = (U_ML, U_SSD, U_ATT, U_GATE)
    dtypes = (BF16, BF16, BF16, F32)
    return pl.pallas_call(
        _inproj_kernel,
        grid=(bsz, t // tm),
        in_specs=[
            pl.BlockSpec((None, tm, d), lambda b, i: (b, i, 0)),
            pl.BlockSpec((None, None, 6, d), lambda b, i: (layer, mod_row(b), 0, 0)),
            pl.BlockSpec((None, 1, d), lambda b, i: (layer, 0, 0)),
            pl.BlockSpec((None, d, U_TOTAL), lambda b, i: (layer, 0, 0), pipeline_mode=pl.Buffered(1)),
        ],
        out_specs=[pl.BlockSpec((None, tm, wd), lambda b, i: (b, i, 0)) for wd in widths],
        out_shape=[jax.ShapeDtypeStruct((bsz, t, wd), dt) for wd, dt in zip(widths, dtypes)],
        compiler_params=pltpu.CompilerParams(
            dimension_semantics=("arbitrary", "arbitrary"), vmem_limit_bytes=VMEM_LIMIT),
        name="in_proj",
    )(x, mods, g1, w)


def _post_kernel(x_ref, yml_ref, yssd_ref, yatt_ref, mod_ref, g_ref, wo_ref, wup_ref, wdn_ref, o_ref):
    proj = (_dot(yml_ref[...], wo_ref[0:ML_WIDTH, :])
            + _dot(yssd_ref[...], wo_ref[ML_WIDTH:ML_WIDTH + SSD_WIDTH, :])
            + _dot(yatt_ref[...], wo_ref[ML_WIDTH + SSD_WIDTH:MIX_WIDTH, :]))
    x1 = x_ref[...] + mod_ref[2:3, :] * proj
    ms = jnp.mean(x1 * x1, axis=-1, keepdims=True)
    xn = x1 * lax.rsqrt(ms + EPS)
    hb = (xn * (g_ref[...] * (1.0 + mod_ref[4:5, :])) + mod_ref[3:4, :]).astype(BF16)
    d_ff = wup_ref.shape[1]
    step = 1024
    acc = None
    for j in range(0, d_ff, step):
        up = jnp.maximum(_dot(hb, wup_ref[:, j:j + step]), 0.0)
        part = _dot((up * up).astype(BF16), wdn_ref[j:j + step, :])
        acc = part if acc is None else acc + part
    o_ref[...] = x1 + mod_ref[5:6, :] * acc


def _post(x, yml, yssd, yatt, mods, mod_row, g2, wo, wup, wdn, layer, tm):
    bsz, t, d = x.shape
    d_ff = wup.shape[2]
    const = dict(pipeline_mode=pl.Buffered(1))
    return pl.pallas_call(
        _post_kernel,
        grid=(bsz, t // tm),
        in_specs=[
            pl.BlockSpec((None, tm, d), lambda b, i: (b, i, 0)),
            pl.BlockSpec((None, tm, ML_WIDTH), lambda b, i: (b, i, 0)),
            pl.BlockSpec((None, tm, SSD_WIDTH), lambda b, i: (b, i, 0)),
            pl.BlockSpec((None, tm, ATT_WIDTH), lambda b, i: (b, i, 0)),
            pl.BlockSpec((None, None, 6, d), lambda b, i: (layer, mod_row(b), 0, 0)),
            pl.BlockSpec((None, 1, d), lambda b, i: (layer, 0, 0)),
            pl.BlockSpec((None, MIX_WIDTH, d), lambda b, i: (layer, 0, 0), **const),
            pl.BlockSpec((None, d, d_ff), lambda b, i: (layer, 0, 0), **const),
            pl.BlockSpec((None, d_ff, d), lambda b, i: (layer, 0, 0), **const),
        ],
        out_specs=pl.BlockSpec((None, tm, d), lambda b, i: (b, i, 0)),
        out_shape=jax.ShapeDtypeStruct((bsz, t, d), F32),
        compiler_params=pltpu.CompilerParams(
            dimension_semantics=("arbitrary", "arbitrary"), vmem_limit_bytes=VMEM_LIMIT),
        name="post",
    )(x, yml, yssd, yatt, mods, g2, wo, wup, wdn)


ML_GATES = N_DIR * ML_HEADS


def _mlstm_kernel(ul_ref, uc_ref, gl_ref, gc_ref, pv_ref, ng_ref, ex_ref, *rest,
                  need_ctx_out, n_lat, n_ctx):
    if need_ctx_out:
        yl_ref, yc_ref, arow, packed, ends, mprev, cbk, crun = rest
    else:
        yl_ref, arow, packed, ends, mprev, cbk, crun = rest
        yc_ref = None
    ncl, ncc = n_lat // CHUNK, n_ctx // CHUNK
    nct = ncl + ncc
    nh = ML_HEADS
    ti = _iota((CHUNK, CHUNK), 0)
    si = _iota((CHUNK, CHUNK), 1)
    low = si <= ti
    upp = si >= ti
    tri_low = jnp.where(low, 1.0, 0.0).astype(BF16)
    tri_upp = jnp.where(upp, 1.0, 0.0).astype(BF16)
    fwd_row = _iota((ML_GATES, CHUNK), 0) < nh
    bias = pv_ref[0:1, :]
    neg_inf = -jnp.inf
    ones_blk = jnp.ones((CHUNK, LANES), BF16)

    def prep_group(g_ref, c0, n, goff):
        def per_chunk(x):
            return [x[:, i * CHUNK:(i + 1) * CHUNK] for i in range(n)]

        def wide(xs):
            return xs[0] if n == 1 else jnp.concatenate(xs, axis=1)

        gts = [(g_ref[(c0 + i) * CHUNK:(c0 + i + 1) * CHUNK, :] + bias).T for i in range(n)]
        lf = _log_sigmoid(wide([gt[0:ML_GATES] for gt in gts]))
        li = wide([gt[LI_LANE0:LI_LANE0 + ML_GATES] for gt in gts])
        sums = [_row_cumsums((tri_upp, tri_low), x) for x in per_chunk(lf)]
        fwd = _iota((ML_GATES, n * CHUNK), 0) < nh
        tok = _iota((ML_GATES, n * CHUNK), 1) & (CHUNK - 1)
        b = jnp.where(fwd, wide([s[0] for s in sums]), wide([s[1] for s in sums]))
        a = li - b
        cm = a
        k = 1
        while k < CHUNK:
            before = jnp.where(tok >= k, pltpu.roll(cm, k, 1), neg_inf)
            after = jnp.where(tok < CHUNK - k, pltpu.roll(cm, n * CHUNK - k, 1), neg_inf)
            cm = jnp.maximum(cm, jnp.where(fwd, before, after))
            k *= 2
        parts = [p.astype(F32) for p in _split3(cm) + _split3(b)]
        parts.append(jnp.zeros((CHUNK - len(parts) * ML_GATES, n * CHUNK), F32))
        stacked = jnp.concatenate(parts, axis=0)
        for i in range(n):
            cg = goff + c0 + i
            cols = slice(i * CHUNK, (i + 1) * CHUNK)
            arow[cg] = a[:, cols]
            packed[cg] = stacked[:, cols].T.astype(BF16)
            for plane, x in enumerate((cm, b)):
                last = x[:, (i + 1) * CHUNK - 1:(i + 1) * CHUNK]
                first = x[:, i * CHUNK:i * CHUNK + 1]
                ends[cg, plane] = jnp.broadcast_to(jnp.where(fwd_row[:, 0:1], last, first), (ML_GATES, CHUNK))

    def prep(g_ref, nch, goff):
        for c0 in range(0, nch, PREP_GROUP):
            prep_group(g_ref, c0, min(PREP_GROUP, nch - c0), goff)

    prep(gc_ref, ncc, 0)
    prep(gl_ref, ncl, ncc)

    def m_scan(lo, hi, reverse, m0):
        def body(i, m):
            cg = hi - 1 - i if reverse else lo + i
            mprev[cg, int(reverse)] = m
            return ends[cg, 1] + jnp.maximum(m, ends[cg, 0])
        return lax.fori_loop(0, hi - lo, body, m0)

    m_zero = jnp.zeros((ML_GATES, CHUNK), F32)
    m_scan(0, nct, False, m_zero)
    m_scan(ncc, nct, True, m_scan(0, ncc, True, m_zero))

    def load_kv(u_ref, rows, h):
        k = u_ref[rows, ML_WIDTH + h * LANES:ML_WIDTH + (h + 1) * LANES]
        v = u_ref[rows, 2 * ML_WIDTH + h * LANES:2 * ML_WIDTH + (h + 1) * LANES]
        return k.T, jnp.concatenate([v, ones_blk], axis=1)

    def update(d, h, cg, kt, v1):
        j = d * nh + h
        m_prev = mprev[cg, d, j:j + 1, 0:1]
        m_end = jnp.maximum(m_prev, ends[cg, 0, j:j + 1, 0:1])
        w_row = jnp.exp(arow[cg, j:j + 1, :] - m_end) * ML_SCALE
        ktw = (kt.astype(F32) * w_row).astype(BF16)
        crun[d, h] = jnp.exp(m_prev - m_end) * crun[d, h] + _dot(ktw, v1)

    def bwd_pass(u_ref, nch, goff, keep):
        def body(i, carry):
            c = nch - 1 - i
            rows = _chunk_rows(c)
            for h in range(nh):
                if keep:
                    cbk[goff + c, h] = crun[1, h].astype(BF16)
                kt, v1 = load_kv(u_ref, rows, h)
                update(1, h, goff + c, kt, v1)
            return carry
        lax.fori_loop(0, nch, body, 0, unroll=UNROLL_LIGHT)

    def fwd_pass(u_ref, y_ref, nch, goff):
        def body(c, carry):
            rows = _chunk_rows(c)
            cg = goff + c
            for h in range(nh):
                kt, v1 = load_kv(u_ref, rows, h)
                if y_ref is not None:
                    q = u_ref[rows, h * LANES:(h + 1) * LANES]
                    o = u_ref[rows, 3 * ML_WIDTH + h * LANES:3 * ML_WIDTH + (h + 1) * LANES]
                    sqk = _dot(q, kt) * ML_SCALE
                    qc = _dot(q, jnp.concatenate([crun[0, h].astype(BF16), cbk[cg, h]], axis=1))
                    s_dirs, stats = [], []
                    for d, mask in ((0, low), (1, upp)):
                        j = d * nh + h
                        rep = _dot(packed[cg], ex_ref[:, 2 * j * LANES:2 * (j + 1) * LANES])
                        m_prev = mprev[cg, d, j:j + 1, 0:1]
                        m_t = jnp.maximum(rep[:, 0:LANES], m_prev)
                        s_dirs.append(sqk * jnp.exp(jnp.where(mask, arow[cg, j:j + 1, :] - m_t, neg_inf)))
                        stats.append((m_prev, m_t, rep[:, LANES:2 * LANES]))
                    sv = _dot(jnp.concatenate(s_dirs, axis=0).astype(BF16), v1)
                    hsum = None
                    for d, (m_prev, m_t, b_t) in enumerate(stats):
                        g = jnp.exp(m_prev - m_t)
                        sv_d = sv[d * CHUNK:(d + 1) * CHUNK, :]
                        qc_d = qc[:, 2 * d * LANES:2 * (d + 1) * LANES]
                        num = sv_d[:, 0:LANES] + g * qc_d[:, 0:LANES]
                        den = sv_d[:, LANES:2 * LANES] + g * qc_d[:, LANES:2 * LANES]
                        hd = num / jnp.maximum(jnp.abs(den), jnp.exp(-(b_t + m_t)))
                        hsum = hd if hsum is None else hsum + hd
                    ms = jnp.mean(hsum * hsum, axis=1, keepdims=True)
                    y = hsum * lax.rsqrt(ms + EPS) * ng_ref[:, h * LANES:(h + 1) * LANES]
                    y = y * _sigmoid(o.astype(F32))
                    y_ref[rows, h * LANES:(h + 1) * LANES] = y.astype(y_ref.dtype)
                update(0, h, cg, kt, v1)
            return carry
        lax.fori_loop(0, nch, body, 0, unroll=UNROLL)

    crun[...] = jnp.zeros(crun.shape, F32)
    bwd_pass(uc_ref, ncc, 0, need_ctx_out)
    bwd_pass(ul_ref, ncl, ncc, True)
    fwd_pass(uc_ref, yc_ref, ncc, 0)
    fwd_pass(ul_ref, yl_ref, ncl, ncc)


def _ml_expand_table():
    expand = np.zeros((LANES, ML_GATES * 2 * LANES), np.float32)
    for j in range(ML_GATES):
        for i in range(6):
            half = i // 3
            expand[ML_GATES * i + j, (2 * j + half) * LANES:(2 * j + half + 1) * LANES] = 1.0
    return jnp.asarray(expand, BF16)


def _mlstm_mixer(uml_l, uml_c, g_l, g_c, pvec, norm_g, need_ctx_out):
    expand = _ml_expand_table()
    bsz, n_lat, _ = uml_l.shape
    n_ctx = uml_c.shape[1]
    nct = (n_lat + n_ctx) // CHUNK
    kern = functools.partial(_mlstm_kernel, need_ctx_out=need_ctx_out, n_lat=n_lat, n_ctx=n_ctx)
    out_specs = [pl.BlockSpec((None, n_lat, ML_WIDTH), lambda b: (b, 0, 0))]
    out_shape = [jax.ShapeDtypeStruct((bsz, n_lat, ML_WIDTH), BF16)]
    if need_ctx_out:
        out_specs.append(pl.BlockSpec((None, n_ctx, ML_WIDTH), lambda b: (b, 0, 0)))
        out_shape.append(jax.ShapeDtypeStruct((bsz, n_ctx, ML_WIDTH), BF16))
    outs = pl.pallas_call(
        kern,
        grid=(bsz,),
        in_specs=[
            pl.BlockSpec((None, n_lat, U_ML), lambda b: (b, 0, 0)),
            pl.BlockSpec((None, n_ctx, U_ML), lambda b: (b, 0, 0)),
            pl.BlockSpec((None, n_lat, U_GATE), lambda b: (b, 0, 0)),
            pl.BlockSpec((None, n_ctx, U_GATE), lambda b: (b, 0, 0)),
            pl.BlockSpec((8, LANES), lambda b: (0, 0)),
            pl.BlockSpec((1, ML_WIDTH), lambda b: (0, 0)),
            pl.BlockSpec(expand.shape, lambda b: (0, 0)),
        ],
        out_specs=out_specs,
        out_shape=out_shape,
        scratch_shapes=[
            pltpu.VMEM((nct, ML_GATES, CHUNK), F32),
            pltpu.VMEM((nct, CHUNK, LANES), BF16),
            pltpu.VMEM((nct, 2, ML_GATES, CHUNK), F32),
            pltpu.VMEM((nct, N_DIR, ML_GATES, CHUNK), F32),
            pltpu.VMEM((nct, ML_HEADS, ML_HEAD_DIM, 2 * LANES), BF16),
            pltpu.VMEM((N_DIR, ML_HEADS, ML_HEAD_DIM, 2 * LANES), F32),
        ],
        compiler_params=pltpu.CompilerParams(
            dimension_semantics=("arbitrary",), vmem_limit_bytes=VMEM_LIMIT),
        name="mlstm",
    )(uml_l, uml_c, g_l, g_c, pvec, norm_g, expand)
    return (outs[0], outs[1]) if need_ctx_out else (outs[0], None)


def _ssd_kernel(ul_ref, uc_ref, gl_ref, gc_ref, pv_ref, cw_ref, cb_ref, dsk_ref, ng_ref, ex_ref, *rest,
                need_ctx_out, n_lat, n_ctx):
    if need_ctx_out:
        yl_ref, yc_ref, xact, bt, acum, packed, act, dtt, hbk, hrun = rest
    else:
        yl_ref, xact, bt, acum, packed, act, dtt, hbk, hrun = rest
        yc_ref = None
    ncl, ncc = n_lat // CHUNK, n_ctx // CHUNK
    nh, hg = SSD_HEADS, SSD_HEADS // SSD_GROUPS
    gw = hg * SSD_HEAD_DIM
    gn = SSD_GROUPS * SSD_STATE
    ti = _iota((CHUNK, CHUNK), 0)
    si = _iota((CHUNK, CHUNK), 1)
    low_strict = si < ti
    upp_strict = si > ti
    tri_low = jnp.where(si <= ti, 1.0, 0.0).astype(BF16)
    tri_upp = jnp.where(si >= ti, 1.0, 0.0).astype(BF16)
    lane = _iota((CHUNK, LANES), 1)
    nj = N_DIR * nh
    ac_lane0 = 6 * nj
    fwd_row = _iota((nj, CHUNK), 0) < nh
    bias0 = pv_ref[0:1, :]
    a_rows = jnp.broadcast_to(-jnp.exp(pv_ref[2:3, :]), (LANES, LANES)).T[DT_LANE0:DT_LANE0 + nj]
    shift_down = jnp.where(si == ti - 1, 1.0, 0.0).astype(BF16)
    shift_up = jnp.where(si == ti + 1, 1.0, 0.0).astype(BF16)
    lane_group0 = lane < SSD_STATE
    row_group0 = _iota((gn, CHUNK), 0) < SSD_STATE

    def prep(u_ref, g_ref, nch, goff, nrows):
        def body(c, carry):
            r = pl.multiple_of(c * CHUNK, CHUNK)
            cur = u_ref[pl.ds(r, CHUNK), SSD_WIDTH:U_SSD]
            rp = pl.multiple_of(jnp.maximum(r - CHUNK, 0), CHUNK)
            rn = pl.multiple_of(jnp.minimum(r + CHUNK, nrows - CHUNK), CHUNK)
            prev = u_ref[pl.ds(rp, CHUNK), SSD_WIDTH:U_SSD]
            nxt = u_ref[pl.ds(rn, CHUNK), SSD_WIDTH:U_SSD]
            from_prev = jnp.where((ti == 0) & (si == CHUNK - 1) & (c > 0), 1.0, 0.0).astype(BF16)
            from_next = jnp.where((ti == CHUNK - 1) & (si == 0) & (c < nch - 1), 1.0, 0.0).astype(BF16)
            xm = _dot(jnp.concatenate([shift_down, from_prev], axis=1), jnp.concatenate([cur, prev], axis=0))
            xp = _dot(jnp.concatenate([shift_up, from_next], axis=1), jnp.concatenate([cur, nxt], axis=0))
            conv = (cw_ref[0:1, :] * xm + cw_ref[1:2, :] * cur.astype(F32) + cw_ref[2:3, :] * xp
                    + cb_ref[...])
            xa = (conv * _sigmoid(conv)).astype(BF16)
            xact[goff + c] = xa
            bt[goff + c] = xa[:, SSD_WIDTH:SSD_WIDTH + gn].T

            gt = (g_ref[pl.ds(r, CHUNK), :] + bias0).T
            dt = _softplus(gt[DT_LANE0:DT_LANE0 + nj])
            prefix, suffix = _row_cumsums((tri_upp, tri_low), dt * a_rows)
            ac = jnp.where(fwd_row, prefix, suffix)
            a_end = jnp.where(fwd_row[:, 0:1], ac[:, CHUNK - 1:CHUNK], ac[:, 0:1])
            act[goff + c] = ac - jnp.log(dt)
            dtt[goff + c] = dt
            parts = [p.astype(F32) for p in _split3(jnp.exp(ac)) + _split3(jnp.exp(a_end - ac) * dt)]
            parts += [ac, jnp.zeros((CHUNK - 7 * nj, CHUNK), F32)]
            tile = jnp.concatenate(parts, axis=0).T
            acum[goff + c] = tile
            packed[goff + c] = tile.astype(BF16)
            return carry
        lax.fori_loop(0, nch, body, 0, unroll=UNROLL_LIGHT)

    prep(uc_ref, gc_ref, ncc, 0, n_ctx)
    prep(ul_ref, gl_ref, ncl, ncc, n_lat)

    def e_cols(d, g):
        return slice(d * SSD_WIDTH + g * gw, d * SSD_WIDTH + (g + 1) * gw)

    def w_cols(d, g):
        return slice((N_DIR + d) * SSD_WIDTH + g * gw, (N_DIR + d) * SSD_WIDTH + (g + 1) * gw)

    def update_state(d, cg):
        pk = packed[cg]
        xa = xact[cg]
        b_t = bt[cg]
        zero = jnp.zeros_like(b_t)
        upd = []
        for g in range(SSD_GROUPS):
            wx = (_dot(pk, ex_ref[:, w_cols(d, g)]) * xa[:, g * gw:(g + 1) * gw].astype(F32)).astype(BF16)
            upd.append(_dot(jnp.where(row_group0 == (g == 0), b_t, zero), wx))
        tail, last = (slice(CHUNK - 16, CHUNK), 15) if d == 0 else (slice(0, 16), 0)
        decay = _dot(pk[tail], ex_ref[:, d * SSD_WIDTH:(d + 1) * SSD_WIDTH])[last:last + 1, :]
        hrun[d] = decay * hrun[d] + jnp.concatenate(upd, axis=1)

    def bwd_pass(nch, goff, keep):
        def body(i, carry):
            cg = goff + nch - 1 - i
            if keep:
                hbk[cg] = hrun[1].astype(BF16)
            update_state(1, cg)
            return carry
        lax.fori_loop(0, nch, body, 0, unroll=UNROLL_LIGHT)

    def fwd_pass(u_ref, y_ref, nch, goff):
        def body(c, carry):
            cg = goff + c
            if y_ref is not None:
                rows = _chunk_rows(c)
                xa = xact[cg]
                x = xa[:, 0:SSD_WIDTH]
                b_t = bt[cg]
                c_all = xa[:, SSD_WIDTH + gn:SSD_XBC]
                ac = acum[cg]
                ac_t = act[cg]
                dt_t = dtt[cg]
                lane_w = _iota((CHUNK, gw), 1)
                pk = packed[cg]
                yz_groups, ssq = [], None
                for g in range(SSD_GROUPS):
                    cols = slice(g * gw, (g + 1) * gw)
                    c_g = jnp.where(lane_group0 == (g == 0), c_all, jnp.zeros_like(c_all))
                    cb = _dot(c_g, b_t)
                    s_heads = []
                    for hh in range(hg):
                        jf = g * hg + hh
                        jb = jf + nh
                        on_diag = jnp.log(dt_t[jf:jf + 1, :] + dt_t[jb:jb + 1, :])
                        ac_f = ac[:, ac_lane0 + jf:ac_lane0 + jf + 1]
                        ac_b = ac[:, ac_lane0 + jb:ac_lane0 + jb + 1]
                        expo = jnp.where(low_strict, ac_f - ac_t[jf:jf + 1, :],
                                         jnp.where(upp_strict, ac_b - ac_t[jb:jb + 1, :], on_diag))
                        s_heads.append((cb * jnp.exp(expo)).astype(BF16))
                    x_g = x[:, cols]
                    bd = jnp.concatenate(
                        [jnp.where((lane_w >= hh * SSD_HEAD_DIM) & (lane_w < (hh + 1) * SSD_HEAD_DIM),
                                   x_g, jnp.zeros_like(x_g)) for hh in range(hg)], axis=0)
                    y = _dot(jnp.concatenate(s_heads, axis=1), bd)
                    y = y + _dot(c_all, hrun[0, :, cols].astype(BF16)) * _dot(pk, ex_ref[:, e_cols(0, g)])
                    y = y + _dot(c_all, hbk[cg, :, cols]) * _dot(pk, ex_ref[:, e_cols(1, g)])
                    y = y + dsk_ref[:, cols] * x_g.astype(F32)
                    z = u_ref[rows, cols].astype(F32)
                    yz = y * (z * _sigmoid(z))
                    part = jnp.sum(yz * yz, axis=1, keepdims=True)
                    ssq = part if ssq is None else ssq + part
                    yz_groups.append(yz)
                scale = lax.rsqrt(ssq * (1.0 / SSD_WIDTH) + EPS)
                for g, yz in enumerate(yz_groups):
                    cols = slice(g * gw, (g + 1) * gw)
                    y_ref[rows, cols] = (yz * scale * ng_ref[:, cols]).astype(y_ref.dtype)
            update_state(0, cg)
            return carry
        lax.fori_loop(0, nch, body, 0, unroll=UNROLL)

    hrun[...] = jnp.zeros(hrun.shape, F32)
    bwd_pass(ncc, 0, need_ctx_out)
    bwd_pass(ncl, ncc, True)
    fwd_pass(uc_ref, yc_ref, ncc, 0)
    fwd_pass(ul_ref, yl_ref, ncl, ncc)


def _ssd_expand_table():
    nj = N_DIR * SSD_HEADS
    expand = np.zeros((LANES, 2 * N_DIR * SSD_WIDTH), np.float32)
    for i in range(6):
        for d in range(N_DIR):
            for h in range(SSD_HEADS):
                j = d * SSD_HEADS + h
                c0 = ((i // 3) * N_DIR + d) * SSD_WIDTH + h * SSD_HEAD_DIM
                expand[nj * i + j, c0:c0 + SSD_HEAD_DIM] = 1.0
    return jnp.asarray(expand, BF16)


def _ssd_mixer(ussd_l, ussd_c, g_l, g_c, pvec, conv_w, conv_b, d_skip, norm_g, need_ctx_out):
    expand = _ssd_expand_table()
    bsz, n_lat, _ = ussd_l.shape
    n_ctx = ussd_c.shape[1]
    nct = (n_lat + n_ctx) // CHUNK
    gn = SSD_GROUPS * SSD_STATE
    kern = functools.partial(_ssd_kernel, need_ctx_out=need_ctx_out, n_lat=n_lat, n_ctx=n_ctx)
    out_specs = [pl.BlockSpec((None, n_lat, SSD_WIDTH), lambda b: (b, 0, 0))]
    out_shape = [jax.ShapeDtypeStruct((bsz, n_lat, SSD_WIDTH), BF16)]
    if need_ctx_out:
        out_specs.append(pl.BlockSpec((None, n_ctx, SSD_WIDTH), lambda b: (b, 0, 0)))
        out_shape.append(jax.ShapeDtypeStruct((bsz, n_ctx, SSD_WIDTH), BF16))
    outs = pl.pallas_call(
        kern,
        grid=(bsz,),
        in_specs=[
            pl.BlockSpec((None, n_lat, U_SSD), lambda b: (b, 0, 0)),
            pl.BlockSpec((None, n_ctx, U_SSD), lambda b: (b, 0, 0)),
            pl.BlockSpec((None, n_lat, LANES), lambda b: (b, 0, 0)),
            pl.BlockSpec((None, n_ctx, LANES), lambda b: (b, 0, 0)),
            pl.BlockSpec((8, LANES), lambda b: (0, 0)),
            pl.BlockSpec((3, SSD_XBC), lambda b: (0, 0)),
            pl.BlockSpec((1, SSD_XBC), lambda b: (0, 0)),
            pl.BlockSpec((1, SSD_WIDTH), lambda b: (0, 0)),
            pl.BlockSpec((1, SSD_WIDTH), lambda b: (0, 0)),
            pl.BlockSpec(expand.shape, lambda b: (0, 0)),
        ],
        out_specs=out_specs,
        out_shape=out_shape,
        scratch_shapes=[
            pltpu.VMEM((nct, CHUNK, SSD_XBC), BF16),
            pltpu.VMEM((nct, gn, CHUNK), BF16),
            pltpu.VMEM((nct, CHUNK, LANES), F32),
            pltpu.VMEM((nct, CHUNK, LANES), BF16),
            pltpu.VMEM((nct, N_DIR * SSD_HEADS, CHUNK), F32),
            pltpu.VMEM((nct, N_DIR * SSD_HEADS, CHUNK), F32),
            pltpu.VMEM((nct, gn, SSD_WIDTH), BF16),
            pltpu.VMEM((N_DIR, gn, SSD_WIDTH), F32),
        ],
        compiler_params=pltpu.CompilerParams(
            dimension_semantics=("arbitrary",), vmem_limit_bytes=VMEM_LIMIT),
        name="ssd",
    )(ussd_l, ussd_c, g_l, g_c, pvec, conv_w, conv_b, d_skip, norm_g, expand)
    return (outs[0], outs[1]) if need_ctx_out else (outs[0], None)


def _head_norm(x, gain):
    rows = x.shape[0]
    blks = [x[:, j * LANES:(j + 1) * LANES] for j in range(x.shape[1] // LANES)]
    same_head = (_iota((LANES, LANES), 0) < ATT_HEAD_DIM) == (_iota((LANES, LANES), 1) < ATT_HEAD_DIM)
    sq = jnp.concatenate([b * b for b in blks], axis=0).astype(BF16)
    ms = _dot(sq, jnp.where(same_head, 1.0, 0.0).astype(BF16)) * (1.0 / ATT_HEAD_DIM)
    outs = [b * lax.rsqrt(ms[j * rows:(j + 1) * rows] + EPS) * gain for j, b in enumerate(blks)]
    return outs[0] if len(outs) == 1 else jnp.concatenate(outs, axis=1)


def _rope(x, cos_t, sin_t):
    w = x.shape[1]
    reps = w // LANES
    nfreq = ATT_HEAD_DIM // 4
    partner = jnp.where((_iota(x.shape, 1) & (2 * nfreq - 1)) < nfreq,
                        pltpu.roll(x, w - nfreq, 1), pltpu.roll(x, nfreq, 1))
    if reps > 1:
        cos_t = jnp.concatenate([cos_t] * reps, axis=1)
        sin_t = jnp.concatenate([sin_t] * reps, axis=1)
    return x * cos_t + partner * sin_t


def _dup_heads(x):
    lane_lo = _iota(x.shape, 1) < ATT_HEAD_DIM
    sw = pltpu.roll(x, ATT_HEAD_DIM, 1)
    return jnp.concatenate([jnp.where(lane_lo, x, sw), jnp.where(lane_lo, sw, x)], axis=1)


def _attn_kernel(ul_ref, uc_ref, cos_ref, sin_ref, qg_ref, kg_ref, sink_ref, *rest,
                 need_ctx_out, n_lat, n_ctx):
    if need_ctx_out:
        yl_ref, yc_ref, qp, kb, vb, kc, vc, qcp = rest
    else:
        yl_ref, qp, kb, vb, kc, vc = rest
        yc_ref = qcp = None
    ncl, ncc = n_lat // CHUNK, n_ctx // CHUNK
    rep = ATT_Q_HEADS // ATT_KV_HEADS
    kv0 = ATT_WIDTH
    lane_lo = _iota((CHUNK, LANES), 1) < ATT_HEAD_DIM
    neg_inf = -jnp.inf
    qg = qg_ref[...]
    kg = kg_ref[...]

    zeros_blk = jnp.zeros((CHUNK, vb.shape[1]), BF16)
    vb[0:CHUNK, :] = zeros_blk
    vb[CHUNK + n_lat:2 * CHUNK + n_lat, :] = zeros_blk
    kb[0] = jnp.zeros(kb.shape[1:], BF16)
    kb[ncl + 1] = jnp.zeros(kb.shape[1:], BF16)

    def store_keys_transposed(ref, blk, k):
        kt = k.T
        for g in range(ATT_KV_HEADS):
            head = kt[g * ATT_HEAD_DIM:(g + 1) * ATT_HEAD_DIM]
            ref[blk, g * LANES:(g + 1) * LANES, :] = jnp.concatenate([head, head], axis=0).astype(BF16)

    def values_with_ones(v):
        d = _dup_heads(v).astype(BF16)
        ones = jnp.ones((v.shape[0], LANES), BF16)
        return jnp.concatenate([d[:, 0:LANES], ones, d[:, LANES:2 * LANES], ones], axis=1)

    def prep_lat(c, carry):
        rows = _chunk_rows(c)
        cos_t = cos_ref[rows, :]
        sin_t = sin_ref[rows, :]
        q = _rope(_head_norm(ul_ref[rows, 0:ATT_WIDTH].astype(F32), qg), cos_t, sin_t)
        qp[rows, :] = (q * ATT_SCALE).astype(BF16)
        k = _rope(_head_norm(ul_ref[rows, kv0:kv0 + LANES].astype(F32), kg), cos_t, sin_t)
        v = ul_ref[rows, kv0 + LANES:kv0 + 2 * LANES].astype(F32)
        dst = pl.ds(pl.multiple_of((c + 1) * CHUNK, CHUNK), CHUNK)
        store_keys_transposed(kb, c + 1, k)
        vb[dst, :] = values_with_ones(v)
        return carry
    lax.fori_loop(0, ncl, prep_lat, 0, unroll=UNROLL_LIGHT)

    def prep_ctx(c, carry):
        rows = _chunk_rows(c)
        k = _head_norm(uc_ref[rows, kv0:kv0 + LANES].astype(F32), kg)
        v = uc_ref[rows, kv0 + LANES:kv0 + 2 * LANES].astype(F32)
        store_keys_transposed(kc, c, k)
        vc[rows, :] = values_with_ones(v)
        if need_ctx_out:
            q = _head_norm(uc_ref[rows, 0:ATT_WIDTH].astype(F32), qg)
            qcp[rows, :] = (q * ATT_SCALE).astype(BF16)
        return carry
    lax.fori_loop(0, ncc, prep_ctx, 0, unroll=UNROLL)

    def attend(q_ref, y_ref, rows, band):
        for g in range(ATT_KV_HEADS):
            kl = slice(g * LANES, (g + 1) * LANES)
            vl = slice(2 * g * LANES, 2 * (g + 1) * LANES)
            k_ctx = jnp.concatenate([kc[j, kl, :] for j in range(ncc)], axis=1)
            if band is not None:
                blk, band_rows, valid = band
                k_band = jnp.concatenate([kb[blk + j, kl, :] for j in range(3)], axis=1)
            for r0 in range(0, rep, ATT_STACK):
                heads = [g * rep + r0 + i for i in range(ATT_STACK)]
                qms = []
                for hq in heads:
                    qb = q_ref[rows, (hq // 2) * LANES:(hq // 2 + 1) * LANES]
                    qms.append(jnp.where(lane_lo != (hq % 2 == 1), qb, jnp.zeros_like(qb)))
                qs = qms[0] if ATT_STACK == 1 else jnp.concatenate(qms, axis=0)
                s_c = _dot(qs, k_ctx)
                if band is not None:
                    s_b = _dot(qs, k_band)
                ps, sink_terms = [], []
                for i, hq in enumerate(heads):
                    sl = slice(i * CHUNK, (i + 1) * CHUNK)
                    sk = sink_ref[hq]
                    s = s_c[sl]
                    if band is not None:
                        prev_k, own_k, next_k = (slice(j * CHUNK, (j + 1) * CHUNK) for j in range(3))
                        s = jnp.concatenate([s, jnp.where(valid[:, prev_k], s_b[sl, prev_k], neg_inf), s_b[sl, own_k],
                                             jnp.where(valid[:, next_k], s_b[sl, next_k], neg_inf)], axis=1)
                    m = jnp.maximum(jnp.max(s, axis=1, keepdims=True), sk)
                    ps.append(jnp.exp(s - m).astype(BF16))
                    sink_terms.append(jnp.exp(sk - m))
                p = ps[0] if ATT_STACK == 1 else jnp.concatenate(ps, axis=0)
                acc = _dot(p[:, 0:n_ctx], vc[:, vl])
                if band is not None:
                    acc = acc + _dot(p[:, n_ctx:], vb[band_rows, vl])
                for i, hq in enumerate(heads):
                    sl = slice(i * CHUNK, (i + 1) * CHUNK)
                    o = acc[sl, 0:LANES] / (acc[sl, LANES:2 * LANES] + sink_terms[i])
                    if hq % 2 == 1:
                        jb = hq // 2
                        y_ref[rows, jb * LANES:(jb + 1) * LANES] = jnp.where(lane_lo, o_even, o).astype(y_ref.dtype)
                    o_even = o

    assert WINDOW == CHUNK
    qq = _iota((CHUNK, 3 * CHUNK), 0)
    kk = _iota((CHUNK, 3 * CHUNK), 1)
    in_window = (kk >= qq) & (kk <= qq + 2 * WINDOW)

    def lat_block(i, carry):
        kpos = kk + (i - 1) * CHUNK
        valid = in_window & (kpos >= 0) & (kpos < n_lat)
        band_rows = pl.ds(pl.multiple_of(i * CHUNK, CHUNK), 3 * CHUNK)
        attend(qp, yl_ref, _chunk_rows(i), (i, band_rows, valid))
        return carry
    lax.fori_loop(0, ncl, lat_block, 0, unroll=UNROLL)

    if need_ctx_out:
        def ctx_block(i, carry):
            attend(qcp, yc_ref, _chunk_rows(i), None)
            return carry
        lax.fori_loop(0, ncc, ctx_block, 0, unroll=UNROLL)


def _attn_mixer(uatt_l, uatt_c, cos_t, sin_t, qgain, kgain, sink, need_ctx_out):
    bsz, n_lat, _ = uatt_l.shape
    n_ctx = uatt_c.shape[1]
    kern = functools.partial(_attn_kernel, need_ctx_out=need_ctx_out, n_lat=n_lat, n_ctx=n_ctx)
    out_specs = [pl.BlockSpec((None, n_lat, ATT_WIDTH), lambda b: (b, 0, 0))]
    out_shape = [jax.ShapeDtypeStruct((bsz, n_lat, ATT_WIDTH), BF16)]
    scratch = [
        pltpu.VMEM((n_lat, ATT_WIDTH), BF16),
        pltpu.VMEM((n_lat // CHUNK + 2, 2 * LANES, CHUNK), BF16),
        pltpu.VMEM((n_lat + 2 * CHUNK, 4 * LANES), BF16),
        pltpu.VMEM((n_ctx // CHUNK, 2 * LANES, CHUNK), BF16),
        pltpu.VMEM((n_ctx, 4 * LANES), BF16),
    ]
    if need_ctx_out:
        out_specs.append(pl.BlockSpec((None, n_ctx, ATT_WIDTH), lambda b: (b, 0, 0)))
        out_shape.append(jax.ShapeDtypeStruct((bsz, n_ctx, ATT_WIDTH), BF16))
        scratch.append(pltpu.VMEM((n_ctx, ATT_WIDTH), BF16))
    outs = pl.pallas_call(
        kern,
        grid=(bsz,),
        in_specs=[
            pl.BlockSpec((None, n_lat, U_ATT), lambda b: (b, 0, 0)),
            pl.BlockSpec((None, n_ctx, U_ATT), lambda b: (b, 0, 0)),
            pl.BlockSpec((n_lat, LANES), lambda b: (0, 0)),
            pl.BlockSpec((n_lat, LANES), lambda b: (0, 0)),
            pl.BlockSpec((1, LANES), lambda b: (0, 0)),
            pl.BlockSpec((1, LANES), lambda b: (0, 0)),
            pl.BlockSpec(memory_space=pltpu.SMEM),
        ],
        out_specs=out_specs,
        out_shape=out_shape,
        scratch_shapes=scratch,
        compiler_params=pltpu.CompilerParams(
            dimension_semantics=("arbitrary",), vmem_limit_bytes=VMEM_LIMIT),
        name="attention",
    )(uatt_l, uatt_c, cos_t, sin_t, qgain, kgain, sink)
    return (outs[0], outs[1]) if need_ctx_out else (outs[0], None)


def _in_layout():
    names = (('ml_q', ML_WIDTH), ('ml_k', ML_WIDTH), ('ml_v', ML_WIDTH), ('ml_o', ML_WIDTH),
             ('ml_i', N_DIR * ML_HEADS), ('ml_f', N_DIR * ML_HEADS),
             ('ssd_z', SSD_WIDTH), ('ssd_xbc', SSD_XBC), ('ssd_dt', N_DIR * SSD_HEADS),
             ('att_q', ATT_WIDTH), ('att_k', ATT_KV_WIDTH), ('att_v', ATT_KV_WIDTH))
    out, off = {}, 0
    for name, width in names:
        out[name] = (off, off + width)
        off += width
    return out


def _col_map():
    lay = _in_layout()
    take = lambda name: np.arange(lay[name][0], lay[name][1])
    zeros = lambda n: np.full((n,), -1)
    n_f, n_dt, n_i = N_DIR * ML_HEADS, N_DIR * SSD_HEADS, N_DIR * ML_HEADS
    return np.concatenate(
        [take('ml_q'), take('ml_k'), take('ml_v'), take('ml_o'),
         take('ssd_z'), take('ssd_xbc'),
         take('att_q'), take('att_k'), take('att_v'),
         take('ml_f'), take('ssd_dt'), take('ml_i'), zeros(LANES - n_f - n_dt - n_i)])


def _permute_plan():
    src = _col_map()
    plan = []
    for t in range(U_TOTAL // LANES):
        cols = src[t * LANES:(t + 1) * LANES]
        runs, i = [], 0
        while i < LANES:
            j = i
            while j + 1 < LANES and (cols[j + 1] == cols[j] + 1 if cols[j] >= 0 else cols[j + 1] < 0):
                j += 1
            runs.append((int(cols[i]), j - i + 1))
            i = j + 1
        plan.append(tuple(runs))
    return tuple(plan)


def _wprep_kernel(wt_ref, o_ref, *, plan):
    kb = wt_ref.shape[1]
    for t, runs in enumerate(plan):
        pieces = [wt_ref[s:s + n, :] if s >= 0 else jnp.zeros((n, kb), F32) for s, n in runs]
        blk = pieces[0] if len(pieces) == 1 else jnp.concatenate(pieces, axis=0)
        o_ref[:, t * LANES:(t + 1) * LANES] = blk.T.astype(BF16)


def _permute_weights(w_in):
    depth, d, in_width = w_in.shape
    plan = _permute_plan()
    kb = 256
    return pl.pallas_call(
        functools.partial(_wprep_kernel, plan=plan),
        grid=(depth, d // kb),
        in_specs=[pl.BlockSpec((None, in_width, kb), lambda l, i: (l, 0, i))],
        out_specs=pl.BlockSpec((None, kb, U_TOTAL), lambda l, i: (l, i, 0)),
        out_shape=jax.ShapeDtypeStruct((depth, d, U_TOTAL), BF16),
        compiler_params=pltpu.CompilerParams(
            dimension_semantics=("arbitrary", "arbitrary"), vmem_limit_bytes=VMEM_LIMIT),
        name="permute_weights",
    )(jnp.swapaxes(w_in, 1, 2))


def _gate_rows(ml_b_i, ml_b_f, ssd_dt_bias, ssd_a_log):
    n_f, n_dt = N_DIR * ML_HEADS, N_DIR * SSD_HEADS
    row0 = jnp.concatenate([ml_b_f.reshape(-1), ssd_dt_bias.reshape(-1), ml_b_i.reshape(-1),
                            jnp.zeros((LANES - 2 * n_f - n_dt,), F32)])
    row2 = jnp.concatenate([jnp.zeros((n_f,), F32), ssd_a_log.reshape(-1), jnp.zeros((LANES - n_f - n_dt,), F32)])
    zero = jnp.zeros((1, LANES), F32)
    return jnp.concatenate([row0[None], zero, row2[None], jnp.zeros((5, LANES), F32)], axis=0)


def _rope_tables(n_lat):
    nfreq = ATT_HEAD_DIM // 4
    pos = np.arange(n_lat)
    rows, cols = pos // GRID_W, pos % GRID_W
    inv = np.float32(ROPE_BASE) ** (-np.arange(nfreq, dtype=np.float32) / np.float32(nfreq))
    ang_r = rows.astype(np.float32)[:, None] * inv
    ang_c = cols.astype(np.float32)[:, None] * inv
    cos_h = np.concatenate([np.cos(ang_r)] * 2 + [np.cos(ang_c)] * 2, axis=1)
    sin_h = np.concatenate([-np.sin(ang_r), np.sin(ang_r), -np.sin(ang_c), np.sin(ang_c)], axis=1)
    return (jnp.asarray(np.concatenate([cos_h] * 2, axis=1), F32),
            jnp.asarray(np.concatenate([sin_h] * 2, axis=1), F32))


def _token_mixers(u_l, u_c, p, layer, tables, need_ctx_out):
    uml_l, ussd_l, uatt_l, g_l = u_l
    uml_c, ussd_c, uatt_c, g_c = u_c
    pvec = _gate_rows(p['ml_b_i'][layer], p['ml_b_f'][layer], p['ssd_dt_bias'][layer], p['ssd_a_log'][layer])
    m_l, m_c = _mlstm_mixer(uml_l, uml_c, g_l, g_c, pvec, p['ml_norm_g'][layer][None], need_ctx_out)
    s_l, s_c = _ssd_mixer(ussd_l, ussd_c, g_l, g_c, pvec, p['ssd_conv_w'][layer], p['ssd_conv_b'][layer][None],
                          jnp.repeat(p['ssd_d'][layer], SSD_HEAD_DIM)[None], p['ssd_norm_g'][layer][None],
                          need_ctx_out)
    a_l, a_c = _attn_mixer(uatt_l, uatt_c, tables['cos'], tables['sin'],
                           jnp.tile(p['att_qn_g'][layer], 2)[None], jnp.tile(p['att_kn_g'][layer], 2)[None],
                           p['att_sink'][layer], need_ctx_out)
    return (m_l, s_l, a_l), (m_c, s_c, a_c)


def kernel(x, c, ctx, c_ctx, w_mod, b_mod, norm1_g, w_in, ml_b_i, ml_b_f, ml_norm_g, ssd_conv_w, ssd_conv_b,
           ssd_a_log, ssd_dt_bias, ssd_d, ssd_norm_g, att_qn_g, att_kn_g, att_sink, w_out, norm2_g, w_up, w_down):
    p = dict(ml_b_i=ml_b_i, ml_b_f=ml_b_f, ml_norm_g=ml_norm_g, ssd_conv_w=ssd_conv_w, ssd_conv_b=ssd_conv_b,
             ssd_a_log=ssd_a_log, ssd_dt_bias=ssd_dt_bias, ssd_d=ssd_d, ssd_norm_g=ssd_norm_g,
             att_qn_g=att_qn_g, att_kn_g=att_kn_g, att_sink=att_sink)
    bsz, n_lat, d = x.shape
    n_ctx = ctx.shape[1]
    depth = w_mod.shape[0]
    mod_rows = 16
    cc = jnp.concatenate([c, c_ctx[None], jnp.zeros((mod_rows - bsz - 1, d), F32)], axis=0)
    mods = _modulation(cc, w_mod, b_mod).reshape(depth, mod_rows, 6, d)
    cos_t, sin_t = _rope_tables(n_lat)
    tables = dict(cos=cos_t, sin=sin_t)
    lat_row = lambda b: b
    ctx_row = lambda b: bsz
    h_ctx = ctx
    w = _permute_weights(w_in)
    wo, wup, wdn = w_out.astype(BF16), w_up.astype(BF16), w_down.astype(BF16)
    g1, g2 = norm1_g[:, None, :], norm2_g[:, None, :]
    for layer in range(depth):
        need_ctx_out = layer < depth - 1
        u_l = _in_proj(x, mods, lat_row, g1, w, layer, tm=1024)
        u_c = _in_proj(h_ctx, mods, ctx_row, g1, w, layer, tm=256)
        y_l, y_c = _token_mixers(u_l, u_c, p, layer, tables, need_ctx_out)
        x = _post(x, *y_l, mods, lat_row, g2, wo, wup, wdn, layer, tm=512)
        if need_ctx_out:
            h_ctx = _post(h_ctx, *y_c, mods, ctx_row, g2, wo, wup, wdn, layer, tm=256)
    return x
```

```python
import functools
import math

import numpy as np
import jax
import jax.numpy as jnp
from jax import lax
from jax.experimental import pallas as pl
from jax.experimental.pallas import tpu as pltpu

F32 = jnp.float32
BF16 = jnp.bfloat16

EPS = 1e-6
GRID_W = 64
N_DIR = 2
ML_HEADS = 4
ML_HEAD_DIM = 128
ML_WIDTH = ML_HEADS * ML_HEAD_DIM
SSD_HEADS = 8
SSD_HEAD_DIM = 64
SSD_WIDTH = SSD_HEADS * SSD_HEAD_DIM
SSD_STATE = 64
SSD_GROUPS = 2
SSD_XBC = SSD_WIDTH + 2 * SSD_GROUPS * SSD_STATE
ATT_Q_HEADS = 8
ATT_KV_HEADS = 2
ATT_HEAD_DIM = 64
ATT_WIDTH = ATT_Q_HEADS * ATT_HEAD_DIM
ATT_KV_WIDTH = ATT_KV_HEADS * ATT_HEAD_DIM
WINDOW = 128
ROPE_BASE = 10000.0
MIX_WIDTH = ML_WIDTH + SSD_WIDTH + ATT_WIDTH

LANES = 128
CHUNK = 128
VMEM_LIMIT = 56 * 1024 * 1024
UNROLL = 16
ATT_STACK = 4
PREP_GROUP = 8

U_ML = 4 * ML_WIDTH
U_SSD = SSD_WIDTH + SSD_XBC
U_ATT = ATT_WIDTH + 2 * ATT_KV_WIDTH
U_GATE = LANES
U_TOTAL = U_ML + U_SSD + U_ATT + U_GATE
DT_LANE0 = N_DIR * ML_HEADS
LI_LANE0 = DT_LANE0 + N_DIR * SSD_HEADS

ML_SCALE = ML_HEAD_DIM ** -0.5
ATT_SCALE = ATT_HEAD_DIM ** -0.5


def _dot(a, b):
    return jnp.dot(a, b, preferred_element_type=F32)


def _split2(x):
    hi = x.astype(BF16)
    lo = (x - hi.astype(F32)).astype(BF16)
    return hi, lo


def _split3(x):
    hi = x.astype(BF16)
    r = x - hi.astype(F32)
    mid = r.astype(BF16)
    lo = (r - mid.astype(F32)).astype(BF16)
    return hi, mid, lo


def _row_cumsums(tris, x):
    rows = x.shape[0]
    parts = _split3(jnp.concatenate([x, jnp.zeros_like(x)], axis=0))
    return [(_dot(parts[0], tri) + _dot(parts[1], tri) + _dot(parts[2], tri))[0:rows] for tri in tris]


def _sigmoid(x):
    return 1.0 / (1.0 + jnp.exp2(x * (-1.0 / math.log(2.0))))


def _log_sigmoid(x):
    return jnp.minimum(x, 0.0) - jnp.log1p(jnp.exp(-jnp.abs(x)))


def _softplus(x):
    return jnp.maximum(x, 0.0) + jnp.log1p(jnp.exp(-jnp.abs(x)))


def _iota(shape, dim):
    return lax.broadcasted_iota(jnp.int32, shape, dim)


def _chunk_rows(c):
    return pl.ds(pl.multiple_of(c * CHUNK, CHUNK), CHUNK)


def _mod_kernel(s_ref, w_ref, b_ref, o_ref):
    s = s_ref[...]
    s = s * _sigmoid(s)
    w = w_ref[...]
    s_hi, s_lo = _split2(s)
    w_hi, w_lo = _split2(w)
    o_ref[...] = _dot(s_hi, w_hi) + (_dot(s_hi, w_lo) + _dot(s_lo, w_hi)) + b_ref[...]


def _modulation(cc, w_mod, b_mod):
    depth, d, d6 = w_mod.shape
    r = cc.shape[0]
    bn = 2048
    return pl.pallas_call(
        _mod_kernel,
        grid=(depth, d6 // bn),
        in_specs=[
            pl.BlockSpec((r, d), lambda l, j: (0, 0)),
            pl.BlockSpec((None, d, bn), lambda l, j: (l, 0, j)),
            pl.BlockSpec((None, 1, bn), lambda l, j: (l, 0, j)),
        ],
        out_specs=pl.BlockSpec((None, r, bn), lambda l, j: (l, 0, j)),
        out_shape=jax.ShapeDtypeStruct((depth, r, d6), F32),
        compiler_params=pltpu.CompilerParams(
            dimension_semantics=("arbitrary", "arbitrary"), vmem_limit_bytes=VMEM_LIMIT),
        name="modulation",
    )(cc, w_mod, b_mod.reshape(depth, 1, d6))


def _inproj_kernel(x_ref, mod_ref, g_ref, w_ref, oml_ref, ossd_ref, oatt_ref, og_ref):
    x = x_ref[...]
    ms = jnp.mean(x * x, axis=-1, keepdims=True)
    xn = x * lax.rsqrt(ms + EPS)
    h = xn * (g_ref[...] * (1.0 + mod_ref[1:2, :])) + mod_ref[0:1, :]
    hb = h.astype(BF16)
    step = 512
    off = 0
    for o_ref, width in ((oml_ref, U_ML), (ossd_ref, U_SSD), (oatt_ref, U_ATT), (og_ref, U_GATE)):
        for j in range(0, width, step):
            wj = min(step, width - j)
            o_ref[:, j:j + wj] = _dot(hb, w_ref[:, off + j:off + j + wj]).astype(o_ref.dtype)
        off += width


def _in_proj(x, mods, mod_row, g1, w, layer, tm):
    bsz, t, d = x.shape
    widths = (U_ML, U_SSD, U_ATT, U_GATE)
    dtypes = (BF16, BF16, BF16, F32)
    return pl.pallas_call(
        _inproj_kernel,
        grid=(bsz, t // tm),
        in_specs=[
            pl.BlockSpec((None, tm, d), lambda b, i: (b, i, 0)),
            pl.BlockSpec((None, None, 6, d), lambda b, i: (layer, mod_row(b), 0, 0)),
            pl.BlockSpec((None, 1, d), lambda b, i: (layer, 0, 0)),
            pl.BlockSpec((None, d, U_TOTAL), lambda b, i: (layer, 0, 0), pipeline_mode=pl.Buffered(1)),
        ],
        out_specs=[pl.BlockSpec((None, tm, wd), lambda b, i: (b, i, 0)) for wd in widths],
        out_shape=[jax.ShapeDtypeStruct((bsz, t, wd), dt) for wd, dt in zip(widths, dtypes)],
        compiler_params=pltpu.CompilerParams(
            dimension_semantics=("arbitrary", "arbitrary"), vmem_limit_bytes=VMEM_LIMIT),
        name="in_proj",
    )(x, mods, g1, w)


def _post_kernel(x_ref, yml_ref, yssd_ref, yatt_ref, mod_ref, g_ref, wo_ref, wup_ref, wdn_ref, o_ref):
    proj = (_dot(yml_ref[...], wo_ref[0:ML_WIDTH, :])
            + _dot(yssd_ref[...], wo_ref[ML_WIDTH:ML_WIDTH + SSD_WIDTH, :])
            + _dot(yatt_ref[...], wo_ref[ML_WIDTH + SSD_WIDTH:MIX_WIDTH, :]))
    x1 = x_ref[...] + mod_ref[2:3, :] * proj
    ms = jnp.mean(x1 * x1, axis=-1, keepdims=True)
    xn = x1 * lax.rsqrt(ms + EPS)
    hb = (xn * (g_ref[...] * (1.0 + mod_ref[4:5, :])) + mod_ref[3:4, :]).astype(BF16)
    d_ff = wup_ref.shape[1]
    step = 1024
    acc = None
    for j in range(0, d_ff, step):
        up = jnp.maximum(_dot(hb, wup_ref[:, j:j + step]), 0.0)
        part = _dot((up * up).astype(BF16), wdn_ref[j:j + step, :])
        acc = part if acc is None else acc + part
    o_ref[...] = x1 + mod_ref[5:6, :] * acc


def _post(x, yml, yssd, yatt, mods, mod_row, g2, wo, wup, wdn, layer, tm):
    bsz, t, d = x.shape
    d_ff = wup.shape[2]
    const = dict(pipeline_mode=pl.Buffered(1))
    return pl.pallas_call(
        _post_kernel,
        grid=(bsz, t // tm),
        in_specs=[
            pl.BlockSpec((None, tm, d), lambda b, i: (b, i, 0)),
            pl.BlockSpec((None, tm, ML_WIDTH), lambda b, i: (b, i, 0)),
            pl.BlockSpec((None, tm, SSD_WIDTH), lambda b, i: (b, i, 0)),
            pl.BlockSpec((None, tm, ATT_WIDTH), lambda b, i: (b, i, 0)),
            pl.BlockSpec((None, None, 6, d), lambda b, i: (layer, mod_row(b), 0, 0)),
            pl.BlockSpec((None, 1, d), lambda b, i: (layer, 0, 0)),
            pl.BlockSpec((None, MIX_WIDTH, d), lambda b, i: (layer, 0, 0), **const),
            pl.BlockSpec((None, d, d_ff), lambda b, i: (layer, 0, 0), **const),
            pl.BlockSpec((None, d_ff, d), lambda b, i: (layer, 0, 0), **const),
        ],
        out_specs=pl.BlockSpec((None, tm, d), lambda b, i: (b, i, 0)),
        out_shape=jax.ShapeDtypeStruct((bsz, t, d), F32),
        compiler_params=pltpu.CompilerParams(
            dimension_semantics=("arbitrary", "arbitrary"), vmem_limit_bytes=VMEM_LIMIT),
        name="post",
    )(x, yml, yssd, yatt, mods, g2, wo, wup, wdn)


ML_GATES = N_DIR * ML_HEADS


def _mlstm_kernel(ul_ref, uc_ref, gl_ref, gc_ref, pv_ref, ng_ref, ex_ref, *rest,
                  need_ctx_out, n_lat, n_ctx):
    if need_ctx_out:
        yl_ref, yc_ref, arow, packed, ends, mprev, cbk, crun = rest
    else:
        yl_ref, arow, packed, ends, mprev, cbk, crun = rest
        yc_ref = None
    ncl, ncc = n_lat // CHUNK, n_ctx // CHUNK
    nct = ncl + ncc
    nh = ML_HEADS
    ti = _iota((CHUNK, CHUNK), 0)
    si = _iota((CHUNK, CHUNK), 1)
    low = si <= ti
    upp = si >= ti
    tri_low = jnp.where(low, 1.0, 0.0).astype(BF16)
    tri_upp = jnp.where(upp, 1.0, 0.0).astype(BF16)
    fwd_row = _iota((ML_GATES, CHUNK), 0) < nh
    bias = pv_ref[0:1, :]
    neg_inf = -jnp.inf
    ones_blk = jnp.ones((CHUNK, LANES), BF16)

    def prep_group(g_ref, c0, n, goff):
        def per_chunk(x):
            return [x[:, i * CHUNK:(i + 1) * CHUNK] for i in range(n)]

        def wide(xs):
            return xs[0] if n == 1 else jnp.concatenate(xs, axis=1)

        gts = [(g_ref[(c0 + i) * CHUNK:(c0 + i + 1) * CHUNK, :] + bias).T for i in range(n)]
        lf = _log_sigmoid(wide([gt[0:ML_GATES] for gt in gts]))
        li = wide([gt[LI_LANE0:LI_LANE0 + ML_GATES] for gt in gts])
        sums = [_row_cumsums((tri_upp, tri_low), x) for x in per_chunk(lf)]
        fwd = _iota((ML_GATES, n * CHUNK), 0) < nh
        tok = _iota((ML_GATES, n * CHUNK), 1) & (CHUNK - 1)
        b = jnp.where(fwd, wide([s[0] for s in sums]), wide([s[1] for s in sums]))
        a = li - b
        cm = a
        k = 1
        while k < CHUNK:
            before = jnp.where(tok >= k, pltpu.roll(cm, k, 1), neg_inf)
            after = jnp.where(tok < CHUNK - k, pltpu.roll(cm, n * CHUNK - k, 1), neg_inf)
            cm = jnp.maximum(cm, jnp.where(fwd, before, after))
            k *= 2
        parts = [p.astype(F32) for p in _split3(cm) + _split3(b)]
        parts.append(jnp.zeros((CHUNK - len(parts) * ML_GATES, n * CHUNK), F32))
        stacked = jnp.concatenate(parts, axis=0)
        for i in range(n):
            cg = goff + c0 + i
            cols = slice(i * CHUNK, (i + 1) * CHUNK)
            arow[cg] = a[:, cols]
            packed[cg] = stacked[:, cols].T.astype(BF16)
            for plane, x in enumerate((cm, b)):
                last = x[:, (i + 1) * CHUNK - 1:(i + 1) * CHUNK]
                first = x[:, i * CHUNK:i * CHUNK + 1]
                ends[cg, plane] = jnp.broadcast_to(jnp.where(fwd_row[:, 0:1], last, first), (ML_GATES, CHUNK))

    def prep(g_ref, nch, goff):
        for c0 in range(0, nch, PREP_GROUP):
            prep_group(g_ref, c0, min(PREP_GROUP, nch - c0), goff)

    prep(gc_ref, ncc, 0)
    prep(gl_ref, ncl, ncc)

    def m_scan(lo, hi, reverse, m0):
        def body(i, m):
            cg = hi - 1 - i if reverse else lo + i
            mprev[cg, int(reverse)] = m
            return ends[cg, 1] + jnp.maximum(m, ends[cg, 0])
        return lax.fori_loop(0, hi - lo, body, m0)

    m_zero = jnp.zeros((ML_GATES, CHUNK), F32)
    m_scan(0, nct, False, m_zero)
    m_scan(ncc, nct, True, m_scan(0, ncc, True, m_zero))

    def load_kv(u_ref, rows, h):
        k = u_ref[rows, ML_WIDTH + h * LANES:ML_WIDTH + (h + 1) * LANES]
        v = u_ref[rows, 2 * ML_WIDTH + h * LANES:2 * ML_WIDTH + (h + 1) * LANES]
        return k.T, jnp.concatenate([v, ones_blk], axis=1)

    def update(d, h, cg, kt, v1):
        j = d * nh + h
        m_prev = mprev[cg, d, j:j + 1, 0:1]
        m_end = jnp.maximum(m_prev, ends[cg, 0, j:j + 1, 0:1])
        w_row = jnp.exp(arow[cg, j:j + 1, :] - m_end) * ML_SCALE
        ktw = (kt.astype(F32) * w_row).astype(BF16)
        crun[d, h] = jnp.exp(m_prev - m_end) * crun[d, h] + _dot(ktw, v1)

    def bwd_pass(u_ref, nch, goff, keep):
        def body(i, carry):
            c = nch - 1 - i
            rows = _chunk_rows(c)
            for h in range(nh):
                if keep:
                    cbk[goff + c, h] = crun[1, h].astype(BF16)
                kt, v1 = load_kv(u_ref, rows, h)
                update(1, h, goff + c, kt, v1)
            return carry
        lax.fori_loop(0, nch, body, 0, unroll=UNROLL)

    def fwd_pass(u_ref, y_ref, nch, goff):
        def body(c, carry):
            rows = _chunk_rows(c)
            cg = goff + c
            for h in range(nh):
                kt, v1 = load_kv(u_ref, rows, h)
                if y_ref is not None:
                    q = u_ref[rows, h * LANES:(h + 1) * LANES]
                    o = u_ref[rows, 3 * ML_WIDTH + h * LANES:3 * ML_WIDTH + (h + 1) * LANES]
                    sqk = _dot(q, kt) * ML_SCALE
                    qc = _dot(q, jnp.concatenate([crun[0, h].astype(BF16), cbk[cg, h]], axis=1))
                    s_dirs, stats = [], []
                    for d, mask in ((0, low), (1, upp)):
                        j = d * nh + h
                        rep = _dot(packed[cg], ex_ref[:, 2 * j * LANES:2 * (j + 1) * LANES])
                        m_prev = mprev[cg, d, j:j + 1, 0:1]
                        m_t = jnp.maximum(rep[:, 0:LANES], m_prev)
                        s_dirs.append(sqk * jnp.exp(jnp.where(mask, arow[cg, j:j + 1, :] - m_t, neg_inf)))
                        stats.append((m_prev, m_t, rep[:, LANES:2 * LANES]))
                    sv = _dot(jnp.concatenate(s_dirs, axis=0).astype(BF16), v1)
                    hsum = None
                    for d, (m_prev, m_t, b_t) in enumerate(stats):
                        g = jnp.exp(m_prev - m_t)
                        sv_d = sv[d * CHUNK:(d + 1) * CHUNK, :]
                        qc_d = qc[:, 2 * d * LANES:2 * (d + 1) * LANES]
                        num = sv_d[:, 0:LANES] + g * qc_d[:, 0:LANES]
                        den = sv_d[:, LANES:2 * LANES] + g * qc_d[:, LANES:2 * LANES]
                        hd = num / jnp.maximum(jnp.abs(den), jnp.exp(-(b_t + m_t)))
                        hsum = hd if hsum is None else hsum + hd
                    ms = jnp.mean(hsum * hsum, axis=1, keepdims=True)
                    y = hsum * lax.rsqrt(ms + EPS) * ng_ref[:, h * LANES:(h + 1) * LANES]
                    y = y * _sigmoid(o.astype(F32))
                    y_ref[rows, h * LANES:(h + 1) * LANES] = y.astype(y_ref.dtype)
                update(0, h, cg, kt, v1)
            return carry
        lax.fori_loop(0, nch, body, 0, unroll=UNROLL)

    crun[...] = jnp.zeros(crun.shape, F32)
    bwd_pass(uc_ref, ncc, 0, need_ctx_out)
    bwd_pass(ul_ref, ncl, ncc, True)
    fwd_pass(uc_ref, yc_ref, ncc, 0)
    fwd_pass(ul_ref, yl_ref, ncl, ncc)


def _ml_expand_table():
    expand = np.zeros((LANES, ML_GATES * 2 * LANES), np.float32)
    for j in range(ML_GATES):
        for i in range(6):
            half = i // 3
            expand[ML_GATES * i + j, (2 * j + half) * LANES:(2 * j + half + 1) * LANES] = 1.0
    return jnp.asarray(expand, BF16)


def _mlstm_mixer(uml_l, uml_c, g_l, g_c, pvec, norm_g, need_ctx_out):
    expand = _ml_expand_table()
    bsz, n_lat, _ = uml_l.shape
    n_ctx = uml_c.shape[1]
    nct = (n_lat + n_ctx) // CHUNK
    kern = functools.partial(_mlstm_kernel, need_ctx_out=need_ctx_out, n_lat=n_lat, n_ctx=n_ctx)
    out_specs = [pl.BlockSpec((None, n_lat, ML_WIDTH), lambda b: (b, 0, 0))]
    out_shape = [jax.ShapeDtypeStruct((bsz, n_lat, ML_WIDTH), BF16)]
    if need_ctx_out:
        out_specs.append(pl.BlockSpec((None, n_ctx, ML_WIDTH), lambda b: (b, 0, 0)))
        out_shape.append(jax.ShapeDtypeStruct((bsz, n_ctx, ML_WIDTH), BF16))
    outs = pl.pallas_call(
        kern,
        grid=(bsz,),
        in_specs=[
            pl.BlockSpec((None, n_lat, U_ML), lambda b: (b, 0, 0)),
            pl.BlockSpec((None, n_ctx, U_ML), lambda b: (b, 0, 0)),
            pl.BlockSpec((None, n_lat, U_GATE), lambda b: (b, 0, 0)),
            pl.BlockSpec((None, n_ctx, U_GATE), lambda b: (b, 0, 0)),
            pl.BlockSpec((8, LANES), lambda b: (0, 0)),
            pl.BlockSpec((1, ML_WIDTH), lambda b: (0, 0)),
            pl.BlockSpec(expand.shape, lambda b: (0, 0)),
        ],
        out_specs=out_specs,
        out_shape=out_shape,
        scratch_shapes=[
            pltpu.VMEM((nct, ML_GATES, CHUNK), F32),
            pltpu.VMEM((nct, CHUNK, LANES), BF16),
            pltpu.VMEM((nct, 2, ML_GATES, CHUNK), F32),
            pltpu.VMEM((nct, N_DIR, ML_GATES, CHUNK), F32),
            pltpu.VMEM((nct, ML_HEADS, ML_HEAD_DIM, 2 * LANES), BF16),
            pltpu.VMEM((N_DIR, ML_HEADS, ML_HEAD_DIM, 2 * LANES), F32),
        ],
        compiler_params=pltpu.CompilerParams(
            dimension_semantics=("arbitrary",), vmem_limit_bytes=VMEM_LIMIT),
        name="mlstm",
    )(uml_l, uml_c, g_l, g_c, pvec, norm_g, expand)
    return (outs[0], outs[1]) if need_ctx_out else (outs[0], None)


def _ssd_kernel(ul_ref, uc_ref, gl_ref, gc_ref, pv_ref, cw_ref, cb_ref, dsk_ref, ng_ref, ex_ref, *rest,
                need_ctx_out, n_lat, n_ctx):
    if need_ctx_out:
        yl_ref, yc_ref, xact, bt, acum, packed, act, dtt, hbk, hrun = rest
    else:
        yl_ref, xact, bt, acum, packed, act, dtt, hbk, hrun = rest
        yc_ref = None
    ncl, ncc = n_lat // CHUNK, n_ctx // CHUNK
    nh, hg = SSD_HEADS, SSD_HEADS // SSD_GROUPS
    gw = hg * SSD_HEAD_DIM
    gn = SSD_GROUPS * SSD_STATE
    ti = _iota((CHUNK, CHUNK), 0)
    si = _iota((CHUNK, CHUNK), 1)
    low_strict = si < ti
    upp_strict = si > ti
    tri_low = jnp.where(si <= ti, 1.0, 0.0).astype(BF16)
    tri_upp = jnp.where(si >= ti, 1.0, 0.0).astype(BF16)
    lane = _iota((CHUNK, LANES), 1)
    nj = N_DIR * nh
    ac_lane0 = 6 * nj
    fwd_row = _iota((nj, CHUNK), 0) < nh
    bias0 = pv_ref[0:1, :]
    a_rows = jnp.broadcast_to(-jnp.exp(pv_ref[2:3, :]), (LANES, LANES)).T[DT_LANE0:DT_LANE0 + nj]
    shift_down = jnp.where(si == ti - 1, 1.0, 0.0).astype(BF16)
    shift_up = jnp.where(si == ti + 1, 1.0, 0.0).astype(BF16)
    lane_group0 = lane < SSD_STATE
    row_group0 = _iota((gn, CHUNK), 0) < SSD_STATE

    def prep(u_ref, g_ref, nch, goff, nrows):
        def body(c, carry):
            r = pl.multiple_of(c * CHUNK, CHUNK)
            cur = u_ref[pl.ds(r, CHUNK), SSD_WIDTH:U_SSD]
            rp = pl.multiple_of(jnp.maximum(r - CHUNK, 0), CHUNK)
            rn = pl.multiple_of(jnp.minimum(r + CHUNK, nrows - CHUNK), CHUNK)
            prev = u_ref[pl.ds(rp, CHUNK), SSD_WIDTH:U_SSD]
            nxt = u_ref[pl.ds(rn, CHUNK), SSD_WIDTH:U_SSD]
            from_prev = jnp.where((ti == 0) & (si == CHUNK - 1) & (c > 0), 1.0, 0.0).astype(BF16)
            from_next = jnp.where((ti == CHUNK - 1) & (si == 0) & (c < nch - 1), 1.0, 0.0).astype(BF16)
            xm = _dot(jnp.concatenate([shift_down, from_prev], axis=1), jnp.concatenate([cur, prev], axis=0))
            xp = _dot(jnp.concatenate([shift_up, from_next], axis=1), jnp.concatenate([cur, nxt], axis=0))
            conv = (cw_ref[0:1, :] * xm + cw_ref[1:2, :] * cur.astype(F32) + cw_ref[2:3, :] * xp
                    + cb_ref[...])
            xa = (conv * _sigmoid(conv)).astype(BF16)
            xact[goff + c] = xa
            bt[goff + c] = xa[:, SSD_WIDTH:SSD_WIDTH + gn].T

            gt = (g_ref[pl.ds(r, CHUNK), :] + bias0).T
            dt = _softplus(gt[DT_LANE0:DT_LANE0 + nj])
            prefix, suffix = _row_cumsums((tri_upp, tri_low), dt * a_rows)
            ac = jnp.where(fwd_row, prefix, suffix)
            a_end = jnp.where(fwd_row[:, 0:1], ac[:, CHUNK - 1:CHUNK], ac[:, 0:1])
            act[goff + c] = ac - jnp.log(dt)
            dtt[goff + c] = dt
            parts = [p.astype(F32) for p in _split3(jnp.exp(ac)) + _split3(jnp.exp(a_end - ac) * dt)]
            parts += [ac, jnp.zeros((CHUNK - 7 * nj, CHUNK), F32)]
            tile = jnp.concatenate(parts, axis=0).T
            acum[goff + c] = tile
            packed[goff + c] = tile.astype(BF16)
            return carry
        lax.fori_loop(0, nch, body, 0, unroll=UNROLL)

    prep(uc_ref, gc_ref, ncc, 0, n_ctx)
    prep(ul_ref, gl_ref, ncl, ncc, n_lat)

    def e_cols(d, g):
        return slice(d * SSD_WIDTH + g * gw, d * SSD_WIDTH + (g + 1) * gw)

    def w_cols(d, g):
        return slice((N_DIR + d) * SSD_WIDTH + g * gw, (N_DIR + d) * SSD_WIDTH + (g + 1) * gw)

    def update_state(d, cg):
        pk = packed[cg]
        xa = xact[cg]
        b_t = bt[cg]
        zero = jnp.zeros_like(b_t)
        upd = []
        for g in range(SSD_GROUPS):
            wx = (_dot(pk, ex_ref[:, w_cols(d, g)]) * xa[:, g * gw:(g + 1) * gw].astype(F32)).astype(BF16)
            upd.append(_dot(jnp.where(row_group0 == (g == 0), b_t, zero), wx))
        tail, last = (slice(CHUNK - 16, CHUNK), 15) if d == 0 else (slice(0, 16), 0)
        decay = _dot(pk[tail], ex_ref[:, d * SSD_WIDTH:(d + 1) * SSD_WIDTH])[last:last + 1, :]
        hrun[d] = decay * hrun[d] + jnp.concatenate(upd, axis=1)

    def bwd_pass(nch, goff, keep):
        def body(i, carry):
            cg = goff + nch - 1 - i
            if keep:
                hbk[cg] = hrun[1].astype(BF16)
            update_state(1, cg)
            return carry
        lax.fori_loop(0, nch, body, 0, unroll=UNROLL)

    def fwd_pass(u_ref, y_ref, nch, goff):
        def body(c, carry):
            cg = goff + c
            if y_ref is not None:
                rows = _chunk_rows(c)
                xa = xact[cg]
                x = xa[:, 0:SSD_WIDTH]
                b_t = bt[cg]
                c_all = xa[:, SSD_WIDTH + gn:SSD_XBC]
                ac = acum[cg]
                ac_t = act[cg]
                dt_t = dtt[cg]
                lane_w = _iota((CHUNK, gw), 1)
                pk = packed[cg]
                yz_groups, ssq = [], None
                for g in range(SSD_GROUPS):
                    cols = slice(g * gw, (g + 1) * gw)
                    c_g = jnp.where(lane_group0 == (g == 0), c_all, jnp.zeros_like(c_all))
                    cb = _dot(c_g, b_t)
                    s_heads = []
                    for hh in range(hg):
                        jf = g * hg + hh
                        jb = jf + nh
                        on_diag = jnp.log(dt_t[jf:jf + 1, :] + dt_t[jb:jb + 1, :])
                        ac_f = ac[:, ac_lane0 + jf:ac_lane0 + jf + 1]
                        ac_b = ac[:, ac_lane0 + jb:ac_lane0 + jb + 1]
                        expo = jnp.where(low_strict, ac_f - ac_t[jf:jf + 1, :],
                                         jnp.where(upp_strict, ac_b - ac_t[jb:jb + 1, :], on_diag))
                        s_heads.append((cb * jnp.exp(expo)).astype(BF16))
                    x_g = x[:, cols]
                    bd = jnp.concatenate(
                        [jnp.where((lane_w >= hh * SSD_HEAD_DIM) & (lane_w < (hh + 1) * SSD_HEAD_DIM),
                                   x_g, jnp.zeros_like(x_g)) for hh in range(hg)], axis=0)
                    y = _dot(jnp.concatenate(s_heads, axis=1), bd)
                    y = y + _dot(c_all, hrun[0, :, cols].astype(BF16)) * _dot(pk, ex_ref[:, e_cols(0, g)])
                    y = y + _dot(c_all, hbk[cg, :, cols]) * _dot(pk, ex_ref[:, e_cols(1, g)])
                    y = y + dsk_ref[:, cols] * x_g.astype(F32)
                    z = u_ref[rows, cols].astype(F32)
                    yz = y * (z * _sigmoid(z))
                    part = jnp.sum(yz * yz, axis=1, keepdims=True)
                    ssq = part if ssq is None else ssq + part
                    yz_groups.append(yz)
                scale = lax.rsqrt(ssq * (1.0 / SSD_WIDTH) + EPS)
                for g, yz in enumerate(yz_groups):
                    cols = slice(g * gw, (g + 1) * gw)
                    y_ref[rows, cols] = (yz * scale * ng_ref[:, cols]).astype(y_ref.dtype)
            update_state(0, cg)
            return carry
        lax.fori_loop(0, nch, body, 0, unroll=UNROLL)

    hrun[...] = jnp.zeros(hrun.shape, F32)
    bwd_pass(ncc, 0, need_ctx_out)
    bwd_pass(ncl, ncc, True)
    fwd_pass(uc_ref, yc_ref, ncc, 0)
    fwd_pass(ul_ref, yl_ref, ncl, ncc)


def _ssd_expand_table():
    nj = N_DIR * SSD_HEADS
    expand = np.zeros((LANES, 2 * N_DIR * SSD_WIDTH), np.float32)
    for i in range(6):
        for d in range(N_DIR):
            for h in range(SSD_HEADS):
                j = d * SSD_HEADS + h
                c0 = ((i // 3) * N_DIR + d) * SSD_WIDTH + h * SSD_HEAD_DIM
                expand[nj * i + j, c0:c0 + SSD_HEAD_DIM] = 1.0
    return jnp.asarray(expand, BF16)


def _ssd_mixer(ussd_l, ussd_c, g_l, g_c, pvec, conv_w, conv_b, d_skip, norm_g, need_ctx_out):
    expand = _ssd_expand_table()
    bsz, n_lat, _ = ussd_l.shape
    n_ctx = ussd_c.shape[1]
    nct = (n_lat + n_ctx) // CHUNK
    gn = SSD_GROUPS * SSD_STATE
    kern = functools.partial(_ssd_kernel, need_ctx_out=need_ctx_out, n_lat=n_lat, n_ctx=n_ctx)
    out_specs = [pl.BlockSpec((None, n_lat, SSD_WIDTH), lambda b: (b, 0, 0))]
    out_shape = [jax.ShapeDtypeStruct((bsz, n_lat, SSD_WIDTH), BF16)]
    if need_ctx_out:
        out_specs.append(pl.BlockSpec((None, n_ctx, SSD_WIDTH), lambda b: (b, 0, 0)))
        out_shape.append(jax.ShapeDtypeStruct((bsz, n_ctx, SSD_WIDTH), BF16))
    outs = pl.pallas_call(
        kern,
        grid=(bsz,),
        in_specs=[
            pl.BlockSpec((None, n_lat, U_SSD), lambda b: (b, 0, 0)),
            pl.BlockSpec((None, n_ctx, U_SSD), lambda b: (b, 0, 0)),
            pl.BlockSpec((None, n_lat, LANES), lambda b: (b, 0, 0)),
            pl.BlockSpec((None, n_ctx, LANES), lambda b: (b, 0, 0)),
            pl.BlockSpec((8, LANES), lambda b: (0, 0)),
            pl.BlockSpec((3, SSD_XBC), lambda b: (0, 0)),
            pl.BlockSpec((1, SSD_XBC), lambda b: (0, 0)),
            pl.BlockSpec((1, SSD_WIDTH), lambda b: (0, 0)),
            pl.BlockSpec((1, SSD_WIDTH), lambda b: (0, 0)),
            pl.BlockSpec(expand.shape, lambda b: (0, 0)),
        ],
        out_specs=out_specs,
        out_shape=out_shape,
        scratch_shapes=[
            pltpu.VMEM((nct, CHUNK, SSD_XBC), BF16),
            pltpu.VMEM((nct, gn, CHUNK), BF16),
            pltpu.VMEM((nct, CHUNK, LANES), F32),
            pltpu.VMEM((nct, CHUNK, LANES), BF16),
            pltpu.VMEM((nct, N_DIR * SSD_HEADS, CHUNK), F32),
            pltpu.VMEM((nct, N_DIR * SSD_HEADS, CHUNK), F32),
            pltpu.VMEM((nct, gn, SSD_WIDTH), BF16),
            pltpu.VMEM((N_DIR, gn, SSD_WIDTH), F32),
        ],
        compiler_params=pltpu.CompilerParams(
            dimension_semantics=("arbitrary",), vmem_limit_bytes=VMEM_LIMIT),
        name="ssd",
    )(ussd_l, ussd_c, g_l, g_c, pvec, conv_w, conv_b, d_skip, norm_g, expand)
    return (outs[0], outs[1]) if need_ctx_out else (outs[0], None)


def _head_norm(x, gain):
    rows = x.shape[0]
    blks = [x[:, j * LANES:(j + 1) * LANES] for j in range(x.shape[1] // LANES)]
    same_head = (_iota((LANES, LANES), 0) < ATT_HEAD_DIM) == (_iota((LANES, LANES), 1) < ATT_HEAD_DIM)
    sq = jnp.concatenate([b * b for b in blks], axis=0).astype(BF16)
    ms = _dot(sq, jnp.where(same_head, 1.0, 0.0).astype(BF16)) * (1.0 / ATT_HEAD_DIM)
    outs = [b * lax.rsqrt(ms[j * rows:(j + 1) * rows] + EPS) * gain for j, b in enumerate(blks)]
    return outs[0] if len(outs) == 1 else jnp.concatenate(outs, axis=1)


def _rope(x, cos_t, sin_t):
    w = x.shape[1]
    reps = w // LANES
    nfreq = ATT_HEAD_DIM // 4
    partner = jnp.where((_iota(x.shape, 1) & (2 * nfreq - 1)) < nfreq,
                        pltpu.roll(x, w - nfreq, 1), pltpu.roll(x, nfreq, 1))
    if reps > 1:
        cos_t = jnp.concatenate([cos_t] * reps, axis=1)
        sin_t = jnp.concatenate([sin_t] * reps, axis=1)
    return x * cos_t + partner * sin_t


def _dup_heads(x):
    lane_lo = _iota(x.shape, 1) < ATT_HEAD_DIM
    sw = pltpu.roll(x, ATT_HEAD_DIM, 1)
    return jnp.concatenate([jnp.where(lane_lo, x, sw), jnp.where(lane_lo, sw, x)], axis=1)


def _attn_kernel(ul_ref, uc_ref, cos_ref, sin_ref, qg_ref, kg_ref, sink_ref, *rest,
                 need_ctx_out, n_lat, n_ctx):
    if need_ctx_out:
        yl_ref, yc_ref, qp, kb, vb, kc, vc, qcp = rest
    else:
        yl_ref, qp, kb, vb, kc, vc = rest
        yc_ref = qcp = None
    ncl, ncc = n_lat // CHUNK, n_ctx // CHUNK
    rep = ATT_Q_HEADS // ATT_KV_HEADS
    kv0 = ATT_WIDTH
    lane_lo = _iota((CHUNK, LANES), 1) < ATT_HEAD_DIM
    neg_inf = -jnp.inf
    qg = qg_ref[...]
    kg = kg_ref[...]

    zeros_blk = jnp.zeros((CHUNK, vb.shape[1]), BF16)
    vb[0:CHUNK, :] = zeros_blk
    vb[CHUNK + n_lat:2 * CHUNK + n_lat, :] = zeros_blk
    kb[0] = jnp.zeros(kb.shape[1:], BF16)
    kb[ncl + 1] = jnp.zeros(kb.shape[1:], BF16)

    def store_keys_transposed(ref, blk, k):
        kt = k.T
        for g in range(ATT_KV_HEADS):
            head = kt[g * ATT_HEAD_DIM:(g + 1) * ATT_HEAD_DIM]
            ref[blk, g * LANES:(g + 1) * LANES, :] = jnp.concatenate([head, head], axis=0).astype(BF16)

    def values_with_ones(v):
        d = _dup_heads(v).astype(BF16)
        ones = jnp.ones((v.shape[0], LANES), BF16)
        return jnp.concatenate([d[:, 0:LANES], ones, d[:, LANES:2 * LANES], ones], axis=1)

    def prep_lat(c, carry):
        rows = _chunk_rows(c)
        cos_t = cos_ref[rows, :]
        sin_t = sin_ref[rows, :]
        q = _rope(_head_norm(ul_ref[rows, 0:ATT_WIDTH].astype(F32), qg), cos_t, sin_t)
        qp[rows, :] = (q * ATT_SCALE).astype(BF16)
        k = _rope(_head_norm(ul_ref[rows, kv0:kv0 + LANES].astype(F32), kg), cos_t, sin_t)
        v = ul_ref[rows, kv0 + LANES:kv0 + 2 * LANES].astype(F32)
        dst = pl.ds(pl.multiple_of((c + 1) * CHUNK, CHUNK), CHUNK)
        store_keys_transposed(kb, c + 1, k)
        vb[dst, :] = values_with_ones(v)
        return carry
    lax.fori_loop(0, ncl, prep_lat, 0, unroll=UNROLL)

    def prep_ctx(c, carry):
        rows = _chunk_rows(c)
        k = _head_norm(uc_ref[rows, kv0:kv0 + LANES].astype(F32), kg)
        v = uc_ref[rows, kv0 + LANES:kv0 + 2 * LANES].astype(F32)
        store_keys_transposed(kc, c, k)
        vc[rows, :] = values_with_ones(v)
        if need_ctx_out:
            q = _head_norm(uc_ref[rows, 0:ATT_WIDTH].astype(F32), qg)
            qcp[rows, :] = (q * ATT_SCALE).astype(BF16)
        return carry
    lax.fori_loop(0, ncc, prep_ctx, 0, unroll=UNROLL)

    def attend(q_ref, y_ref, rows, band):
        for g in range(ATT_KV_HEADS):
            kl = slice(g * LANES, (g + 1) * LANES)
            vl = slice(2 * g * LANES, 2 * (g + 1) * LANES)
            k_ctx = jnp.concatenate([kc[j, kl, :] for j in range(ncc)], axis=1)
            if band is not None:
                blk, band_rows, valid = band
                k_band = jnp.concatenate([kb[blk + j, kl, :] for j in range(3)], axis=1)
            for r0 in range(0, rep, ATT_STACK):
                heads = [g * rep + r0 + i for i in range(ATT_STACK)]
                qms = []
                for hq in heads:
                    qb = q_ref[rows, (hq // 2) * LANES:(hq // 2 + 1) * LANES]
                    qms.append(jnp.where(lane_lo != (hq % 2 == 1), qb, jnp.zeros_like(qb)))
                qs = qms[0] if ATT_STACK == 1 else jnp.concatenate(qms, axis=0)
                s_c = _dot(qs, k_ctx)
                if band is not None:
                    s_b = _dot(qs, k_band)
                ps, sink_terms = [], []
                for i, hq in enumerate(heads):
                    sl = slice(i * CHUNK, (i + 1) * CHUNK)
                    sk = sink_ref[hq]
                    s = s_c[sl]
                    if band is not None:
                        prev_k, own_k, next_k = (slice(j * CHUNK, (j + 1) * CHUNK) for j in range(3))
                        s = jnp.concatenate([s, jnp.where(valid[:, prev_k], s_b[sl, prev_k], neg_inf), s_b[sl, own_k],
                                             jnp.where(valid[:, next_k], s_b[sl, next_k], neg_inf)], axis=1)
                    m = jnp.maximum(jnp.max(s, axis=1, keepdims=True), sk)
                    ps.append(jnp.exp(s - m).astype(BF16))
                    sink_terms.append(jnp.exp(sk - m))
                p = ps[0] if ATT_STACK == 1 else jnp.concatenate(ps, axis=0)
                acc = _dot(p[:, 0:n_ctx], vc[:, vl])
                if band is not None:
                    acc = acc + _dot(p[:, n_ctx:], vb[band_rows, vl])
                for i, hq in enumerate(heads):
                    sl = slice(i * CHUNK, (i + 1) * CHUNK)
                    o = acc[sl, 0:LANES] / (acc[sl, LANES:2 * LANES] + sink_terms[i])
                    if hq % 2 == 1:
                        jb = hq // 2
                        y_ref[rows, jb * LANES:(jb + 1) * LANES] = jnp.where(lane_lo, o_even, o).astype(y_ref.dtype)
                    o_even = o

    assert WINDOW == CHUNK
    qq = _iota((CHUNK, 3 * CHUNK), 0)
    kk = _iota((CHUNK, 3 * CHUNK), 1)
    in_window = (kk >= qq) & (kk <= qq + 2 * WINDOW)

    def lat_block(i, carry):
        kpos = kk + (i - 1) * CHUNK
        valid = in_window & (kpos >= 0) & (kpos < n_lat)
        band_rows = pl.ds(pl.multiple_of(i * CHUNK, CHUNK), 3 * CHUNK)
        attend(qp, yl_ref, _chunk_rows(i), (i, band_rows, valid))
        return carry
    lax.fori_loop(0, ncl, lat_block, 0, unroll=UNROLL)

    if need_ctx_out:
        def ctx_block(i, carry):
            attend(qcp, yc_ref, _chunk_rows(i), None)
            return carry
        lax.fori_loop(0, ncc, ctx_block, 0, unroll=UNROLL)


def _attn_mixer(uatt_l, uatt_c, cos_t, sin_t, qgain, kgain, sink, need_ctx_out):
    bsz, n_lat, _ = uatt_l.shape
    n_ctx = uatt_c.shape[1]
    kern = functools.partial(_attn_kernel, need_ctx_out=need_ctx_out, n_lat=n_lat, n_ctx=n_ctx)
    out_specs = [pl.BlockSpec((None, n_lat, ATT_WIDTH), lambda b: (b, 0, 0))]
    out_shape = [jax.ShapeDtypeStruct((bsz, n_lat, ATT_WIDTH), BF16)]
    scratch = [
        pltpu.VMEM((n_lat, ATT_WIDTH), BF16),
        pltpu.VMEM((n_lat // CHUNK + 2, 2 * LANES, CHUNK), BF16),
        pltpu.VMEM((n_lat + 2 * CHUNK, 4 * LANES), BF16),
        pltpu.VMEM((n_ctx // CHUNK, 2 * LANES, CHUNK), BF16),
        pltpu.VMEM((n_ctx, 4 * LANES), BF16),
    ]
    if need_ctx_out:
        out_specs.append(pl.BlockSpec((None, n_ctx, ATT_WIDTH), lambda b: (b, 0, 0)))
        out_shape.append(jax.ShapeDtypeStruct((bsz, n_ctx, ATT_WIDTH), BF16))
        scratch.append(pltpu.VMEM((n_ctx, ATT_WIDTH), BF16))
    outs = pl.pallas_call(
        kern,
        grid=(bsz,),
        in_specs=[
            pl.BlockSpec((None, n_lat, U_ATT), lambda b: (b, 0, 0)),
            pl.BlockSpec((None, n_ctx, U_ATT), lambda b: (b, 0, 0)),
            pl.BlockSpec((n_lat, LANES), lambda b: (0, 0)),
            pl.BlockSpec((n_lat, LANES), lambda b: (0, 0)),
            pl.BlockSpec((1, LANES), lambda b: (0, 0)),
            pl.BlockSpec((1, LANES), lambda b: (0, 0)),
            pl.BlockSpec(memory_space=pltpu.SMEM),
        ],
        out_specs=out_specs,
        out_shape=out_shape,
        scratch_shapes=scratch,
        compiler_params=pltpu.CompilerParams(
            dimension_semantics=("arbitrary",), vmem_limit_bytes=VMEM_LIMIT),
        name="attention",
    )(uatt_l, uatt_c, cos_t, sin_t, qgain, kgain, sink)
    return (outs[0], outs[1]) if need_ctx_out else (outs[0], None)


def _in_layout():
    names = (('ml_q', ML_WIDTH), ('ml_k', ML_WIDTH), ('ml_v', ML_WIDTH), ('ml_o', ML_WIDTH),
             ('ml_i', N_DIR * ML_HEADS), ('ml_f', N_DIR * ML_HEADS),
             ('ssd_z', SSD_WIDTH), ('ssd_xbc', SSD_XBC), ('ssd_dt', N_DIR * SSD_HEADS),
             ('att_q', ATT_WIDTH), ('att_k', ATT_KV_WIDTH), ('att_v', ATT_KV_WIDTH))
    out, off = {}, 0
    for name, width in names:
        out[name] = (off, off + width)
        off += width
    return out


def _col_map():
    lay = _in_layout()
    take = lambda name: np.arange(lay[name][0], lay[name][1])
    zeros = lambda n: np.full((n,), -1)
    n_f, n_dt, n_i = N_DIR * ML_HEADS, N_DIR * SSD_HEADS, N_DIR * ML_HEADS
    return np.concatenate(
        [take('ml_q'), take('ml_k'), take('ml_v'), take('ml_o'),
         take('ssd_z'), take('ssd_xbc'),
         take('att_q'), take('att_k'), take('att_v'),
         take('ml_f'), take('ssd_dt'), take('ml_i'), zeros(LANES - n_f - n_dt - n_i)])


def _permute_plan():
    src = _col_map()
    plan = []
    for t in range(U_TOTAL // LANES):
        cols = src[t * LANES:(t + 1) * LANES]
        runs, i = [], 0
        while i < LANES:
            j = i
            while j + 1 < LANES and (cols[j + 1] == cols[j] + 1 if cols[j] >= 0 else cols[j + 1] < 0):
                j += 1
            runs.append((int(cols[i]), j - i + 1))
            i = j + 1
        plan.append(tuple(runs))
    return tuple(plan)


def _wprep_kernel(wt_ref, o_ref, *, plan):
    kb = wt_ref.shape[1]
    for t, runs in enumerate(plan):
        pieces = [wt_ref[s:s + n, :] if s >= 0 else jnp.zeros((n, kb), F32) for s, n in runs]
        blk = pieces[0] if len(pieces) == 1 else jnp.concatenate(pieces, axis=0)
        o_ref[:, t * LANES:(t + 1) * LANES] = blk.T.astype(BF16)


def _permute_weights(w_in):
    depth, d, in_width = w_in.shape
    plan = _permute_plan()
    kb = 256
    return pl.pallas_call(
        functools.partial(_wprep_kernel, plan=plan),
        grid=(depth, d // kb),
        in_specs=[pl.BlockSpec((None, in_width, kb), lambda l, i: (l, 0, i))],
        out_specs=pl.BlockSpec((None, kb, U_TOTAL), lambda l, i: (l, i, 0)),
        out_shape=jax.ShapeDtypeStruct((depth, d, U_TOTAL), BF16),
        compiler_params=pltpu.CompilerParams(
            dimension_semantics=("arbitrary", "arbitrary"), vmem_limit_bytes=VMEM_LIMIT),
        name="permute_weights",
    )(jnp.swapaxes(w_in, 1, 2))


def _gate_rows(ml_b_i, ml_b_f, ssd_dt_bias, ssd_a_log):
    n_f, n_dt = N_DIR * ML_HEADS, N_DIR * SSD_HEADS
    row0 = jnp.concatenate([ml_b_f.reshape(-1), ssd_dt_bias.reshape(-1), ml_b_i.reshape(-1),
                            jnp.zeros((LANES - 2 * n_f - n_dt,), F32)])
    row2 = jnp.concatenate([jnp.zeros((n_f,), F32), ssd_a_log.reshape(-1), jnp.zeros((LANES - n_f - n_dt,), F32)])
    zero = jnp.zeros((1, LANES), F32)
    return jnp.concatenate([row0[None], zero, row2[None], jnp.zeros((5, LANES), F32)], axis=0)


def _rope_tables(n_lat):
    nfreq = ATT_HEAD_DIM // 4
    pos = np.arange(n_lat)
    rows, cols = pos // GRID_W, pos % GRID_W
    inv = np.float32(ROPE_BASE) ** (-np.arange(nfreq, dtype=np.float32) / np.float32(nfreq))
    ang_r = rows.astype(np.float32)[:, None] * inv
    ang_c = cols.astype(np.float32)[:, None] * inv
    cos_h = np.concatenate([np.cos(ang_r)] * 2 + [np.cos(ang_c)] * 2, axis=1)
    sin_h = np.concatenate([-np.sin(ang_r), np.sin(ang_r), -np.sin(ang_c), np.sin(ang_c)], axis=1)
    return (jnp.asarray(np.concatenate([cos_h] * 2, axis=1), F32),
            jnp.asarray(np.concatenate([sin_h] * 2, axis=1), F32))


def _token_mixers(u_l, u_c, p, layer, tables, need_ctx_out):
    uml_l, ussd_l, uatt_l, g_l = u_l
    uml_c, ussd_c, uatt_c, g_c = u_c
    pvec = _gate_rows(p['ml_b_i'][layer], p['ml_b_f'][layer], p['ssd_dt_bias'][layer], p['ssd_a_log'][layer])
    m_l, m_c = _mlstm_mixer(uml_l, uml_c, g_l, g_c, pvec, p['ml_norm_g'][layer][None], need_ctx_out)
    s_l, s_c = _ssd_mixer(ussd_l, ussd_c, g_l, g_c, pvec, p['ssd_conv_w'][layer], p['ssd_conv_b'][layer][None],
                          jnp.repeat(p['ssd_d'][layer], SSD_HEAD_DIM)[None], p['ssd_norm_g'][layer][None],
                          need_ctx_out)
    a_l, a_c = _attn_mixer(uatt_l, uatt_c, tables['cos'], tables['sin'],
                           jnp.tile(p['att_qn_g'][layer], 2)[None], jnp.tile(p['att_kn_g'][layer], 2)[None],
                           p['att_sink'][layer], need_ctx_out)
    return (m_l, s_l, a_l), (m_c, s_c, a_c)


def kernel(x, c, ctx, c_ctx, w_mod, b_mod, norm1_g, w_in, ml_b_i, ml_b_f, ml_norm_g, ssd_conv_w, ssd_conv_b,
           ssd_a_log, ssd_dt_bias, ssd_d, ssd_norm_g, att_qn_g, att_kn_g, att_sink, w_out, norm2_g, w_up, w_down):
    p = dict(ml_b_i=ml_b_i, ml_b_f=ml_b_f, ml_norm_g=ml_norm_g, ssd_conv_w=ssd_conv_w, ssd_conv_b=ssd_conv_b,
             ssd_a_log=ssd_a_log, ssd_dt_bias=ssd_dt_bias, ssd_d=ssd_d, ssd_norm_g=ssd_norm_g,
             att_qn_g=att_qn_g, att_kn_g=att_kn_g, att_sink=att_sink)
    bsz, n_lat, d = x.shape
    n_ctx = ctx.shape[1]
    depth = w_mod.shape[0]
    mod_rows = 16
    cc = jnp.concatenate([c, c_ctx[None], jnp.zeros((mod_rows - bsz - 1, d), F32)], axis=0)
    mods = _modulation(cc, w_mod, b_mod).reshape(depth, mod_rows, 6, d)
    cos_t, sin_t = _rope_tables(n_lat)
    tables = dict(cos=cos_t, sin=sin_t)
    lat_row = lambda b: b
    ctx_row = lambda b: bsz
    h_ctx = ctx
    w = _permute_weights(w_in)
    wo, wup, wdn = w_out.astype(BF16), w_up.astype(BF16), w_down.astype(BF16)
    g1, g2 = norm1_g[:, None, :], norm2_g[:, None, :]
    for layer in range(depth):
        need_ctx_out = layer < depth - 1
        u_l = _in_proj(x, mods, lat_row, g1, w, layer, tm=1024)
        u_c = _in_proj(h_ctx, mods, ctx_row, g1, w, layer, tm=256)
        y_l, y_c = _token_mixers(u_l, u_c, p, layer, tables, need_ctx_out)
        x = _post(x, *y_l, mods, lat_row, g2, wo, wup, wdn, layer, tm=512)
        if need_ctx_out:
            h_ctx = _post(h_ctx, *y_c, mods, ctx_row, g2, wo, wup, wdn, layer, tm=256)
    return x
```

```python
import functools
import math

import numpy as np
import jax
import jax.numpy as jnp
from jax import lax
from jax.experimental import pallas as pl
from jax.experimental.pallas import tpu as pltpu

F32 = jnp.float32
BF16 = jnp.bfloat16

EPS = 1e-6
GRID_W = 64
N_DIR = 2
ML_HEADS = 4
ML_HEAD_DIM = 128
ML_WIDTH = ML_HEADS * ML_HEAD_DIM
SSD_HEADS = 8
SSD_HEAD_DIM = 64
SSD_WIDTH = SSD_HEADS * SSD_HEAD_DIM
SSD_STATE = 64
SSD_GROUPS = 2
SSD_XBC = SSD_WIDTH + 2 * SSD_GROUPS * SSD_STATE
ATT_Q_HEADS = 8
ATT_KV_HEADS = 2
ATT_HEAD_DIM = 64
ATT_WIDTH = ATT_Q_HEADS * ATT_HEAD_DIM
ATT_KV_WIDTH = ATT_KV_HEADS * ATT_HEAD_DIM
WINDOW = 128
ROPE_BASE = 10000.0
MIX_WIDTH = ML_WIDTH + SSD_WIDTH + ATT_WIDTH

LANES = 128
CHUNK = 128
VMEM_LIMIT = 56 * 1024 * 1024
UNROLL = 16
ATT_STACK = 4
PREP_GROUP = 16

U_ML = 4 * ML_WIDTH
U_SSD = SSD_WIDTH + SSD_XBC
U_ATT = ATT_WIDTH + 2 * ATT_KV_WIDTH
U_GATE = LANES
U_TOTAL = U_ML + U_SSD + U_ATT + U_GATE
DT_LANE0 = N_DIR * ML_HEADS
LI_LANE0 = DT_LANE0 + N_DIR * SSD_HEADS

ML_SCALE = ML_HEAD_DIM ** -0.5
ATT_SCALE = ATT_HEAD_DIM ** -0.5


def _dot(a, b):
    return jnp.dot(a, b, preferred_element_type=F32)


def _split2(x):
    hi = x.astype(BF16)
    lo = (x - hi.astype(F32)).astype(BF16)
    return hi, lo


def _split3(x):
    hi = x.astype(BF16)
    r = x - hi.astype(F32)
    mid = r.astype(BF16)
    lo = (r - mid.astype(F32)).astype(BF16)
    return hi, mid, lo


def _row_cumsums(tris, x):
    rows = x.shape[0]
    parts = _split3(jnp.concatenate([x, jnp.zeros_like(x)], axis=0))
    return [(_dot(parts[0], tri) + _dot(parts[1], tri) + _dot(parts[2], tri))[0:rows] for tri in tris]


def _sigmoid(x):
    return 1.0 / (1.0 + jnp.exp2(x * (-1.0 / math.log(2.0))))


def _log_sigmoid(x):
    return jnp.minimum(x, 0.0) - jnp.log1p(jnp.exp(-jnp.abs(x)))


def _softplus(x):
    return jnp.maximum(x, 0.0) + jnp.log1p(jnp.exp(-jnp.abs(x)))


def _iota(shape, dim):
    return lax.broadcasted_iota(jnp.int32, shape, dim)


def _chunk_rows(c):
    return pl.ds(pl.multiple_of(c * CHUNK, CHUNK), CHUNK)


def _mod_kernel(s_ref, w_ref, b_ref, o_ref):
    s = s_ref[...]
    s = s * _sigmoid(s)
    w = w_ref[...]
    s_hi, s_lo = _split2(s)
    w_hi, w_lo = _split2(w)
    o_ref[...] = _dot(s_hi, w_hi) + (_dot(s_hi, w_lo) + _dot(s_lo, w_hi)) + b_ref[...]


def _modulation(cc, w_mod, b_mod):
    depth, d, d6 = w_mod.shape
    r = cc.shape[0]
    bn = 2048
    return pl.pallas_call(
        _mod_kernel,
        grid=(depth, d6 // bn),
        in_specs=[
            pl.BlockSpec((r, d), lambda l, j: (0, 0)),
            pl.BlockSpec((None, d, bn), lambda l, j: (l, 0, j)),
            pl.BlockSpec((None, 1, bn), lambda l, j: (l, 0, j)),
        ],
        out_specs=pl.BlockSpec((None, r, bn), lambda l, j: (l, 0, j)),
        out_shape=jax.ShapeDtypeStruct((depth, r, d6), F32),
        compiler_params=pltpu.CompilerParams(
            dimension_semantics=("arbitrary", "arbitrary"), vmem_limit_bytes=VMEM_LIMIT),
        name="modulation",
    )(cc, w_mod, b_mod.reshape(depth, 1, d6))


def _inproj_kernel(x_ref, mod_ref, g_ref, w_ref, oml_ref, ossd_ref, oatt_ref, og_ref):
    x = x_ref[...]
    ms = jnp.mean(x * x, axis=-1, keepdims=True)
    xn = x * lax.rsqrt(ms + EPS)
    h = xn * (g_ref[...] * (1.0 + mod_ref[1:2, :])) + mod_ref[0:1, :]
    hb = h.astype(BF16)
    step = 512
    off = 0
    for o_ref, width in ((oml_ref, U_ML), (ossd_ref, U_SSD), (oatt_ref, U_ATT), (og_ref, U_GATE)):
        for j in range(0, width, step):
            wj = min(step, width - j)
            o_ref[:, j:j + wj] = _dot(hb, w_ref[:, off + j:off + j + wj]).astype(o_ref.dtype)
        off += width


def _in_proj(x, mods, mod_row, g1, w, layer, tm):
    bsz, t, d = x.shape
    widths = (U_ML, U_SSD, U_ATT, U_GATE)
    dtypes = (BF16, BF16, BF16, F32)
    return pl.pallas_call(
        _inproj_kernel,
        grid=(bsz, t // tm),
        in_specs=[
            pl.BlockSpec((None, tm, d), lambda b, i: (b, i, 0)),
            pl.BlockSpec((None, None, 6, d), lambda b, i: (layer, mod_row(b), 0, 0)),
            pl.BlockSpec((None, 1, d), lambda b, i: (layer, 0, 0)),
            pl.BlockSpec((None, d, U_TOTAL), lambda b, i: (layer, 0, 0), pipeline_mode=pl.Buffered(1)),
        ],
        out_specs=[pl.BlockSpec((None, tm, wd), lambda b, i: (b, i, 0)) for wd in widths],
        out_shape=[jax.ShapeDtypeStruct((bsz, t, wd), dt) for wd, dt in zip(widths, dtypes)],
        compiler_params=pltpu.CompilerParams(
            dimension_semantics=("arbitrary", "arbitrary"), vmem_limit_bytes=VMEM_LIMIT),
        name="in_proj",
    )(x, mods, g1, w)


def _post_kernel(x_ref, yml_ref, yssd_ref, yatt_ref, mod_ref, g_ref, wo_ref, wup_ref, wdn_ref, o_ref):
    proj = (_dot(yml_ref[...], wo_ref[0:ML_WIDTH, :])
            + _dot(yssd_ref[...], wo_ref[ML_WIDTH:ML_WIDTH + SSD_WIDTH, :])
            + _dot(yatt_ref[...], wo_ref[ML_WIDTH + SSD_WIDTH:MIX_WIDTH, :]))
    x1 = x_ref[...] + mod_ref[2:3, :] * proj
    ms = jnp.mean(x1 * x1, axis=-1, keepdims=True)
    xn = x1 * lax.rsqrt(ms + EPS)
    hb = (xn * (g_ref[...] * (1.0 + mod_ref[4:5, :])) + mod_ref[3:4, :]).astype(BF16)
    d_ff = wup_ref.shape[1]
    step = 1024
    acc = None
    for j in range(0, d_ff, step):
        up = jnp.maximum(_dot(hb, wup_ref[:, j:j + step]), 0.0)
        part = _dot((up * up).astype(BF16), wdn_ref[j:j + step, :])
        acc = part if acc is None else acc + part
    o_ref[...] = x1 + mod_ref[5:6, :] * acc


def _post(x, yml, yssd, yatt, mods, mod_row, g2, wo, wup, wdn, layer, tm):
    bsz, t, d = x.shape
    d_ff = wup.shape[2]
    const = dict(pipeline_mode=pl.Buffered(1))
    return pl.pallas_call(
        _post_kernel,
        grid=(bsz, t // tm),
        in_specs=[
            pl.BlockSpec((None, tm, d), lambda b, i: (b, i, 0)),
            pl.BlockSpec((None, tm, ML_WIDTH), lambda b, i: (b, i, 0)),
            pl.BlockSpec((None, tm, SSD_WIDTH), lambda b, i: (b, i, 0)),
            pl.BlockSpec((None, tm, ATT_WIDTH), lambda b, i: (b, i, 0)),
            pl.BlockSpec((None, None, 6, d), lambda b, i: (layer, mod_row(b), 0, 0)),
            pl.BlockSpec((None, 1, d), lambda b, i: (layer, 0, 0)),
            pl.BlockSpec((None, MIX_WIDTH, d), lambda b, i: (layer, 0, 0), **const),
            pl.BlockSpec((None, d, d_ff), lambda b, i: (layer, 0, 0), **const),
            pl.BlockSpec((None, d_ff, d), lambda b, i: (layer, 0, 0), **const),
        ],
        out_specs=pl.BlockSpec((None, tm, d), lambda b, i: (b, i, 0)),
        out_shape=jax.ShapeDtypeStruct((bsz, t, d), F32),
        compiler_params=pltpu.CompilerParams(
            dimension_semantics=("arbitrary", "arbitrary"), vmem_limit_bytes=VMEM_LIMIT),
        name="post",
    )(x, yml, yssd, yatt, mods, g2, wo, wup, wdn)


ML_GATES = N_DIR * ML_HEADS


def _mlstm_kernel(ul_ref, uc_ref, gl_ref, gc_ref, pv_ref, ng_ref, ex_ref, *rest,
                  need_ctx_out, n_lat, n_ctx):
    if need_ctx_out:
        yl_ref, yc_ref, arow, packed, ends, mprev, cbk, crun = rest
    else:
        yl_ref, arow, packed, ends, mprev, cbk, crun = rest
        yc_ref = None
    ncl, ncc = n_lat // CHUNK, n_ctx // CHUNK
    nct = ncl + ncc
    nh = ML_HEADS
    ti = _iota((CHUNK, CHUNK), 0)
    si = _iota((CHUNK, CHUNK), 1)
    low = si <= ti
    upp = si >= ti
    tri_low = jnp.where(low, 1.0, 0.0).astype(BF16)
    tri_upp = jnp.where(upp, 1.0, 0.0).astype(BF16)
    fwd_row = _iota((ML_GATES, CHUNK), 0) < nh
    bias = pv_ref[0:1, :]
    neg_inf = -jnp.inf
    ones_blk = jnp.ones((CHUNK, LANES), BF16)

    def prep_group(g_ref, c0, n, goff):
        def per_chunk(x):
            return [x[:, i * CHUNK:(i + 1) * CHUNK] for i in range(n)]

        def wide(xs):
            return xs[0] if n == 1 else jnp.concatenate(xs, axis=1)

        gts = [(g_ref[(c0 + i) * CHUNK:(c0 + i + 1) * CHUNK, :] + bias).T for i in range(n)]
        lf = _log_sigmoid(wide([gt[0:ML_GATES] for gt in gts]))
        li = wide([gt[LI_LANE0:LI_LANE0 + ML_GATES] for gt in gts])
        sums = [_row_cumsums((tri_upp, tri_low), x) for x in per_chunk(lf)]
        fwd = _iota((ML_GATES, n * CHUNK), 0) < nh
        tok = _iota((ML_GATES, n * CHUNK), 1) & (CHUNK - 1)
        b = jnp.where(fwd, wide([s[0] for s in sums]), wide([s[1] for s in sums]))
        a = li - b
        cm = a
        k = 1
        while k < CHUNK:
            before = jnp.where(tok >= k, pltpu.roll(cm, k, 1), neg_inf)
            after = jnp.where(tok < CHUNK - k, pltpu.roll(cm, n * CHUNK - k, 1), neg_inf)
            cm = jnp.maximum(cm, jnp.where(fwd, before, after))
            k *= 2
        parts = [p.astype(F32) for p in _split3(cm) + _split3(b)]
        parts.append(jnp.zeros((CHUNK - len(parts) * ML_GATES, n * CHUNK), F32))
        stacked = jnp.concatenate(parts, axis=0)
        for i in range(n):
            cg = goff + c0 + i
            cols = slice(i * CHUNK, (i + 1) * CHUNK)
            arow[cg] = a[:, cols]
            packed[cg] = stacked[:, cols].T.astype(BF16)
            for plane, x in enumerate((cm, b)):
                last = x[:, (i + 1) * CHUNK - 1:(i + 1) * CHUNK]
                first = x[:, i * CHUNK:i * CHUNK + 1]
                ends[cg, plane] = jnp.broadcast_to(jnp.where(fwd_row[:, 0:1], last, first), (ML_GATES, CHUNK))

    def prep(g_ref, nch, goff):
        for c0 in range(0, nch, PREP_GROUP):
            prep_group(g_ref, c0, min(PREP_GROUP, nch - c0), goff)

    prep(gc_ref, ncc, 0)
    prep(gl_ref, ncl, ncc)

    def m_scan(lo, hi, reverse, m):
        for cg in (range(hi - 1, lo - 1, -1) if reverse else range(lo, hi)):
            mprev[cg, int(reverse)] = m
            m = ends[cg, 1] + jnp.maximum(m, ends[cg, 0])
        return m

    m_zero = jnp.zeros((ML_GATES, CHUNK), F32)
    m_scan(0, nct, False, m_zero)
    m_scan(ncc, nct, True, m_scan(0, ncc, True, m_zero))

    def load_kv(u_ref, rows, h):
        k = u_ref[rows, ML_WIDTH + h * LANES:ML_WIDTH + (h + 1) * LANES]
        v = u_ref[rows, 2 * ML_WIDTH + h * LANES:2 * ML_WIDTH + (h + 1) * LANES]
        return k.T, jnp.concatenate([v, ones_blk], axis=1)

    def update(d, h, cg, kt, v1):
        j = d * nh + h
        m_prev = mprev[cg, d, j:j + 1, 0:1]
        m_end = jnp.maximum(m_prev, ends[cg, 0, j:j + 1, 0:1])
        w_row = jnp.exp(arow[cg, j:j + 1, :] - m_end) * ML_SCALE
        ktw = (kt.astype(F32) * w_row).astype(BF16)
        crun[d, h] = jnp.exp(m_prev - m_end) * crun[d, h] + _dot(ktw, v1)

    def bwd_pass(u_ref, nch, goff, keep):
        def body(i, carry):
            c = nch - 1 - i
            rows = _chunk_rows(c)
            for h in range(nh):
                if keep:
                    cbk[goff + c, h] = crun[1, h].astype(BF16)
                kt, v1 = load_kv(u_ref, rows, h)
                update(1, h, goff + c, kt, v1)
            return carry
        lax.fori_loop(0, nch, body, 0, unroll=UNROLL)

    def fwd_pass(u_ref, y_ref, nch, goff):
        def body(c, carry):
            rows = _chunk_rows(c)
            cg = goff + c
            for h in range(nh):
                kt, v1 = load_kv(u_ref, rows, h)
                if y_ref is not None:
                    q = u_ref[rows, h * LANES:(h + 1) * LANES]
                    o = u_ref[rows, 3 * ML_WIDTH + h * LANES:3 * ML_WIDTH + (h + 1) * LANES]
                    sqk = _dot(q, kt) * ML_SCALE
                    qc = _dot(q, jnp.concatenate([crun[0, h].astype(BF16), cbk[cg, h]], axis=1))
                    s_dirs, stats = [], []
                    for d, mask in ((0, low), (1, upp)):
                        j = d * nh + h
                        rep = _dot(packed[cg], ex_ref[:, 2 * j * LANES:2 * (j + 1) * LANES])
                        m_prev = mprev[cg, d, j:j + 1, 0:1]
                        m_t = jnp.maximum(rep[:, 0:LANES], m_prev)
                        s_dirs.append(sqk * jnp.exp(jnp.where(mask, arow[cg, j:j + 1, :] - m_t, neg_inf)))
                        stats.append((m_prev, m_t, rep[:, LANES:2 * LANES]))
                    sv = _dot(jnp.concatenate(s_dirs, axis=0).astype(BF16), v1)
                    hsum = None
                    for d, (m_prev, m_t, b_t) in enumerate(stats):
                        g = jnp.exp(m_prev - m_t)
                        sv_d = sv[d * CHUNK:(d + 1) * CHUNK, :]
                        qc_d = qc[:, 2 * d * LANES:2 * (d + 1) * LANES]
                        num = sv_d[:, 0:LANES] + g * qc_d[:, 0:LANES]
                        den = sv_d[:, LANES:2 * LANES] + g * qc_d[:, LANES:2 * LANES]
                        hd = num / jnp.maximum(jnp.abs(den), jnp.exp(-(b_t + m_t)))
                        hsum = hd if hsum is None else hsum + hd
                    ms = jnp.mean(hsum * hsum, axis=1, keepdims=True)
                    y = hsum * lax.rsqrt(ms + EPS) * ng_ref[:, h * LANES:(h + 1) * LANES]
                    y = y * _sigmoid(o.astype(F32))
                    y_ref[rows, h * LANES:(h + 1) * LANES] = y.astype(y_ref.dtype)
                update(0, h, cg, kt, v1)
            return carry
        lax.fori_loop(0, nch, body, 0, unroll=UNROLL)

    crun[...] = jnp.zeros(crun.shape, F32)
    bwd_pass(uc_ref, ncc, 0, need_ctx_out)
    bwd_pass(ul_ref, ncl, ncc, True)
    fwd_pass(uc_ref, yc_ref, ncc, 0)
    fwd_pass(ul_ref, yl_ref, ncl, ncc)


def _ml_expand_table():
    expand = np.zeros((LANES, ML_GATES * 2 * LANES), np.float32)
    for j in range(ML_GATES):
        for i in range(6):
            half = i // 3
            expand[ML_GATES * i + j, (2 * j + half) * LANES:(2 * j + half + 1) * LANES] = 1.0
    return jnp.asarray(expand, BF16)


def _mlstm_mixer(uml_l, uml_c, g_l, g_c, pvec, norm_g, need_ctx_out):
    expand = _ml_expand_table()
    bsz, n_lat, _ = uml_l.shape
    n_ctx = uml_c.shape[1]
    nct = (n_lat + n_ctx) // CHUNK
    kern = functools.partial(_mlstm_kernel, need_ctx_out=need_ctx_out, n_lat=n_lat, n_ctx=n_ctx)
    out_specs = [pl.BlockSpec((None, n_lat, ML_WIDTH), lambda b: (b, 0, 0))]
    out_shape = [jax.ShapeDtypeStruct((bsz, n_lat, ML_WIDTH), BF16)]
    if need_ctx_out:
        out_specs.append(pl.BlockSpec((None, n_ctx, ML_WIDTH), lambda b: (b, 0, 0)))
        out_shape.append(jax.ShapeDtypeStruct((bsz, n_ctx, ML_WIDTH), BF16))
    outs = pl.pallas_call(
        kern,
        grid=(bsz,),
        in_specs=[
            pl.BlockSpec((None, n_lat, U_ML), lambda b: (b, 0, 0)),
            pl.BlockSpec((None, n_ctx, U_ML), lambda b: (b, 0, 0)),
            pl.BlockSpec((None, n_lat, U_GATE), lambda b: (b, 0, 0)),
            pl.BlockSpec((None, n_ctx, U_GATE), lambda b: (b, 0, 0)),
            pl.BlockSpec((8, LANES), lambda b: (0, 0)),
            pl.BlockSpec((1, ML_WIDTH), lambda b: (0, 0)),
            pl.BlockSpec(expand.shape, lambda b: (0, 0)),
        ],
        out_specs=out_specs,
        out_shape=out_shape,
        scratch_shapes=[
            pltpu.VMEM((nct, ML_GATES, CHUNK), F32),
            pltpu.VMEM((nct, CHUNK, LANES), BF16),
            pltpu.VMEM((nct, 2, ML_GATES, CHUNK), F32),
            pltpu.VMEM((nct, N_DIR, ML_GATES, CHUNK), F32),
            pltpu.VMEM((nct, ML_HEADS, ML_HEAD_DIM, 2 * LANES), BF16),
            pltpu.VMEM((N_DIR, ML_HEADS, ML_HEAD_DIM, 2 * LANES), F32),
        ],
        compiler_params=pltpu.CompilerParams(
            dimension_semantics=("arbitrary",), vmem_limit_bytes=VMEM_LIMIT),
        name="mlstm",
    )(uml_l, uml_c, g_l, g_c, pvec, norm_g, expand)
    return (outs[0], outs[1]) if need_ctx_out else (outs[0], None)


def _ssd_kernel(ul_ref, uc_ref, gl_ref, gc_ref, pv_ref, cw_ref, cb_ref, dsk_ref, ng_ref, ex_ref, *rest,
                need_ctx_out, n_lat, n_ctx):
    if need_ctx_out:
        yl_ref, yc_ref, xact, bt, acum, packed, act, dtt, hbk, hrun = rest
    else:
        yl_ref, xact, bt, acum, packed, act, dtt, hbk, hrun = rest
        yc_ref = None
    ncl, ncc = n_lat // CHUNK, n_ctx // CHUNK
    nh, hg = SSD_HEADS, SSD_HEADS // SSD_GROUPS
    gw = hg * SSD_HEAD_DIM
    gn = SSD_GROUPS * SSD_STATE
    ti = _iota((CHUNK, CHUNK), 0)
    si = _iota((CHUNK, CHUNK), 1)
    low_strict = si < ti
    upp_strict = si > ti
    tri_low = jnp.where(si <= ti, 1.0, 0.0).astype(BF16)
    tri_upp = jnp.where(si >= ti, 1.0, 0.0).astype(BF16)
    lane = _iota((CHUNK, LANES), 1)
    nj = N_DIR * nh
    ac_lane0 = 6 * nj
    fwd_row = _iota((nj, CHUNK), 0) < nh
    bias0 = pv_ref[0:1, :]
    a_rows = jnp.broadcast_to(-jnp.exp(pv_ref[2:3, :]), (LANES, LANES)).T[DT_LANE0:DT_LANE0 + nj]
    shift_down = jnp.where(si == ti - 1, 1.0, 0.0).astype(BF16)
    shift_up = jnp.where(si == ti + 1, 1.0, 0.0).astype(BF16)
    lane_group0 = lane < SSD_STATE
    row_group0 = _iota((gn, CHUNK), 0) < SSD_STATE

    def prep(u_ref, g_ref, nch, goff, nrows):
        def body(c, carry):
            r = pl.multiple_of(c * CHUNK, CHUNK)
            cur = u_ref[pl.ds(r, CHUNK), SSD_WIDTH:U_SSD]
            rp = pl.multiple_of(jnp.maximum(r - CHUNK, 0), CHUNK)
            rn = pl.multiple_of(jnp.minimum(r + CHUNK, nrows - CHUNK), CHUNK)
            prev = u_ref[pl.ds(rp, CHUNK), SSD_WIDTH:U_SSD]
            nxt = u_ref[pl.ds(rn, CHUNK), SSD_WIDTH:U_SSD]
            from_prev = jnp.where((ti == 0) & (si == CHUNK - 1) & (c > 0), 1.0, 0.0).astype(BF16)
            from_next = jnp.where((ti == CHUNK - 1) & (si == 0) & (c < nch - 1), 1.0, 0.0).astype(BF16)
            xm = _dot(jnp.concatenate([shift_down, from_prev], axis=1), jnp.concatenate([cur, prev], axis=0))
            xp = _dot(jnp.concatenate([shift_up, from_next], axis=1), jnp.concatenate([cur, nxt], axis=0))
            conv = (cw_ref[0:1, :] * xm + cw_ref[1:2, :] * cur.astype(F32) + cw_ref[2:3, :] * xp
                    + cb_ref[...])
            xa = (conv * _sigmoid(conv)).astype(BF16)
            xact[goff + c] = xa
            bt[goff + c] = xa[:, SSD_WIDTH:SSD_WIDTH + gn].T

            gt = (g_ref[pl.ds(r, CHUNK), :] + bias0).T
            dt = _softplus(gt[DT_LANE0:DT_LANE0 + nj])
            prefix, suffix = _row_cumsums((tri_upp, tri_low), dt * a_rows)
            ac = jnp.where(fwd_row, prefix, suffix)
            a_end = jnp.where(fwd_row[:, 0:1], ac[:, CHUNK - 1:CHUNK], ac[:, 0:1])
            act[goff + c] = ac - jnp.log(dt)
            dtt[goff + c] = dt
            parts = [p.astype(F32) for p in _split3(jnp.exp(ac)) + _split3(jnp.exp(a_end - ac) * dt)]
            parts += [ac, jnp.zeros((CHUNK - 7 * nj, CHUNK), F32)]
            tile = jnp.concatenate(parts, axis=0).T
            acum[goff + c] = tile
            packed[goff + c] = tile.astype(BF16)
            return carry
        lax.fori_loop(0, nch, body, 0, unroll=UNROLL)

    prep(uc_ref, gc_ref, ncc, 0, n_ctx)
    prep(ul_ref, gl_ref, ncl, ncc, n_lat)

    def e_cols(d, g):
        return slice(d * SSD_WIDTH + g * gw, d * SSD_WIDTH + (g + 1) * gw)

    def w_cols(d, g):
        return slice((N_DIR + d) * SSD_WIDTH + g * gw, (N_DIR + d) * SSD_WIDTH + (g + 1) * gw)

    def update_state(d, cg):
        pk = packed[cg]
        xa = xact[cg]
        b_t = bt[cg]
        zero = jnp.zeros_like(b_t)
        upd = []
        for g in range(SSD_GROUPS):
            wx = (_dot(pk, ex_ref[:, w_cols(d, g)]) * xa[:, g * gw:(g + 1) * gw].astype(F32)).astype(BF16)
            upd.append(_dot(jnp.where(row_group0 == (g == 0), b_t, zero), wx))
        tail, last = (slice(CHUNK - 16, CHUNK), 15) if d == 0 else (slice(0, 16), 0)
        decay = _dot(pk[tail], ex_ref[:, d * SSD_WIDTH:(d + 1) * SSD_WIDTH])[last:last + 1, :]
        hrun[d] = decay * hrun[d] + jnp.concatenate(upd, axis=1)

    def bwd_pass(nch, goff, keep):
        def body(i, carry):
            cg = goff + nch - 1 - i
            if keep:
                hbk[cg] = hrun[1].astype(BF16)
            update_state(1, cg)
            return carry
        lax.fori_loop(0, nch, body, 0, unroll=UNROLL)

    def fwd_pass(u_ref, y_ref, nch, goff):
        def body(c, carry):
            cg = goff + c
            if y_ref is not None:
                rows = _chunk_rows(c)
                xa = xact[cg]
                x = xa[:, 0:SSD_WIDTH]
                b_t = bt[cg]
                c_all = xa[:, SSD_WIDTH + gn:SSD_XBC]
                ac = acum[cg]
                ac_t = act[cg]
                dt_t = dtt[cg]
                lane_w = _iota((CHUNK, gw), 1)
                pk = packed[cg]
                yz_groups, ssq = [], None
                for g in range(SSD_GROUPS):
                    cols = slice(g * gw, (g + 1) * gw)
                    c_g = jnp.where(lane_group0 == (g == 0), c_all, jnp.zeros_like(c_all))
                    cb = _dot(c_g, b_t)
                    s_heads = []
                    for hh in range(hg):
                        jf = g * hg + hh
                        jb = jf + nh
                        on_diag = jnp.log(dt_t[jf:jf + 1, :] + dt_t[jb:jb + 1, :])
                        ac_f = ac[:, ac_lane0 + jf:ac_lane0 + jf + 1]
                        ac_b = ac[:, ac_lane0 + jb:ac_lane0 + jb + 1]
                        expo = jnp.where(low_strict, ac_f - ac_t[jf:jf + 1, :],
                                         jnp.where(upp_strict, ac_b - ac_t[jb:jb + 1, :], on_diag))
                        s_heads.append((cb * jnp.exp(expo)).astype(BF16))
                    x_g = x[:, cols]
                    bd = jnp.concatenate(
                        [jnp.where((lane_w >= hh * SSD_HEAD_DIM) & (lane_w < (hh + 1) * SSD_HEAD_DIM),
                                   x_g, jnp.zeros_like(x_g)) for hh in range(hg)], axis=0)
                    y = _dot(jnp.concatenate(s_heads, axis=1), bd)
                    y = y + _dot(c_all, hrun[0, :, cols].astype(BF16)) * _dot(pk, ex_ref[:, e_cols(0, g)])
                    y = y + _dot(c_all, hbk[cg, :, cols]) * _dot(pk, ex_ref[:, e_cols(1, g)])
                    y = y + dsk_ref[:, cols] * x_g.astype(F32)
                    z = u_ref[rows, cols].astype(F32)
                    yz = y * (z * _sigmoid(z))
                    part = jnp.sum(yz * yz, axis=1, keepdims=True)
                    ssq = part if ssq is None else ssq + part
                    yz_groups.append(yz)
                scale = lax.rsqrt(ssq * (1.0 / SSD_WIDTH) + EPS)
                for g, yz in enumerate(yz_groups):
                    cols = slice(g * gw, (g + 1) * gw)
                    y_ref[rows, cols] = (yz * scale * ng_ref[:, cols]).astype(y_ref.dtype)
            update_state(0, cg)
            return carry
        lax.fori_loop(0, nch, body, 0, unroll=UNROLL)

    hrun[...] = jnp.zeros(hrun.shape, F32)
    bwd_pass(ncc, 0, need_ctx_out)
    bwd_pass(ncl, ncc, True)
    fwd_pass(uc_ref, yc_ref, ncc, 0)
    fwd_pass(ul_ref, yl_ref, ncl, ncc)


def _ssd_expand_table():
    nj = N_DIR * SSD_HEADS
    expand = np.zeros((LANES, 2 * N_DIR * SSD_WIDTH), np.float32)
    for i in range(6):
        for d in range(N_DIR):
            for h in range(SSD_HEADS):
                j = d * SSD_HEADS + h
                c0 = ((i // 3) * N_DIR + d) * SSD_WIDTH + h * SSD_HEAD_DIM
                expand[nj * i + j, c0:c0 + SSD_HEAD_DIM] = 1.0
    return jnp.asarray(expand, BF16)


def _ssd_mixer(ussd_l, ussd_c, g_l, g_c, pvec, conv_w, conv_b, d_skip, norm_g, need_ctx_out):
    expand = _ssd_expand_table()
    bsz, n_lat, _ = ussd_l.shape
    n_ctx = ussd_c.shape[1]
    nct = (n_lat + n_ctx) // CHUNK
    gn = SSD_GROUPS * SSD_STATE
    kern = functools.partial(_ssd_kernel, need_ctx_out=need_ctx_out, n_lat=n_lat, n_ctx=n_ctx)
    out_specs = [pl.BlockSpec((None, n_lat, SSD_WIDTH), lambda b: (b, 0, 0))]
    out_shape = [jax.ShapeDtypeStruct((bsz, n_lat, SSD_WIDTH), BF16)]
    if need_ctx_out:
        out_specs.append(pl.BlockSpec((None, n_ctx, SSD_WIDTH), lambda b: (b, 0, 0)))
        out_shape.append(jax.ShapeDtypeStruct((bsz, n_ctx, SSD_WIDTH), BF16))
    outs = pl.pallas_call(
        kern,
        grid=(bsz,),
        in_specs=[
            pl.BlockSpec((None, n_lat, U_SSD), lambda b: (b, 0, 0)),
            pl.BlockSpec((None, n_ctx, U_SSD), lambda b: (b, 0, 0)),
            pl.BlockSpec((None, n_lat, LANES), lambda b: (b, 0, 0)),
            pl.BlockSpec((None, n_ctx, LANES), lambda b: (b, 0, 0)),
            pl.BlockSpec((8, LANES), lambda b: (0, 0)),
            pl.BlockSpec((3, SSD_XBC), lambda b: (0, 0)),
            pl.BlockSpec((1, SSD_XBC), lambda b: (0, 0)),
            pl.BlockSpec((1, SSD_WIDTH), lambda b: (0, 0)),
            pl.BlockSpec((1, SSD_WIDTH), lambda b: (0, 0)),
            pl.BlockSpec(expand.shape, lambda b: (0, 0)),
        ],
        out_specs=out_specs,
        out_shape=out_shape,
        scratch_shapes=[
            pltpu.VMEM((nct, CHUNK, SSD_XBC), BF16),
            pltpu.VMEM((nct, gn, CHUNK), BF16),
            pltpu.VMEM((nct, CHUNK, LANES), F32),
            pltpu.VMEM((nct, CHUNK, LANES), BF16),
            pltpu.VMEM((nct, N_DIR * SSD_HEADS, CHUNK), F32),
            pltpu.VMEM((nct, N_DIR * SSD_HEADS, CHUNK), F32),
            pltpu.VMEM((nct, gn, SSD_WIDTH), BF16),
            pltpu.VMEM((N_DIR, gn, SSD_WIDTH), F32),
        ],
        compiler_params=pltpu.CompilerParams(
            dimension_semantics=("arbitrary",), vmem_limit_bytes=VMEM_LIMIT),
        name="ssd",
    )(ussd_l, ussd_c, g_l, g_c, pvec, conv_w, conv_b, d_skip, norm_g, expand)
    return (outs[0], outs[1]) if need_ctx_out else (outs[0], None)


def _head_norm(x, gain):
    rows = x.shape[0]
    blks = [x[:, j * LANES:(j + 1) * LANES] for j in range(x.shape[1] // LANES)]
    same_head = (_iota((LANES, LANES), 0) < ATT_HEAD_DIM) == (_iota((LANES, LANES), 1) < ATT_HEAD_DIM)
    sq = jnp.concatenate([b * b for b in blks], axis=0).astype(BF16)
    ms = _dot(sq, jnp.where(same_head, 1.0, 0.0).astype(BF16)) * (1.0 / ATT_HEAD_DIM)
    outs = [b * lax.rsqrt(ms[j * rows:(j + 1) * rows] + EPS) * gain for j, b in enumerate(blks)]
    return outs[0] if len(outs) == 1 else jnp.concatenate(outs, axis=1)


def _rope(x, cos_t, sin_t):
    w = x.shape[1]
    reps = w // LANES
    nfreq = ATT_HEAD_DIM // 4
    partner = jnp.where((_iota(x.shape, 1) & (2 * nfreq - 1)) < nfreq,
                        pltpu.roll(x, w - nfreq, 1), pltpu.roll(x, nfreq, 1))
    if reps > 1:
        cos_t = jnp.concatenate([cos_t] * reps, axis=1)
        sin_t = jnp.concatenate([sin_t] * reps, axis=1)
    return x * cos_t + partner * sin_t


def _dup_heads(x):
    lane_lo = _iota(x.shape, 1) < ATT_HEAD_DIM
    sw = pltpu.roll(x, ATT_HEAD_DIM, 1)
    return jnp.concatenate([jnp.where(lane_lo, x, sw), jnp.where(lane_lo, sw, x)], axis=1)


def _attn_kernel(ul_ref, uc_ref, cos_ref, sin_ref, qg_ref, kg_ref, sink_ref, *rest,
                 need_ctx_out, n_lat, n_ctx):
    if need_ctx_out:
        yl_ref, yc_ref, qp, kb, vb, kc, vc, qcp = rest
    else:
        yl_ref, qp, kb, vb, kc, vc = rest
        yc_ref = qcp = None
    ncl, ncc = n_lat // CHUNK, n_ctx // CHUNK
    rep = ATT_Q_HEADS // ATT_KV_HEADS
    kv0 = ATT_WIDTH
    lane_lo = _iota((CHUNK, LANES), 1) < ATT_HEAD_DIM
    neg_inf = -jnp.inf
    qg = qg_ref[...]
    kg = kg_ref[...]

    zeros_blk = jnp.zeros((CHUNK, vb.shape[1]), BF16)
    vb[0:CHUNK, :] = zeros_blk
    vb[CHUNK + n_lat:2 * CHUNK + n_lat, :] = zeros_blk
    kb[0] = jnp.zeros(kb.shape[1:], BF16)
    kb[ncl + 1] = jnp.zeros(kb.shape[1:], BF16)

    def store_keys_transposed(ref, blk, k):
        kt = k.T
        for g in range(ATT_KV_HEADS):
            head = kt[g * ATT_HEAD_DIM:(g + 1) * ATT_HEAD_DIM]
            ref[blk, g * LANES:(g + 1) * LANES, :] = jnp.concatenate([head, head], axis=0).astype(BF16)

    def values_with_ones(v):
        d = _dup_heads(v).astype(BF16)
        ones = jnp.ones((v.shape[0], LANES), BF16)
        return jnp.concatenate([d[:, 0:LANES], ones, d[:, LANES:2 * LANES], ones], axis=1)

    def prep_lat(c, carry):
        rows = _chunk_rows(c)
        cos_t = cos_ref[rows, :]
        sin_t = sin_ref[rows, :]
        q = _rope(_head_norm(ul_ref[rows, 0:ATT_WIDTH].astype(F32), qg), cos_t, sin_t)
        qp[rows, :] = (q * ATT_SCALE).astype(BF16)
        k = _rope(_head_norm(ul_ref[rows, kv0:kv0 + LANES].astype(F32), kg), cos_t, sin_t)
        v = ul_ref[rows, kv0 + LANES:kv0 + 2 * LANES].astype(F32)
        dst = pl.ds(pl.multiple_of((c + 1) * CHUNK, CHUNK), CHUNK)
        store_keys_transposed(kb, c + 1, k)
        vb[dst, :] = values_with_ones(v)
        return carry
    lax.fori_loop(0, ncl, prep_lat, 0, unroll=UNROLL)

    def prep_ctx(c, carry):
        rows = _chunk_rows(c)
        k = _head_norm(uc_ref[rows, kv0:kv0 + LANES].astype(F32), kg)
        v = uc_ref[rows, kv0 + LANES:kv0 + 2 * LANES].astype(F32)
        store_keys_transposed(kc, c, k)
        vc[rows, :] = values_with_ones(v)
        if need_ctx_out:
            q = _head_norm(uc_ref[rows, 0:ATT_WIDTH].astype(F32), qg)
            qcp[rows, :] = (q * ATT_SCALE).astype(BF16)
        return carry
    lax.fori_loop(0, ncc, prep_ctx, 0, unroll=UNROLL)

    def attend(q_ref, y_ref, rows, band):
        for g in range(ATT_KV_HEADS):
            kl = slice(g * LANES, (g + 1) * LANES)
            vl = slice(2 * g * LANES, 2 * (g + 1) * LANES)
            k_ctx = jnp.concatenate([kc[j, kl, :] for j in range(ncc)], axis=1)
            if band is not None:
                blk, band_rows, valid = band
                k_band = jnp.concatenate([kb[blk + j, kl, :] for j in range(3)], axis=1)
            for r0 in range(0, rep, ATT_STACK):
                heads = [g * rep + r0 + i for i in range(ATT_STACK)]
                qms = []
                for hq in heads:
                    qb = q_ref[rows, (hq // 2) * LANES:(hq // 2 + 1) * LANES]
                    qms.append(jnp.where(lane_lo != (hq % 2 == 1), qb, jnp.zeros_like(qb)))
                qs = qms[0] if ATT_STACK == 1 else jnp.concatenate(qms, axis=0)
                s_c = _dot(qs, k_ctx)
                if band is not None:
                    s_b = _dot(qs, k_band)
                ps, sink_terms = [], []
                for i, hq in enumerate(heads):
                    sl = slice(i * CHUNK, (i + 1) * CHUNK)
                    sk = sink_ref[hq]
                    s = s_c[sl]
                    if band is not None:
                        prev_k, own_k, next_k = (slice(j * CHUNK, (j + 1) * CHUNK) for j in range(3))
                        s = jnp.concatenate([s, jnp.where(valid[:, prev_k], s_b[sl, prev_k], neg_inf), s_b[sl, own_k],
                                             jnp.where(valid[:, next_k], s_b[sl, next_k], neg_inf)], axis=1)
                    m = jnp.maximum(jnp.max(s, axis=1, keepdims=True), sk)
                    ps.append(jnp.exp(s - m).astype(BF16))
                    sink_terms.append(jnp.exp(sk - m))
                p = ps[0] if ATT_STACK == 1 else jnp.concatenate(ps, axis=0)
                acc = _dot(p[:, 0:n_ctx], vc[:, vl])
                if band is not None:
                    acc = acc + _dot(p[:, n_ctx:], vb[band_rows, vl])
                for i, hq in enumerate(heads):
                    sl = slice(i * CHUNK, (i + 1) * CHUNK)
                    o = acc[sl, 0:LANES] / (acc[sl, LANES:2 * LANES] + sink_terms[i])
                    if hq % 2 == 1:
                        jb = hq // 2
                        y_ref[rows, jb * LANES:(jb + 1) * LANES] = jnp.where(lane_lo, o_even, o).astype(y_ref.dtype)
                    o_even = o

    assert WINDOW == CHUNK
    qq = _iota((CHUNK, 3 * CHUNK), 0)
    kk = _iota((CHUNK, 3 * CHUNK), 1)
    in_window = (kk >= qq) & (kk <= qq + 2 * WINDOW)

    def lat_block(i, carry):
        kpos = kk + (i - 1) * CHUNK
        valid = in_window & (kpos >= 0) & (kpos < n_lat)
        band_rows = pl.ds(pl.multiple_of(i * CHUNK, CHUNK), 3 * CHUNK)
        attend(qp, yl_ref, _chunk_rows(i), (i, band_rows, valid))
        return carry
    lax.fori_loop(0, ncl, lat_block, 0, unroll=UNROLL)

    if need_ctx_out:
        def ctx_block(i, carry):
            attend(qcp, yc_ref, _chunk_rows(i), None)
            return carry
        lax.fori_loop(0, ncc, ctx_block, 0, unroll=UNROLL)


def _attn_mixer(uatt_l, uatt_c, cos_t, sin_t, qgain, kgain, sink, need_ctx_out):
    bsz, n_lat, _ = uatt_l.shape
    n_ctx = uatt_c.shape[1]
    kern = functools.partial(_attn_kernel, need_ctx_out=need_ctx_out, n_lat=n_lat, n_ctx=n_ctx)
    out_specs = [pl.BlockSpec((None, n_lat, ATT_WIDTH), lambda b: (b, 0, 0))]
    out_shape = [jax.ShapeDtypeStruct((bsz, n_lat, ATT_WIDTH), BF16)]
    scratch = [
        pltpu.VMEM((n_lat, ATT_WIDTH), BF16),
        pltpu.VMEM((n_lat // CHUNK + 2, 2 * LANES, CHUNK), BF16),
        pltpu.VMEM((n_lat + 2 * CHUNK, 4 * LANES), BF16),
        pltpu.VMEM((n_ctx // CHUNK, 2 * LANES, CHUNK), BF16),
        pltpu.VMEM((n_ctx, 4 * LANES), BF16),
    ]
    if need_ctx_out:
        out_specs.append(pl.BlockSpec((None, n_ctx, ATT_WIDTH), lambda b: (b, 0, 0)))
        out_shape.append(jax.ShapeDtypeStruct((bsz, n_ctx, ATT_WIDTH), BF16))
        scratch.append(pltpu.VMEM((n_ctx, ATT_WIDTH), BF16))
    outs = pl.pallas_call(
        kern,
        grid=(bsz,),
        in_specs=[
            pl.BlockSpec((None, n_lat, U_ATT), lambda b: (b, 0, 0)),
            pl.BlockSpec((None, n_ctx, U_ATT), lambda b: (b, 0, 0)),
            pl.BlockSpec((n_lat, LANES), lambda b: (0, 0)),
            pl.BlockSpec((n_lat, LANES), lambda b: (0, 0)),
            pl.BlockSpec((1, LANES), lambda b: (0, 0)),
            pl.BlockSpec((1, LANES), lambda b: (0, 0)),
            pl.BlockSpec(memory_space=pltpu.SMEM),
        ],
        out_specs=out_specs,
        out_shape=out_shape,
        scratch_shapes=scratch,
        compiler_params=pltpu.CompilerParams(
            dimension_semantics=("arbitrary",), vmem_limit_bytes=VMEM_LIMIT),
        name="attention",
    )(uatt_l, uatt_c, cos_t, sin_t, qgain, kgain, sink)
    return (outs[0], outs[1]) if need_ctx_out else (outs[0], None)


def _in_layout():
    names = (('ml_q', ML_WIDTH), ('ml_k', ML_WIDTH), ('ml_v', ML_WIDTH), ('ml_o', ML_WIDTH),
             ('ml_i', N_DIR * ML_HEADS), ('ml_f', N_DIR * ML_HEADS),
             ('ssd_z', SSD_WIDTH), ('ssd_xbc', SSD_XBC), ('ssd_dt', N_DIR * SSD_HEADS),
             ('att_q', ATT_WIDTH), ('att_k', ATT_KV_WIDTH), ('att_v', ATT_KV_WIDTH))
    out, off = {}, 0
    for name, width in names:
        out[name] = (off, off + width)
        off += width
    return out


def _col_map():
    lay = _in_layout()
    take = lambda name: np.arange(lay[name][0], lay[name][1])
    zeros = lambda n: np.full((n,), -1)
    n_f, n_dt, n_i = N_DIR * ML_HEADS, N_DIR * SSD_HEADS, N_DIR * ML_HEADS
    return np.concatenate(
        [take('ml_q'), take('ml_k'), take('ml_v'), take('ml_o'),
         take('ssd_z'), take('ssd_xbc'),
         take('att_q'), take('att_k'), take('att_v'),
         take('ml_f'), take('ssd_dt'), take('ml_i'), zeros(LANES - n_f - n_dt - n_i)])


def _permute_plan():
    src = _col_map()
    plan = []
    for t in range(U_TOTAL // LANES):
        cols = src[t * LANES:(t + 1) * LANES]
        runs, i = [], 0
        while i < LANES:
            j = i
            while j + 1 < LANES and (cols[j + 1] == cols[j] + 1 if cols[j] >= 0 else cols[j + 1] < 0):
                j += 1
            runs.append((int(cols[i]), j - i + 1))
            i = j + 1
        plan.append(tuple(runs))
    return tuple(plan)


def _wprep_kernel(wt_ref, o_ref, *, plan):
    kb = wt_ref.shape[1]
    for t, runs in enumerate(plan):
        pieces = [wt_ref[s:s + n, :] if s >= 0 else jnp.zeros((n, kb), F32) for s, n in runs]
        blk = pieces[0] if len(pieces) == 1 else jnp.concatenate(pieces, axis=0)
        o_ref[:, t * LANES:(t + 1) * LANES] = blk.T.astype(BF16)


def _permute_weights(w_in):
    depth, d, in_width = w_in.shape
    plan = _permute_plan()
    kb = 256
    return pl.pallas_call(
        functools.partial(_wprep_kernel, plan=plan),
        grid=(depth, d // kb),
        in_specs=[pl.BlockSpec((None, in_width, kb), lambda l, i: (l, 0, i))],
        out_specs=pl.BlockSpec((None, kb, U_TOTAL), lambda l, i: (l, i, 0)),
        out_shape=jax.ShapeDtypeStruct((depth, d, U_TOTAL), BF16),
        compiler_params=pltpu.CompilerParams(
            dimension_semantics=("arbitrary", "arbitrary"), vmem_limit_bytes=VMEM_LIMIT),
        name="permute_weights",
    )(jnp.swapaxes(w_in, 1, 2))


def _gate_rows(ml_b_i, ml_b_f, ssd_dt_bias, ssd_a_log):
    n_f, n_dt = N_DIR * ML_HEADS, N_DIR * SSD_HEADS
    row0 = jnp.concatenate([ml_b_f.reshape(-1), ssd_dt_bias.reshape(-1), ml_b_i.reshape(-1),
                            jnp.zeros((LANES - 2 * n_f - n_dt,), F32)])
    row2 = jnp.concatenate([jnp.zeros((n_f,), F32), ssd_a_log.reshape(-1), jnp.zeros((LANES - n_f - n_dt,), F32)])
    zero = jnp.zeros((1, LANES), F32)
    return jnp.concatenate([row0[None], zero, row2[None], jnp.zeros((5, LANES), F32)], axis=0)


def _rope_tables(n_lat):
    nfreq = ATT_HEAD_DIM // 4
    pos = np.arange(n_lat)
    rows, cols = pos // GRID_W, pos % GRID_W
    inv = np.float32(ROPE_BASE) ** (-np.arange(nfreq, dtype=np.float32) / np.float32(nfreq))
    ang_r = rows.astype(np.float32)[:, None] * inv
    ang_c = cols.astype(np.float32)[:, None] * inv
    cos_h = np.concatenate([np.cos(ang_r)] * 2 + [np.cos(ang_c)] * 2, axis=1)
    sin_h = np.concatenate([-np.sin(ang_r), np.sin(ang_r), -np.sin(ang_c), np.sin(ang_c)], axis=1)
    return (jnp.asarray(np.concatenate([cos_h] * 2, axis=1), F32),
            jnp.asarray(np.concatenate([sin_h] * 2, axis=1), F32))


def _token_mixers(u_l, u_c, p, layer, tables, need_ctx_out):
    uml_l, ussd_l, uatt_l, g_l = u_l
    uml_c, ussd_c, uatt_c, g_c = u_c
    pvec = _gate_rows(p['ml_b_i'][layer], p['ml_b_f'][layer], p['ssd_dt_bias'][layer], p['ssd_a_log'][layer])
    m_l, m_c = _mlstm_mixer(uml_l, uml_c, g_l, g_c, pvec, p['ml_norm_g'][layer][None], need_ctx_out)
    s_l, s_c = _ssd_mixer(ussd_l, ussd_c, g_l, g_c, pvec, p['ssd_conv_w'][layer], p['ssd_conv_b'][layer][None],
                          jnp.repeat(p['ssd_d'][layer], SSD_HEAD_DIM)[None], p['ssd_norm_g'][layer][None],
                          need_ctx_out)
    a_l, a_c = _attn_mixer(uatt_l, uatt_c, tables['cos'], tables['sin'],
                           jnp.tile(p['att_qn_g'][layer], 2)[None], jnp.tile(p['att_kn_g'][layer], 2)[None],
                           p['att_sink'][layer], need_ctx_out)
    return (m_l, s_l, a_l), (m_c, s_c, a_c)


def kernel(x, c, ctx, c_ctx, w_mod, b_mod, norm1_g, w_in, ml_b_i, ml_b_f, ml_norm_g, ssd_conv_w, ssd_conv_b,
           ssd_a_log, ssd_dt_bias, ssd_d, ssd_norm_g, att_qn_g, att_kn_g, att_sink, w_out, norm2_g, w_up, w_down):
    p = dict(ml_b_i=ml_b_i, ml_b_f=ml_b_f, ml_norm_g=ml_norm_g, ssd_conv_w=ssd_conv_w, ssd_conv_b=ssd_conv_b,
             ssd_a_log=ssd_a_log, ssd_dt_bias=ssd_dt_bias, ssd_d=ssd_d, ssd_norm_g=ssd_norm_g,
             att_qn_g=att_qn_g, att_kn_g=att_kn_g, att_sink=att_sink)
    bsz, n_lat, d = x.shape
    n_ctx = ctx.shape[1]
    depth = w_mod.shape[0]
    mod_rows = 16
    cc = jnp.concatenate([c, c_ctx[None], jnp.zeros((mod_rows - bsz - 1, d), F32)], axis=0)
    mods = _modulation(cc, w_mod, b_mod).reshape(depth, mod_rows, 6, d)
    cos_t, sin_t = _rope_tables(n_lat)
    tables = dict(cos=cos_t, sin=sin_t)
    lat_row = lambda b: b
    ctx_row = lambda b: bsz
    h_ctx = ctx
    w = _permute_weights(w_in)
    wo, wup, wdn = w_out.astype(BF16), w_up.astype(BF16), w_down.astype(BF16)
    g1, g2 = norm1_g[:, None, :], norm2_g[:, None, :]
    for layer in range(depth):
        need_ctx_out = layer < depth - 1
        u_l = _in_proj(x, mods, lat_row, g1, w, layer, tm=1024)
        u_c = _in_proj(h_ctx, mods, ctx_row, g1, w, layer, tm=256)
        y_l, y_c = _token_mixers(u_l, u_c, p, layer, tables, need_ctx_out)
        x = _post(x, *y_l, mods, lat_row, g2, wo, wup, wdn, layer, tm=512)
        if need_ctx_out:
            h_ctx = _post(h_ctx, *y_c, mods, ctx_row, g2, wo, wup, wdn, layer, tm=256)
    return x
```

```python
import functools
import math

import numpy as np
import jax
import jax.numpy as jnp
from jax import lax
from jax.experimental import pallas as pl
from jax.experimental.pallas import tpu as pltpu

F32 = jnp.float32
BF16 = jnp.bfloat16

EPS = 1e-6
GRID_W = 64
N_DIR = 2
ML_HEADS = 4
ML_HEAD_DIM = 128
ML_WIDTH = ML_HEADS * ML_HEAD_DIM
SSD_HEADS = 8
SSD_HEAD_DIM = 64
SSD_WIDTH = SSD_HEADS * SSD_HEAD_DIM
SSD_STATE = 64
SSD_GROUPS = 2
SSD_XBC = SSD_WIDTH + 2 * SSD_GROUPS * SSD_STATE
ATT_Q_HEADS = 8
ATT_KV_HEADS = 2
ATT_HEAD_DIM = 64
ATT_WIDTH = ATT_Q_HEADS * ATT_HEAD_DIM
ATT_KV_WIDTH = ATT_KV_HEADS * ATT_HEAD_DIM
WINDOW = 128
ROPE_BASE = 10000.0
MIX_WIDTH = ML_WIDTH + SSD_WIDTH + ATT_WIDTH

LANES = 128
CHUNK = 128
VMEM_LIMIT = 56 * 1024 * 1024
UNROLL = 16
ATT_STACK = 4
PREP_GROUP = 8

U_ML = 4 * ML_WIDTH
U_SSD = SSD_WIDTH + SSD_XBC
U_ATT = ATT_WIDTH + 2 * ATT_KV_WIDTH
U_GATE = LANES
U_TOTAL = U_ML + U_SSD + U_ATT + U_GATE
DT_LANE0 = N_DIR * ML_HEADS
LI_LANE0 = DT_LANE0 + N_DIR * SSD_HEADS

ML_SCALE = ML_HEAD_DIM ** -0.5
ATT_SCALE = ATT_HEAD_DIM ** -0.5


def _dot(a, b):
    return jnp.dot(a, b, preferred_element_type=F32)


def _split2(x):
    hi = x.astype(BF16)
    lo = (x - hi.astype(F32)).astype(BF16)
    return hi, lo


def _split3(x):
    hi = x.astype(BF16)
    r = x - hi.astype(F32)
    mid = r.astype(BF16)
    lo = (r - mid.astype(F32)).astype(BF16)
    return hi, mid, lo


def _row_cumsums(tris, x):
    rows = x.shape[0]
    parts = _split3(jnp.concatenate([x, jnp.zeros_like(x)], axis=0))
    return [(_dot(parts[0], tri) + _dot(parts[1], tri) + _dot(parts[2], tri))[0:rows] for tri in tris]


def _sigmoid(x):
    return 1.0 / (1.0 + jnp.exp2(x * (-1.0 / math.log(2.0))))


def _log_sigmoid(x):
    return jnp.minimum(x, 0.0) - jnp.log1p(jnp.exp(-jnp.abs(x)))


def _softplus(x):
    return jnp.maximum(x, 0.0) + jnp.log1p(jnp.exp(-jnp.abs(x)))


def _iota(shape, dim):
    return lax.broadcasted_iota(jnp.int32, shape, dim)


def _chunk_rows(c):
    return pl.ds(pl.multiple_of(c * CHUNK, CHUNK), CHUNK)


def _mod_kernel(s_ref, w_ref, b_ref, o_ref):
    s = s_ref[...]
    s = s * _sigmoid(s)
    w = w_ref[...]
    s_hi, s_lo = _split2(s)
    w_hi, w_lo = _split2(w)
    o_ref[...] = _dot(s_hi, w_hi) + (_dot(s_hi, w_lo) + _dot(s_lo, w_hi)) + b_ref[...]


def _modulation(cc, w_mod, b_mod):
    depth, d, d6 = w_mod.shape
    r = cc.shape[0]
    bn = 2048
    return pl.pallas_call(
        _mod_kernel,
        grid=(depth, d6 // bn),
        in_specs=[
            pl.BlockSpec((r, d), lambda l, j: (0, 0)),
            pl.BlockSpec((None, d, bn), lambda l, j: (l, 0, j)),
            pl.BlockSpec((None, 1, bn), lambda l, j: (l, 0, j)),
        ],
        out_specs=pl.BlockSpec((None, r, bn), lambda l, j: (l, 0, j)),
        out_shape=jax.ShapeDtypeStruct((depth, r, d6), F32),
        compiler_params=pltpu.CompilerParams(
            dimension_semantics=("arbitrary", "arbitrary"), vmem_limit_bytes=VMEM_LIMIT),
        name="modulation",
    )(cc, w_mod, b_mod.reshape(depth, 1, d6))


def _inproj_kernel(x_ref, mod_ref, g_ref, w_ref, oml_ref, ossd_ref, oatt_ref, og_ref):
    x = x_ref[...]
    ms = jnp.mean(x * x, axis=-1, keepdims=True)
    xn = x * lax.rsqrt(ms + EPS)
    h = xn * (g_ref[...] * (1.0 + mod_ref[1:2, :])) + mod_ref[0:1, :]
    hb = h.astype(BF16)
    step = 512
    off = 0
    for o_ref, width in ((oml_ref, U_ML), (ossd_ref, U_SSD), (oatt_ref, U_ATT), (og_ref, U_GATE)):
        for j in range(0, width, step):
            wj = min(step, width - j)
            o_ref[:, j:j + wj] = _dot(hb, w_ref[:, off + j:off + j + wj]).astype(o_ref.dtype)
        off += width


def _in_proj(x, mods, mod_row, g1, w, layer, tm):
    bsz, t, d = x.shape
    widths = (U_ML, U_SSD, U_ATT, U_GATE)
    dtypes = (BF16, BF16, BF16, F32)
    return pl.pallas_call(
        _inproj_kernel,
        grid=(bsz, t // tm),
        in_specs=[
            pl.BlockSpec((None, tm, d), lambda b, i: (b, i, 0)),
            pl.BlockSpec((None, None, 6, d), lambda b, i: (layer, mod_row(b), 0, 0)),
            pl.BlockSpec((None, 1, d), lambda b, i: (layer, 0, 0)),
            pl.BlockSpec((None, d, U_TOTAL), lambda b, i: (layer, 0, 0), pipeline_mode=pl.Buffered(1)),
        ],
        out_specs=[pl.BlockSpec((None, tm, wd), lambda b, i: (b, i, 0)) for wd in widths],
        out_shape=[jax.ShapeDtypeStruct((bsz, t, wd), dt) for wd, dt in zip(widths, dtypes)],
        compiler_params=pltpu.CompilerParams(
            dimension_semantics=("arbitrary", "arbitrary"), vmem_limit_bytes=VMEM_LIMIT),
        name="in_proj",
    )(x, mods, g1, w)


def _post_kernel(x_ref, yml_ref, yssd_ref, yatt_ref, mod_ref, g_ref, wo_ref, wup_ref, wdn_ref, o_ref):
    proj = (_dot(yml_ref[...], wo_ref[0:ML_WIDTH, :])
            + _dot(yssd_ref[...], wo_ref[ML_WIDTH:ML_WIDTH + SSD_WIDTH, :])
            + _dot(yatt_ref[...], wo_ref[ML_WIDTH + SSD_WIDTH:MIX_WIDTH, :]))
    x1 = x_ref[...] + mod_ref[2:3, :] * proj
    ms = jnp.mean(x1 * x1, axis=-1, keepdims=True)
    xn = x1 * lax.rsqrt(ms + EPS)
    hb = (xn * (g_ref[...] * (1.0 + mod_ref[4:5, :])) + mod_ref[3:4, :]).astype(BF16)
    d_ff = wup_ref.shape[1]
    step = 1024
    acc = None
    for j in range(0, d_ff, step):
        up = jnp.maximum(_dot(hb, wup_ref[:, j:j + step]), 0.0)
        part = _dot((up * up).astype(BF16), wdn_ref[j:j + step, :])
        acc = part if acc is None else acc + part
    o_ref[...] = x1 + mod_ref[5:6, :] * acc


def _post(x, yml, yssd, yatt, mods, mod_row, g2, wo, wup, wdn, layer, tm):
    bsz, t, d = x.shape
    d_ff = wup.shape[2]
    const = dict(pipeline_mode=pl.Buffered(1))
    return pl.pallas_call(
        _post_kernel,
        grid=(bsz, t // tm),
        in_specs=[
            pl.BlockSpec((None, tm, d), lambda b, i: (b, i, 0)),
            pl.BlockSpec((None, tm, ML_WIDTH), lambda b, i: (b, i, 0)),
            pl.BlockSpec((None, tm, SSD_WIDTH), lambda b, i: (b, i, 0)),
            pl.BlockSpec((None, tm, ATT_WIDTH), lambda b, i: (b, i, 0)),
            pl.BlockSpec((None, None, 6, d), lambda b, i: (layer, mod_row(b), 0, 0)),
            pl.BlockSpec((None, 1, d), lambda b, i: (layer, 0, 0)),
            pl.BlockSpec((None, MIX_WIDTH, d), lambda b, i: (layer, 0, 0), **const),
            pl.BlockSpec((None, d, d_ff), lambda b, i: (layer, 0, 0), **const),
            pl.BlockSpec((None, d_ff, d), lambda b, i: (layer, 0, 0), **const),
        ],
        out_specs=pl.BlockSpec((None, tm, d), lambda b, i: (b, i, 0)),
        out_shape=jax.ShapeDtypeStruct((bsz, t, d), F32),
        compiler_params=pltpu.CompilerParams(
            dimension_semantics=("arbitrary", "arbitrary"), vmem_limit_bytes=VMEM_LIMIT),
        name="post",
    )(x, yml, yssd, yatt, mods, g2, wo, wup, wdn)


ML_GATES = N_DIR * ML_HEADS


def _mlstm_kernel(ul_ref, uc_ref, gl_ref, gc_ref, pv_ref, ng_ref, ex_ref, *rest,
                  need_ctx_out, n_lat, n_ctx):
    if need_ctx_out:
        yl_ref, yc_ref, arow, packed, ends, mprev, cbk, crun = rest
    else:
        yl_ref, arow, packed, ends, mprev, cbk, crun = rest
        yc_ref = None
    ncl, ncc = n_lat // CHUNK, n_ctx // CHUNK
    nct = ncl + ncc
    nh = ML_HEADS
    ti = _iota((CHUNK, CHUNK), 0)
    si = _iota((CHUNK, CHUNK), 1)
    low = si <= ti
    upp = si >= ti
    tri_low = jnp.where(low, 1.0, 0.0).astype(BF16)
    tri_upp = jnp.where(upp, 1.0, 0.0).astype(BF16)
    fwd_row = _iota((ML_GATES, CHUNK), 0) < nh
    bias = pv_ref[0:1, :]
    neg_inf = -jnp.inf
    ones_blk = jnp.ones((CHUNK, LANES), BF16)

    def prep_group(g_ref, c0, n, goff):
        def per_chunk(x):
            return [x[:, i * CHUNK:(i + 1) * CHUNK] for i in range(n)]

        def wide(xs):
            return xs[0] if n == 1 else jnp.concatenate(xs, axis=1)

        gts = [(g_ref[(c0 + i) * CHUNK:(c0 + i + 1) * CHUNK, :] + bias).T for i in range(n)]
        lf = _log_sigmoid(wide([gt[0:ML_GATES] for gt in gts]))
        li = wide([gt[LI_LANE0:LI_LANE0 + ML_GATES] for gt in gts])
        sums = [_row_cumsums((tri_upp, tri_low), x) for x in per_chunk(lf)]
        fwd = _iota((ML_GATES, n * CHUNK), 0) < nh
        tok = _iota((ML_GATES, n * CHUNK), 1) & (CHUNK - 1)
        b = jnp.where(fwd, wide([s[0] for s in sums]), wide([s[1] for s in sums]))
        a = li - b
        cm = a
        k = 1
        while k < CHUNK:
            before = jnp.where(tok >= k, pltpu.roll(cm, k, 1), neg_inf)
            after = jnp.where(tok < CHUNK - k, pltpu.roll(cm, n * CHUNK - k, 1), neg_inf)
            cm = jnp.maximum(cm, jnp.where(fwd, before, after))
            k *= 2
        parts = [p.astype(F32) for p in _split3(cm) + _split3(b)]
        parts.append(jnp.zeros((CHUNK - len(parts) * ML_GATES, n * CHUNK), F32))
        stacked = jnp.concatenate(parts, axis=0)
        for i in range(n):
            cg = goff + c0 + i
            cols = slice(i * CHUNK, (i + 1) * CHUNK)
            arow[cg] = a[:, cols]
            packed[cg] = stacked[:, cols].T.astype(BF16)
            for plane, x in enumerate((cm, b)):
                last = x[:, (i + 1) * CHUNK - 1:(i + 1) * CHUNK]
                first = x[:, i * CHUNK:i * CHUNK + 1]
                ends[cg, plane] = jnp.broadcast_to(jnp.where(fwd_row[:, 0:1], last, first), (ML_GATES, CHUNK))

    def prep(g_ref, nch, goff):
        for c0 in range(0, nch, PREP_GROUP):
            prep_group(g_ref, c0, min(PREP_GROUP, nch - c0), goff)

    prep(gc_ref, ncc, 0)
    prep(gl_ref, ncl, ncc)

    def m_scan(lo, hi, reverse, m):
        for cg in (range(hi - 1, lo - 1, -1) if reverse else range(lo, hi)):
            mprev[cg, int(reverse)] = m
            m = ends[cg, 1] + jnp.maximum(m, ends[cg, 0])
        return m

    m_zero = jnp.zeros((ML_GATES, CHUNK), F32)
    m_scan(0, nct, False, m_zero)
    m_scan(ncc, nct, True, m_scan(0, ncc, True, m_zero))

    def load_kv(u_ref, rows, h):
        k = u_ref[rows, ML_WIDTH + h * LANES:ML_WIDTH + (h + 1) * LANES]
        v = u_ref[rows, 2 * ML_WIDTH + h * LANES:2 * ML_WIDTH + (h + 1) * LANES]
        return k.T, jnp.concatenate([v, ones_blk], axis=1)

    def update(d, h, cg, kt, v1):
        j = d * nh + h
        m_prev = mprev[cg, d, j:j + 1, 0:1]
        m_end = jnp.maximum(m_prev, ends[cg, 0, j:j + 1, 0:1])
        w_row = jnp.exp(arow[cg, j:j + 1, :] - m_end) * ML_SCALE
        ktw = (kt.astype(F32) * w_row).astype(BF16)
        crun[d, h] = jnp.exp(m_prev - m_end) * crun[d, h] + _dot(ktw, v1)

    def bwd_pass(u_ref, nch, goff, keep):
        def body(i, carry):
            c = nch - 1 - i
            rows = _chunk_rows(c)
            for h in range(nh):
                if keep:
                    cbk[goff + c, h] = crun[1, h].astype(BF16)
                kt, v1 = load_kv(u_ref, rows, h)
                update(1, h, goff + c, kt, v1)
            return carry
        lax.fori_loop(0, nch, body, 0, unroll=UNROLL)

    def fwd_pass(u_ref, y_ref, nch, goff):
        def body(c, carry):
            rows = _chunk_rows(c)
            cg = goff + c
            for h in range(nh):
                kt, v1 = load_kv(u_ref, rows, h)
                if y_ref is not None:
                    q = u_ref[rows, h * LANES:(h + 1) * LANES]
                    o = u_ref[rows, 3 * ML_WIDTH + h * LANES:3 * ML_WIDTH + (h + 1) * LANES]
                    sqk = _dot(q, kt) * ML_SCALE
                    qc = _dot(q, jnp.concatenate([crun[0, h].astype(BF16), cbk[cg, h]], axis=1))
                    s_dirs, stats = [], []
                    for d, mask in ((0, low), (1, upp)):
                        j = d * nh + h
                        rep = _dot(packed[cg], ex_ref[:, 2 * j * LANES:2 * (j + 1) * LANES])
                        m_prev = mprev[cg, d, j:j + 1, 0:1]
                        m_t = jnp.maximum(rep[:, 0:LANES], m_prev)
                        s_dirs.append(sqk * jnp.exp(jnp.where(mask, arow[cg, j:j + 1, :] - m_t, neg_inf)))
                        stats.append((m_prev, m_t, rep[:, LANES:2 * LANES]))
                    sv = _dot(jnp.concatenate(s_dirs, axis=0).astype(BF16), v1)
                    hsum = None
                    for d, (m_prev, m_t, b_t) in enumerate(stats):
                        g = jnp.exp(m_prev - m_t)
                        sv_d = sv[d * CHUNK:(d + 1) * CHUNK, :]
                        qc_d = qc[:, 2 * d * LANES:2 * (d + 1) * LANES]
                        num = sv_d[:, 0:LANES] + g * qc_d[:, 0:LANES]
                        den = sv_d[:, LANES:2 * LANES] + g * qc_d[:, LANES:2 * LANES]
                        hd = num / jnp.maximum(jnp.abs(den), jnp.exp(-(b_t + m_t)))
                        hsum = hd if hsum is None else hsum + hd
                    ms = jnp.mean(hsum * hsum, axis=1, keepdims=True)
                    y = hsum * lax.rsqrt(ms + EPS) * ng_ref[:, h * LANES:(h + 1) * LANES]
                    y = y * _sigmoid(o.astype(F32))
                    y_ref[rows, h * LANES:(h + 1) * LANES] = y.astype(y_ref.dtype)
                update(0, h, cg, kt, v1)
            return carry
        lax.fori_loop(0, nch, body, 0, unroll=UNROLL)

    crun[...] = jnp.zeros(crun.shape, F32)
    bwd_pass(uc_ref, ncc, 0, need_ctx_out)
    bwd_pass(ul_ref, ncl, ncc, True)
    fwd_pass(uc_ref, yc_ref, ncc, 0)
    fwd_pass(ul_ref, yl_ref, ncl, ncc)


def _ml_expand_table():
    expand = np.zeros((LANES, ML_GATES * 2 * LANES), np.float32)
    for j in range(ML_GATES):
        for i in range(6):
            half = i // 3
            expand[ML_GATES * i + j, (2 * j + half) * LANES:(2 * j + half + 1) * LANES] = 1.0
    return jnp.asarray(expand, BF16)


def _mlstm_mixer(uml_l, uml_c, g_l, g_c, pvec, norm_g, need_ctx_out):
    expand = _ml_expand_table()
    bsz, n_lat, _ = uml_l.shape
    n_ctx = uml_c.shape[1]
    nct = (n_lat + n_ctx) // CHUNK
    kern = functools.partial(_mlstm_kernel, need_ctx_out=need_ctx_out, n_lat=n_lat, n_ctx=n_ctx)
    out_specs = [pl.BlockSpec((None, n_lat, ML_WIDTH), lambda b: (b, 0, 0))]
    out_shape = [jax.ShapeDtypeStruct((bsz, n_lat, ML_WIDTH), BF16)]
    if need_ctx_out:
        out_specs.append(pl.BlockSpec((None, n_ctx, ML_WIDTH), lambda b: (b, 0, 0)))
        out_shape.append(jax.ShapeDtypeStruct((bsz, n_ctx, ML_WIDTH), BF16))
    outs = pl.pallas_call(
        kern,
        grid=(bsz,),
        in_specs=[
            pl.BlockSpec((None, n_lat, U_ML), lambda b: (b, 0, 0)),
            pl.BlockSpec((None, n_ctx, U_ML), lambda b: (b, 0, 0)),
            pl.BlockSpec((None, n_lat, U_GATE), lambda b: (b, 0, 0)),
            pl.BlockSpec((None, n_ctx, U_GATE), lambda b: (b, 0, 0)),
            pl.BlockSpec((8, LANES), lambda b: (0, 0)),
            pl.BlockSpec((1, ML_WIDTH), lambda b: (0, 0)),
            pl.BlockSpec(expand.shape, lambda b: (0, 0)),
        ],
        out_specs=out_specs,
        out_shape=out_shape,
        scratch_shapes=[
            pltpu.VMEM((nct, ML_GATES, CHUNK), F32),
            pltpu.VMEM((nct, CHUNK, LANES), BF16),
            pltpu.VMEM((nct, 2, ML_GATES, CHUNK), F32),
            pltpu.VMEM((nct, N_DIR, ML_GATES, CHUNK), F32),
            pltpu.VMEM((nct, ML_HEADS, ML_HEAD_DIM, 2 * LANES), BF16),
            pltpu.VMEM((N_DIR, ML_HEADS, ML_HEAD_DIM, 2 * LANES), F32),
        ],
        compiler_params=pltpu.CompilerParams(
            dimension_semantics=("arbitrary",), vmem_limit_bytes=VMEM_LIMIT),
        name="mlstm",
    )(uml_l, uml_c, g_l, g_c, pvec, norm_g, expand)
    return (outs[0], outs[1]) if need_ctx_out else (outs[0], None)


def _ssd_kernel(ul_ref, uc_ref, gl_ref, gc_ref, pv_ref, cw_ref, cb_ref, dsk_ref, ng_ref, ex_ref, *rest,
                need_ctx_out, n_lat, n_ctx):
    if need_ctx_out:
        yl_ref, yc_ref, xact, bt, acum, packed, act, dtt, hbk, hrun = rest
    else:
        yl_ref, xact, bt, acum, packed, act, dtt, hbk, hrun = rest
        yc_ref = None
    ncl, ncc = n_lat // CHUNK, n_ctx // CHUNK
    nh, hg = SSD_HEADS, SSD_HEADS // SSD_GROUPS
    gw = hg * SSD_HEAD_DIM
    gn = SSD_GROUPS * SSD_STATE
    ti = _iota((CHUNK, CHUNK), 0)
    si = _iota((CHUNK, CHUNK), 1)
    low_strict = si < ti
    upp_strict = si > ti
    tri_low = jnp.where(si <= ti, 1.0, 0.0).astype(BF16)
    tri_upp = jnp.where(si >= ti, 1.0, 0.0).astype(BF16)
    lane = _iota((CHUNK, LANES), 1)
    nj = N_DIR * nh
    ac_lane0 = 6 * nj
    fwd_row = _iota((nj, CHUNK), 0) < nh
    bias0 = pv_ref[0:1, :]
    a_rows = jnp.broadcast_to(-jnp.exp(pv_ref[2:3, :]), (LANES, LANES)).T[DT_LANE0:DT_LANE0 + nj]
    shift_down = jnp.where(si == ti - 1, 1.0, 0.0).astype(BF16)
    shift_up = jnp.where(si == ti + 1, 1.0, 0.0).astype(BF16)
    lane_group0 = lane < SSD_STATE
    row_group0 = _iota((gn, CHUNK), 0) < SSD_STATE

    def prep(u_ref, g_ref, nch, goff, nrows):
        def body(c, carry):
            r = pl.multiple_of(c * CHUNK, CHUNK)
            cur = u_ref[pl.ds(r, CHUNK), SSD_WIDTH:U_SSD]
            rp = pl.multiple_of(jnp.maximum(r - CHUNK, 0), CHUNK)
            rn = pl.multiple_of(jnp.minimum(r + CHUNK, nrows - CHUNK), CHUNK)
            prev = u_ref[pl.ds(rp, CHUNK), SSD_WIDTH:U_SSD]
            nxt = u_ref[pl.ds(rn, CHUNK), SSD_WIDTH:U_SSD]
            from_prev = jnp.where((ti == 0) & (si == CHUNK - 1) & (c > 0), 1.0, 0.0).astype(BF16)
            from_next = jnp.where((ti == CHUNK - 1) & (si == 0) & (c < nch - 1), 1.0, 0.0).astype(BF16)
            xm = _dot(jnp.concatenate([shift_down, from_prev], axis=1), jnp.concatenate([cur, prev], axis=0))
            xp = _dot(jnp.concatenate([shift_up, from_next], axis=1), jnp.concatenate([cur, nxt], axis=0))
            conv = (cw_ref[0:1, :] * xm + cw_ref[1:2, :] * cur.astype(F32) + cw_ref[2:3, :] * xp
                    + cb_ref[...])
            xa = (conv * _sigmoid(conv)).astype(BF16)
            xact[goff + c] = xa
            bt[goff + c] = xa[:, SSD_WIDTH:SSD_WIDTH + gn].T

            gt = (g_ref[pl.ds(r, CHUNK), :] + bias0).T
            dt = _softplus(gt[DT_LANE0:DT_LANE0 + nj])
            prefix, suffix = _row_cumsums((tri_upp, tri_low), dt * a_rows)
            ac = jnp.where(fwd_row, prefix, suffix)
            a_end = jnp.where(fwd_row[:, 0:1], ac[:, CHUNK - 1:CHUNK], ac[:, 0:1])
            act[goff + c] = ac - jnp.log(dt)
            dtt[goff + c] = dt
            parts = [p.astype(F32) for p in _split3(jnp.exp(ac)) + _split3(jnp.exp(a_end - ac) * dt)]
            parts += [ac, jnp.zeros((CHUNK - 7 * nj, CHUNK), F32)]
            tile = jnp.concatenate(parts, axis=0).T
            acum[goff + c] = tile
            packed[goff + c] = tile.astype(BF16)
            return carry
        lax.fori_loop(0, nch, body, 0, unroll=UNROLL)

    prep(uc_ref, gc_ref, ncc, 0, n_ctx)
    prep(ul_ref, gl_ref, ncl, ncc, n_lat)

    def e_cols(d, g):
        return slice(d * SSD_WIDTH + g * gw, d * SSD_WIDTH + (g + 1) * gw)

    def w_cols(d, g):
        return slice((N_DIR + d) * SSD_WIDTH + g * gw, (N_DIR + d) * SSD_WIDTH + (g + 1) * gw)

    def update_state(d, cg):
        pk = packed[cg]
        xa = xact[cg]
        b_t = bt[cg]
        zero = jnp.zeros_like(b_t)
        upd = []
        for g in range(SSD_GROUPS):
            wx = (_dot(pk, ex_ref[:, w_cols(d, g)]) * xa[:, g * gw:(g + 1) * gw].astype(F32)).astype(BF16)
            upd.append(_dot(jnp.where(row_group0 == (g == 0), b_t, zero), wx))
        tail, last = (slice(CHUNK - 16, CHUNK), 15) if d == 0 else (slice(0, 16), 0)
        decay = _dot(pk[tail], ex_ref[:, d * SSD_WIDTH:(d + 1) * SSD_WIDTH])[last:last + 1, :]
        hrun[d] = decay * hrun[d] + jnp.concatenate(upd, axis=1)

    def bwd_pass(nch, goff, keep):
        def body(i, carry):
            cg = goff + nch - 1 - i
            if keep:
                hbk[cg] = hrun[1].astype(BF16)
            update_state(1, cg)
            return carry
        lax.fori_loop(0, nch, body, 0, unroll=UNROLL)

    def fwd_pass(u_ref, y_ref, nch, goff):
        def body(c, carry):
            cg = goff + c
            if y_ref is not None:
                rows = _chunk_rows(c)
                xa = xact[cg]
                x = xa[:, 0:SSD_WIDTH]
                b_t = bt[cg]
                c_all = xa[:, SSD_WIDTH + gn:SSD_XBC]
                ac = acum[cg]
                ac_t = act[cg]
                dt_t = dtt[cg]
                lane_w = _iota((CHUNK, gw), 1)
                pk = packed[cg]
                yz_groups, ssq = [], None
                for g in range(SSD_GROUPS):
                    cols = slice(g * gw, (g + 1) * gw)
                    c_g = jnp.where(lane_group0 == (g == 0), c_all, jnp.zeros_like(c_all))
                    cb = _dot(c_g, b_t)
                    s_heads = []
                    for hh in range(hg):
                        jf = g * hg + hh
                        jb = jf + nh
                        on_diag = jnp.log(dt_t[jf:jf + 1, :] + dt_t[jb:jb + 1, :])
                        ac_f = ac[:, ac_lane0 + jf:ac_lane0 + jf + 1]
                        ac_b = ac[:, ac_lane0 + jb:ac_lane0 + jb + 1]
                        expo = jnp.where(low_strict, ac_f - ac_t[jf:jf + 1, :],
                                         jnp.where(upp_strict, ac_b - ac_t[jb:jb + 1, :], on_diag))
                        s_heads.append((cb * jnp.exp(expo)).astype(BF16))
                    x_g = x[:, cols]
                    bd = jnp.concatenate(
                        [jnp.where((lane_w >= hh * SSD_HEAD_DIM) & (lane_w < (hh + 1) * SSD_HEAD_DIM),
                                   x_g, jnp.zeros_like(x_g)) for hh in range(hg)], axis=0)
                    y = _dot(jnp.concatenate(s_heads, axis=1), bd)
                    y = y + _dot(c_all, hrun[0, :, cols].astype(BF16)) * _dot(pk, ex_ref[:, e_cols(0, g)])
                    y = y + _dot(c_all, hbk[cg, :, cols]) * _dot(pk, ex_ref[:, e_cols(1, g)])
                    y = y + dsk_ref[:, cols] * x_g.astype(F32)
                    z = u_ref[rows, cols].astype(F32)
                    yz = y * (z * _sigmoid(z))
                    part = jnp.sum(yz * yz, axis=1, keepdims=True)
                    ssq = part if ssq is None else ssq + part
                    yz_groups.append(yz)
                scale = lax.rsqrt(ssq * (1.0 / SSD_WIDTH) + EPS)
                for g, yz in enumerate(yz_groups):
                    cols = slice(g * gw, (g + 1) * gw)
                    y_ref[rows, cols] = (yz * scale * ng_ref[:, cols]).astype(y_ref.dtype)
            update_state(0, cg)
            return carry
        lax.fori_loop(0, nch, body, 0, unroll=UNROLL)

    hrun[...] = jnp.zeros(hrun.shape, F32)
    bwd_pass(ncc, 0, need_ctx_out)
    bwd_pass(ncl, ncc, True)
    fwd_pass(uc_ref, yc_ref, ncc, 0)
    fwd_pass(ul_ref, yl_ref, ncl, ncc)


def _ssd_expand_table():
    nj = N_DIR * SSD_HEADS
    expand = np.zeros((LANES, 2 * N_DIR * SSD_WIDTH), np.float32)
    for i in range(6):
        for d in range(N_DIR):
            for h in range(SSD_HEADS):
                j = d * SSD_HEADS + h
                c0 = ((i // 3) * N_DIR + d) * SSD_WIDTH + h * SSD_HEAD_DIM
                expand[nj * i + j, c0:c0 + SSD_HEAD_DIM] = 1.0
    return jnp.asarray(expand, BF16)


def _ssd_mixer(ussd_l, ussd_c, g_l, g_c, pvec, conv_w, conv_b, d_skip, norm_g, need_ctx_out):
    expand = _ssd_expand_table()
    bsz, n_lat, _ = ussd_l.shape
    n_ctx = ussd_c.shape[1]
    nct = (n_lat + n_ctx) // CHUNK
    gn = SSD_GROUPS * SSD_STATE
    kern = functools.partial(_ssd_kernel, need_ctx_out=need_ctx_out, n_lat=n_lat, n_ctx=n_ctx)
    out_specs = [pl.BlockSpec((None, n_lat, SSD_WIDTH), lambda b: (b, 0, 0))]
    out_shape = [jax.ShapeDtypeStruct((bsz, n_lat, SSD_WIDTH), BF16)]
    if need_ctx_out:
        out_specs.append(pl.BlockSpec((None, n_ctx, SSD_WIDTH), lambda b: (b, 0, 0)))
        out_shape.append(jax.ShapeDtypeStruct((bsz, n_ctx, SSD_WIDTH), BF16))
    outs = pl.pallas_call(
        kern,
        grid=(bsz,),
        in_specs=[
            pl.BlockSpec((None, n_lat, U_SSD), lambda b: (b, 0, 0)),
            pl.BlockSpec((None, n_ctx, U_SSD), lambda b: (b, 0, 0)),
            pl.BlockSpec((None, n_lat, LANES), lambda b: (b, 0, 0)),
            pl.BlockSpec((None, n_ctx, LANES), lambda b: (b, 0, 0)),
            pl.BlockSpec((8, LANES), lambda b: (0, 0)),
            pl.BlockSpec((3, SSD_XBC), lambda b: (0, 0)),
            pl.BlockSpec((1, SSD_XBC), lambda b: (0, 0)),
            pl.BlockSpec((1, SSD_WIDTH), lambda b: (0, 0)),
            pl.BlockSpec((1, SSD_WIDTH), lambda b: (0, 0)),
            pl.BlockSpec(expand.shape, lambda b: (0, 0)),
        ],
        out_specs=out_specs,
        out_shape=out_shape,
        scratch_shapes=[
            pltpu.VMEM((nct, CHUNK, SSD_XBC), BF16),
            pltpu.VMEM((nct, gn, CHUNK), BF16),
            pltpu.VMEM((nct, CHUNK, LANES), F32),
            pltpu.VMEM((nct, CHUNK, LANES), BF16),
            pltpu.VMEM((nct, N_DIR * SSD_HEADS, CHUNK), F32),
            pltpu.VMEM((nct, N_DIR * SSD_HEADS, CHUNK), F32),
            pltpu.VMEM((nct, gn, SSD_WIDTH), BF16),
            pltpu.VMEM((N_DIR, gn, SSD_WIDTH), F32),
        ],
        compiler_params=pltpu.CompilerParams(
            dimension_semantics=("arbitrary",), vmem_limit_bytes=VMEM_LIMIT),
        name="ssd",
    )(ussd_l, ussd_c, g_l, g_c, pvec, conv_w, conv_b, d_skip, norm_g, expand)
    return (outs[0], outs[1]) if need_ctx_out else (outs[0], None)


def _head_norm(x, gain):
    rows = x.shape[0]
    blks = [x[:, j * LANES:(j + 1) * LANES] for j in range(x.shape[1] // LANES)]
    same_head = (_iota((LANES, LANES), 0) < ATT_HEAD_DIM) == (_iota((LANES, LANES), 1) < ATT_HEAD_DIM)
    sq = jnp.concatenate([b * b for b in blks], axis=0).astype(BF16)
    ms = _dot(sq, jnp.where(same_head, 1.0, 0.0).astype(BF16)) * (1.0 / ATT_HEAD_DIM)
    outs = [b * lax.rsqrt(ms[j * rows:(j + 1) * rows] + EPS) * gain for j, b in enumerate(blks)]
    return outs[0] if len(outs) == 1 else jnp.concatenate(outs, axis=1)


def _rope(x, cos_t, sin_t):
    w = x.shape[1]
    reps = w // LANES
    nfreq = ATT_HEAD_DIM // 4
    partner = jnp.where((_iota(x.shape, 1) & (2 * nfreq - 1)) < nfreq,
                        pltpu.roll(x, w - nfreq, 1), pltpu.roll(x, nfreq, 1))
    if reps > 1:
        cos_t = jnp.concatenate([cos_t] * reps, axis=1)
        sin_t = jnp.concatenate([sin_t] * reps, axis=1)
    return x * cos_t + partner * sin_t


def _dup_heads(x):
    lane_lo = _iota(x.shape, 1) < ATT_HEAD_DIM
    sw = pltpu.roll(x, ATT_HEAD_DIM, 1)
    return jnp.concatenate([jnp.where(lane_lo, x, sw), jnp.where(lane_lo, sw, x)], axis=1)


def _attn_kernel(ul_ref, uc_ref, cos_ref, sin_ref, qg_ref, kg_ref, sink_ref, *rest,
                 need_ctx_out, n_lat, n_ctx):
    if need_ctx_out:
        yl_ref, yc_ref, qp, kb, vb, kc, vc, qcp = rest
    else:
        yl_ref, qp, kb, vb, kc, vc = rest
        yc_ref = qcp = None
    ncl, ncc = n_lat // CHUNK, n_ctx // CHUNK
    rep = ATT_Q_HEADS // ATT_KV_HEADS
    kv0 = ATT_WIDTH
    lane_lo = _iota((CHUNK, LANES), 1) < ATT_HEAD_DIM
    neg_inf = -jnp.inf
    qg = qg_ref[...]
    kg = kg_ref[...]

    zeros_blk = jnp.zeros((CHUNK, vb.shape[1]), BF16)
    vb[0:CHUNK, :] = zeros_blk
    vb[CHUNK + n_lat:2 * CHUNK + n_lat, :] = zeros_blk
    kb[0] = jnp.zeros(kb.shape[1:], BF16)
    kb[ncl + 1] = jnp.zeros(kb.shape[1:], BF16)

    def store_keys_transposed(ref, blk, k):
        kt = k.T
        for g in range(ATT_KV_HEADS):
            head = kt[g * ATT_HEAD_DIM:(g + 1) * ATT_HEAD_DIM]
            ref[blk, g * LANES:(g + 1) * LANES, :] = jnp.concatenate([head, head], axis=0).astype(BF16)

    def values_with_ones(v):
        d = _dup_heads(v).astype(BF16)
        ones = jnp.ones((v.shape[0], LANES), BF16)
        return jnp.concatenate([d[:, 0:LANES], ones, d[:, LANES:2 * LANES], ones], axis=1)

    def prep_lat(c, carry):
        rows = _chunk_rows(c)
        cos_t = cos_ref[rows, :]
        sin_t = sin_ref[rows, :]
        q = _rope(_head_norm(ul_ref[rows, 0:ATT_WIDTH].astype(F32), qg), cos_t, sin_t)
        qp[rows, :] = (q * ATT_SCALE).astype(BF16)
        k = _rope(_head_norm(ul_ref[rows, kv0:kv0 + LANES].astype(F32), kg), cos_t, sin_t)
        v = ul_ref[rows, kv0 + LANES:kv0 + 2 * LANES].astype(F32)
        dst = pl.ds(pl.multiple_of((c + 1) * CHUNK, CHUNK), CHUNK)
        store_keys_transposed(kb, c + 1, k)
        vb[dst, :] = values_with_ones(v)
        return carry
    lax.fori_loop(0, ncl, prep_lat, 0, unroll=UNROLL)

    def prep_ctx(c, carry):
        rows = _chunk_rows(c)
        k = _head_norm(uc_ref[rows, kv0:kv0 + LANES].astype(F32), kg)
        v = uc_ref[rows, kv0 + LANES:kv0 + 2 * LANES].astype(F32)
        store_keys_transposed(kc, c, k)
        vc[rows, :] = values_with_ones(v)
        if need_ctx_out:
            q = _head_norm(uc_ref[rows, 0:ATT_WIDTH].astype(F32), qg)
            qcp[rows, :] = (q * ATT_SCALE).astype(BF16)
        return carry
    lax.fori_loop(0, ncc, prep_ctx, 0, unroll=UNROLL)

    def attend(q_ref, y_ref, rows, band):
        for g in range(ATT_KV_HEADS):
            kl = slice(g * LANES, (g + 1) * LANES)
            vl = slice(2 * g * LANES, 2 * (g + 1) * LANES)
            k_ctx = jnp.concatenate([kc[j, kl, :] for j in range(ncc)], axis=1)
            if band is not None:
                blk, band_rows, valid = band
                k_band = jnp.concatenate([kb[blk + j, kl, :] for j in range(3)], axis=1)
            for r0 in range(0, rep, ATT_STACK):
                heads = [g * rep + r0 + i for i in range(ATT_STACK)]
                qms = []
                for hq in heads:
                    qb = q_ref[rows, (hq // 2) * LANES:(hq // 2 + 1) * LANES]
                    qms.append(jnp.where(lane_lo != (hq % 2 == 1), qb, jnp.zeros_like(qb)))
                qs = qms[0] if ATT_STACK == 1 else jnp.concatenate(qms, axis=0)
                s_c = _dot(qs, k_ctx)
                if band is not None:
                    s_b = _dot(qs, k_band)
                ps, sink_terms = [], []
                for i, hq in enumerate(heads):
                    sl = slice(i * CHUNK, (i + 1) * CHUNK)
                    sk = sink_ref[hq]
                    s = s_c[sl]
                    if band is not None:
                        prev_k, own_k, next_k = (slice(j * CHUNK, (j + 1) * CHUNK) for j in range(3))
                        s = jnp.concatenate([s, jnp.where(valid[:, prev_k], s_b[sl, prev_k], neg_inf), s_b[sl, own_k],
                                             jnp.where(valid[:, next_k], s_b[sl, next_k], neg_inf)], axis=1)
                    m = jnp.maximum(jnp.max(s, axis=1, keepdims=True), sk)
                    ps.append(jnp.exp(s - m).astype(BF16))
                    sink_terms.append(jnp.exp(sk - m))
                p = ps[0] if ATT_STACK == 1 else jnp.concatenate(ps, axis=0)
                acc = _dot(p[:, 0:n_ctx], vc[:, vl])
                if band is not None:
                    acc = acc + _dot(p[:, n_ctx:], vb[band_rows, vl])
                for i, hq in enumerate(heads):
                    sl = slice(i * CHUNK, (i + 1) * CHUNK)
                    o = acc[sl, 0:LANES] / (acc[sl, LANES:2 * LANES] + sink_terms[i])
                    if hq % 2 == 1:
                        jb = hq // 2
                        y_ref[rows, jb * LANES:(jb + 1) * LANES] = jnp.where(lane_lo, o_even, o).astype(y_ref.dtype)
                    o_even = o

    assert WINDOW == CHUNK
    qq = _iota((CHUNK, 3 * CHUNK), 0)
    kk = _iota((CHUNK, 3 * CHUNK), 1)
    in_window = (kk >= qq) & (kk <= qq + 2 * WINDOW)

    def lat_block(i, carry):
        kpos = kk + (i - 1) * CHUNK
        valid = in_window & (kpos >= 0) & (kpos < n_lat)
        band_rows = pl.ds(pl.multiple_of(i * CHUNK, CHUNK), 3 * CHUNK)
        attend(qp, yl_ref, _chunk_rows(i), (i, band_rows, valid))
        return carry
    lax.fori_loop(0, ncl, lat_block, 0, unroll=UNROLL)

    if need_ctx_out:
        def ctx_block(i, carry):
            attend(qcp, yc_ref, _chunk_rows(i), None)
            return carry
        lax.fori_loop(0, ncc, ctx_block, 0, unroll=UNROLL)


def _attn_mixer(uatt_l, uatt_c, cos_t, sin_t, qgain, kgain, sink, need_ctx_out):
    bsz, n_lat, _ = uatt_l.shape
    n_ctx = uatt_c.shape[1]
    kern = functools.partial(_attn_kernel, need_ctx_out=need_ctx_out, n_lat=n_lat, n_ctx=n_ctx)
    out_specs = [pl.BlockSpec((None, n_lat, ATT_WIDTH), lambda b: (b, 0, 0))]
    out_shape = [jax.ShapeDtypeStruct((bsz, n_lat, ATT_WIDTH), BF16)]
    scratch = [
        pltpu.VMEM((n_lat, ATT_WIDTH), BF16),
        pltpu.VMEM((n_lat // CHUNK + 2, 2 * LANES, CHUNK), BF16),
        pltpu.VMEM((n_lat + 2 * CHUNK, 4 * LANES), BF16),
        pltpu.VMEM((n_ctx // CHUNK, 2 * LANES, CHUNK), BF16),
        pltpu.VMEM((n_ctx, 4 * LANES), BF16),
    ]
    if need_ctx_out:
        out_specs.append(pl.BlockSpec((None, n_ctx, ATT_WIDTH), lambda b: (b, 0, 0)))
        out_shape.append(jax.ShapeDtypeStruct((bsz, n_ctx, ATT_WIDTH), BF16))
        scratch.append(pltpu.VMEM((n_ctx, ATT_WIDTH), BF16))
    outs = pl.pallas_call(
        kern,
        grid=(bsz,),
        in_specs=[
            pl.BlockSpec((None, n_lat, U_ATT), lambda b: (b, 0, 0)),
            pl.BlockSpec((None, n_ctx, U_ATT), lambda b: (b, 0, 0)),
            pl.BlockSpec((n_lat, LANES), lambda b: (0, 0)),
            pl.BlockSpec((n_lat, LANES), lambda b: (0, 0)),
            pl.BlockSpec((1, LANES), lambda b: (0, 0)),
            pl.BlockSpec((1, LANES), lambda b: (0, 0)),
            pl.BlockSpec(memory_space=pltpu.SMEM),
        ],
        out_specs=out_specs,
        out_shape=out_shape,
        scratch_shapes=scratch,
        compiler_params=pltpu.CompilerParams(
            dimension_semantics=("arbitrary",), vmem_limit_bytes=VMEM_LIMIT),
        name="attention",
    )(uatt_l, uatt_c, cos_t, sin_t, qgain, kgain, sink)
    return (outs[0], outs[1]) if need_ctx_out else (outs[0], None)


def _in_layout():
    names = (('ml_q', ML_WIDTH), ('ml_k', ML_WIDTH), ('ml_v', ML_WIDTH), ('ml_o', ML_WIDTH),
             ('ml_i', N_DIR * ML_HEADS), ('ml_f', N_DIR * ML_HEADS),
             ('ssd_z', SSD_WIDTH), ('ssd_xbc', SSD_XBC), ('ssd_dt', N_DIR * SSD_HEADS),
             ('att_q', ATT_WIDTH), ('att_k', ATT_KV_WIDTH), ('att_v', ATT_KV_WIDTH))
    out, off = {}, 0
    for name, width in names:
        out[name] = (off, off + width)
        off += width
    return out


def _col_map():
    lay = _in_layout()
    take = lambda name: np.arange(lay[name][0], lay[name][1])
    zeros = lambda n: np.full((n,), -1)
    n_f, n_dt, n_i = N_DIR * ML_HEADS, N_DIR * SSD_HEADS, N_DIR * ML_HEADS
    return np.concatenate(
        [take('ml_q'), take('ml_k'), take('ml_v'), take('ml_o'),
         take('ssd_z'), take('ssd_xbc'),
         take('att_q'), take('att_k'), take('att_v'),
         take('ml_f'), take('ssd_dt'), take('ml_i'), zeros(LANES - n_f - n_dt - n_i)])


def _permute_plan():
    src = _col_map()
    plan = []
    for t in range(U_TOTAL // LANES):
        cols = src[t * LANES:(t + 1) * LANES]
        runs, i = [], 0
        while i < LANES:
            j = i
            while j + 1 < LANES and (cols[j + 1] == cols[j] + 1 if cols[j] >= 0 else cols[j + 1] < 0):
                j += 1
            runs.append((int(cols[i]), j - i + 1))
            i = j + 1
        plan.append(tuple(runs))
    return tuple(plan)


def _wprep_kernel(wt_ref, o_ref, *, plan):
    kb = wt_ref.shape[1]
    for t, runs in enumerate(plan):
        pieces = [wt_ref[s:s + n, :] if s >= 0 else jnp.zeros((n, kb), F32) for s, n in runs]
        blk = pieces[0] if len(pieces) == 1 else jnp.concatenate(pieces, axis=0)
        o_ref[:, t * LANES:(t + 1) * LANES] = blk.T.astype(BF16)


def _permute_weights(w_in):
    depth, d, in_width = w_in.shape
    plan = _permute_plan()
    kb = 256
    return pl.pallas_call(
        functools.partial(_wprep_kernel, plan=plan),
        grid=(depth, d // kb),
        in_specs=[pl.BlockSpec((None, in_width, kb), lambda l, i: (l, 0, i))],
        out_specs=pl.BlockSpec((None, kb, U_TOTAL), lambda l, i: (l, i, 0)),
        out_shape=jax.ShapeDtypeStruct((depth, d, U_TOTAL), BF16),
        compiler_params=pltpu.CompilerParams(
            dimension_semantics=("arbitrary", "arbitrary"), vmem_limit_bytes=VMEM_LIMIT),
        name="permute_weights",
    )(jnp.swapaxes(w_in, 1, 2))


def _gate_rows(ml_b_i, ml_b_f, ssd_dt_bias, ssd_a_log):
    n_f, n_dt = N_DIR * ML_HEADS, N_DIR * SSD_HEADS
    row0 = jnp.concatenate([ml_b_f.reshape(-1), ssd_dt_bias.reshape(-1), ml_b_i.reshape(-1),
                            jnp.zeros((LANES - 2 * n_f - n_dt,), F32)])
    row2 = jnp.concatenate([jnp.zeros((n_f,), F32), ssd_a_log.reshape(-1), jnp.zeros((LANES - n_f - n_dt,), F32)])
    zero = jnp.zeros((1, LANES), F32)
    return jnp.concatenate([row0[None], zero, row2[None], jnp.zeros((5, LANES), F32)], axis=0)


def _rope_tables(n_lat):
    nfreq = ATT_HEAD_DIM // 4
    pos = np.arange(n_lat)
    rows, cols = pos // GRID_W, pos % GRID_W
    inv = np.float32(ROPE_BASE) ** (-np.arange(nfreq, dtype=np.float32) / np.float32(nfreq))
    ang_r = rows.astype(np.float32)[:, None] * inv
    ang_c = cols.astype(np.float32)[:, None] * inv
    cos_h = np.concatenate([np.cos(ang_r)] * 2 + [np.cos(ang_c)] * 2, axis=1)
    sin_h = np.concatenate([-np.sin(ang_r), np.sin(ang_r), -np.sin(ang_c), np.sin(ang_c)], axis=1)
    return (jnp.asarray(np.concatenate([cos_h] * 2, axis=1), F32),
            jnp.asarray(np.concatenate([sin_h] * 2, axis=1), F32))


def _token_mixers(u_l, u_c, p, layer, tables, need_ctx_out):
    uml_l, ussd_l, uatt_l, g_l = u_l
    uml_c, ussd_c, uatt_c, g_c = u_c
    pvec = _gate_rows(p['ml_b_i'][layer], p['ml_b_f'][layer], p['ssd_dt_bias'][layer], p['ssd_a_log'][layer])
    m_l, m_c = _mlstm_mixer(uml_l, uml_c, g_l, g_c, pvec, p['ml_norm_g'][layer][None], need_ctx_out)
    s_l, s_c = _ssd_mixer(ussd_l, ussd_c, g_l, g_c, pvec, p['ssd_conv_w'][layer], p['ssd_conv_b'][layer][None],
                          jnp.repeat(p['ssd_d'][layer], SSD_HEAD_DIM)[None], p['ssd_norm_g'][layer][None],
                          need_ctx_out)
    a_l, a_c = _attn_mixer(uatt_l, uatt_c, tables['cos'], tables['sin'],
                           jnp.tile(p['att_qn_g'][layer], 2)[None], jnp.tile(p['att_kn_g'][layer], 2)[None],
                           p['att_sink'][layer], need_ctx_out)
    return (m_l, s_l, a_l), (m_c, s_c, a_c)


def kernel(x, c, ctx, c_ctx, w_mod, b_mod, norm1_g, w_in, ml_b_i, ml_b_f, ml_norm_g, ssd_conv_w, ssd_conv_b,
           ssd_a_log, ssd_dt_bias, ssd_d, ssd_norm_g, att_qn_g, att_kn_g, att_sink, w_out, norm2_g, w_up, w_down):
    p = dict(ml_b_i=ml_b_i, ml_b_f=ml_b_f, ml_norm_g=ml_norm_g, ssd_conv_w=ssd_conv_w, ssd_conv_b=ssd_conv_b,
             ssd_a_log=ssd_a_log, ssd_dt_bias=ssd_dt_bias, ssd_d=ssd_d, ssd_norm_g=ssd_norm_g,
             att_qn_g=att_qn_g, att_kn_g=att_kn_g, att_sink=att_sink)
    bsz, n_lat, d = x.shape
    n_ctx = ctx.shape[1]
    depth = w_mod.shape[0]
    mod_rows = 16
    cc = jnp.concatenate([c, c_ctx[None], jnp.zeros((mod_rows - bsz - 1, d), F32)], axis=0)
    mods = _modulation(cc, w_mod, b_mod).reshape(depth, mod_rows, 6, d)
    cos_t, sin_t = _rope_tables(n_lat)
    tables = dict(cos=cos_t, sin=sin_t)
    lat_row = lambda b: b
    ctx_row = lambda b: bsz
    h_ctx = ctx
    w = _permute_weights(w_in)
    wo, wup, wdn = w_out.astype(BF16), w_up.astype(BF16), w_down.astype(BF16)
    g1, g2 = norm1_g[:, None, :], norm2_g[:, None, :]
    for layer in range(depth):
        need_ctx_out = layer < depth - 1
        u_l = _in_proj(x, mods, lat_row, g1, w, layer, tm=1024)
        u_c = _in_proj(h_ctx, mods, ctx_row, g1, w, layer, tm=256)
        y_l, y_c = _token_mixers(u_l, u_c, p, layer, tables, need_ctx_out)
        x = _post(x, *y_l, mods, lat_row, g2, wo, wup, wdn, layer, tm=512)
        if need_ctx_out:
            h_ctx = _post(h_ctx, *y_c, mods, ctx_row, g2, wo, wup, wdn, layer, tm=256)
    return x
```

```python
import functools
import math

import numpy as np
import jax
import jax.numpy as jnp
from jax import lax
from jax.experimental import pallas as pl
from jax.experimental.pallas import tpu as pltpu

F32 = jnp.float32
BF16 = jnp.bfloat16

EPS = 1e-6
GRID_W = 64
N_DIR = 2
ML_HEADS = 4
ML_HEAD_DIM = 128
ML_WIDTH = ML_HEADS * ML_HEAD_DIM
SSD_HEADS = 8
SSD_HEAD_DIM = 64
SSD_WIDTH = SSD_HEADS * SSD_HEAD_DIM
SSD_STATE = 64
SSD_GROUPS = 2
SSD_XBC = SSD_WIDTH + 2 * SSD_GROUPS * SSD_STATE
ATT_Q_HEADS = 8
ATT_KV_HEADS = 2
ATT_HEAD_DIM = 64
ATT_WIDTH = ATT_Q_HEADS * ATT_HEAD_DIM
ATT_KV_WIDTH = ATT_KV_HEADS * ATT_HEAD_DIM
WINDOW = 128
ROPE_BASE = 10000.0
MIX_WIDTH = ML_WIDTH + SSD_WIDTH + ATT_WIDTH

LANES = 128
CHUNK = 128
VMEM_LIMIT = 56 * 1024 * 1024
UNROLL = 16
ATT_STACK = 4
ATT_BATCH = 2
PREP_GROUP = 8

U_ML = 4 * ML_WIDTH
U_SSD = SSD_WIDTH + SSD_XBC
U_ATT = ATT_WIDTH + 2 * ATT_KV_WIDTH
U_GATE = LANES
U_TOTAL = U_ML + U_SSD + U_ATT + U_GATE
DT_LANE0 = N_DIR * ML_HEADS
LI_LANE0 = DT_LANE0 + N_DIR * SSD_HEADS

ML_SCALE = ML_HEAD_DIM ** -0.5
ATT_SCALE = ATT_HEAD_DIM ** -0.5


def _dot(a, b):
    return jnp.dot(a, b, preferred_element_type=F32)


def _split2(x):
    hi = x.astype(BF16)
    lo = (x - hi.astype(F32)).astype(BF16)
    return hi, lo


def _split3(x):
    hi = x.astype(BF16)
    r = x - hi.astype(F32)
    mid = r.astype(BF16)
    lo = (r - mid.astype(F32)).astype(BF16)
    return hi, mid, lo


def _row_cumsums(tris, x):
    rows = x.shape[0]
    parts = _split3(jnp.concatenate([x, jnp.zeros_like(x)], axis=0))
    return [(_dot(parts[0], tri) + _dot(parts[1], tri) + _dot(parts[2], tri))[0:rows] for tri in tris]


def _sigmoid(x):
    return 1.0 / (1.0 + jnp.exp2(x * (-1.0 / math.log(2.0))))


def _log_sigmoid(x):
    return jnp.minimum(x, 0.0) - jnp.log1p(jnp.exp(-jnp.abs(x)))


def _softplus(x):
    return jnp.maximum(x, 0.0) + jnp.log1p(jnp.exp(-jnp.abs(x)))


def _iota(shape, dim):
    return lax.broadcasted_iota(jnp.int32, shape, dim)


def _chunk_rows(c):
    return pl.ds(pl.multiple_of(c * CHUNK, CHUNK), CHUNK)


def _mod_kernel(s_ref, w_ref, b_ref, o_ref):
    s = s_ref[...]
    s = s * _sigmoid(s)
    w = w_ref[...]
    s_hi, s_lo = _split2(s)
    w_hi, w_lo = _split2(w)
    o_ref[...] = _dot(s_hi, w_hi) + (_dot(s_hi, w_lo) + _dot(s_lo, w_hi)) + b_ref[...]


def _modulation(cc, w_mod, b_mod):
    depth, d, d6 = w_mod.shape
    r = cc.shape[0]
    bn = 2048
    return pl.pallas_call(
        _mod_kernel,
        grid=(depth, d6 // bn),
        in_specs=[
            pl.BlockSpec((r, d), lambda l, j: (0, 0)),
            pl.BlockSpec((None, d, bn), lambda l, j: (l, 0, j)),
            pl.BlockSpec((None, 1, bn), lambda l, j: (l, 0, j)),
        ],
        out_specs=pl.BlockSpec((None, r, bn), lambda l, j: (l, 0, j)),
        out_shape=jax.ShapeDtypeStruct((depth, r, d6), F32),
        compiler_params=pltpu.CompilerParams(
            dimension_semantics=("arbitrary", "arbitrary"), vmem_limit_bytes=VMEM_LIMIT),
        name="modulation",
    )(cc, w_mod, b_mod.reshape(depth, 1, d6))


def _inproj_kernel(x_ref, mod_ref, g_ref, w_ref, oml_ref, ossd_ref, oatt_ref, og_ref):
    x = x_ref[...]
    ms = jnp.mean(x * x, axis=-1, keepdims=True)
    xn = x * lax.rsqrt(ms + EPS)
    h = xn * (g_ref[...] * (1.0 + mod_ref[1:2, :])) + mod_ref[0:1, :]
    hb = h.astype(BF16)
    step = 512
    off = 0
    for o_ref, width in ((oml_ref, U_ML), (ossd_ref, U_SSD), (oatt_ref, U_ATT), (og_ref, U_GATE)):
        for j in range(0, width, step):
            wj = min(step, width - j)
            o_ref[:, j:j + wj] = _dot(hb, w_ref[:, off + j:off + j + wj]).astype(o_ref.dtype)
        off += width


def _in_proj(x, mods, mod_row, g1, w, layer, tm):
    bsz, t, d = x.shape
    widths = (U_ML, U_SSD, U_ATT, U_GATE)
    dtypes = (BF16, BF16, BF16, F32)
    return pl.pallas_call(
        _inproj_kernel,
        grid=(bsz, t // tm),
        in_specs=[
            pl.BlockSpec((None, tm, d), lambda b, i: (b, i, 0)),
            pl.BlockSpec((None, None, 6, d), lambda b, i: (layer, mod_row(b), 0, 0)),
            pl.BlockSpec((None, 1, d), lambda b, i: (layer, 0, 0)),
            pl.BlockSpec((None, d, U_TOTAL), lambda b, i: (layer, 0, 0), pipeline_mode=pl.Buffered(1)),
        ],
        out_specs=[pl.BlockSpec((None, tm, wd), lambda b, i: (b, i, 0)) for wd in widths],
        out_shape=[jax.ShapeDtypeStruct((bsz, t, wd), dt) for wd, dt in zip(widths, dtypes)],
        compiler_params=pltpu.CompilerParams(
            dimension_semantics=("arbitrary", "arbitrary"), vmem_limit_bytes=VMEM_LIMIT),
        name="in_proj",
    )(x, mods, g1, w)


def _post_kernel(x_ref, yml_ref, yssd_ref, yatt_ref, mod_ref, g_ref, wo_ref, wup_ref, wdn_ref, o_ref):
    proj = (_dot(yml_ref[...], wo_ref[0:ML_WIDTH, :])
            + _dot(yssd_ref[...], wo_ref[ML_WIDTH:ML_WIDTH + SSD_WIDTH, :])
            + _dot(yatt_ref[...], wo_ref[ML_WIDTH + SSD_WIDTH:MIX_WIDTH, :]))
    x1 = x_ref[...] + mod_ref[2:3, :] * proj
    ms = jnp.mean(x1 * x1, axis=-1, keepdims=True)
    xn = x1 * lax.rsqrt(ms + EPS)
    hb = (xn * (g_ref[...] * (1.0 + mod_ref[4:5, :])) + mod_ref[3:4, :]).astype(BF16)
    d_ff = wup_ref.shape[1]
    step = 1024
    acc = None
    for j in range(0, d_ff, step):
        up = jnp.maximum(_dot(hb, wup_ref[:, j:j + step]), 0.0)
        part = _dot((up * up).astype(BF16), wdn_ref[j:j + step, :])
        acc = part if acc is None else acc + part
    o_ref[...] = x1 + mod_ref[5:6, :] * acc


def _post(x, yml, yssd, yatt, mods, mod_row, g2, wo, wup, wdn, layer, tm):
    bsz, t, d = x.shape
    d_ff = wup.shape[2]
    const = dict(pipeline_mode=pl.Buffered(1))
    return pl.pallas_call(
        _post_kernel,
        grid=(bsz, t // tm),
        in_specs=[
            pl.BlockSpec((None, tm, d), lambda b, i: (b, i, 0)),
            pl.BlockSpec((None, tm, ML_WIDTH), lambda b, i: (b, i, 0)),
            pl.BlockSpec((None, tm, SSD_WIDTH), lambda b, i: (b, i, 0)),
            pl.BlockSpec((None, tm, ATT_WIDTH), lambda b, i: (b, i, 0)),
            pl.BlockSpec((None, None, 6, d), lambda b, i: (layer, mod_row(b), 0, 0)),
            pl.BlockSpec((None, 1, d), lambda b, i: (layer, 0, 0)),
            pl.BlockSpec((None, MIX_WIDTH, d), lambda b, i: (layer, 0, 0), **const),
            pl.BlockSpec((None, d, d_ff), lambda b, i: (layer, 0, 0), **const),
            pl.BlockSpec((None, d_ff, d), lambda b, i: (layer, 0, 0), **const),
        ],
        out_specs=pl.BlockSpec((None, tm, d), lambda b, i: (b, i, 0)),
        out_shape=jax.ShapeDtypeStruct((bsz, t, d), F32),
        compiler_params=pltpu.CompilerParams(
            dimension_semantics=("arbitrary", "arbitrary"), vmem_limit_bytes=VMEM_LIMIT),
        name="post",
    )(x, yml, yssd, yatt, mods, g2, wo, wup, wdn)


ML_GATES = N_DIR * ML_HEADS


def _mlstm_kernel(ul_ref, uc_ref, gl_ref, gc_ref, pv_ref, ng_ref, ex_ref, *rest,
                  need_ctx_out, n_lat, n_ctx):
    if need_ctx_out:
        yl_ref, yc_ref, arow, packed, ends, mprev, cbk, crun = rest
    else:
        yl_ref, arow, packed, ends, mprev, cbk, crun = rest
        yc_ref = None
    ncl, ncc = n_lat // CHUNK, n_ctx // CHUNK
    nct = ncl + ncc
    nh = ML_HEADS
    ti = _iota((CHUNK, CHUNK), 0)
    si = _iota((CHUNK, CHUNK), 1)
    low = si <= ti
    upp = si >= ti
    tri_low = jnp.where(low, 1.0, 0.0).astype(BF16)
    tri_upp = jnp.where(upp, 1.0, 0.0).astype(BF16)
    fwd_row = _iota((ML_GATES, CHUNK), 0) < nh
    bias = pv_ref[0:1, :]
    neg_inf = -jnp.inf
    ones_blk = jnp.ones((CHUNK, LANES), BF16)

    def prep_group(g_ref, c0, n, goff):
        def per_chunk(x):
            return [x[:, i * CHUNK:(i + 1) * CHUNK] for i in range(n)]

        def wide(xs):
            return xs[0] if n == 1 else jnp.concatenate(xs, axis=1)

        gts = [(g_ref[(c0 + i) * CHUNK:(c0 + i + 1) * CHUNK, :] + bias).T for i in range(n)]
        lf = _log_sigmoid(wide([gt[0:ML_GATES] for gt in gts]))
        li = wide([gt[LI_LANE0:LI_LANE0 + ML_GATES] for gt in gts])
        sums = [_row_cumsums((tri_upp, tri_low), x) for x in per_chunk(lf)]
        fwd = _iota((ML_GATES, n * CHUNK), 0) < nh
        tok = _iota((ML_GATES, n * CHUNK), 1) & (CHUNK - 1)
        b = jnp.where(fwd, wide([s[0] for s in sums]), wide([s[1] for s in sums]))
        a = li - b
        cm = a
        k = 1
        while k < CHUNK:
            before = jnp.where(tok >= k, pltpu.roll(cm, k, 1), neg_inf)
            after = jnp.where(tok < CHUNK - k, pltpu.roll(cm, n * CHUNK - k, 1), neg_inf)
            cm = jnp.maximum(cm, jnp.where(fwd, before, after))
            k *= 2
        parts = [p.astype(F32) for p in _split3(cm) + _split3(b)]
        parts.append(jnp.zeros((CHUNK - len(parts) * ML_GATES, n * CHUNK), F32))
        stacked = jnp.concatenate(parts, axis=0)
        for i in range(n):
            cg = goff + c0 + i
            cols = slice(i * CHUNK, (i + 1) * CHUNK)
            arow[cg] = a[:, cols]
            packed[cg] = stacked[:, cols].T.astype(BF16)
            for plane, x in enumerate((cm, b)):
                last = x[:, (i + 1) * CHUNK - 1:(i + 1) * CHUNK]
                first = x[:, i * CHUNK:i * CHUNK + 1]
                ends[cg, plane] = jnp.broadcast_to(jnp.where(fwd_row[:, 0:1], last, first), (ML_GATES, CHUNK))

    def prep(g_ref, nch, goff):
        for c0 in range(0, nch, PREP_GROUP):
            prep_group(g_ref, c0, min(PREP_GROUP, nch - c0), goff)

    prep(gc_ref, ncc, 0)
    prep(gl_ref, ncl, ncc)

    def m_scan(lo, hi, reverse, m0):
        def body(i, m):
            cg = hi - 1 - i if reverse else lo + i
            mprev[cg, int(reverse)] = m
            return ends[cg, 1] + jnp.maximum(m, ends[cg, 0])
        return lax.fori_loop(0, hi - lo, body, m0)

    m_zero = jnp.zeros((ML_GATES, CHUNK), F32)
    m_scan(0, nct, False, m_zero)
    m_scan(ncc, nct, True, m_scan(0, ncc, True, m_zero))

    def load_kv(u_ref, rows, h):
        k = u_ref[rows, ML_WIDTH + h * LANES:ML_WIDTH + (h + 1) * LANES]
        v = u_ref[rows, 2 * ML_WIDTH + h * LANES:2 * ML_WIDTH + (h + 1) * LANES]
        return k.T, jnp.concatenate([v, ones_blk], axis=1)

    def update(d, h, cg, kt, v1):
        j = d * nh + h
        m_prev = mprev[cg, d, j:j + 1, 0:1]
        m_end = jnp.maximum(m_prev, ends[cg, 0, j:j + 1, 0:1])
        w_row = jnp.exp(arow[cg, j:j + 1, :] - m_end) * ML_SCALE
        ktw = (kt.astype(F32) * w_row).astype(BF16)
        crun[d, h] = jnp.exp(m_prev - m_end) * crun[d, h] + _dot(ktw, v1)

    def bwd_pass(u_ref, nch, goff, keep):
        def body(i, carry):
            c = nch - 1 - i
            rows = _chunk_rows(c)
            for h in range(nh):
                if keep:
                    cbk[goff + c, h] = crun[1, h].astype(BF16)
                kt, v1 = load_kv(u_ref, rows, h)
                update(1, h, goff + c, kt, v1)
            return carry
        lax.fori_loop(0, nch, body, 0, unroll=UNROLL)

    def fwd_pass(u_ref, y_ref, nch, goff):
        def body(c, carry):
            rows = _chunk_rows(c)
            cg = goff + c
            for h in range(nh):
                kt, v1 = load_kv(u_ref, rows, h)
                if y_ref is not None:
                    q = u_ref[rows, h * LANES:(h + 1) * LANES]
                    o = u_ref[rows, 3 * ML_WIDTH + h * LANES:3 * ML_WIDTH + (h + 1) * LANES]
                    sqk = _dot(q, kt) * ML_SCALE
                    qc = _dot(q, jnp.concatenate([crun[0, h].astype(BF16), cbk[cg, h]], axis=1))
                    s_dirs, stats = [], []
                    for d, mask in ((0, low), (1, upp)):
                        j = d * nh + h
                        rep = _dot(packed[cg], ex_ref[:, 2 * j * LANES:2 * (j + 1) * LANES])
                        m_prev = mprev[cg, d, j:j + 1, 0:1]
                        m_t = jnp.maximum(rep[:, 0:LANES], m_prev)
                        s_dirs.append(sqk * jnp.exp(jnp.where(mask, arow[cg, j:j + 1, :] - m_t, neg_inf)))
                        stats.append((m_prev, m_t, rep[:, LANES:2 * LANES]))
                    sv = _dot(jnp.concatenate(s_dirs, axis=0).astype(BF16), v1)
                    hsum = None
                    for d, (m_prev, m_t, b_t) in enumerate(stats):
                        g = jnp.exp(m_prev - m_t)
                        sv_d = sv[d * CHUNK:(d + 1) * CHUNK, :]
                        qc_d = qc[:, 2 * d * LANES:2 * (d + 1) * LANES]
                        num = sv_d[:, 0:LANES] + g * qc_d[:, 0:LANES]
                        den = sv_d[:, LANES:2 * LANES] + g * qc_d[:, LANES:2 * LANES]
                        hd = num / jnp.maximum(jnp.abs(den), jnp.exp(-(b_t + m_t)))
                        hsum = hd if hsum is None else hsum + hd
                    ms = jnp.mean(hsum * hsum, axis=1, keepdims=True)
                    y = hsum * lax.rsqrt(ms + EPS) * ng_ref[:, h * LANES:(h + 1) * LANES]
                    y = y * _sigmoid(o.astype(F32))
                    y_ref[rows, h * LANES:(h + 1) * LANES] = y.astype(y_ref.dtype)
                update(0, h, cg, kt, v1)
            return carry
        lax.fori_loop(0, nch, body, 0, unroll=UNROLL)

    crun[...] = jnp.zeros(crun.shape, F32)
    bwd_pass(uc_ref, ncc, 0, need_ctx_out)
    bwd_pass(ul_ref, ncl, ncc, True)
    fwd_pass(uc_ref, yc_ref, ncc, 0)
    fwd_pass(ul_ref, yl_ref, ncl, ncc)


def _ml_expand_table():
    expand = np.zeros((LANES, ML_GATES * 2 * LANES), np.float32)
    for j in range(ML_GATES):
        for i in range(6):
            half = i // 3
            expand[ML_GATES * i + j, (2 * j + half) * LANES:(2 * j + half + 1) * LANES] = 1.0
    return jnp.asarray(expand, BF16)


def _mlstm_mixer(uml_l, uml_c, g_l, g_c, pvec, norm_g, need_ctx_out):
    expand = _ml_expand_table()
    bsz, n_lat, _ = uml_l.shape
    n_ctx = uml_c.shape[1]
    nct = (n_lat + n_ctx) // CHUNK
    kern = functools.partial(_mlstm_kernel, need_ctx_out=need_ctx_out, n_lat=n_lat, n_ctx=n_ctx)
    out_specs = [pl.BlockSpec((None, n_lat, ML_WIDTH), lambda b: (b, 0, 0))]
    out_shape = [jax.ShapeDtypeStruct((bsz, n_lat, ML_WIDTH), BF16)]
    if need_ctx_out:
        out_specs.append(pl.BlockSpec((None, n_ctx, ML_WIDTH), lambda b: (b, 0, 0)))
        out_shape.append(jax.ShapeDtypeStruct((bsz, n_ctx, ML_WIDTH), BF16))
    outs = pl.pallas_call(
        kern,
        grid=(bsz,),
        in_specs=[
            pl.BlockSpec((None, n_lat, U_ML), lambda b: (b, 0, 0)),
            pl.BlockSpec((None, n_ctx, U_ML), lambda b: (b, 0, 0)),
            pl.BlockSpec((None, n_lat, U_GATE), lambda b: (b, 0, 0)),
            pl.BlockSpec((None, n_ctx, U_GATE), lambda b: (b, 0, 0)),
            pl.BlockSpec((8, LANES), lambda b: (0, 0)),
            pl.BlockSpec((1, ML_WIDTH), lambda b: (0, 0)),
            pl.BlockSpec(expand.shape, lambda b: (0, 0)),
        ],
        out_specs=out_specs,
        out_shape=out_shape,
        scratch_shapes=[
            pltpu.VMEM((nct, ML_GATES, CHUNK), F32),
            pltpu.VMEM((nct, CHUNK, LANES), BF16),
            pltpu.VMEM((nct, 2, ML_GATES, CHUNK), F32),
            pltpu.VMEM((nct, N_DIR, ML_GATES, CHUNK), F32),
            pltpu.VMEM((nct, ML_HEADS, ML_HEAD_DIM, 2 * LANES), BF16),
            pltpu.VMEM((N_DIR, ML_HEADS, ML_HEAD_DIM, 2 * LANES), F32),
        ],
        compiler_params=pltpu.CompilerParams(
            dimension_semantics=("arbitrary",), vmem_limit_bytes=VMEM_LIMIT),
        name="mlstm",
    )(uml_l, uml_c, g_l, g_c, pvec, norm_g, expand)
    return (outs[0], outs[1]) if need_ctx_out else (outs[0], None)


def _ssd_kernel(ul_ref, uc_ref, gl_ref, gc_ref, pv_ref, cw_ref, cb_ref, dsk_ref, ng_ref, ex_ref, *rest,
                need_ctx_out, n_lat, n_ctx):
    if need_ctx_out:
        yl_ref, yc_ref, xact, bt, acum, packed, act, dtt, hbk, hrun = rest
    else:
        yl_ref, xact, bt, acum, packed, act, dtt, hbk, hrun = rest
        yc_ref = None
    ncl, ncc = n_lat // CHUNK, n_ctx // CHUNK
    nh, hg = SSD_HEADS, SSD_HEADS // SSD_GROUPS
    gw = hg * SSD_HEAD_DIM
    gn = SSD_GROUPS * SSD_STATE
    ti = _iota((CHUNK, CHUNK), 0)
    si = _iota((CHUNK, CHUNK), 1)
    low_strict = si < ti
    upp_strict = si > ti
    tri_low = jnp.where(si <= ti, 1.0, 0.0).astype(BF16)
    tri_upp = jnp.where(si >= ti, 1.0, 0.0).astype(BF16)
    lane = _iota((CHUNK, LANES), 1)
    nj = N_DIR * nh
    ac_lane0 = 6 * nj
    fwd_row = _iota((nj, CHUNK), 0) < nh
    bias0 = pv_ref[0:1, :]
    a_rows = jnp.broadcast_to(-jnp.exp(pv_ref[2:3, :]), (LANES, LANES)).T[DT_LANE0:DT_LANE0 + nj]
    shift_down = jnp.where(si == ti - 1, 1.0, 0.0).astype(BF16)
    shift_up = jnp.where(si == ti + 1, 1.0, 0.0).astype(BF16)
    lane_group0 = lane < SSD_STATE
    row_group0 = _iota((gn, CHUNK), 0) < SSD_STATE

    def prep(u_ref, g_ref, nch, goff, nrows):
        def body(c, carry):
            r = pl.multiple_of(c * CHUNK, CHUNK)
            cur = u_ref[pl.ds(r, CHUNK), SSD_WIDTH:U_SSD]
            rp = pl.multiple_of(jnp.maximum(r - CHUNK, 0), CHUNK)
            rn = pl.multiple_of(jnp.minimum(r + CHUNK, nrows - CHUNK), CHUNK)
            prev = u_ref[pl.ds(rp, CHUNK), SSD_WIDTH:U_SSD]
            nxt = u_ref[pl.ds(rn, CHUNK), SSD_WIDTH:U_SSD]
            from_prev = jnp.where((ti == 0) & (si == CHUNK - 1) & (c > 0), 1.0, 0.0).astype(BF16)
            from_next = jnp.where((ti == CHUNK - 1) & (si == 0) & (c < nch - 1), 1.0, 0.0).astype(BF16)
            xm = _dot(jnp.concatenate([shift_down, from_prev], axis=1), jnp.concatenate([cur, prev], axis=0))
            xp = _dot(jnp.concatenate([shift_up, from_next], axis=1), jnp.concatenate([cur, nxt], axis=0))
            conv = (cw_ref[0:1, :] * xm + cw_ref[1:2, :] * cur.astype(F32) + cw_ref[2:3, :] * xp
                    + cb_ref[...])
            xa = (conv * _sigmoid(conv)).astype(BF16)
            xact[goff + c] = xa
            bt[goff + c] = xa[:, SSD_WIDTH:SSD_WIDTH + gn].T

            gt = (g_ref[pl.ds(r, CHUNK), :] + bias0).T
            dt = _softplus(gt[DT_LANE0:DT_LANE0 + nj])
            prefix, suffix = _row_cumsums((tri_upp, tri_low), dt * a_rows)
            ac = jnp.where(fwd_row, prefix, suffix)
            a_end = jnp.where(fwd_row[:, 0:1], ac[:, CHUNK - 1:CHUNK], ac[:, 0:1])
            act[goff + c] = ac - jnp.log(dt)
            dtt[goff + c] = dt
            parts = [p.astype(F32) for p in _split3(jnp.exp(ac)) + _split3(jnp.exp(a_end - ac) * dt)]
            parts += [ac, jnp.zeros((CHUNK - 7 * nj, CHUNK), F32)]
            tile = jnp.concatenate(parts, axis=0).T
            acum[goff + c] = tile
            packed[goff + c] = tile.astype(BF16)
            return carry
        lax.fori_loop(0, nch, body, 0, unroll=UNROLL)

    prep(uc_ref, gc_ref, ncc, 0, n_ctx)
    prep(ul_ref, gl_ref, ncl, ncc, n_lat)

    def e_cols(d, g):
        return slice(d * SSD_WIDTH + g * gw, d * SSD_WIDTH + (g + 1) * gw)

    def w_cols(d, g):
        return slice((N_DIR + d) * SSD_WIDTH + g * gw, (N_DIR + d) * SSD_WIDTH + (g + 1) * gw)

    def update_state(d, cg):
        pk = packed[cg]
        xa = xact[cg]
        b_t = bt[cg]
        zero = jnp.zeros_like(b_t)
        upd = []
        for g in range(SSD_GROUPS):
            wx = (_dot(pk, ex_ref[:, w_cols(d, g)]) * xa[:, g * gw:(g + 1) * gw].astype(F32)).astype(BF16)
            upd.append(_dot(jnp.where(row_group0 == (g == 0), b_t, zero), wx))
        tail, last = (slice(CHUNK - 16, CHUNK), 15) if d == 0 else (slice(0, 16), 0)
        decay = _dot(pk[tail], ex_ref[:, d * SSD_WIDTH:(d + 1) * SSD_WIDTH])[last:last + 1, :]
        hrun[d] = decay * hrun[d] + jnp.concatenate(upd, axis=1)

    def bwd_pass(nch, goff, keep):
        def body(i, carry):
            cg = goff + nch - 1 - i
            if keep:
                hbk[cg] = hrun[1].astype(BF16)
            update_state(1, cg)
            return carry
        lax.fori_loop(0, nch, body, 0, unroll=UNROLL)

    def fwd_pass(u_ref, y_ref, nch, goff):
        def body(c, carry):
            cg = goff + c
            if y_ref is not None:
                rows = _chunk_rows(c)
                xa = xact[cg]
                x = xa[:, 0:SSD_WIDTH]
                b_t = bt[cg]
                c_all = xa[:, SSD_WIDTH + gn:SSD_XBC]
                ac = acum[cg]
                ac_t = act[cg]
                dt_t = dtt[cg]
                lane_w = _iota((CHUNK, gw), 1)
                pk = packed[cg]
                yz_groups, ssq = [], None
                for g in range(SSD_GROUPS):
                    cols = slice(g * gw, (g + 1) * gw)
                    c_g = jnp.where(lane_group0 == (g == 0), c_all, jnp.zeros_like(c_all))
                    cb = _dot(c_g, b_t)
                    s_heads = []
                    for hh in range(hg):
                        jf = g * hg + hh
                        jb = jf + nh
                        on_diag = jnp.log(dt_t[jf:jf + 1, :] + dt_t[jb:jb + 1, :])
                        ac_f = ac[:, ac_lane0 + jf:ac_lane0 + jf + 1]
                        ac_b = ac[:, ac_lane0 + jb:ac_lane0 + jb + 1]
                        expo = jnp.where(low_strict, ac_f - ac_t[jf:jf + 1, :],
                                         jnp.where(upp_strict, ac_b - ac_t[jb:jb + 1, :], on_diag))
                        s_heads.append((cb * jnp.exp(expo)).astype(BF16))
                    x_g = x[:, cols]
                    bd = jnp.concatenate(
                        [jnp.where((lane_w >= hh * SSD_HEAD_DIM) & (lane_w < (hh + 1) * SSD_HEAD_DIM),
                                   x_g, jnp.zeros_like(x_g)) for hh in range(hg)], axis=0)
                    y = _dot(jnp.concatenate(s_heads, axis=1), bd)
                    y = y + _dot(c_all, hrun[0, :, cols].astype(BF16)) * _dot(pk, ex_ref[:, e_cols(0, g)])
                    y = y + _dot(c_all, hbk[cg, :, cols]) * _dot(pk, ex_ref[:, e_cols(1, g)])
                    y = y + dsk_ref[:, cols] * x_g.astype(F32)
                    z = u_ref[rows, cols].astype(F32)
                    yz = y * (z * _sigmoid(z))
                    part = jnp.sum(yz * yz, axis=1, keepdims=True)
                    ssq = part if ssq is None else ssq + part
                    yz_groups.append(yz)
                scale = lax.rsqrt(ssq * (1.0 / SSD_WIDTH) + EPS)
                for g, yz in enumerate(yz_groups):
                    cols = slice(g * gw, (g + 1) * gw)
                    y_ref[rows, cols] = (yz * scale * ng_ref[:, cols]).astype(y_ref.dtype)
            update_state(0, cg)
            return carry
        lax.fori_loop(0, nch, body, 0, unroll=UNROLL)

    hrun[...] = jnp.zeros(hrun.shape, F32)
    bwd_pass(ncc, 0, need_ctx_out)
    bwd_pass(ncl, ncc, True)
    fwd_pass(uc_ref, yc_ref, ncc, 0)
    fwd_pass(ul_ref, yl_ref, ncl, ncc)


def _ssd_expand_table():
    nj = N_DIR * SSD_HEADS
    expand = np.zeros((LANES, 2 * N_DIR * SSD_WIDTH), np.float32)
    for i in range(6):
        for d in range(N_DIR):
            for h in range(SSD_HEADS):
                j = d * SSD_HEADS + h
                c0 = ((i // 3) * N_DIR + d) * SSD_WIDTH + h * SSD_HEAD_DIM
                expand[nj * i + j, c0:c0 + SSD_HEAD_DIM] = 1.0
    return jnp.asarray(expand, BF16)


def _ssd_mixer(ussd_l, ussd_c, g_l, g_c, pvec, conv_w, conv_b, d_skip, norm_g, need_ctx_out):
    expand = _ssd_expand_table()
    bsz, n_lat, _ = ussd_l.shape
    n_ctx = ussd_c.shape[1]
    nct = (n_lat + n_ctx) // CHUNK
    gn = SSD_GROUPS * SSD_STATE
    kern = functools.partial(_ssd_kernel, need_ctx_out=need_ctx_out, n_lat=n_lat, n_ctx=n_ctx)
    out_specs = [pl.BlockSpec((None, n_lat, SSD_WIDTH), lambda b: (b, 0, 0))]
    out_shape = [jax.ShapeDtypeStruct((bsz, n_lat, SSD_WIDTH), BF16)]
    if need_ctx_out:
        out_specs.append(pl.BlockSpec((None, n_ctx, SSD_WIDTH), lambda b: (b, 0, 0)))
        out_shape.append(jax.ShapeDtypeStruct((bsz, n_ctx, SSD_WIDTH), BF16))
    outs = pl.pallas_call(
        kern,
        grid=(bsz,),
        in_specs=[
            pl.BlockSpec((None, n_lat, U_SSD), lambda b: (b, 0, 0)),
            pl.BlockSpec((None, n_ctx, U_SSD), lambda b: (b, 0, 0)),
            pl.BlockSpec((None, n_lat, LANES), lambda b: (b, 0, 0)),
            pl.BlockSpec((None, n_ctx, LANES), lambda b: (b, 0, 0)),
            pl.BlockSpec((8, LANES), lambda b: (0, 0)),
            pl.BlockSpec((3, SSD_XBC), lambda b: (0, 0)),
            pl.BlockSpec((1, SSD_XBC), lambda b: (0, 0)),
            pl.BlockSpec((1, SSD_WIDTH), lambda b: (0, 0)),
            pl.BlockSpec((1, SSD_WIDTH), lambda b: (0, 0)),
            pl.BlockSpec(expand.shape, lambda b: (0, 0)),
        ],
        out_specs=out_specs,
        out_shape=out_shape,
        scratch_shapes=[
            pltpu.VMEM((nct, CHUNK, SSD_XBC), BF16),
            pltpu.VMEM((nct, gn, CHUNK), BF16),
            pltpu.VMEM((nct, CHUNK, LANES), F32),
            pltpu.VMEM((nct, CHUNK, LANES), BF16),
            pltpu.VMEM((nct, N_DIR * SSD_HEADS, CHUNK), F32),
            pltpu.VMEM((nct, N_DIR * SSD_HEADS, CHUNK), F32),
            pltpu.VMEM((nct, gn, SSD_WIDTH), BF16),
            pltpu.VMEM((N_DIR, gn, SSD_WIDTH), F32),
        ],
        compiler_params=pltpu.CompilerParams(
            dimension_semantics=("arbitrary",), vmem_limit_bytes=VMEM_LIMIT),
        name="ssd",
    )(ussd_l, ussd_c, g_l, g_c, pvec, conv_w, conv_b, d_skip, norm_g, expand)
    return (outs[0], outs[1]) if need_ctx_out else (outs[0], None)


def _head_norm(x, gain):
    rows = x.shape[0]
    blks = [x[:, j * LANES:(j + 1) * LANES] for j in range(x.shape[1] // LANES)]
    same_head = (_iota((LANES, LANES), 0) < ATT_HEAD_DIM) == (_iota((LANES, LANES), 1) < ATT_HEAD_DIM)
    sq = jnp.concatenate([b * b for b in blks], axis=0).astype(BF16)
    ms = _dot(sq, jnp.where(same_head, 1.0, 0.0).astype(BF16)) * (1.0 / ATT_HEAD_DIM)
    outs = [b * lax.rsqrt(ms[j * rows:(j + 1) * rows] + EPS) * gain for j, b in enumerate(blks)]
    return outs[0] if len(outs) == 1 else jnp.concatenate(outs, axis=1)


def _rope(x, cos_t, sin_t):
    w = x.shape[1]
    reps = w // LANES
    nfreq = ATT_HEAD_DIM // 4
    partner = jnp.where((_iota(x.shape, 1) & (2 * nfreq - 1)) < nfreq,
                        pltpu.roll(x, w - nfreq, 1), pltpu.roll(x, nfreq, 1))
    if reps > 1:
        cos_t = jnp.concatenate([cos_t] * reps, axis=1)
        sin_t = jnp.concatenate([sin_t] * reps, axis=1)
    return x * cos_t + partner * sin_t


def _dup_heads(x):
    lane_lo = _iota(x.shape, 1) < ATT_HEAD_DIM
    sw = pltpu.roll(x, ATT_HEAD_DIM, 1)
    return jnp.concatenate([jnp.where(lane_lo, x, sw), jnp.where(lane_lo, sw, x)], axis=1)


def _attn_kernel(ul_ref, uc_ref, cos_ref, sin_ref, qg_ref, kg_ref, sink_ref, *rest,
                 need_ctx_out, n_lat, n_ctx):
    if need_ctx_out:
        yl_ref, yc_ref, qp, kb, vb, kc, vc, qcp = rest
    else:
        yl_ref, qp, kb, vb, kc, vc = rest
        yc_ref = qcp = None
    ncl, ncc = n_lat // CHUNK, n_ctx // CHUNK
    rep = ATT_Q_HEADS // ATT_KV_HEADS
    kv0 = ATT_WIDTH
    lane_lo = _iota((CHUNK, LANES), 1) < ATT_HEAD_DIM
    neg_inf = -jnp.inf
    qg = qg_ref[...]
    kg = kg_ref[...]

    zeros_blk = jnp.zeros((CHUNK, vb.shape[1]), BF16)
    vb[0:CHUNK, :] = zeros_blk
    vb[CHUNK + n_lat:2 * CHUNK + n_lat, :] = zeros_blk
    kb[0] = jnp.zeros(kb.shape[1:], BF16)
    kb[ncl + 1] = jnp.zeros(kb.shape[1:], BF16)

    def store_keys_transposed(ref, blk, k):
        kt = k.T
        for g in range(ATT_KV_HEADS):
            head = kt[g * ATT_HEAD_DIM:(g + 1) * ATT_HEAD_DIM]
            ref[blk, g * LANES:(g + 1) * LANES, :] = jnp.concatenate([head, head], axis=0).astype(BF16)

    def values_with_ones(v):
        d = _dup_heads(v).astype(BF16)
        ones = jnp.ones((v.shape[0], LANES), BF16)
        return jnp.concatenate([d[:, 0:LANES], ones, d[:, LANES:2 * LANES], ones], axis=1)

    def prep_lat(c, carry):
        rows = _chunk_rows(c)
        cos_t = cos_ref[rows, :]
        sin_t = sin_ref[rows, :]
        q = _rope(_head_norm(ul_ref[rows, 0:ATT_WIDTH].astype(F32), qg), cos_t, sin_t)
        qp[rows, :] = (q * ATT_SCALE).astype(BF16)
        k = _rope(_head_norm(ul_ref[rows, kv0:kv0 + LANES].astype(F32), kg), cos_t, sin_t)
        v = ul_ref[rows, kv0 + LANES:kv0 + 2 * LANES].astype(F32)
        dst = pl.ds(pl.multiple_of((c + 1) * CHUNK, CHUNK), CHUNK)
        store_keys_transposed(kb, c + 1, k)
        vb[dst, :] = values_with_ones(v)
        return carry
    lax.fori_loop(0, ncl, prep_lat, 0, unroll=UNROLL)

    def prep_ctx(c, carry):
        rows = _chunk_rows(c)
        k = _head_norm(uc_ref[rows, kv0:kv0 + LANES].astype(F32), kg)
        v = uc_ref[rows, kv0 + LANES:kv0 + 2 * LANES].astype(F32)
        store_keys_transposed(kc, c, k)
        vc[rows, :] = values_with_ones(v)
        if need_ctx_out:
            q = _head_norm(uc_ref[rows, 0:ATT_WIDTH].astype(F32), qg)
            qcp[rows, :] = (q * ATT_SCALE).astype(BF16)
        return carry
    lax.fori_loop(0, ncc, prep_ctx, 0, unroll=UNROLL)

    def attend(q_ref, y_ref, rows, band):
        for g in range(ATT_KV_HEADS):
            kl = slice(g * LANES, (g + 1) * LANES)
            vl = slice(2 * g * LANES, 2 * (g + 1) * LANES)
            k_ctx = jnp.concatenate([kc[j, kl, :] for j in range(ncc)], axis=1)
            if band is not None:
                blk, band_rows, valid = band
                k_band = jnp.concatenate([kb[blk + j, kl, :] for j in range(3)], axis=1)
            for r0 in range(0, rep, ATT_STACK):
                heads = [g * rep + r0 + i for i in range(ATT_STACK)]
                qms = []
                for hq in heads:
                    qb = q_ref[rows, (hq // 2) * LANES:(hq // 2 + 1) * LANES]
                    qms.append(jnp.where(lane_lo != (hq % 2 == 1), qb, jnp.zeros_like(qb)))
                qs = qms[0] if ATT_STACK == 1 else jnp.concatenate(qms, axis=0)
                s_c = _dot(qs, k_ctx)
                if band is not None:
                    s_b = _dot(qs, k_band)
                ps, sink_terms = [], []
                for i, hq in enumerate(heads):
                    sl = slice(i * CHUNK, (i + 1) * CHUNK)
                    sk = sink_ref[hq]
                    s = s_c[sl]
                    if band is not None:
                        prev_k, own_k, next_k = (slice(j * CHUNK, (j + 1) * CHUNK) for j in range(3))
                        s = jnp.concatenate([s, jnp.where(valid[:, prev_k], s_b[sl, prev_k], neg_inf), s_b[sl, own_k],
                                             jnp.where(valid[:, next_k], s_b[sl, next_k], neg_inf)], axis=1)
                    m = jnp.maximum(jnp.max(s, axis=1, keepdims=True), sk)
                    ps.append(jnp.exp(s - m).astype(BF16))
                    sink_terms.append(jnp.exp(sk - m))
                p = ps[0] if ATT_STACK == 1 else jnp.concatenate(ps, axis=0)
                acc = _dot(p[:, 0:n_ctx], vc[:, vl])
                if band is not None:
                    acc = acc + _dot(p[:, n_ctx:], vb[band_rows, vl])
                for i, hq in enumerate(heads):
                    sl = slice(i * CHUNK, (i + 1) * CHUNK)
                    o = acc[sl, 0:LANES] / (acc[sl, LANES:2 * LANES] + sink_terms[i])
                    if hq % 2 == 1:
                        jb = hq // 2
                        y_ref[rows, jb * LANES:(jb + 1) * LANES] = jnp.where(lane_lo, o_even, o).astype(y_ref.dtype)
                    o_even = o

    assert WINDOW == CHUNK
    qq = _iota((CHUNK, 3 * CHUNK), 0)
    kk = _iota((CHUNK, 3 * CHUNK), 1)
    in_window = (kk >= qq) & (kk <= qq + 2 * WINDOW)

    def lat_block(i, carry):
        kpos = kk + (i - 1) * CHUNK
        valid = in_window & (kpos >= 0) & (kpos < n_lat)
        band_rows = pl.ds(pl.multiple_of(i * CHUNK, CHUNK), 3 * CHUNK)
        attend(qp, yl_ref, _chunk_rows(i), (i, band_rows, valid))
        return carry
    lax.fori_loop(0, ncl, lat_block, 0, unroll=UNROLL)

    if need_ctx_out:
        def ctx_block(i, carry):
            attend(qcp, yc_ref, _chunk_rows(i), None)
            return carry
        lax.fori_loop(0, ncc, ctx_block, 0, unroll=UNROLL)


def _attn_group_kernel(ul_ref, uc_ref, cos_ref, sin_ref, qg_ref, kg_ref, sink_ref, *rest, **static):
    for e in range(ATT_BATCH):
        _attn_kernel(ul_ref.at[e], uc_ref.at[e], cos_ref, sin_ref, qg_ref, kg_ref, sink_ref,
                     *[r.at[e] for r in rest], **static)


def _attn_mixer(uatt_l, uatt_c, cos_t, sin_t, qgain, kgain, sink, need_ctx_out):
    bsz, n_lat, _ = uatt_l.shape
    n_ctx = uatt_c.shape[1]
    nb = ATT_BATCH
    kern = functools.partial(_attn_group_kernel, need_ctx_out=need_ctx_out, n_lat=n_lat, n_ctx=n_ctx)
    out_specs = [pl.BlockSpec((nb, n_lat, ATT_WIDTH), lambda b: (b, 0, 0))]
    out_shape = [jax.ShapeDtypeStruct((bsz, n_lat, ATT_WIDTH), BF16)]
    scratch = [
        pltpu.VMEM((nb, n_lat, ATT_WIDTH), BF16),
        pltpu.VMEM((nb, n_lat // CHUNK + 2, 2 * LANES, CHUNK), BF16),
        pltpu.VMEM((nb, n_lat + 2 * CHUNK, 4 * LANES), BF16),
        pltpu.VMEM((nb, n_ctx // CHUNK, 2 * LANES, CHUNK), BF16),
        pltpu.VMEM((nb, n_ctx, 4 * LANES), BF16),
    ]
    if need_ctx_out:
        out_specs.append(pl.BlockSpec((nb, n_ctx, ATT_WIDTH), lambda b: (b, 0, 0)))
        out_shape.append(jax.ShapeDtypeStruct((bsz, n_ctx, ATT_WIDTH), BF16))
        scratch.append(pltpu.VMEM((nb, n_ctx, ATT_WIDTH), BF16))
    outs = pl.pallas_call(
        kern,
        grid=(bsz // nb,),
        in_specs=[
            pl.BlockSpec((nb, n_lat, U_ATT), lambda b: (b, 0, 0)),
            pl.BlockSpec((nb, n_ctx, U_ATT), lambda b: (b, 0, 0)),
            pl.BlockSpec((n_lat, LANES), lambda b: (0, 0)),
            pl.BlockSpec((n_lat, LANES), lambda b: (0, 0)),
            pl.BlockSpec((1, LANES), lambda b: (0, 0)),
            pl.BlockSpec((1, LANES), lambda b: (0, 0)),
            pl.BlockSpec(memory_space=pltpu.SMEM),
        ],
        out_specs=out_specs,
        out_shape=out_shape,
        scratch_shapes=scratch,
        compiler_params=pltpu.CompilerParams(
            dimension_semantics=("arbitrary",), vmem_limit_bytes=VMEM_LIMIT),
        name="attention",
    )(uatt_l, uatt_c, cos_t, sin_t, qgain, kgain, sink)
    return (outs[0], outs[1]) if need_ctx_out else (outs[0], None)


def _in_layout():
    names = (('ml_q', ML_WIDTH), ('ml_k', ML_WIDTH), ('ml_v', ML_WIDTH), ('ml_o', ML_WIDTH),
             ('ml_i', N_DIR * ML_HEADS), ('ml_f', N_DIR * ML_HEADS),
             ('ssd_z', SSD_WIDTH), ('ssd_xbc', SSD_XBC), ('ssd_dt', N_DIR * SSD_HEADS),
             ('att_q', ATT_WIDTH), ('att_k', ATT_KV_WIDTH), ('att_v', ATT_KV_WIDTH))
    out, off = {}, 0
    for name, width in names:
        out[name] = (off, off + width)
        off += width
    return out


def _col_map():
    lay = _in_layout()
    take = lambda name: np.arange(lay[name][0], lay[name][1])
    zeros = lambda n: np.full((n,), -1)
    n_f, n_dt, n_i = N_DIR * ML_HEADS, N_DIR * SSD_HEADS, N_DIR * ML_HEADS
    return np.concatenate(
        [take('ml_q'), take('ml_k'), take('ml_v'), take('ml_o'),
         take('ssd_z'), take('ssd_xbc'),
         take('att_q'), take('att_k'), take('att_v'),
         take('ml_f'), take('ssd_dt'), take('ml_i'), zeros(LANES - n_f - n_dt - n_i)])


def _permute_plan():
    src = _col_map()
    plan = []
    for t in range(U_TOTAL // LANES):
        cols = src[t * LANES:(t + 1) * LANES]
        runs, i = [], 0
        while i < LANES:
            j = i
            while j + 1 < LANES and (cols[j + 1] == cols[j] + 1 if cols[j] >= 0 else cols[j + 1] < 0):
                j += 1
            runs.append((int(cols[i]), j - i + 1))
            i = j + 1
        plan.append(tuple(runs))
    return tuple(plan)


def _wprep_kernel(wt_ref, o_ref, *, plan):
    kb = wt_ref.shape[1]
    for t, runs in enumerate(plan):
        pieces = [wt_ref[s:s + n, :] if s >= 0 else jnp.zeros((n, kb), F32) for s, n in runs]
        blk = pieces[0] if len(pieces) == 1 else jnp.concatenate(pieces, axis=0)
        o_ref[:, t * LANES:(t + 1) * LANES] = blk.T.astype(BF16)


def _permute_weights(w_in):
    depth, d, in_width = w_in.shape
    plan = _permute_plan()
    kb = 256
    return pl.pallas_call(
        functools.partial(_wprep_kernel, plan=plan),
        grid=(depth, d // kb),
        in_specs=[pl.BlockSpec((None, in_width, kb), lambda l, i: (l, 0, i))],
        out_specs=pl.BlockSpec((None, kb, U_TOTAL), lambda l, i: (l, i, 0)),
        out_shape=jax.ShapeDtypeStruct((depth, d, U_TOTAL), BF16),
        compiler_params=pltpu.CompilerParams(
            dimension_semantics=("arbitrary", "arbitrary"), vmem_limit_bytes=VMEM_LIMIT),
        name="permute_weights",
    )(jnp.swapaxes(w_in, 1, 2))


def _gate_rows(ml_b_i, ml_b_f, ssd_dt_bias, ssd_a_log):
    n_f, n_dt = N_DIR * ML_HEADS, N_DIR * SSD_HEADS
    row0 = jnp.concatenate([ml_b_f.reshape(-1), ssd_dt_bias.reshape(-1), ml_b_i.reshape(-1),
                            jnp.zeros((LANES - 2 * n_f - n_dt,), F32)])
    row2 = jnp.concatenate([jnp.zeros((n_f,), F32), ssd_a_log.reshape(-1), jnp.zeros((LANES - n_f - n_dt,), F32)])
    zero = jnp.zeros((1, LANES), F32)
    return jnp.concatenate([row0[None], zero, row2[None], jnp.zeros((5, LANES), F32)], axis=0)


def _rope_tables(n_lat):
    nfreq = ATT_HEAD_DIM // 4
    pos = np.arange(n_lat)
    rows, cols = pos // GRID_W, pos % GRID_W
    inv = np.float32(ROPE_BASE) ** (-np.arange(nfreq, dtype=np.float32) / np.float32(nfreq))
    ang_r = rows.astype(np.float32)[:, None] * inv
    ang_c = cols.astype(np.float32)[:, None] * inv
    cos_h = np.concatenate([np.cos(ang_r)] * 2 + [np.cos(ang_c)] * 2, axis=1)
    sin_h = np.concatenate([-np.sin(ang_r), np.sin(ang_r), -np.sin(ang_c), np.sin(ang_c)], axis=1)
    return (jnp.asarray(np.concatenate([cos_h] * 2, axis=1), F32),
            jnp.asarray(np.concatenate([sin_h] * 2, axis=1), F32))


def _token_mixers(u_l, u_c, p, layer, tables, need_ctx_out):
    uml_l, ussd_l, uatt_l, g_l = u_l
    uml_c, ussd_c, uatt_c, g_c = u_c
    pvec = _gate_rows(p['ml_b_i'][layer], p['ml_b_f'][layer], p['ssd_dt_bias'][layer], p['ssd_a_log'][layer])
    m_l, m_c = _mlstm_mixer(uml_l, uml_c, g_l, g_c, pvec, p['ml_norm_g'][layer][None], need_ctx_out)
    s_l, s_c = _ssd_mixer(ussd_l, ussd_c, g_l, g_c, pvec, p['ssd_conv_w'][layer], p['ssd_conv_b'][layer][None],
                          jnp.repeat(p['ssd_d'][layer], SSD_HEAD_DIM)[None], p['ssd_norm_g'][layer][None],
                          need_ctx_out)
    a_l, a_c = _attn_mixer(uatt_l, uatt_c, tables['cos'], tables['sin'],
                           jnp.tile(p['att_qn_g'][layer], 2)[None], jnp.tile(p['att_kn_g'][layer], 2)[None],
                           p['att_sink'][layer], need_ctx_out)
    return (m_l, s_l, a_l), (m_c, s_c, a_c)


def kernel(x, c, ctx, c_ctx, w_mod, b_mod, norm1_g, w_in, ml_b_i, ml_b_f, ml_norm_g, ssd_conv_w, ssd_conv_b,
           ssd_a_log, ssd_dt_bias, ssd_d, ssd_norm_g, att_qn_g, att_kn_g, att_sink, w_out, norm2_g, w_up, w_down):
    p = dict(ml_b_i=ml_b_i, ml_b_f=ml_b_f, ml_norm_g=ml_norm_g, ssd_conv_w=ssd_conv_w, ssd_conv_b=ssd_conv_b,
             ssd_a_log=ssd_a_log, ssd_dt_bias=ssd_dt_bias, ssd_d=ssd_d, ssd_norm_g=ssd_norm_g,
             att_qn_g=att_qn_g, att_kn_g=att_kn_g, att_sink=att_sink)
    bsz, n_lat, d = x.shape
    n_ctx = ctx.shape[1]
    depth = w_mod.shape[0]
    mod_rows = 16
    cc = jnp.concatenate([c, c_ctx[None], jnp.zeros((mod_rows - bsz - 1, d), F32)], axis=0)
    mods = _modulation(cc, w_mod, b_mod).reshape(depth, mod_rows, 6, d)
    cos_t, sin_t = _rope_tables(n_lat)
    tables = dict(cos=cos_t, sin=sin_t)
    lat_row = lambda b: b
    ctx_row = lambda b: bsz
    h_ctx = ctx
    w = _permute_weights(w_in)
    wo, wup, wdn = w_out.astype(BF16), w_up.astype(BF16), w_down.astype(BF16)
    g1, g2 = norm1_g[:, None, :], norm2_g[:, None, :]
    for layer in range(depth):
        need_ctx_out = layer < depth - 1
        u_l = _in_proj(x, mods, lat_row, g1, w, layer, tm=1024)
        u_c = _in_proj(h_ctx, mods, ctx_row, g1, w, layer, tm=256)
        y_l, y_c = _token_mixers(u_l, u_c, p, layer, tables, need_ctx_out)
        x = _post(x, *y_l, mods, lat_row, g2, wo, wup, wdn, layer, tm=512)
        if need_ctx_out:
            h_ctx = _post(h_ctx, *y_c, mods, ctx_row, g2, wo, wup, wdn, layer, tm=256)
    return x
```
